```python
import math
import jax, jax.numpy as jnp
from jax import lax
import numpy as np

D_MODEL = 1024
BATCH = 8
SEQ = 4096
DEPTH = 2

EPS = 1e-6
ATT_HEAD_DIM = 64
ATT_WIDTH = D_MODEL // 2
ATT_HEADS = ATT_WIDTH // ATT_HEAD_DIM
Q_BLOCK = 128
POOL_WIDTH = D_MODEL // 4
POOL_WINDOWS = (2, 4, 8, 16)
POOL_GROUPS = len(POOL_WINDOWS)
POOL_GROUP_DIM = POOL_WIDTH // POOL_GROUPS
SSM_WIDTH = D_MODEL - ATT_WIDTH - POOL_WIDTH
SSM_GROUP_CH = 16
SSM_GROUPS = SSM_WIDTH // SSM_GROUP_CH
SSM_STATE = 64
IN_COLS = 3 * ATT_WIDTH + ATT_HEADS + POOL_WIDTH + SSM_WIDTH
IN_SPLITS = (ATT_WIDTH, 2 * ATT_WIDTH, 3 * ATT_WIDTH, 3 * ATT_WIDTH + ATT_HEADS,
             3 * ATT_WIDTH + ATT_HEADS + POOL_WIDTH)
PEER_HEADS = 8
PEER_KEYS = 128
PEER_EXPERTS = PEER_KEYS * PEER_KEYS
PEER_QDIM = 256
PEER_HALF = PEER_QDIM // 2
PEER_TOPK = 16
PEER_CHUNK = 128

kernel_name = 'hybrid_fox_pool_s5_peer'


def rms_norm(x, g):
    xf = x.astype(jnp.float32)
    y = xf * lax.rsqrt(jnp.mean(xf * xf, axis=-1, keepdims=True) + EPS)
    return (y * g.astype(jnp.float32)).astype(x.dtype)


def forgetting_attention(q, k, v, f_logit, q_gain, k_gain):
    b, s, h, dh = q.shape
    qf = rms_norm(q, q_gain).astype(jnp.float32) * (dh ** -0.5)
    kf = rms_norm(k, k_gain).astype(jnp.float32)
    vf = v.astype(jnp.float32)
    c_t = jnp.cumsum(jax.nn.log_sigmoid(f_logit.astype(jnp.float32)), axis=1).transpose(0, 2, 1)
    nb = s // Q_BLOCK
    q_blocks = qf.reshape(b, nb, Q_BLOCK, h, dh).transpose(1, 0, 2, 3, 4)
    c_blocks = c_t.reshape(b, h, nb, Q_BLOCK).transpose(2, 0, 1, 3)
    key_pos = jnp.arange(s)

    def block(args):
        i, q_i, c_i = args
        logits = jnp.einsum('bqhd,bkhd->bhqk', q_i, kf) + c_i[..., :, None] - c_t[:, :, None, :]
        q_pos = i * Q_BLOCK + jnp.arange(Q_BLOCK)
        causal = key_pos[None, :] <= q_pos[:, None]
        p = jax.nn.softmax(jnp.where(causal, logits, -jnp.inf), axis=-1)
        return jnp.einsum('bhqk,bkhd->bqhd', p, vf)

    out = lax.map(block, (jnp.arange(nb), q_blocks, c_blocks))
    return out.transpose(1, 0, 2, 3, 4).reshape(b, s, h * dh).astype(v.dtype)


def multiscale_pool(u, pool_w, pool_scale):
    b, s, _ = u.shape
    uf = u.astype(jnp.float32).reshape(b, s, POOL_GROUPS, POOL_GROUP_DIM)
    cs = jnp.pad(jnp.cumsum(uf, axis=1), ((0, 0), (1, 0), (0, 0), (0, 0)))
    pos = jnp.arange(s)
    means = []
    for g, w in enumerate(POOL_WINDOWS):
        lo = jnp.maximum(pos + 1 - w, 0)
        cnt = (pos + 1 - lo).astype(jnp.float32)[None, :, None]
        means.append((cs[:, pos + 1, g] - cs[:, lo, g]) / cnt)
    pooled = jnp.stack(means, axis=2) - uf
    mixed = jnp.einsum('bsgc,gcd->bsgd', pooled, pool_w.astype(jnp.float32)).reshape(b, s, POOL_WIDTH)
    return (mixed * pool_scale.astype(jnp.float32)).astype(u.dtype)


def s5_mixer(u, a_re, a_im, log_dt, b_re, b_im, c_re, c_im, d_skip, glu_w, glu_b):
    bsz, s, _ = u.shape
    uf = u.astype(jnp.float32).reshape(bsz, s, SSM_GROUPS, SSM_GROUP_CH)
    a = lax.complex(a_re.astype(jnp.float32), a_im.astype(jnp.float32))
    dt = jnp.exp(log_dt.astype(jnp.float32))[:, None]
    a_bar = jnp.exp(a * dt)
    b_mat = lax.complex(b_re.astype(jnp.float32), b_im.astype(jnp.float32))
    b_bar = ((a_bar - 1.0) / a)[..., None] * b_mat
    bu = jnp.einsum('bsgc,gpc->bsgp', uf.astype(jnp.complex64), b_bar)
    a_full = jnp.broadcast_to(a_bar, bu.shape)

    def combine(left, right):
        a_l, x_l = left
        a_r, x_r = right
        return a_l * a_r, a_r * x_l + x_r

    _, states = lax.associative_scan(combine, (a_full, bu), axis=1)
    c_mat = lax.complex(c_re.astype(jnp.float32), c_im.astype(jnp.float32))
    y = jnp.real(jnp.einsum('bsgp,gcp->bsgc', states, c_mat))
    y = y + d_skip.astype(jnp.float32).reshape(SSM_GROUPS, SSM_GROUP_CH) * uf
    z = jax.nn.gelu(y.reshape(bsz, s, SSM_WIDTH))
    out = z * jax.nn.sigmoid(z @ glu_w.astype(jnp.float32) + glu_b.astype(jnp.float32))
    return out.astype(u.dtype)


def mixer_sublayer(x, norm_g, w_in, f_bias, q_gain, k_gain, pool_w, pool_scale,
                   a_re, a_im, log_dt, b_re, b_im, c_re, c_im, d_skip, glu_w, glu_b,
                   out_norm_g, w_out):
    b, s, _ = x.shape
    h = rms_norm(x, norm_g)
    proj = jnp.einsum('bsd,de->bse', h, w_in)
    q, k, v, f_logit, u_pool, u_ssm = jnp.split(proj, IN_SPLITS, axis=-1)
    shp = (b, s, ATT_HEADS, ATT_HEAD_DIM)
    att = forgetting_attention(q.reshape(shp), k.reshape(shp), v.reshape(shp),
                               f_logit + f_bias, q_gain, k_gain)
    pool = multiscale_pool(u_pool, pool_w, pool_scale)
    ssm = s5_mixer(u_ssm, a_re, a_im, log_dt, b_re, b_im, c_re, c_im, d_skip, glu_w, glu_b)
    p0 = ATT_WIDTH
    p1 = ATT_WIDTH + POOL_WIDTH
    merged = jnp.concatenate([rms_norm(att, out_norm_g[:p0]),
                              rms_norm(pool, out_norm_g[p0:p1]),
                              rms_norm(ssm, out_norm_g[p1:])], axis=-1)
    return x + jnp.einsum('bse,ed->bsd', merged, w_out)


def peer_ffn(h, w_q, sub_keys, u_tab, v_tab):
    bsz, s, d = h.shape
    q = jnp.einsum('bsd,de->bse', h, w_q).astype(jnp.float32).reshape(bsz, s, PEER_HEADS, 2, PEER_HALF)
    scores = jnp.einsum('bshid,hikd->bshik', q, sub_keys.astype(jnp.float32))
    top_s, top_i = lax.top_k(scores, PEER_TOPK)
    cand = (top_s[..., 0, :, None] + top_s[..., 1, None, :]).reshape(bsz, s, PEER_HEADS, PEER_TOPK * PEER_TOPK)
    best_s, best_j = lax.top_k(cand, PEER_TOPK)
    i1 = jnp.take_along_axis(top_i[..., 0, :], best_j // PEER_TOPK, axis=-1)
    i2 = jnp.take_along_axis(top_i[..., 1, :], best_j % PEER_TOPK, axis=-1)
    experts = i1 * PEER_KEYS + i2
    gates = jax.nn.softmax(best_s, axis=-1)
    n_chunks = (bsz * s) // PEER_CHUNK
    hk = PEER_HEADS * PEER_TOPK
    h_c = h.astype(jnp.float32).reshape(n_chunks, PEER_CHUNK, d)
    e_c = experts.reshape(n_chunks, PEER_CHUNK, hk)
    g_c = gates.reshape(n_chunks, PEER_CHUNK, hk)

    def chunk(args):
        hh, ee, gg = args
        u_sel = u_tab[ee].astype(jnp.float32)
        v_sel = v_tab[ee].astype(jnp.float32)
        act = jax.nn.gelu(jnp.einsum('td,tkd->tk', hh, u_sel))
        return jnp.einsum('tk,tkd->td', gg * act, v_sel)

    out = lax.map(chunk, (h_c, e_c, g_c))
    return out.reshape(bsz, s, d).astype(h.dtype)


def setup_inputs(seed: int = 0) -> dict:
    key = jax.random.key(seed)
    ks = jax.random.split(key, 26)
    L, D = DEPTH, D_MODEL
    f32 = jnp.float32

    def nrm(k, shape, scale):
        return jax.random.normal(k, shape, f32) * scale

    def gain(k, shape):
        return 1.0 + 0.05 * jax.random.normal(k, shape, f32)

    n_idx = jnp.arange(SSM_STATE, dtype=f32)
    return {
        'x': nrm(ks[0], (BATCH, SEQ, D), 1.0),
        'norm1_g': gain(ks[1], (L, D)),
        'w_in': nrm(ks[2], (L, D, IN_COLS), D ** -0.5),
        'f_bias': jax.random.uniform(ks[3], (L, ATT_HEADS), f32, 1.0, 5.0),
        'q_gain': gain(ks[4], (L, ATT_HEAD_DIM)),
        'k_gain': gain(ks[5], (L, ATT_HEAD_DIM)),
        'pool_w': nrm(ks[6], (L, POOL_GROUPS, POOL_GROUP_DIM, POOL_GROUP_DIM), POOL_GROUP_DIM ** -0.5),
        'pool_scale': gain(ks[7], (L, POOL_WIDTH)),
        'ssm_a_re': -0.5 + 0.01 * jax.random.normal(ks[8], (L, SSM_GROUPS, SSM_STATE), f32),
        'ssm_a_im': math.pi * n_idx + 0.01 * jax.random.normal(ks[9], (L, SSM_GROUPS, SSM_STATE), f32),
        'ssm_log_dt': jax.random.uniform(ks[10], (L, SSM_GROUPS), f32, math.log(1e-3), math.log(1e-1)),
        'ssm_b_re': nrm(ks[11], (L, SSM_GROUPS, SSM_STATE, SSM_GROUP_CH), (2 * SSM_GROUP_CH) ** -0.5),
        'ssm_b_im': nrm(ks[12], (L, SSM_GROUPS, SSM_STATE, SSM_GROUP_CH), (2 * SSM_GROUP_CH) ** -0.5),
        'ssm_c_re': nrm(ks[13], (L, SSM_GROUPS, SSM_GROUP_CH, SSM_STATE), 0.5),
        'ssm_c_im': nrm(ks[14], (L, SSM_GROUPS, SSM_GROUP_CH, SSM_STATE), 0.5),
        'ssm_d': gain(ks[15], (L, SSM_WIDTH)),
        'glu_w': nrm(ks[16], (L, SSM_WIDTH, SSM_WIDTH), SSM_WIDTH ** -0.5),
        'glu_b': nrm(ks[17], (L, SSM_WIDTH), 0.01),
        'out_norm_g': gain(ks[18], (L, D)),
        'w_out': nrm(ks[19], (L, D, D), D ** -0.5),
        'norm2_g': gain(ks[20], (L, D)),
        'peer_wq': nrm(ks[21], (L, D, PEER_HEADS * PEER_QDIM), D ** -0.5),
        'peer_keys': nrm(ks[22], (L, PEER_HEADS, 2, PEER_KEYS, PEER_HALF), PEER_HALF ** -0.5),
        'peer_u': nrm(ks[23], (L, PEER_EXPERTS, D), D ** -0.5),
        'peer_v': nrm(ks[24], (L, PEER_EXPERTS, D), PEER_HEADS ** -0.5),
    }


def reference(x, norm1_g, w_in, f_bias, q_gain, k_gain, pool_w, pool_scale,
              ssm_a_re, ssm_a_im, ssm_log_dt, ssm_b_re, ssm_b_im, ssm_c_re, ssm_c_im,
              ssm_d, glu_w, glu_b, out_norm_g, w_out, norm2_g,
              peer_wq, peer_keys, peer_u, peer_v):
    for l in range(DEPTH):
        x = mixer_sublayer(x, norm1_g[l], w_in[l], f_bias[l], q_gain[l], k_gain[l],
                           pool_w[l], pool_scale[l], ssm_a_re[l], ssm_a_im[l], ssm_log_dt[l],
                           ssm_b_re[l], ssm_b_im[l], ssm_c_re[l], ssm_c_im[l], ssm_d[l],
                           glu_w[l], glu_b[l], out_norm_g[l], w_out[l])
        x = x + peer_ffn(rms_norm(x, norm2_g[l]), peer_wq[l], peer_keys[l], peer_u[l], peer_v[l])
    return x
```

```python
import functools
import math

import jax
import jax.numpy as jnp
from jax import lax
from jax.experimental import pallas as pl
from jax.experimental.pallas import tpu as pltpu

F32 = jnp.float32
BF16 = jnp.bfloat16
U32 = jnp.uint32
I32 = jnp.int32

EPS = 1e-6
HEAD_DIM = 64
POOL_WINDOWS = (2, 4, 8, 16)
POOL_HALO = 16
SSM_STATE = 64
SSM_GROUP_CH = 16
PEER_KEYS = 128
PEER_TOPK = 16

LANES = 128
SUBLANES = 8
VMEM_LIMIT_BYTES = 56 * 1024 * 1024

_NT = (((1,), (1,)), ((), ()))


def _rmsnorm(x, g):
    return x * lax.rsqrt(jnp.mean(x * x, axis=-1, keepdims=True) + EPS) * g


def _gelu_tanh(x):
    c = math.sqrt(2.0 / math.pi)
    return x * (0.5 * (1.0 + jnp.tanh(c * (x + 0.044715 * (x * x * x)))))


def _params(sem, vmem=None):
    return pltpu.CompilerParams(dimension_semantics=sem, vmem_limit_bytes=vmem)


def _inproj_kernel(x_ref, g_ref, wqkv_ref, wf_ref, wpool_ref, wssm_ref, fb_ref,
                   qg_ref, kg_ref, seg_ref,
                   q_out, k_out, v_out, c_out, pool_out, ssm_out, carry_ref,
                   *, att_w, n_heads):
    i = pl.program_id(1)

    @pl.when(i == 0)
    def _():
        carry_ref[...] = jnp.zeros_like(carry_ref)

    x = x_ref[...]
    ts = x.shape[0]
    h = _rmsnorm(x, g_ref[...])
    hb = h.astype(BF16)
    qkv = jnp.dot(hb, wqkv_ref[...], preferred_element_type=F32)
    seg = seg_ref[...]

    def headnorm(t, gain):
        ms = jnp.dot((t * t).astype(BF16), seg, preferred_element_type=F32)
        return t * lax.rsqrt(ms + EPS) * gain

    q_out[...] = headnorm(qkv[:, :att_w], qg_ref[...]).astype(BF16)
    k_out[...] = headnorm(qkv[:, att_w:2 * att_w], kg_ref[...]).astype(BF16)
    v_out[...] = qkv[:, 2 * att_w:].astype(BF16)
    pool_out[...] = jnp.dot(hb, wpool_ref[...], preferred_element_type=F32)
    ssm_out[...] = jnp.dot(hb, wssm_ref[...], preferred_element_type=F32)

    f = jnp.dot(h, wf_ref[...], precision=lax.Precision.HIGHEST,
                preferred_element_type=F32) + fb_ref[...]
    ls = jnp.minimum(f, 0.0) - jnp.log1p(jnp.exp(-jnp.abs(f)))
    r = lax.broadcasted_iota(I32, (ts, ts), 0)
    c = lax.broadcasted_iota(I32, (ts, ts), 1)
    tri = jnp.where(c <= r, 1.0, 0.0).astype(F32)
    cs = jnp.dot(tri, ls, precision=lax.Precision.HIGHEST,
                 preferred_element_type=F32) + carry_ref[0:1, :]
    carry_ref[...] = jnp.broadcast_to(cs[ts - 1:ts, :], carry_ref.shape)
    c_out[...] = cs[:, :n_heads]


def _inproj(x, g, wqkv, wf, wpool, wssm, fb, qg, kg, seg, *, ts):
    b, s, d = x.shape
    att_w = seg.shape[0]
    n_heads = att_w // HEAD_DIM
    pw = wpool.shape[1]
    sw = wssm.shape[1]
    full = lambda a: pl.BlockSpec(a.shape, lambda bi, i: (0,) * a.ndim)
    tok = lambda w: pl.BlockSpec((None, ts, w), lambda bi, i: (bi, i, 0))
    return pl.pallas_call(
        functools.partial(_inproj_kernel, att_w=att_w, n_heads=n_heads),
        grid=(b, s // ts),
        in_specs=[tok(d), full(g), full(wqkv), full(wf), full(wpool), full(wssm),
                  full(fb), full(qg), full(kg), full(seg)],
        out_specs=[tok(att_w), tok(att_w), tok(att_w), tok(n_heads), tok(pw),
                   pl.BlockSpec((ts, sw), lambda bi, i: (i, bi))],
        out_shape=[jax.ShapeDtypeStruct((b, s, att_w), BF16),
                   jax.ShapeDtypeStruct((b, s, att_w), BF16),
                   jax.ShapeDtypeStruct((b, s, att_w), BF16),
                   jax.ShapeDtypeStruct((b, s, n_heads), F32),
                   jax.ShapeDtypeStruct((b, s, pw), F32),
                   jax.ShapeDtypeStruct((s, b * sw), F32)],
        scratch_shapes=[pltpu.VMEM((SUBLANES, LANES), F32)],
        compiler_params=_params(("arbitrary", "arbitrary"), VMEM_LIMIT_BYTES),
        name="inproj",
    )(x, g, wqkv, wf, wpool, wssm, fb, qg, kg, seg)


def _attn_kernel(q_ref, k_ref, v_ref, ct_ref, cr_ref, o_ref, *, blk):
    p = pl.program_id(1)
    i = pl.program_id(2)
    q = q_ref[...]
    lane = lax.broadcasted_iota(I32, (blk, LANES), 1)
    ct = ct_ref[...]
    hl = lax.broadcasted_iota(I32, ct.shape, 1)
    row = lax.broadcasted_iota(I32, (blk, blk), 0)
    col = lax.broadcasted_iota(I32, (blk, blk), 1)
    outs = []
    for hh in range(2):
        head = 2 * p + hh
        in_head = (lane < HEAD_DIM) if hh == 0 else (lane >= HEAD_DIM)
        qm = jnp.where(in_head, q, jnp.zeros_like(q))
        cq = jnp.sum(jnp.where(hl == head, ct, 0.0), axis=1, keepdims=True)

        def scores(kb):
            start = pl.multiple_of(kb * blk, blk)
            kblk = k_ref[pl.ds(start, blk), :]
            s = lax.dot_general(qm, kblk, _NT, preferred_element_type=F32)
            ck = cr_ref[pl.ds(head, 1), pl.ds(start, blk)]
            return (s + cq) - ck, start

        def update(carry, z, start):
            m, l, acc = carry
            m_new = jnp.maximum(m, jnp.max(z, axis=1, keepdims=True))
            alpha = jnp.exp(m - m_new)
            pe = jnp.exp(z - m_new)
            l = alpha * l + jnp.sum(pe, axis=1, keepdims=True)
            pv = jnp.dot(pe.astype(BF16), v_ref[pl.ds(start, blk), :],
                         preferred_element_type=F32)
            return m_new, l, alpha * acc + pv

        def body(kb, carry):
            z, start = scores(kb)
            return update(carry, z, start)

        init = (jnp.full((blk, 1), -jnp.inf, F32), jnp.zeros((blk, 1), F32),
                jnp.zeros((blk, LANES), F32))
        carry = lax.fori_loop(0, i, body, init)
        z, start = scores(i)
        z = jnp.where(col <= row, z, -jnp.inf)
        m, l, acc = update(carry, z, start)
        outs.append(acc / l)
    o_ref[...] = jnp.where(lane < HEAD_DIM, outs[0], outs[1])


def _attention(q, k, v, c_tok, c_row, *, blk):
    b, s, w = q.shape
    n_heads = c_tok.shape[2]
    return pl.pallas_call(
        functools.partial(_attn_kernel, blk=blk),
        grid=(b, w // LANES, s // blk),
        in_specs=[pl.BlockSpec((None, blk, LANES), lambda bi, p, i: (bi, i, p)),
                  pl.BlockSpec((None, s, LANES), lambda bi, p, i: (bi, 0, p)),
                  pl.BlockSpec((None, s, LANES), lambda bi, p, i: (bi, 0, p)),
                  pl.BlockSpec((None, blk, n_heads), lambda bi, p, i: (bi, i, 0)),
                  pl.BlockSpec((None, n_heads, s), lambda bi, p, i: (bi, 0, 0))],
        out_specs=pl.BlockSpec((None, blk, LANES), lambda bi, p, i: (bi, i, p)),
        out_shape=jax.ShapeDtypeStruct((b, s, w), F32),
        compiler_params=_params(("arbitrary", "arbitrary", "arbitrary"), VMEM_LIMIT_BYTES),
        name="fox_attention",
    )(q, k, v, c_tok, c_row)


def _pool_kernel(u_ref, prev_ref, w_ref, sc_ref, o_ref, *, group_dim):
    i = pl.program_id(1)
    cur = u_ref[...]
    ts, pw = cur.shape
    prev = jnp.where(i > 0, prev_ref[...], 0.0)
    ext = jnp.concatenate([prev, cur], axis=0)
    pos = i * ts + lax.broadcasted_iota(I32, (ts, 1), 0)
    grp = lax.broadcasted_iota(I32, (ts, pw), 1) // group_dim
    acc = cur
    pooled = jnp.zeros_like(cur)
    for j in range(1, max(POOL_WINDOWS)):
        acc = acc + ext[POOL_HALO - j:POOL_HALO - j + ts, :]
        win = j + 1
        if win in POOL_WINDOWS:
            cnt = jnp.minimum(pos + 1, win).astype(F32)
            pooled = jnp.where(grp == POOL_WINDOWS.index(win), acc / cnt, pooled)
    pooled = pooled - cur
    o_ref[...] = jnp.dot(pooled.astype(BF16), w_ref[...],
                         preferred_element_type=F32) * sc_ref[...]


def _pool(u, wblk, scale, *, ts):
    b, s, pw = u.shape
    hb = ts // POOL_HALO
    return pl.pallas_call(
        functools.partial(_pool_kernel, group_dim=pw // len(POOL_WINDOWS)),
        grid=(b, s // ts),
        in_specs=[pl.BlockSpec((None, ts, pw), lambda bi, i: (bi, i, 0)),
                  pl.BlockSpec((None, POOL_HALO, pw),
                               lambda bi, i: (bi, jnp.maximum(i * hb - 1, 0), 0)),
                  pl.BlockSpec(wblk.shape, lambda bi, i: (0, 0)),
                  pl.BlockSpec(scale.shape, lambda bi, i: (0, 0))],
        out_specs=pl.BlockSpec((None, ts, pw), lambda bi, i: (bi, i, 0)),
        out_shape=jax.ShapeDtypeStruct((b, s, pw), F32),
        compiler_params=_params(("arbitrary", "arbitrary")),
        name="pool",
    )(u, u, wblk, scale)


def _s5_kernel(u_ref, are_ref, aim_ref, bblk_ref, cblk_ref, d_ref, gw_ref, gb_ref,
               o_ref, st_ref, hre_ref, him_ref, *, nb, n):
    @pl.when(pl.program_id(0) == 0)
    def _():
        hre_ref[...] = jnp.zeros_like(hre_ref)
        him_ref[...] = jnp.zeros_like(him_ref)

    u = u_ref[...]
    tc = u.shape[0] // nb
    st_ref[...] = jnp.dot(u.astype(BF16), bblk_ref[...], preferred_element_type=F32)
    ar = jnp.broadcast_to(are_ref[...], (nb, n))
    ai = jnp.broadcast_to(aim_ref[...], (nb, n))

    def step(t, carry):
        hr, hi = carry
        r0 = pl.multiple_of(t * nb, nb)
        br = st_ref[pl.ds(r0, nb), 0:n]
        bi = st_ref[pl.ds(r0, nb), n:2 * n]
        nr = (ar * hr - ai * hi) + br
        ni = (ar * hi + ai * hr) + bi
        st_ref[pl.ds(r0, nb), 0:n] = nr
        st_ref[pl.ds(r0, nb), n:2 * n] = ni
        return nr, ni

    hr, hi = lax.fori_loop(0, tc, step, (hre_ref[...], him_ref[...]))
    hre_ref[...] = hr
    him_ref[...] = hi
    y = jnp.dot(st_ref[...].astype(BF16), cblk_ref[...], preferred_element_type=F32)
    z = _gelu_tanh(y + d_ref[...] * u)
    gate = jax.nn.sigmoid(jnp.dot(z.astype(BF16), gw_ref[...], preferred_element_type=F32)
                          + gb_ref[...])
    o_ref[...] = z * gate


def _s5(u_tm, a_re, a_im, bblk, cblk, dskip, gw, gb, *, nb, tc):
    rows, sw = u_tm.shape
    n = a_re.shape[1]
    rb = tc * nb
    full = lambda a: pl.BlockSpec(a.shape, lambda i: (0,) * a.ndim)
    return pl.pallas_call(
        functools.partial(_s5_kernel, nb=nb, n=n),
        grid=(rows // rb,),
        in_specs=[pl.BlockSpec((rb, sw), lambda i: (i, 0)), full(a_re), full(a_im),
                  full(bblk), full(cblk), full(dskip), full(gw), full(gb)],
        out_specs=pl.BlockSpec((rb, sw), lambda i: (i, 0)),
        out_shape=jax.ShapeDtypeStruct((rows, sw), F32),
        scratch_shapes=[pltpu.VMEM((rb, 2 * n), F32), pltpu.VMEM((nb, n), F32),
                        pltpu.VMEM((nb, n), F32)],
        compiler_params=_params(("arbitrary",), VMEM_LIMIT_BYTES),
        name="s5_scan",
    )(u_tm, a_re, a_im, bblk, cblk, dskip, gw, gb)


def _merge_kernel(x_ref, att_ref, pool_ref, ssm_ref, og_ref, w_ref, o_ref, *, att_w, pool_w):
    og = og_ref[...]
    p1 = att_w + pool_w
    merged = jnp.concatenate([
        _rmsnorm(att_ref[...], og[:, :att_w]),
        _rmsnorm(pool_ref[...], og[:, att_w:p1]),
        _rmsnorm(ssm_ref[...], og[:, p1:])], axis=1).astype(BF16)
    o_ref[...] = x_ref[...] + jnp.dot(merged, w_ref[...], preferred_element_type=F32)


def _merge(x, att, pool, ssm_tm, og, w, *, ts):
    b, s, d = x.shape
    att_w, pool_w = att.shape[2], pool.shape[2]
    sw = ssm_tm.shape[1] // b
    tok = lambda wd: pl.BlockSpec((None, ts, wd), lambda bi, i: (bi, i, 0))
    return pl.pallas_call(
        functools.partial(_merge_kernel, att_w=att_w, pool_w=pool_w),
        grid=(b, s // ts),
        in_specs=[tok(d), tok(att_w), tok(pool_w),
                  pl.BlockSpec((ts, sw), lambda bi, i: (i, bi)),
                  pl.BlockSpec(og.shape, lambda bi, i: (0, 0)),
                  pl.BlockSpec(w.shape, lambda bi, i: (0, 0))],
        out_specs=tok(d),
        out_shape=jax.ShapeDtypeStruct((b, s, d), F32),
        compiler_params=_params(("arbitrary", "arbitrary"), VMEM_LIMIT_BYTES),
        name="merge_outproj",
    )(x, att, pool, ssm_tm, og, w)


def _topk_rows(a, k, extra=None):
    nrow, nl = a.shape
    row = lax.broadcasted_iota(I32, (nrow, nl), 0).astype(F32)
    out_row = lax.broadcasted_iota(I32, (k, nl), 0)
    vals = jnp.zeros((k, nl), F32)
    sel = jnp.zeros((k, nl), F32)
    for j in range(k):
        m = jnp.max(a, axis=0, keepdims=True)
        ix = jnp.min(jnp.where(a == m, row, float(nrow)), axis=0, keepdims=True)
        hit = row == ix
        if extra is None:
            pick = ix
        else:
            pick = jnp.sum(jnp.where(hit, extra, 0.0), axis=0, keepdims=True)
        vals = jnp.where(out_row == j, m, vals)
        sel = jnp.where(out_row == j, pick, sel)
        a = jnp.where(hit, -jnp.inf, a)
    return vals, sel


def _route_kernel(x_ref, g_ref, wq_ref, keys_ref, h_out, off_out, sh_out, gate_out, q_sc,
                  *, n_half):
    hd = pl.program_id(1)
    qd = 2 * PEER_KEYS

    @pl.when(hd == 0)
    def _():
        h = _rmsnorm(x_ref[...], g_ref[...])
        h_out[...] = h
        q_sc[...] = jnp.dot(h.astype(BF16), wq_ref[...], preferred_element_type=F32)

    q = q_sc[:, pl.ds(pl.multiple_of(hd * qd, qd), qd)].astype(BF16)
    tops = []
    for i in range(2):
        sc = lax.dot_general(keys_ref[i], q[:, i * PEER_KEYS:(i + 1) * PEER_KEYS], _NT,
                             preferred_element_type=F32)
        tops.append(_topk_rows(sc, PEER_TOPK))
    (s1, i1), (s2, i2) = tops
    cand = jnp.concatenate([s1[a:a + 1, :] + s2 for a in range(PEER_TOPK)], axis=0)
    ecand = jnp.concatenate([i1[a:a + 1, :] * float(PEER_KEYS) + i2
                             for a in range(PEER_TOPK)], axis=0)
    best, experts = _topk_rows(cand, PEER_TOPK, extra=ecand)
    w = jnp.exp(best - best[0:1, :])
    gate_out[...] = w / jnp.sum(w, axis=0, keepdims=True)
    e = experts.astype(I32)
    hi_half = e < n_half
    off_out[...] = jnp.where(hi_half, e, e - n_half) * SUBLANES
    sh_out[...] = jnp.where(hi_half, 0, 16)


def _route(x, g, wq, keys, *, tm, n_experts):
    t, d = x.shape
    n_heads = keys.shape[0]
    hk = pl.BlockSpec((None, PEER_TOPK, tm), lambda i, h: (h, 0, i))
    shp = lambda dt: jax.ShapeDtypeStruct((n_heads, PEER_TOPK, t), dt)
    return pl.pallas_call(
        functools.partial(_route_kernel, n_half=n_experts // 2),
        grid=(t // tm, n_heads),
        in_specs=[pl.BlockSpec((tm, d), lambda i, h: (i, 0)),
                  pl.BlockSpec(g.shape, lambda i, h: (0, 0)),
                  pl.BlockSpec(wq.shape, lambda i, h: (0, 0)),
                  pl.BlockSpec((None, 2, PEER_KEYS, keys.shape[3]), lambda i, h: (h, 0, 0, 0))],
        out_specs=[pl.BlockSpec((tm, d), lambda i, h: (i, 0)), hk, hk, hk],
        out_shape=[jax.ShapeDtypeStruct((t, d), F32), shp(I32), shp(I32), shp(F32)],
        scratch_shapes=[pltpu.VMEM((tm, wq.shape[1]), F32)],
        compiler_params=_params(("arbitrary", "arbitrary"), VMEM_LIMIT_BYTES),
        name="peer_route",
    )(x, g, wq, keys)


def _expert_row(tab_ref, off, sh):
    w = tab_ref[pl.ds(pl.multiple_of(off, SUBLANES), SUBLANES), :]
    return lax.bitcast_convert_type((w << sh.astype(U32)) & jnp.uint32(0xFFFF0000), F32)


def _fold(a, b, sh, idx, axis):
    n = a.shape[axis]
    m = (idx & sh) == 0
    x = jnp.where(m, a, pltpu.roll(b, sh, axis))
    y = jnp.where(m, pltpu.roll(a, n - sh, axis), b)
    return x + y


def _fold_tree(vs, idx, axis, top):
    sh = top
    while len(vs) > 1:
        half = len(vs) // 2
        vs = [_fold(vs[j], vs[j + half], sh, idx, axis) for j in range(half)]
        sh //= 2
    return vs[0]


def _peer_u_kernel(off_ref, sh_ref, h_ref, gate_ref, tab_ref, coef_ref, pre_ref, *, tt, hk):
    sub = lax.broadcasted_iota(I32, (SUBLANES, LANES), 0)
    lane = lax.broadcasted_iota(I32, (SUBLANES, LANES), 1)
    ngrp = hk // SUBLANES

    def tok(t, carry):
        h = h_ref[t]
        rs = []
        for g in range(ngrp):
            ps = [_expert_row(tab_ref, off_ref[t, g * SUBLANES + j], sh_ref[t, g * SUBLANES + j]) * h
                  for j in range(SUBLANES)]
            rs.append(_fold_tree(ps, sub, 0, SUBLANES // 2))
        q = _fold_tree(rs, lane, 1, LANES // 2)
        sh = SUBLANES // 2
        while sh >= 1:
            q = q + jnp.where((lane & sh) == 0, pltpu.roll(q, LANES - sh, 1), pltpu.roll(q, sh, 1))
            sh //= 2
        q = jnp.where((lane & (SUBLANES - 1)) == sub, q, 0.0)
        sh = SUBLANES // 2
        while sh >= 1:
            q = q + pltpu.roll(q, sh, 0)
            sh //= 2
        pre_ref[pl.ds(t, 1), :] = q[0:1, :]
        return carry

    lax.fori_loop(0, tt, tok, 0)
    coef_ref[...] = gate_ref[...] * _gelu_tanh(pre_ref[...])


def _peer_v_kernel(off_ref, sh_ref, coef_ref, x_ref, tab_ref, o_ref, *, tt, hk, n_acc):
    def tok(t, carry):
        accs = [None] * n_acc
        for k in range(hk):
            term = coef_ref[t, k] * _expert_row(tab_ref, off_ref[t, k], sh_ref[t, k])
            a = k % n_acc
            accs[a] = term if accs[a] is None else accs[a] + term
        while len(accs) > 1:
            accs = [accs[j] + accs[j + len(accs) // 2] for j in range(len(accs) // 2)]
        o_ref[t] = x_ref[t] + accs[0]
        return carry

    lax.fori_loop(0, tt, tok, 0)


def _resident(a):
    return pl.BlockSpec(a.shape, lambda i: (0,) * a.ndim, pipeline_mode=pl.Buffered(1))


def _peer_u(off, sh, h3, gates, tab, *, tt):
    t, hk = off.shape
    smem = pl.BlockSpec((tt, hk), lambda i: (i, 0), memory_space=pltpu.SMEM)
    return pl.pallas_call(
        functools.partial(_peer_u_kernel, tt=tt, hk=hk),
        grid=(t // tt,),
        in_specs=[smem, smem,
                  pl.BlockSpec((tt, SUBLANES, LANES), lambda i: (i, 0, 0)),
                  pl.BlockSpec((tt, hk), lambda i: (i, 0)),
                  _resident(tab)],
        out_specs=pl.BlockSpec((tt, hk), lambda i: (i, 0)),
        out_shape=jax.ShapeDtypeStruct((t, hk), F32),
        scratch_shapes=[pltpu.VMEM((tt, hk), F32)],
        compiler_params=_params(("arbitrary",), VMEM_LIMIT_BYTES),
        name="peer_u",
    )(off, sh, h3, gates, tab)


def _peer_v(off, sh, coef, x3, tab, *, tt):
    t, hk = off.shape
    smem = pl.BlockSpec((tt, hk), lambda i: (i, 0), memory_space=pltpu.SMEM)
    tile = pl.BlockSpec((tt, SUBLANES, LANES), lambda i: (i, 0, 0))
    return pl.pallas_call(
        functools.partial(_peer_v_kernel, tt=tt, hk=hk, n_acc=8),
        grid=(t // tt,),
        in_specs=[smem, smem, smem, tile, _resident(tab)],
        out_specs=tile,
        out_shape=jax.ShapeDtypeStruct(x3.shape, F32),
        compiler_params=_params(("arbitrary",), VMEM_LIMIT_BYTES),
        name="peer_v",
    )(off, sh, coef, x3, tab)


def _pack_table(tab):
    n, d = tab.shape
    b = lax.bitcast_convert_type(tab.astype(BF16), jnp.uint16).astype(U32)
    w = (b[:n // 2] << 16) | b[n // 2:]
    return w.reshape(n // 2 * (d // LANES), LANES)


def _block_diag(blocks):
    g, r, c = blocks.shape
    eye = jnp.eye(g, dtype=blocks.dtype)
    return (eye[:, None, :, None] * blocks[:, :, None, :]).reshape(g * r, g * c)


def _s5_params(a_re, a_im, log_dt, b_re, b_im, c_re, c_im):
    a = lax.complex(a_re, a_im)
    dt = jnp.exp(log_dt)[:, None]
    a_bar = jnp.exp(a * dt)
    b_bar = ((a_bar - 1.0) / a)[..., None] * lax.complex(b_re, b_im)
    bt = jnp.swapaxes(b_bar, 1, 2)
    bblk = jnp.concatenate([_block_diag(jnp.real(bt)), _block_diag(jnp.imag(bt))], axis=1)
    ct_re = jnp.swapaxes(c_re, 1, 2)
    ct_im = jnp.swapaxes(c_im, 1, 2)
    cblk = jnp.concatenate([_block_diag(ct_re), -_block_diag(ct_im)], axis=0)
    n = a_bar.size
    return (jnp.real(a_bar).reshape(1, n), jnp.imag(a_bar).reshape(1, n),
            bblk.astype(BF16), cblk.astype(BF16))


def _tiles(s, t):
    ts = min(512, s)
    blk = min(256, s)
    tc = min(64, s)
    tm = min(256, t)
    tt = min(64, t)
    return ts, blk, tc, tm, tt


def _layer(x, norm1_g, w_in, f_bias, q_gain, k_gain, pool_w, pool_scale,
           a_re, a_im, log_dt, b_re, b_im, c_re, c_im, ssm_d, glu_w, glu_b,
           out_norm_g, w_out, norm2_g, peer_wq, peer_keys, peer_u, peer_v):
    b, s, d = x.shape
    t = b * s
    n_heads = f_bias.shape[0]
    att_w = n_heads * HEAD_DIM
    pool_width = pool_scale.shape[0]
    ssm_w = ssm_d.shape[0]
    ts, blk, tc, tm, tt = _tiles(s, t)
    row = lambda v: v.reshape(1, -1)

    c0, c1 = 3 * att_w, 3 * att_w + n_heads
    wf = jnp.zeros((d, LANES), F32).at[:, :n_heads].set(w_in[:, c0:c1])
    fb = jnp.zeros((1, LANES), F32).at[0, :n_heads].set(f_bias)
    seg = _block_diag(jnp.full((n_heads, HEAD_DIM, HEAD_DIM), 1.0 / HEAD_DIM, F32)).astype(BF16)
    q, k, v, c_tok, u_pool, u_ssm = _inproj(
        x, row(norm1_g), w_in[:, :c0].astype(BF16), wf,
        w_in[:, c1:c1 + pool_width].astype(BF16), w_in[:, c1 + pool_width:].astype(BF16), fb,
        row(jnp.tile(q_gain, n_heads) * HEAD_DIM ** -0.5), row(jnp.tile(k_gain, n_heads)),
        seg, ts=ts)
    att = _attention(q, k, v, c_tok, jnp.swapaxes(c_tok, 1, 2), blk=blk)
    pool = _pool(u_pool, _block_diag(pool_w).astype(BF16), row(pool_scale), ts=ts)
    are, aim, bblk, cblk = _s5_params(a_re, a_im, log_dt, b_re, b_im, c_re, c_im)
    ssm = _s5(u_ssm.reshape(s * b, ssm_w), are, aim, bblk, cblk, row(ssm_d),
              glu_w.astype(BF16), row(glu_b), nb=b, tc=tc)
    x1 = _merge(x, att, pool, ssm.reshape(s, b * ssm_w), row(out_norm_g),
                w_out.astype(BF16), ts=ts)

    n_experts = peer_u.shape[0]
    hk = peer_keys.shape[0] * PEER_TOPK
    kh = peer_keys.astype(BF16)
    h2, off, sh, gates = _route(x1.reshape(t, d), row(norm2_g), peer_wq.astype(BF16), kh,
                                tm=tm, n_experts=n_experts)
    tok_major = lambda a: a.reshape(hk, t).T
    off, sh, gates = tok_major(off), tok_major(sh), tok_major(gates)
    coef = _peer_u(off, sh, h2.reshape(t, SUBLANES, d // SUBLANES), gates,
                   _pack_table(peer_u), tt=tt)
    x2 = _peer_v(off, sh, coef, x1.reshape(t, SUBLANES, d // SUBLANES),
                 _pack_table(peer_v), tt=tt)
    return x2.reshape(b, s, d)


def kernel(x, norm1_g, w_in, f_bias, q_gain, k_gain, pool_w, pool_scale, ssm_a_re, ssm_a_im,
           ssm_log_dt, ssm_b_re, ssm_b_im, ssm_c_re, ssm_c_im, ssm_d, glu_w, glu_b, out_norm_g,
           w_out, norm2_g, peer_wq, peer_keys, peer_u, peer_v):
    per_layer = (norm1_g, w_in, f_bias, q_gain, k_gain, pool_w, pool_scale, ssm_a_re, ssm_a_im,
                 ssm_log_dt, ssm_b_re, ssm_b_im, ssm_c_re, ssm_c_im, ssm_d, glu_w, glu_b,
                 out_norm_g, w_out, norm2_g, peer_wq, peer_keys, peer_u, peer_v)
    for l in range(norm1_g.shape[0]):
        x = _layer(x, *[p[l] for p in per_layer])
    return x
```

```python
import functools
import math

import jax
import jax.numpy as jnp
from jax import lax
from jax.experimental import pallas as pl
from jax.experimental.pallas import tpu as pltpu

F32 = jnp.float32
BF16 = jnp.bfloat16
U32 = jnp.uint32
I32 = jnp.int32

EPS = 1e-6
HEAD_DIM = 64
POOL_WINDOWS = (2, 4, 8, 16)
POOL_HALO = 16
SSM_STATE = 64
SSM_GROUP_CH = 16
PEER_KEYS = 128
PEER_TOPK = 16

LANES = 128
SUBLANES = 8
VMEM_LIMIT_BYTES = 56 * 1024 * 1024

_NT = (((1,), (1,)), ((), ()))


def _rmsnorm(x, g):
    return x * lax.rsqrt(jnp.mean(x * x, axis=-1, keepdims=True) + EPS) * g


def _gelu_tanh(x):
    c = math.sqrt(2.0 / math.pi)
    return x * (0.5 * (1.0 + jnp.tanh(c * (x + 0.044715 * (x * x * x)))))


def _params(sem, vmem=None):
    return pltpu.CompilerParams(dimension_semantics=sem, vmem_limit_bytes=vmem)


def _inproj_kernel(x_ref, g_ref, wqkv_ref, wf_ref, wpool_ref, wssm_ref, fb_ref,
                   qg_ref, kg_ref, seg_ref,
                   q_out, k_out, v_out, c_out, pool_out, ssm_out, carry_ref,
                   *, att_w, n_heads):
    i = pl.program_id(1)

    @pl.when(i == 0)
    def _():
        carry_ref[...] = jnp.zeros_like(carry_ref)

    x = x_ref[...]
    ts = x.shape[0]
    h = _rmsnorm(x, g_ref[...])
    hb = h.astype(BF16)
    qkv = jnp.dot(hb, wqkv_ref[...], preferred_element_type=F32)
    seg = seg_ref[...]

    def headnorm(t, gain):
        ms = jnp.dot((t * t).astype(BF16), seg, preferred_element_type=F32)
        return t * lax.rsqrt(ms + EPS) * gain

    q_out[...] = headnorm(qkv[:, :att_w], qg_ref[...]).astype(BF16)
    k_out[...] = headnorm(qkv[:, att_w:2 * att_w], kg_ref[...]).astype(BF16)
    v_out[...] = qkv[:, 2 * att_w:].astype(BF16)
    pool_out[...] = jnp.dot(hb, wpool_ref[...], preferred_element_type=F32)
    ssm_out[...] = jnp.dot(hb, wssm_ref[...], preferred_element_type=F32)

    f = jnp.dot(h, wf_ref[...], precision=lax.Precision.HIGHEST,
                preferred_element_type=F32) + fb_ref[...]
    ls = jnp.minimum(f, 0.0) - jnp.log1p(jnp.exp(-jnp.abs(f)))
    r = lax.broadcasted_iota(I32, (ts, ts), 0)
    c = lax.broadcasted_iota(I32, (ts, ts), 1)
    tri = jnp.where(c <= r, 1.0, 0.0).astype(F32)
    cs = jnp.dot(tri, ls, precision=lax.Precision.HIGHEST,
                 preferred_element_type=F32) + carry_ref[0:1, :]
    carry_ref[...] = jnp.broadcast_to(cs[ts - 1:ts, :], carry_ref.shape)
    c_out[...] = cs[:, :n_heads]


def _inproj(x, g, wqkv, wf, wpool, wssm, fb, qg, kg, seg, *, ts):
    b, s, d = x.shape
    att_w = seg.shape[0]
    n_heads = att_w // HEAD_DIM
    pw = wpool.shape[1]
    sw = wssm.shape[1]
    full = lambda a: pl.BlockSpec(a.shape, lambda bi, i: (0,) * a.ndim)
    tok = lambda w: pl.BlockSpec((None, ts, w), lambda bi, i: (bi, i, 0))
    return pl.pallas_call(
        functools.partial(_inproj_kernel, att_w=att_w, n_heads=n_heads),
        grid=(b, s // ts),
        in_specs=[tok(d), full(g), full(wqkv), full(wf), full(wpool), full(wssm),
                  full(fb), full(qg), full(kg), full(seg)],
        out_specs=[tok(att_w), tok(att_w), tok(att_w), tok(n_heads), tok(pw),
                   pl.BlockSpec((ts, sw), lambda bi, i: (i, bi))],
        out_shape=[jax.ShapeDtypeStruct((b, s, att_w), BF16),
                   jax.ShapeDtypeStruct((b, s, att_w), BF16),
                   jax.ShapeDtypeStruct((b, s, att_w), BF16),
                   jax.ShapeDtypeStruct((b, s, n_heads), F32),
                   jax.ShapeDtypeStruct((b, s, pw), F32),
                   jax.ShapeDtypeStruct((s, b * sw), F32)],
        scratch_shapes=[pltpu.VMEM((SUBLANES, LANES), F32)],
        compiler_params=_params(("arbitrary", "arbitrary"), VMEM_LIMIT_BYTES),
        name="inproj",
    )(x, g, wqkv, wf, wpool, wssm, fb, qg, kg, seg)


def _attn_kernel(q_ref, k_ref, v_ref, ct_ref, cr_ref, o_ref, *, blk):
    p = pl.program_id(1)
    i = pl.program_id(2)
    q = q_ref[...]
    lane = lax.broadcasted_iota(I32, (blk, LANES), 1)
    ct = ct_ref[...]
    hl = lax.broadcasted_iota(I32, ct.shape, 1)
    row = lax.broadcasted_iota(I32, (blk, blk), 0)
    col = lax.broadcasted_iota(I32, (blk, blk), 1)
    outs = []
    for hh in range(2):
        head = 2 * p + hh
        in_head = (lane < HEAD_DIM) if hh == 0 else (lane >= HEAD_DIM)
        qm = jnp.where(in_head, q, jnp.zeros_like(q))
        cq = jnp.sum(jnp.where(hl == head, ct, 0.0), axis=1, keepdims=True)

        def scores(kb):
            start = pl.multiple_of(kb * blk, blk)
            kblk = k_ref[pl.ds(start, blk), :]
            s = lax.dot_general(qm, kblk, _NT, preferred_element_type=F32)
            ck = cr_ref[pl.ds(head, 1), pl.ds(start, blk)]
            return (s + cq) - ck, start

        def update(carry, z, start):
            m, l, acc = carry
            m_new = jnp.maximum(m, jnp.max(z, axis=1, keepdims=True))
            alpha = jnp.exp(m - m_new)
            pe = jnp.exp(z - m_new)
            l = alpha * l + jnp.sum(pe, axis=1, keepdims=True)
            pv = jnp.dot(pe.astype(BF16), v_ref[pl.ds(start, blk), :],
                         preferred_element_type=F32)
            return m_new, l, alpha * acc + pv

        def body(kb, carry):
            z, start = scores(kb)
            return update(carry, z, start)

        init = (jnp.full((blk, 1), -jnp.inf, F32), jnp.zeros((blk, 1), F32),
                jnp.zeros((blk, LANES), F32))
        carry = lax.fori_loop(0, i, body, init)
        z, start = scores(i)
        z = jnp.where(col <= row, z, -jnp.inf)
        m, l, acc = update(carry, z, start)
        outs.append(acc / l)
    o_ref[...] = jnp.where(lane < HEAD_DIM, outs[0], outs[1])


def _attention(q, k, v, c_tok, c_row, *, blk):
    b, s, w = q.shape
    n_heads = c_tok.shape[2]
    return pl.pallas_call(
        functools.partial(_attn_kernel, blk=blk),
        grid=(b, w // LANES, s // blk),
        in_specs=[pl.BlockSpec((None, blk, LANES), lambda bi, p, i: (bi, i, p)),
                  pl.BlockSpec((None, s, LANES), lambda bi, p, i: (bi, 0, p)),
                  pl.BlockSpec((None, s, LANES), lambda bi, p, i: (bi, 0, p)),
                  pl.BlockSpec((None, blk, n_heads), lambda bi, p, i: (bi, i, 0)),
                  pl.BlockSpec((None, n_heads, s), lambda bi, p, i: (bi, 0, 0))],
        out_specs=pl.BlockSpec((None, blk, LANES), lambda bi, p, i: (bi, i, p)),
        out_shape=jax.ShapeDtypeStruct((b, s, w), F32),
        compiler_params=_params(("arbitrary", "arbitrary", "arbitrary"), VMEM_LIMIT_BYTES),
        name="fox_attention",
    )(q, k, v, c_tok, c_row)


def _pool_kernel(u_ref, prev_ref, w_ref, sc_ref, o_ref, *, group_dim):
    i = pl.program_id(1)
    cur = u_ref[...]
    ts, pw = cur.shape
    prev = jnp.where(i > 0, prev_ref[...], 0.0)
    ext = jnp.concatenate([prev, cur], axis=0)
    pos = i * ts + lax.broadcasted_iota(I32, (ts, 1), 0)
    grp = lax.broadcasted_iota(I32, (ts, pw), 1) // group_dim
    acc = cur
    pooled = jnp.zeros_like(cur)
    for j in range(1, max(POOL_WINDOWS)):
        acc = acc + ext[POOL_HALO - j:POOL_HALO - j + ts, :]
        win = j + 1
        if win in POOL_WINDOWS:
            cnt = jnp.minimum(pos + 1, win).astype(F32)
            pooled = jnp.where(grp == POOL_WINDOWS.index(win), acc / cnt, pooled)
    pooled = pooled - cur
    o_ref[...] = jnp.dot(pooled.astype(BF16), w_ref[...],
                         preferred_element_type=F32) * sc_ref[...]


def _pool(u, wblk, scale, *, ts):
    b, s, pw = u.shape
    hb = ts // POOL_HALO
    return pl.pallas_call(
        functools.partial(_pool_kernel, group_dim=pw // len(POOL_WINDOWS)),
        grid=(b, s // ts),
        in_specs=[pl.BlockSpec((None, ts, pw), lambda bi, i: (bi, i, 0)),
                  pl.BlockSpec((None, POOL_HALO, pw),
                               lambda bi, i: (bi, jnp.maximum(i * hb - 1, 0), 0)),
                  pl.BlockSpec(wblk.shape, lambda bi, i: (0, 0)),
                  pl.BlockSpec(scale.shape, lambda bi, i: (0, 0))],
        out_specs=pl.BlockSpec((None, ts, pw), lambda bi, i: (bi, i, 0)),
        out_shape=jax.ShapeDtypeStruct((b, s, pw), F32),
        compiler_params=_params(("arbitrary", "arbitrary")),
        name="pool",
    )(u, u, wblk, scale)


def _s5_kernel(u_ref, are_ref, aim_ref, bblk_ref, cblk_ref, d_ref, gw_ref, gb_ref,
               o_ref, st_ref, hre_ref, him_ref, *, nb, n):
    @pl.when(pl.program_id(0) == 0)
    def _():
        hre_ref[...] = jnp.zeros_like(hre_ref)
        him_ref[...] = jnp.zeros_like(him_ref)

    u = u_ref[...]
    tc = u.shape[0] // nb
    st_ref[...] = jnp.dot(u.astype(BF16), bblk_ref[...], preferred_element_type=F32)
    ar = jnp.broadcast_to(are_ref[...], (nb, n))
    ai = jnp.broadcast_to(aim_ref[...], (nb, n))

    def step(t, carry):
        hr, hi = carry
        r0 = pl.multiple_of(t * nb, nb)
        br = st_ref[pl.ds(r0, nb), 0:n]
        bi = st_ref[pl.ds(r0, nb), n:2 * n]
        nr = (ar * hr - ai * hi) + br
        ni = (ar * hi + ai * hr) + bi
        st_ref[pl.ds(r0, nb), 0:n] = nr
        st_ref[pl.ds(r0, nb), n:2 * n] = ni
        return nr, ni

    hr, hi = lax.fori_loop(0, tc, step, (hre_ref[...], him_ref[...]))
    hre_ref[...] = hr
    him_ref[...] = hi
    y = jnp.dot(st_ref[...].astype(BF16), cblk_ref[...], preferred_element_type=F32)
    z = _gelu_tanh(y + d_ref[...] * u)
    gate = jax.nn.sigmoid(jnp.dot(z.astype(BF16), gw_ref[...], preferred_element_type=F32)
                          + gb_ref[...])
    o_ref[...] = z * gate


def _s5(u_tm, a_re, a_im, bblk, cblk, dskip, gw, gb, *, nb, tc):
    rows, sw = u_tm.shape
    n = a_re.shape[1]
    rb = tc * nb
    full = lambda a: pl.BlockSpec(a.shape, lambda i: (0,) * a.ndim)
    return pl.pallas_call(
        functools.partial(_s5_kernel, nb=nb, n=n),
        grid=(rows // rb,),
        in_specs=[pl.BlockSpec((rb, sw), lambda i: (i, 0)), full(a_re), full(a_im),
                  full(bblk), full(cblk), full(dskip), full(gw), full(gb)],
        out_specs=pl.BlockSpec((rb, sw), lambda i: (i, 0)),
        out_shape=jax.ShapeDtypeStruct((rows, sw), F32),
        scratch_shapes=[pltpu.VMEM((rb, 2 * n), F32), pltpu.VMEM((nb, n), F32),
                        pltpu.VMEM((nb, n), F32)],
        compiler_params=_params(("arbitrary",), VMEM_LIMIT_BYTES),
        name="s5_scan",
    )(u_tm, a_re, a_im, bblk, cblk, dskip, gw, gb)


def _merge_kernel(x_ref, att_ref, pool_ref, ssm_ref, og_ref, w_ref, o_ref, *, att_w, pool_w):
    og = og_ref[...]
    p1 = att_w + pool_w
    merged = jnp.concatenate([
        _rmsnorm(att_ref[...], og[:, :att_w]),
        _rmsnorm(pool_ref[...], og[:, att_w:p1]),
        _rmsnorm(ssm_ref[...], og[:, p1:])], axis=1).astype(BF16)
    o_ref[...] = x_ref[...] + jnp.dot(merged, w_ref[...], preferred_element_type=F32)


def _merge(x, att, pool, ssm_tm, og, w, *, ts):
    b, s, d = x.shape
    att_w, pool_w = att.shape[2], pool.shape[2]
    sw = ssm_tm.shape[1] // b
    tok = lambda wd: pl.BlockSpec((None, ts, wd), lambda bi, i: (bi, i, 0))
    return pl.pallas_call(
        functools.partial(_merge_kernel, att_w=att_w, pool_w=pool_w),
        grid=(b, s // ts),
        in_specs=[tok(d), tok(att_w), tok(pool_w),
                  pl.BlockSpec((ts, sw), lambda bi, i: (i, bi)),
                  pl.BlockSpec(og.shape, lambda bi, i: (0, 0)),
                  pl.BlockSpec(w.shape, lambda bi, i: (0, 0))],
        out_specs=tok(d),
        out_shape=jax.ShapeDtypeStruct((b, s, d), F32),
        compiler_params=_params(("arbitrary", "arbitrary"), VMEM_LIMIT_BYTES),
        name="merge_outproj",
    )(x, att, pool, ssm_tm, og, w)


def _topk_rows(a, k, extra=None):
    nrow, nl = a.shape
    row = lax.broadcasted_iota(I32, (nrow, nl), 0).astype(F32)
    out_row = lax.broadcasted_iota(I32, (k, nl), 0)
    vals = jnp.zeros((k, nl), F32)
    sel = jnp.zeros((k, nl), F32)
    for j in range(k):
        m = jnp.max(a, axis=0, keepdims=True)
        ix = jnp.min(jnp.where(a == m, row, float(nrow)), axis=0, keepdims=True)
        hit = row == ix
        if extra is None:
            pick = ix
        else:
            pick = jnp.sum(jnp.where(hit, extra, 0.0), axis=0, keepdims=True)
        vals = jnp.where(out_row == j, m, vals)
        sel = jnp.where(out_row == j, pick, sel)
        a = jnp.where(hit, -jnp.inf, a)
    return vals, sel


def _route_kernel(x_ref, g_ref, wq_ref, keys_ref, h_out, off_out, sh_out, gate_out, q_sc,
                  *, n_half):
    hd = pl.program_id(1)
    qd = 2 * PEER_KEYS

    @pl.when(hd == 0)
    def _():
        h = _rmsnorm(x_ref[...], g_ref[...])
        h_out[...] = h
        q_sc[...] = jnp.dot(h.astype(BF16), wq_ref[...], preferred_element_type=F32)

    q = q_sc[:, pl.ds(pl.multiple_of(hd * qd, qd), qd)].astype(BF16)
    tops = []
    for i in range(2):
        sc = lax.dot_general(keys_ref[i], q[:, i * PEER_KEYS:(i + 1) * PEER_KEYS], _NT,
                             preferred_element_type=F32)
        tops.append(_topk_rows(sc, PEER_TOPK))
    (s1, i1), (s2, i2) = tops
    cand = jnp.concatenate([s1[a:a + 1, :] + s2 for a in range(PEER_TOPK)], axis=0)
    ecand = jnp.concatenate([i1[a:a + 1, :] * float(PEER_KEYS) + i2
                             for a in range(PEER_TOPK)], axis=0)
    best, experts = _topk_rows(cand, PEER_TOPK, extra=ecand)
    w = jnp.exp(best - best[0:1, :])
    gate_out[...] = w / jnp.sum(w, axis=0, keepdims=True)
    e = experts.astype(I32)
    hi_half = e < n_half
    off_out[...] = jnp.where(hi_half, e, e - n_half) * SUBLANES
    sh_out[...] = jnp.where(hi_half, 0, 16)


def _route(x, g, wq, keys, *, tm, n_experts):
    t, d = x.shape
    n_heads = keys.shape[0]
    hk = pl.BlockSpec((None, PEER_TOPK, tm), lambda i, h: (h, 0, i))
    shp = lambda dt: jax.ShapeDtypeStruct((n_heads, PEER_TOPK, t), dt)
    return pl.pallas_call(
        functools.partial(_route_kernel, n_half=n_experts // 2),
        grid=(t // tm, n_heads),
        in_specs=[pl.BlockSpec((tm, d), lambda i, h: (i, 0)),
                  pl.BlockSpec(g.shape, lambda i, h: (0, 0)),
                  pl.BlockSpec(wq.shape, lambda i, h: (0, 0)),
                  pl.BlockSpec((None, 2, PEER_KEYS, keys.shape[3]), lambda i, h: (h, 0, 0, 0))],
        out_specs=[pl.BlockSpec((tm, d), lambda i, h: (i, 0)), hk, hk, hk],
        out_shape=[jax.ShapeDtypeStruct((t, d), F32), shp(I32), shp(I32), shp(F32)],
        scratch_shapes=[pltpu.VMEM((tm, wq.shape[1]), F32)],
        compiler_params=_params(("arbitrary", "arbitrary"), VMEM_LIMIT_BYTES),
        name="peer_route",
    )(x, g, wq, keys)


def _expert_row(tab_ref, off, sh):
    w = tab_ref[pl.ds(pl.multiple_of(off, SUBLANES), SUBLANES), :]
    return lax.bitcast_convert_type((w << sh) & jnp.uint32(0xFFFF0000), F32)


def _fold(a, b, sh, idx, axis):
    n = a.shape[axis]
    m = (idx & sh) == 0
    x = jnp.where(m, a, pltpu.roll(b, sh, axis))
    y = jnp.where(m, pltpu.roll(a, n - sh, axis), b)
    return x + y


def _fold_tree(vs, idx, axis, top):
    sh = top
    while len(vs) > 1:
        half = len(vs) // 2
        vs = [_fold(vs[j], vs[j + half], sh, idx, axis) for j in range(half)]
        sh //= 2
    return vs[0]


def _peer_u_kernel(off_ref, sh_ref, h_ref, gate_ref, tab_ref, coef_ref, part_ref, pre_ref, *, tt, hk):
    sub = lax.broadcasted_iota(I32, (SUBLANES, LANES), 0)
    ngrp = hk // SUBLANES

    def tok(t, carry):
        h = h_ref[t]
        base = pl.multiple_of(t * hk, hk)
        for g in range(ngrp):
            ps = [_expert_row(tab_ref, off_ref[t, g * SUBLANES + j],
                              sh_ref[t, g * SUBLANES + j].astype(U32)) * h
                  for j in range(SUBLANES)]
            part_ref[pl.ds(base + g * SUBLANES, SUBLANES), :] = _fold_tree(ps, sub, 0, SUBLANES // 2)
        return carry

    lax.fori_loop(0, tt, tok, 0)

    def lane_sums(t, carry):
        p = part_ref[pl.ds(pl.multiple_of(t * hk, hk), hk), :]
        pre_ref[pl.ds(t, 1), :] = jnp.sum(p.T, axis=0, keepdims=True)
        return carry

    lax.fori_loop(0, tt, lane_sums, 0, unroll=4)
    coef_ref[...] = gate_ref[...] * _gelu_tanh(pre_ref[...])


def _peer_v_kernel(off_ref, sh_ref, coef_ref, x_ref, tab_ref, o_ref, crep_ref, srep_ref, *, tt, hk, n_acc):
    def replicate(t, carry):
        rows = pl.ds(pl.multiple_of(t * hk, hk), hk)
        crep_ref[rows, :] = jnp.broadcast_to(coef_ref[pl.ds(t, 1), :], (hk, hk)).T
        srep_ref[rows, :] = jnp.broadcast_to(sh_ref[pl.ds(t, 1), :], (hk, hk)).T
        return carry

    lax.fori_loop(0, tt, replicate, 0, unroll=4)

    def tok(t, carry):
        accs = [None] * n_acc
        e0 = t * hk
        for k in range(hk):
            cv = jnp.broadcast_to(crep_ref[pl.ds(e0 + k, 1), :], (SUBLANES, LANES))
            sv = jnp.broadcast_to(srep_ref[pl.ds(e0 + k, 1), :], (SUBLANES, LANES)).astype(U32)
            term = cv * _expert_row(tab_ref, off_ref[e0 + k], sv)
            accs[k % n_acc] = term if accs[k % n_acc] is None else accs[k % n_acc] + term
        while len(accs) > 1:
            accs = [accs[j] + accs[j + len(accs) // 2] for j in range(len(accs) // 2)]
        o_ref[t] = x_ref[t] + accs[0]
        return carry

    lax.fori_loop(0, tt, tok, 0)


def _resident(a):
    return pl.BlockSpec(a.shape, lambda i: (0,) * a.ndim, pipeline_mode=pl.Buffered(1))


def _peer_u(off, sh, h3, gates, tab, *, tt):
    t, hk = off.shape
    smem = pl.BlockSpec((tt, hk), lambda i: (i, 0), memory_space=pltpu.SMEM)
    return pl.pallas_call(
        functools.partial(_peer_u_kernel, tt=tt, hk=hk),
        grid=(t // tt,),
        in_specs=[smem, smem,
                  pl.BlockSpec((tt, SUBLANES, LANES), lambda i: (i, 0, 0)),
                  pl.BlockSpec((tt, hk), lambda i: (i, 0)),
                  _resident(tab)],
        out_specs=pl.BlockSpec((tt, hk), lambda i: (i, 0)),
        out_shape=jax.ShapeDtypeStruct((t, hk), F32),
        scratch_shapes=[pltpu.VMEM((tt * hk, LANES), F32), pltpu.VMEM((tt, hk), F32)],
        compiler_params=_params(("arbitrary",), VMEM_LIMIT_BYTES),
        name="peer_u",
    )(off, sh, h3, gates, tab)


def _peer_v(off, sh, coef, x3, tab, *, tt):
    t, hk = off.shape
    smem = pl.BlockSpec((tt * hk,), lambda i: (i,), memory_space=pltpu.SMEM)
    tile = pl.BlockSpec((tt, SUBLANES, LANES), lambda i: (i, 0, 0))
    compact = pl.BlockSpec((tt, hk), lambda i: (i, 0))
    assert hk == LANES
    return pl.pallas_call(
        functools.partial(_peer_v_kernel, tt=tt, hk=hk, n_acc=4),
        grid=(t // tt,),
        in_specs=[smem, compact, compact, tile, _resident(tab)],
        out_specs=tile,
        out_shape=jax.ShapeDtypeStruct(x3.shape, F32),
        scratch_shapes=[pltpu.VMEM((tt * hk, LANES), F32), pltpu.VMEM((tt * hk, LANES), I32)],
        compiler_params=_params(("arbitrary",), VMEM_LIMIT_BYTES),
        name="peer_v",
    )(off.reshape(-1), sh, coef, x3, tab)


def _pack_table(tab):
    n, d = tab.shape
    b = lax.bitcast_convert_type(tab.astype(BF16), jnp.uint16).astype(U32)
    w = (b[:n // 2] << 16) | b[n // 2:]
    return w.reshape(n // 2 * (d // LANES), LANES)


def _block_diag(blocks):
    g, r, c = blocks.shape
    eye = jnp.eye(g, dtype=blocks.dtype)
    return (eye[:, None, :, None] * blocks[:, :, None, :]).reshape(g * r, g * c)


def _s5_params(a_re, a_im, log_dt, b_re, b_im, c_re, c_im):
    a = lax.complex(a_re, a_im)
    dt = jnp.exp(log_dt)[:, None]
    a_bar = jnp.exp(a * dt)
    b_bar = ((a_bar - 1.0) / a)[..., None] * lax.complex(b_re, b_im)
    bt = jnp.swapaxes(b_bar, 1, 2)
    bblk = jnp.concatenate([_block_diag(jnp.real(bt)), _block_diag(jnp.imag(bt))], axis=1)
    ct_re = jnp.swapaxes(c_re, 1, 2)
    ct_im = jnp.swapaxes(c_im, 1, 2)
    cblk = jnp.concatenate([_block_diag(ct_re), -_block_diag(ct_im)], axis=0)
    n = a_bar.size
    return (jnp.real(a_bar).reshape(1, n), jnp.imag(a_bar).reshape(1, n),
            bblk.astype(BF16), cblk.astype(BF16))


def _tiles(s, t):
    ts = min(512, s)
    blk = min(256, s)
    tc = min(64, s)
    tm = min(256, t)
    tt = min(64, t)
    return ts, blk, tc, tm, tt


def _layer(x, norm1_g, w_in, f_bias, q_gain, k_gain, pool_w, pool_scale,
           a_re, a_im, log_dt, b_re, b_im, c_re, c_im, ssm_d, glu_w, glu_b,
           out_norm_g, w_out, norm2_g, peer_wq, peer_keys, peer_u, peer_v):
    b, s, d = x.shape
    t = b * s
    n_heads = f_bias.shape[0]
    att_w = n_heads * HEAD_DIM
    pool_width = pool_scale.shape[0]
    ssm_w = ssm_d.shape[0]
    ts, blk, tc, tm, tt = _tiles(s, t)
    row = lambda v: v.reshape(1, -1)

    c0, c1 = 3 * att_w, 3 * att_w + n_heads
    wf = jnp.zeros((d, LANES), F32).at[:, :n_heads].set(w_in[:, c0:c1])
    fb = jnp.zeros((1, LANES), F32).at[0, :n_heads].set(f_bias)
    seg = _block_diag(jnp.full((n_heads, HEAD_DIM, HEAD_DIM), 1.0 / HEAD_DIM, F32)).astype(BF16)
    q, k, v, c_tok, u_pool, u_ssm = _inproj(
        x, row(norm1_g), w_in[:, :c0].astype(BF16), wf,
        w_in[:, c1:c1 + pool_width].astype(BF16), w_in[:, c1 + pool_width:].astype(BF16), fb,
        row(jnp.tile(q_gain, n_heads) * HEAD_DIM ** -0.5), row(jnp.tile(k_gain, n_heads)),
        seg, ts=ts)
    att = _attention(q, k, v, c_tok, jnp.swapaxes(c_tok, 1, 2), blk=blk)
    pool = _pool(u_pool, _block_diag(pool_w).astype(BF16), row(pool_scale), ts=ts)
    are, aim, bblk, cblk = _s5_params(a_re, a_im, log_dt, b_re, b_im, c_re, c_im)
    ssm = _s5(u_ssm.reshape(s * b, ssm_w), are, aim, bblk, cblk, row(ssm_d),
              glu_w.astype(BF16), row(glu_b), nb=b, tc=tc)
    x1 = _merge(x, att, pool, ssm.reshape(s, b * ssm_w), row(out_norm_g),
                w_out.astype(BF16), ts=ts)

    n_experts = peer_u.shape[0]
    hk = peer_keys.shape[0] * PEER_TOPK
    kh = peer_keys.astype(BF16)
    h2, off, sh, gates = _route(x1.reshape(t, d), row(norm2_g), peer_wq.astype(BF16), kh,
                                tm=tm, n_experts=n_experts)
    tok_major = lambda a: a.reshape(hk, t).T
    off, sh, gates = tok_major(off), tok_major(sh), tok_major(gates)
    coef = _peer_u(off, sh, h2.reshape(t, SUBLANES, d // SUBLANES), gates,
                   _pack_table(peer_u), tt=tt)
    x2 = _peer_v(off, sh, coef, x1.reshape(t, SUBLANES, d // SUBLANES),
                 _pack_table(peer_v), tt=tt)
    return x2.reshape(b, s, d)


def kernel(x, norm1_g, w_in, f_bias, q_gain, k_gain, pool_w, pool_scale, ssm_a_re, ssm_a_im,
           ssm_log_dt, ssm_b_re, ssm_b_im, ssm_c_re, ssm_c_im, ssm_d, glu_w, glu_b, out_norm_g,
           w_out, norm2_g, peer_wq, peer_keys, peer_u, peer_v):
    per_layer = (norm1_g, w_in, f_bias, q_gain, k_gain, pool_w, pool_scale, ssm_a_re, ssm_a_im,
                 ssm_log_dt, ssm_b_re, ssm_b_im, ssm_c_re, ssm_c_im, ssm_d, glu_w, glu_b,
                 out_norm_g, w_out, norm2_g, peer_wq, peer_keys, peer_u, peer_v)
    for l in range(norm1_g.shape[0]):
        x = _layer(x, *[p[l] for p in per_layer])
    return x
```

```python
import functools
import math

import jax
import jax.numpy as jnp
from jax import lax
from jax.experimental import pallas as pl
from jax.experimental.pallas import tpu as pltpu

F32 = jnp.float32
BF16 = jnp.bfloat16
U32 = jnp.uint32
I32 = jnp.int32

EPS = 1e-6
HEAD_DIM = 64
POOL_WINDOWS = (2, 4, 8, 16)
POOL_HALO = 16
SSM_STATE = 64
SSM_GROUP_CH = 16
PEER_KEYS = 128
PEER_TOPK = 16

LANES = 128
SUBLANES = 8
VMEM_LIMIT_BYTES = 56 * 1024 * 1024

_NT = (((1,), (1,)), ((), ()))


def _rmsnorm(x, g):
    return x * lax.rsqrt(jnp.mean(x * x, axis=-1, keepdims=True) + EPS) * g


def _gelu_tanh(x):
    c = math.sqrt(2.0 / math.pi)
    return x * (0.5 * (1.0 + jnp.tanh(c * (x + 0.044715 * (x * x * x)))))


def _params(sem, vmem=None):
    return pltpu.CompilerParams(dimension_semantics=sem, vmem_limit_bytes=vmem)


def _inproj_kernel(x_ref, g_ref, wqkv_ref, wf_ref, wpool_ref, wssm_ref, fb_ref,
                   qg_ref, kg_ref, seg_ref,
                   q_out, k_out, v_out, c_out, pool_out, ssm_out, carry_ref,
                   *, att_w, n_heads):
    i = pl.program_id(1)

    @pl.when(i == 0)
    def _():
        carry_ref[...] = jnp.zeros_like(carry_ref)

    x = x_ref[...]
    ts = x.shape[0]
    h = _rmsnorm(x, g_ref[...])
    hb = h.astype(BF16)
    qkv = jnp.dot(hb, wqkv_ref[...], preferred_element_type=F32)
    seg = seg_ref[...]

    def headnorm(t, gain):
        ms = jnp.dot((t * t).astype(BF16), seg, preferred_element_type=F32)
        return t * lax.rsqrt(ms + EPS) * gain

    q_out[...] = headnorm(qkv[:, :att_w], qg_ref[...]).astype(BF16)
    k_out[...] = headnorm(qkv[:, att_w:2 * att_w], kg_ref[...]).astype(BF16)
    v_out[...] = qkv[:, 2 * att_w:].astype(BF16)
    pool_out[...] = jnp.dot(hb, wpool_ref[...], preferred_element_type=F32)
    ssm_out[...] = jnp.dot(hb, wssm_ref[...], preferred_element_type=F32)

    f = jnp.dot(h, wf_ref[...], precision=lax.Precision.HIGHEST,
                preferred_element_type=F32) + fb_ref[...]
    ls = jnp.minimum(f, 0.0) - jnp.log1p(jnp.exp(-jnp.abs(f)))
    r = lax.broadcasted_iota(I32, (ts, ts), 0)
    c = lax.broadcasted_iota(I32, (ts, ts), 1)
    tri = jnp.where(c <= r, 1.0, 0.0).astype(F32)
    cs = jnp.dot(tri, ls, precision=lax.Precision.HIGHEST,
                 preferred_element_type=F32) + carry_ref[0:1, :]
    carry_ref[...] = jnp.broadcast_to(cs[ts - 1:ts, :], carry_ref.shape)
    c_out[...] = cs[:, :n_heads]


def _inproj(x, g, wqkv, wf, wpool, wssm, fb, qg, kg, seg, *, ts):
    b, s, d = x.shape
    att_w = seg.shape[0]
    n_heads = att_w // HEAD_DIM
    pw = wpool.shape[1]
    sw = wssm.shape[1]
    full = lambda a: pl.BlockSpec(a.shape, lambda bi, i: (0,) * a.ndim)
    tok = lambda w: pl.BlockSpec((None, ts, w), lambda bi, i: (bi, i, 0))
    return pl.pallas_call(
        functools.partial(_inproj_kernel, att_w=att_w, n_heads=n_heads),
        grid=(b, s // ts),
        in_specs=[tok(d), full(g), full(wqkv), full(wf), full(wpool), full(wssm),
                  full(fb), full(qg), full(kg), full(seg)],
        out_specs=[tok(att_w), tok(att_w), tok(att_w), tok(n_heads), tok(pw),
                   pl.BlockSpec((ts, sw), lambda bi, i: (i, bi))],
        out_shape=[jax.ShapeDtypeStruct((b, s, att_w), BF16),
                   jax.ShapeDtypeStruct((b, s, att_w), BF16),
                   jax.ShapeDtypeStruct((b, s, att_w), BF16),
                   jax.ShapeDtypeStruct((b, s, n_heads), F32),
                   jax.ShapeDtypeStruct((b, s, pw), F32),
                   jax.ShapeDtypeStruct((s, b * sw), F32)],
        scratch_shapes=[pltpu.VMEM((SUBLANES, LANES), F32)],
        compiler_params=_params(("arbitrary", "arbitrary"), VMEM_LIMIT_BYTES),
        name="inproj",
    )(x, g, wqkv, wf, wpool, wssm, fb, qg, kg, seg)


def _attn_kernel(q_ref, k_ref, v_ref, ct_ref, cr_ref, o_ref, m_sc, l_sc, acc_sc, *, blk):
    p = pl.program_id(1)
    i = pl.program_id(2)
    q = q_ref[...]
    lane = lax.broadcasted_iota(I32, (blk, LANES), 1)
    ct = ct_ref[...]
    hl = lax.broadcasted_iota(I32, ct.shape, 1)
    heads = (2 * p, 2 * p + 1)
    qm = (jnp.where(lane < HEAD_DIM, q, jnp.zeros_like(q)),
          jnp.where(lane >= HEAD_DIM, q, jnp.zeros_like(q)))
    cq = tuple(jnp.sum(jnp.where(hl == hd, ct, 0.0), axis=1, keepdims=True) for hd in heads)
    m_sc[...] = jnp.full(m_sc.shape, -jnp.inf, F32)
    l_sc[...] = jnp.zeros(l_sc.shape, F32)
    acc_sc[...] = jnp.zeros(acc_sc.shape, F32)

    def step(kb, causal):
        start = pl.multiple_of(kb * blk, blk)
        kblk = k_ref[pl.ds(start, blk), :]
        vblk = v_ref[pl.ds(start, blk), :]
        for hh in range(2):
            s = lax.dot_general(qm[hh], kblk, _NT, preferred_element_type=F32)
            ck = cr_ref[pl.ds(heads[hh], 1), pl.ds(start, blk)]
            z = (s + cq[hh]) - ck
            if causal:
                row = lax.broadcasted_iota(I32, (blk, blk), 0)
                col = lax.broadcasted_iota(I32, (blk, blk), 1)
                z = jnp.where(col <= row, z, -jnp.inf)
            m_old = m_sc[hh]
            m_new = jnp.maximum(m_old, jnp.max(z, axis=1, keepdims=True))
            alpha = jnp.exp(m_old - m_new)
            pe = jnp.exp(z - m_new)
            l_sc[hh] = alpha * l_sc[hh] + jnp.sum(pe, axis=1, keepdims=True)
            acc_sc[hh] = alpha * acc_sc[hh] + jnp.dot(pe.astype(BF16), vblk,
                                                      preferred_element_type=F32)
            m_sc[hh] = m_new

    def body(kb, carry):
        step(kb, False)
        return carry

    lax.fori_loop(0, i, body, 0)
    step(i, True)
    o_ref[...] = jnp.where(lane < HEAD_DIM, acc_sc[0] / l_sc[0], acc_sc[1] / l_sc[1])


def _attention(q, k, v, c_tok, c_row, *, blk):
    b, s, w = q.shape
    n_heads = c_tok.shape[2]
    return pl.pallas_call(
        functools.partial(_attn_kernel, blk=blk),
        grid=(b, w // LANES, s // blk),
        in_specs=[pl.BlockSpec((None, blk, LANES), lambda bi, p, i: (bi, i, p)),
                  pl.BlockSpec((None, s, LANES), lambda bi, p, i: (bi, 0, p)),
                  pl.BlockSpec((None, s, LANES), lambda bi, p, i: (bi, 0, p)),
                  pl.BlockSpec((None, blk, n_heads), lambda bi, p, i: (bi, i, 0)),
                  pl.BlockSpec((None, n_heads, s), lambda bi, p, i: (bi, 0, 0))],
        out_specs=pl.BlockSpec((None, blk, LANES), lambda bi, p, i: (bi, i, p)),
        out_shape=jax.ShapeDtypeStruct((b, s, w), F32),
        scratch_shapes=[pltpu.VMEM((2, blk, 1), F32), pltpu.VMEM((2, blk, 1), F32),
                        pltpu.VMEM((2, blk, LANES), F32)],
        compiler_params=_params(("arbitrary", "arbitrary", "arbitrary"), VMEM_LIMIT_BYTES),
        name="fox_attention",
    )(q, k, v, c_tok, c_row)


def _pool_kernel(u_ref, prev_ref, w_ref, sc_ref, o_ref, *, group_dim):
    i = pl.program_id(1)
    cur = u_ref[...]
    ts, pw = cur.shape
    prev = jnp.where(i > 0, prev_ref[...], 0.0)
    ext = jnp.concatenate([prev, cur], axis=0)
    pos = i * ts + lax.broadcasted_iota(I32, (ts, 1), 0)
    grp = lax.broadcasted_iota(I32, (ts, pw), 1) // group_dim
    acc = cur
    pooled = jnp.zeros_like(cur)
    for j in range(1, max(POOL_WINDOWS)):
        acc = acc + ext[POOL_HALO - j:POOL_HALO - j + ts, :]
        win = j + 1
        if win in POOL_WINDOWS:
            cnt = jnp.minimum(pos + 1, win).astype(F32)
            pooled = jnp.where(grp == POOL_WINDOWS.index(win), acc / cnt, pooled)
    pooled = pooled - cur
    o_ref[...] = jnp.dot(pooled.astype(BF16), w_ref[...],
                         preferred_element_type=F32) * sc_ref[...]


def _pool(u, wblk, scale, *, ts):
    b, s, pw = u.shape
    hb = ts // POOL_HALO
    return pl.pallas_call(
        functools.partial(_pool_kernel, group_dim=pw // len(POOL_WINDOWS)),
        grid=(b, s // ts),
        in_specs=[pl.BlockSpec((None, ts, pw), lambda bi, i: (bi, i, 0)),
                  pl.BlockSpec((None, POOL_HALO, pw),
                               lambda bi, i: (bi, jnp.maximum(i * hb - 1, 0), 0)),
                  pl.BlockSpec(wblk.shape, lambda bi, i: (0, 0)),
                  pl.BlockSpec(scale.shape, lambda bi, i: (0, 0))],
        out_specs=pl.BlockSpec((None, ts, pw), lambda bi, i: (bi, i, 0)),
        out_shape=jax.ShapeDtypeStruct((b, s, pw), F32),
        compiler_params=_params(("arbitrary", "arbitrary")),
        name="pool",
    )(u, u, wblk, scale)


def _s5_kernel(u_ref, are_ref, aim_ref, bblk_ref, cblk_ref, d_ref, gw_ref, gb_ref,
               o_ref, st_ref, hre_ref, him_ref, *, nb, n):
    @pl.when(pl.program_id(0) == 0)
    def _():
        hre_ref[...] = jnp.zeros_like(hre_ref)
        him_ref[...] = jnp.zeros_like(him_ref)

    u = u_ref[...]
    tc = u.shape[0] // nb
    st_ref[...] = jnp.dot(u.astype(BF16), bblk_ref[...], preferred_element_type=F32)
    ar = jnp.broadcast_to(are_ref[...], (nb, n))
    ai = jnp.broadcast_to(aim_ref[...], (nb, n))

    def step(t, carry):
        hr, hi = carry
        r0 = pl.multiple_of(t * nb, nb)
        br = st_ref[pl.ds(r0, nb), 0:n]
        bi = st_ref[pl.ds(r0, nb), n:2 * n]
        nr = (ar * hr - ai * hi) + br
        ni = (ar * hi + ai * hr) + bi
        st_ref[pl.ds(r0, nb), 0:n] = nr
        st_ref[pl.ds(r0, nb), n:2 * n] = ni
        return nr, ni

    hr, hi = lax.fori_loop(0, tc, step, (hre_ref[...], him_ref[...]))
    hre_ref[...] = hr
    him_ref[...] = hi
    y = jnp.dot(st_ref[...].astype(BF16), cblk_ref[...], preferred_element_type=F32)
    z = _gelu_tanh(y + d_ref[...] * u)
    gate = jax.nn.sigmoid(jnp.dot(z.astype(BF16), gw_ref[...], preferred_element_type=F32)
                          + gb_ref[...])
    o_ref[...] = z * gate


def _s5(u_tm, a_re, a_im, bblk, cblk, dskip, gw, gb, *, nb, tc):
    rows, sw = u_tm.shape
    n = a_re.shape[1]
    rb = tc * nb
    full = lambda a: pl.BlockSpec(a.shape, lambda i: (0,) * a.ndim)
    return pl.pallas_call(
        functools.partial(_s5_kernel, nb=nb, n=n),
        grid=(rows // rb,),
        in_specs=[pl.BlockSpec((rb, sw), lambda i: (i, 0)), full(a_re), full(a_im),
                  full(bblk), full(cblk), full(dskip), full(gw), full(gb)],
        out_specs=pl.BlockSpec((rb, sw), lambda i: (i, 0)),
        out_shape=jax.ShapeDtypeStruct((rows, sw), F32),
        scratch_shapes=[pltpu.VMEM((rb, 2 * n), F32), pltpu.VMEM((nb, n), F32),
                        pltpu.VMEM((nb, n), F32)],
        compiler_params=_params(("arbitrary",), VMEM_LIMIT_BYTES),
        name="s5_scan",
    )(u_tm, a_re, a_im, bblk, cblk, dskip, gw, gb)


def _merge_kernel(x_ref, att_ref, pool_ref, ssm_ref, og_ref, w_ref, o_ref, *, att_w, pool_w):
    og = og_ref[...]
    p1 = att_w + pool_w
    merged = jnp.concatenate([
        _rmsnorm(att_ref[...], og[:, :att_w]),
        _rmsnorm(pool_ref[...], og[:, att_w:p1]),
        _rmsnorm(ssm_ref[...], og[:, p1:])], axis=1).astype(BF16)
    o_ref[...] = x_ref[...] + jnp.dot(merged, w_ref[...], preferred_element_type=F32)


def _merge(x, att, pool, ssm_tm, og, w, *, ts):
    b, s, d = x.shape
    att_w, pool_w = att.shape[2], pool.shape[2]
    sw = ssm_tm.shape[1] // b
    tok = lambda wd: pl.BlockSpec((None, ts, wd), lambda bi, i: (bi, i, 0))
    return pl.pallas_call(
        functools.partial(_merge_kernel, att_w=att_w, pool_w=pool_w),
        grid=(b, s // ts),
        in_specs=[tok(d), tok(att_w), tok(pool_w),
                  pl.BlockSpec((ts, sw), lambda bi, i: (i, bi)),
                  pl.BlockSpec(og.shape, lambda bi, i: (0, 0)),
                  pl.BlockSpec(w.shape, lambda bi, i: (0, 0))],
        out_specs=tok(d),
        out_shape=jax.ShapeDtypeStruct((b, s, d), F32),
        compiler_params=_params(("arbitrary", "arbitrary"), VMEM_LIMIT_BYTES),
        name="merge_outproj",
    )(x, att, pool, ssm_tm, og, w)


def _topk_rows(a, k, extra=None):
    nrow, nl = a.shape
    row = lax.broadcasted_iota(I32, (nrow, nl), 0).astype(F32)
    out_row = lax.broadcasted_iota(I32, (k, nl), 0)
    vals = jnp.zeros((k, nl), F32)
    sel = jnp.zeros((k, nl), F32)
    for j in range(k):
        m = jnp.max(a, axis=0, keepdims=True)
        ix = jnp.min(jnp.where(a == m, row, float(nrow)), axis=0, keepdims=True)
        hit = row == ix
        if extra is None:
            pick = ix
        else:
            pick = jnp.sum(jnp.where(hit, extra, 0.0), axis=0, keepdims=True)
        vals = jnp.where(out_row == j, m, vals)
        sel = jnp.where(out_row == j, pick, sel)
        a = jnp.where(hit, -jnp.inf, a)
    return vals, sel


def _route_kernel(x_ref, g_ref, wq_ref, keys_ref, h_out, off_out, sh_out, gate_out, q_sc,
                  *, n_half):
    hd = pl.program_id(1)
    qd = 2 * PEER_KEYS

    @pl.when(hd == 0)
    def _():
        h = _rmsnorm(x_ref[...], g_ref[...])
        h_out[...] = h
        q_sc[...] = jnp.dot(h.astype(BF16), wq_ref[...], preferred_element_type=F32)

    q = q_sc[:, pl.ds(pl.multiple_of(hd * qd, qd), qd)].astype(BF16)
    tops = []
    for i in range(2):
        sc = lax.dot_general(keys_ref[i], q[:, i * PEER_KEYS:(i + 1) * PEER_KEYS], _NT,
                             preferred_element_type=F32)
        tops.append(_topk_rows(sc, PEER_TOPK))
    (s1, i1), (s2, i2) = tops
    k = PEER_TOPK
    sub8 = lax.broadcasted_iota(I32, (SUBLANES, s1.shape[1]), 0)
    cands, ecands = [], []
    a = 0
    while a < k and k // (a + 1) > 1:
        nb = k // (a + 1)
        for b0 in range(0, nb, SUBLANES):
            c = s1[a:a + 1, :] + s2[b0:b0 + SUBLANES, :]
            if b0 + SUBLANES > nb:
                c = jnp.where(sub8 < nb - b0, c, -jnp.inf)
            cands.append(c)
            ecands.append(i1[a:a + 1, :] * float(PEER_KEYS) + i2[b0:b0 + SUBLANES, :])
        a += 1
    cands.append(s1[a:, :] + s2[0:1, :])
    ecands.append(i1[a:, :] * float(PEER_KEYS) + i2[0:1, :])
    best, experts = _topk_rows(jnp.concatenate(cands, axis=0), k,
                               extra=jnp.concatenate(ecands, axis=0))
    w = jnp.exp(best - best[0:1, :])
    gate_out[...] = w / jnp.sum(w, axis=0, keepdims=True)
    e = experts.astype(I32)
    hi_half = e < n_half
    off_out[...] = jnp.where(hi_half, e, e - n_half) * SUBLANES
    sh_out[...] = jnp.where(hi_half, 0, 16)


def _route(x, g, wq, keys, *, tm, n_experts):
    t, d = x.shape
    n_heads = keys.shape[0]
    hk = pl.BlockSpec((None, PEER_TOPK, tm), lambda i, h: (h, 0, i))
    shp = lambda dt: jax.ShapeDtypeStruct((n_heads, PEER_TOPK, t), dt)
    return pl.pallas_call(
        functools.partial(_route_kernel, n_half=n_experts // 2),
        grid=(t // tm, n_heads),
        in_specs=[pl.BlockSpec((tm, d), lambda i, h: (i, 0)),
                  pl.BlockSpec(g.shape, lambda i, h: (0, 0)),
                  pl.BlockSpec(wq.shape, lambda i, h: (0, 0)),
                  pl.BlockSpec((None, 2, PEER_KEYS, keys.shape[3]), lambda i, h: (h, 0, 0, 0))],
        out_specs=[pl.BlockSpec((tm, d), lambda i, h: (i, 0)), hk, hk, hk],
        out_shape=[jax.ShapeDtypeStruct((t, d), F32), shp(I32), shp(I32), shp(F32)],
        scratch_shapes=[pltpu.VMEM((tm, wq.shape[1]), F32)],
        compiler_params=_params(("arbitrary", "arbitrary"), VMEM_LIMIT_BYTES),
        name="peer_route",
    )(x, g, wq, keys)


def _expert_row(tab_ref, off, sh):
    w = tab_ref[pl.ds(pl.multiple_of(off, SUBLANES), SUBLANES), :]
    return lax.bitcast_convert_type((w << sh) & jnp.uint32(0xFFFF0000), F32)


def _fold_rows(vs, sub, rot=0):
    m = len(vs)
    if m == 1:
        return vs[0]
    span = SUBLANES // m
    c1 = _fold_rows(vs[0::2], sub, rot - span)
    c2 = _fold_rows(vs[1::2], sub, rot)
    own = ((sub - rot) & (2 * span - 1)) < span
    return jnp.where(own, c1, c2) + pltpu.roll(jnp.where(own, c2, c1), span, 0)


def _peer_u_kernel(off_ref, sh_ref, h_ref, gate_ref, tab_ref, coef_ref, part_ref, pre_ref, *, tt, hk):
    sub = lax.broadcasted_iota(I32, (SUBLANES, LANES), 0)
    ngrp = hk // SUBLANES

    def tok(t, carry):
        h = h_ref[t]
        base = pl.multiple_of(t * hk, hk)
        for g in range(ngrp):
            ps = [_expert_row(tab_ref, off_ref[t, g * SUBLANES + j],
                              sh_ref[t, g * SUBLANES + j].astype(U32)) * h
                  for j in range(SUBLANES)]
            part_ref[pl.ds(base + g * SUBLANES, SUBLANES), :] = _fold_rows(ps, sub)
        return carry

    lax.fori_loop(0, tt, tok, 0)

    def lane_sums(t, carry):
        p = part_ref[pl.ds(pl.multiple_of(t * hk, hk), hk), :]
        pre_ref[pl.ds(t, 1), :] = jnp.sum(p.T, axis=0, keepdims=True)
        return carry

    lax.fori_loop(0, tt, lane_sums, 0, unroll=4)
    coef_ref[...] = gate_ref[...] * _gelu_tanh(pre_ref[...])


def _peer_v_kernel(off_ref, sh_ref, coef_ref, x_ref, tab_ref, o_ref, crep_ref, srep_ref, *, tt, hk, n_acc):
    def replicate(t, carry):
        rows = pl.ds(pl.multiple_of(t * hk, hk), hk)
        crep_ref[rows, :] = jnp.broadcast_to(coef_ref[pl.ds(t, 1), :], (hk, hk)).T
        srep_ref[rows, :] = jnp.broadcast_to(sh_ref[pl.ds(t, 1), :], (hk, hk)).T
        return carry

    lax.fori_loop(0, tt, replicate, 0, unroll=4)

    def tok(t, carry):
        accs = [None] * n_acc
        e0 = t * hk
        for k in range(hk):
            cv = jnp.broadcast_to(crep_ref[pl.ds(e0 + k, 1), :], (SUBLANES, LANES))
            sv = jnp.broadcast_to(srep_ref[pl.ds(e0 + k, 1), :], (SUBLANES, LANES)).astype(U32)
            term = cv * _expert_row(tab_ref, off_ref[e0 + k], sv)
            accs[k % n_acc] = term if accs[k % n_acc] is None else accs[k % n_acc] + term
        while len(accs) > 1:
            accs = [accs[j] + accs[j + len(accs) // 2] for j in range(len(accs) // 2)]
        o_ref[t] = x_ref[t] + accs[0]
        return carry

    lax.fori_loop(0, tt, tok, 0)


def _resident(a):
    return pl.BlockSpec(a.shape, lambda i: (0,) * a.ndim, pipeline_mode=pl.Buffered(1))


def _peer_u(off, sh, h3, gates, tab, *, tt):
    t, hk = off.shape
    smem = pl.BlockSpec((tt, hk), lambda i: (i, 0), memory_space=pltpu.SMEM)
    return pl.pallas_call(
        functools.partial(_peer_u_kernel, tt=tt, hk=hk),
        grid=(t // tt,),
        in_specs=[smem, smem,
                  pl.BlockSpec((tt, SUBLANES, LANES), lambda i: (i, 0, 0)),
                  pl.BlockSpec((tt, hk), lambda i: (i, 0)),
                  _resident(tab)],
        out_specs=pl.BlockSpec((tt, hk), lambda i: (i, 0)),
        out_shape=jax.ShapeDtypeStruct((t, hk), F32),
        scratch_shapes=[pltpu.VMEM((tt * hk, LANES), F32), pltpu.VMEM((tt, hk), F32)],
        compiler_params=_params(("arbitrary",), VMEM_LIMIT_BYTES),
        name="peer_u",
    )(off, sh, h3, gates, tab)


def _peer_v(off, sh, coef, x3, tab, *, tt):
    t, hk = off.shape
    smem = pl.BlockSpec((tt * hk,), lambda i: (i,), memory_space=pltpu.SMEM)
    tile = pl.BlockSpec((tt, SUBLANES, LANES), lambda i: (i, 0, 0))
    compact = pl.BlockSpec((tt, hk), lambda i: (i, 0))
    assert hk == LANES
    return pl.pallas_call(
        functools.partial(_peer_v_kernel, tt=tt, hk=hk, n_acc=4),
        grid=(t // tt,),
        in_specs=[smem, compact, compact, tile, _resident(tab)],
        out_specs=tile,
        out_shape=jax.ShapeDtypeStruct(x3.shape, F32),
        scratch_shapes=[pltpu.VMEM((tt * hk, LANES), F32), pltpu.VMEM((tt * hk, LANES), I32)],
        compiler_params=_params(("arbitrary",), VMEM_LIMIT_BYTES),
        name="peer_v",
    )(off.reshape(-1), sh, coef, x3, tab)


def _pack_table(tab):
    n, d = tab.shape
    b = lax.bitcast_convert_type(tab.astype(BF16), jnp.uint16).astype(U32)
    w = (b[:n // 2] << 16) | b[n // 2:]
    return w.reshape(n // 2 * (d // LANES), LANES)


def _block_diag(blocks):
    g, r, c = blocks.shape
    eye = jnp.eye(g, dtype=blocks.dtype)
    return (eye[:, None, :, None] * blocks[:, :, None, :]).reshape(g * r, g * c)


def _s5_params(a_re, a_im, log_dt, b_re, b_im, c_re, c_im):
    a = lax.complex(a_re, a_im)
    dt = jnp.exp(log_dt)[:, None]
    a_bar = jnp.exp(a * dt)
    b_bar = ((a_bar - 1.0) / a)[..., None] * lax.complex(b_re, b_im)
    bt = jnp.swapaxes(b_bar, 1, 2)
    bblk = jnp.concatenate([_block_diag(jnp.real(bt)), _block_diag(jnp.imag(bt))], axis=1)
    ct_re = jnp.swapaxes(c_re, 1, 2)
    ct_im = jnp.swapaxes(c_im, 1, 2)
    cblk = jnp.concatenate([_block_diag(ct_re), -_block_diag(ct_im)], axis=0)
    n = a_bar.size
    return (jnp.real(a_bar).reshape(1, n), jnp.imag(a_bar).reshape(1, n),
            bblk.astype(BF16), cblk.astype(BF16))


def _tiles(s, t):
    ts = min(512, s)
    blk = min(512, s)
    tc = min(64, s)
    tm = min(256, t)
    tt = min(64, t)
    return ts, blk, tc, tm, tt


def _layer(x, norm1_g, w_in, f_bias, q_gain, k_gain, pool_w, pool_scale,
           a_re, a_im, log_dt, b_re, b_im, c_re, c_im, ssm_d, glu_w, glu_b,
           out_norm_g, w_out, norm2_g, peer_wq, peer_keys, peer_u, peer_v):
    b, s, d = x.shape
    t = b * s
    n_heads = f_bias.shape[0]
    att_w = n_heads * HEAD_DIM
    pool_width = pool_scale.shape[0]
    ssm_w = ssm_d.shape[0]
    ts, blk, tc, tm, tt = _tiles(s, t)
    row = lambda v: v.reshape(1, -1)

    c0, c1 = 3 * att_w, 3 * att_w + n_heads
    wf = jnp.zeros((d, LANES), F32).at[:, :n_heads].set(w_in[:, c0:c1])
    fb = jnp.zeros((1, LANES), F32).at[0, :n_heads].set(f_bias)
    seg = _block_diag(jnp.full((n_heads, HEAD_DIM, HEAD_DIM), 1.0 / HEAD_DIM, F32)).astype(BF16)
    q, k, v, c_tok, u_pool, u_ssm = _inproj(
        x, row(norm1_g), w_in[:, :c0].astype(BF16), wf,
        w_in[:, c1:c1 + pool_width].astype(BF16), w_in[:, c1 + pool_width:].astype(BF16), fb,
        row(jnp.tile(q_gain, n_heads) * HEAD_DIM ** -0.5), row(jnp.tile(k_gain, n_heads)),
        seg, ts=ts)
    att = _attention(q, k, v, c_tok, jnp.swapaxes(c_tok, 1, 2), blk=blk)
    pool = _pool(u_pool, _block_diag(pool_w).astype(BF16), row(pool_scale), ts=ts)
    are, aim, bblk, cblk = _s5_params(a_re, a_im, log_dt, b_re, b_im, c_re, c_im)
    ssm = _s5(u_ssm.reshape(s * b, ssm_w), are, aim, bblk, cblk, row(ssm_d),
              glu_w.astype(BF16), row(glu_b), nb=b, tc=tc)
    x1 = _merge(x, att, pool, ssm.reshape(s, b * ssm_w), row(out_norm_g),
                w_out.astype(BF16), ts=ts)

    n_experts = peer_u.shape[0]
    hk = peer_keys.shape[0] * PEER_TOPK
    kh = peer_keys.astype(BF16)
    h2, off, sh, gates = _route(x1.reshape(t, d), row(norm2_g), peer_wq.astype(BF16), kh,
                                tm=tm, n_experts=n_experts)
    tok_major = lambda a: a.reshape(hk, t).T
    off, sh, gates = tok_major(off), tok_major(sh), tok_major(gates)
    coef = _peer_u(off, sh, h2.reshape(t, SUBLANES, d // SUBLANES), gates,
                   _pack_table(peer_u), tt=tt)
    x2 = _peer_v(off, sh, coef, x1.reshape(t, SUBLANES, d // SUBLANES),
                 _pack_table(peer_v), tt=tt)
    return x2.reshape(b, s, d)


def kernel(x, norm1_g, w_in, f_bias, q_gain, k_gain, pool_w, pool_scale, ssm_a_re, ssm_a_im,
           ssm_log_dt, ssm_b_re, ssm_b_im, ssm_c_re, ssm_c_im, ssm_d, glu_w, glu_b, out_norm_g,
           w_out, norm2_g, peer_wq, peer_keys, peer_u, peer_v):
    per_layer = (norm1_g, w_in, f_bias, q_gain, k_gain, pool_w, pool_scale, ssm_a_re, ssm_a_im,
                 ssm_log_dt, ssm_b_re, ssm_b_im, ssm_c_re, ssm_c_im, ssm_d, glu_w, glu_b,
                 out_norm_g, w_out, norm2_g, peer_wq, peer_keys, peer_u, peer_v)
    for l in range(norm1_g.shape[0]):
        x = _layer(x, *[p[l] for p in per_layer])
    return x
```

```python
import functools
import math

import jax
import jax.numpy as jnp
from jax import lax
from jax.experimental import pallas as pl
from jax.experimental.pallas import tpu as pltpu

F32 = jnp.float32
BF16 = jnp.bfloat16
U32 = jnp.uint32
I32 = jnp.int32

EPS = 1e-6
HEAD_DIM = 64
POOL_WINDOWS = (2, 4, 8, 16)
POOL_HALO = 16
SSM_STATE = 64
SSM_GROUP_CH = 16
PEER_KEYS = 128
PEER_TOPK = 16

LANES = 128
SUBLANES = 8
VMEM_LIMIT_BYTES = 56 * 1024 * 1024

_NT = (((1,), (1,)), ((), ()))


def _rmsnorm(x, g):
    return x * lax.rsqrt(jnp.mean(x * x, axis=-1, keepdims=True) + EPS) * g


def _gelu_tanh(x):
    c = math.sqrt(2.0 / math.pi)
    return x * (0.5 * (1.0 + jnp.tanh(c * (x + 0.044715 * (x * x * x)))))


def _params(sem, vmem=None):
    return pltpu.CompilerParams(dimension_semantics=sem, vmem_limit_bytes=vmem)


def _inproj_kernel(x_ref, g_ref, wqkv_ref, wf_ref, wpool_ref, wssm_ref, fb_ref,
                   qg_ref, kg_ref, seg_ref,
                   q_out, k_out, v_out, c_out, pool_out, ssm_out, carry_ref,
                   *, att_w, n_heads):
    i = pl.program_id(1)

    @pl.when(i == 0)
    def _():
        carry_ref[...] = jnp.zeros_like(carry_ref)

    x = x_ref[...]
    ts = x.shape[0]
    h = _rmsnorm(x, g_ref[...])
    hb = h.astype(BF16)
    qkv = jnp.dot(hb, wqkv_ref[...], preferred_element_type=F32)
    seg = seg_ref[...]

    def headnorm(t, gain):
        ms = jnp.dot((t * t).astype(BF16), seg, preferred_element_type=F32)
        return t * lax.rsqrt(ms + EPS) * gain

    q_out[...] = headnorm(qkv[:, :att_w], qg_ref[...]).astype(BF16)
    k_out[...] = headnorm(qkv[:, att_w:2 * att_w], kg_ref[...]).astype(BF16)
    v_out[...] = qkv[:, 2 * att_w:].astype(BF16)
    pool_out[...] = jnp.dot(hb, wpool_ref[...], preferred_element_type=F32)
    ssm_out[...] = jnp.dot(hb, wssm_ref[...], preferred_element_type=F32)

    f = jnp.dot(h, wf_ref[...], precision=lax.Precision.HIGHEST,
                preferred_element_type=F32) + fb_ref[...]
    ls = jnp.minimum(f, 0.0) - jnp.log1p(jnp.exp(-jnp.abs(f)))
    r = lax.broadcasted_iota(I32, (ts, ts), 0)
    c = lax.broadcasted_iota(I32, (ts, ts), 1)
    tri = jnp.where(c <= r, 1.0, 0.0).astype(F32)
    cs = jnp.dot(tri, ls, precision=lax.Precision.HIGHEST,
                 preferred_element_type=F32) + carry_ref[0:1, :]
    carry_ref[...] = jnp.broadcast_to(cs[ts - 1:ts, :], carry_ref.shape)
    c_out[...] = cs[:, :n_heads]


def _inproj(x, g, wqkv, wf, wpool, wssm, fb, qg, kg, seg, *, ts):
    b, s, d = x.shape
    att_w = seg.shape[0]
    n_heads = att_w // HEAD_DIM
    pw = wpool.shape[1]
    sw = wssm.shape[1]
    full = lambda a: pl.BlockSpec(a.shape, lambda bi, i: (0,) * a.ndim)
    tok = lambda w: pl.BlockSpec((None, ts, w), lambda bi, i: (bi, i, 0))
    return pl.pallas_call(
        functools.partial(_inproj_kernel, att_w=att_w, n_heads=n_heads),
        grid=(b, s // ts),
        in_specs=[tok(d), full(g), full(wqkv), full(wf), full(wpool), full(wssm),
                  full(fb), full(qg), full(kg), full(seg)],
        out_specs=[tok(att_w), tok(att_w), tok(att_w), tok(n_heads), tok(pw),
                   pl.BlockSpec((ts, sw), lambda bi, i: (i, bi))],
        out_shape=[jax.ShapeDtypeStruct((b, s, att_w), BF16),
                   jax.ShapeDtypeStruct((b, s, att_w), BF16),
                   jax.ShapeDtypeStruct((b, s, att_w), BF16),
                   jax.ShapeDtypeStruct((b, s, n_heads), F32),
                   jax.ShapeDtypeStruct((b, s, pw), F32),
                   jax.ShapeDtypeStruct((s, b * sw), F32)],
        scratch_shapes=[pltpu.VMEM((SUBLANES, LANES), F32)],
        compiler_params=_params(("arbitrary", "arbitrary"), VMEM_LIMIT_BYTES),
        name="inproj",
    )(x, g, wqkv, wf, wpool, wssm, fb, qg, kg, seg)


def _attn_kernel(q_ref, k_ref, v_ref, ct_ref, cr_ref, o_ref, m_sc, l_sc, acc_sc, s_sc, *, blk):
    p = pl.program_id(1)
    i = pl.program_id(2)
    q = q_ref[...]
    lane = lax.broadcasted_iota(I32, (blk, LANES), 1)
    ct = ct_ref[...]
    hl = lax.broadcasted_iota(I32, ct.shape, 1)
    heads = (2 * p, 2 * p + 1)
    qm = (jnp.where(lane < HEAD_DIM, q, jnp.zeros_like(q)),
          jnp.where(lane >= HEAD_DIM, q, jnp.zeros_like(q)))
    cq = tuple(jnp.sum(jnp.where(hl == hd, ct, 0.0), axis=1, keepdims=True) for hd in heads)
    m_sc[...] = jnp.full(m_sc.shape, -jnp.inf, F32)
    l_sc[...] = jnp.zeros(l_sc.shape, F32)
    acc_sc[...] = jnp.zeros(acc_sc.shape, F32)

    def qk(kb):
        kblk = k_ref[pl.ds(pl.multiple_of(kb * blk, blk), blk), :]
        return [lax.dot_general(qm[hh], kblk, _NT, preferred_element_type=F32) for hh in range(2)]

    def step(kb, s_cur, causal):
        start = pl.multiple_of(kb * blk, blk)
        vblk = v_ref[pl.ds(start, blk), :]
        for hh in range(2):
            ck = cr_ref[pl.ds(heads[hh], 1), pl.ds(start, blk)]
            z = (s_cur[hh] + cq[hh]) - ck
            if causal:
                row = lax.broadcasted_iota(I32, (blk, blk), 0)
                col = lax.broadcasted_iota(I32, (blk, blk), 1)
                z = jnp.where(col <= row, z, -jnp.inf)
            m_old = m_sc[hh]
            m_new = jnp.maximum(m_old, jnp.max(z, axis=1, keepdims=True))
            alpha = jnp.exp(m_old - m_new)
            pe = jnp.exp(z - m_new)
            l_sc[hh] = alpha * l_sc[hh] + jnp.sum(pe, axis=1, keepdims=True)
            acc_sc[hh] = alpha * acc_sc[hh] + jnp.dot(pe.astype(BF16), vblk,
                                                      preferred_element_type=F32)
            m_sc[hh] = m_new

    s0 = qk(0)
    s_sc[0] = s0[0]
    s_sc[1] = s0[1]

    def body(kb, carry):
        s_cur = [s_sc[0], s_sc[1]]
        s_next = qk(kb + 1)
        step(kb, s_cur, False)
        s_sc[0] = s_next[0]
        s_sc[1] = s_next[1]
        return carry

    lax.fori_loop(0, i, body, 0)
    step(i, [s_sc[0], s_sc[1]], True)
    o_ref[...] = jnp.where(lane < HEAD_DIM, acc_sc[0] / l_sc[0], acc_sc[1] / l_sc[1])


def _attention(q, k, v, c_tok, c_row, *, blk):
    b, s, w = q.shape
    n_heads = c_tok.shape[2]
    return pl.pallas_call(
        functools.partial(_attn_kernel, blk=blk),
        grid=(b, w // LANES, s // blk),
        in_specs=[pl.BlockSpec((None, blk, LANES), lambda bi, p, i: (bi, i, p)),
                  pl.BlockSpec((None, s, LANES), lambda bi, p, i: (bi, 0, p)),
                  pl.BlockSpec((None, s, LANES), lambda bi, p, i: (bi, 0, p)),
                  pl.BlockSpec((None, blk, n_heads), lambda bi, p, i: (bi, i, 0)),
                  pl.BlockSpec((None, n_heads, s), lambda bi, p, i: (bi, 0, 0))],
        out_specs=pl.BlockSpec((None, blk, LANES), lambda bi, p, i: (bi, i, p)),
        out_shape=jax.ShapeDtypeStruct((b, s, w), F32),
        scratch_shapes=[pltpu.VMEM((2, blk, 1), F32), pltpu.VMEM((2, blk, 1), F32),
                        pltpu.VMEM((2, blk, LANES), F32), pltpu.VMEM((2, blk, blk), F32)],
        compiler_params=_params(("arbitrary", "arbitrary", "arbitrary"), VMEM_LIMIT_BYTES),
        name="fox_attention",
    )(q, k, v, c_tok, c_row)


def _pool_kernel(u_ref, prev_ref, w_ref, sc_ref, o_ref, *, group_dim):
    i = pl.program_id(1)
    cur = u_ref[...]
    ts, pw = cur.shape
    prev = jnp.where(i > 0, prev_ref[...], 0.0)
    ext = jnp.concatenate([prev, cur], axis=0)
    pos = i * ts + lax.broadcasted_iota(I32, (ts, 1), 0)
    grp = lax.broadcasted_iota(I32, (ts, pw), 1) // group_dim
    acc = cur
    pooled = jnp.zeros_like(cur)
    for j in range(1, max(POOL_WINDOWS)):
        acc = acc + ext[POOL_HALO - j:POOL_HALO - j + ts, :]
        win = j + 1
        if win in POOL_WINDOWS:
            cnt = jnp.minimum(pos + 1, win).astype(F32)
            pooled = jnp.where(grp == POOL_WINDOWS.index(win), acc / cnt, pooled)
    pooled = pooled - cur
    o_ref[...] = jnp.dot(pooled.astype(BF16), w_ref[...],
                         preferred_element_type=F32) * sc_ref[...]


def _pool(u, wblk, scale, *, ts):
    b, s, pw = u.shape
    hb = ts // POOL_HALO
    return pl.pallas_call(
        functools.partial(_pool_kernel, group_dim=pw // len(POOL_WINDOWS)),
        grid=(b, s // ts),
        in_specs=[pl.BlockSpec((None, ts, pw), lambda bi, i: (bi, i, 0)),
                  pl.BlockSpec((None, POOL_HALO, pw),
                               lambda bi, i: (bi, jnp.maximum(i * hb - 1, 0), 0)),
                  pl.BlockSpec(wblk.shape, lambda bi, i: (0, 0)),
                  pl.BlockSpec(scale.shape, lambda bi, i: (0, 0))],
        out_specs=pl.BlockSpec((None, ts, pw), lambda bi, i: (bi, i, 0)),
        out_shape=jax.ShapeDtypeStruct((b, s, pw), F32),
        compiler_params=_params(("arbitrary", "arbitrary")),
        name="pool",
    )(u, u, wblk, scale)


def _s5_kernel(u_ref, are_ref, aim_ref, bblk_ref, cblk_ref, d_ref, gw_ref, gb_ref,
               o_ref, st_ref, hre_ref, him_ref, *, nb, n):
    @pl.when(pl.program_id(0) == 0)
    def _():
        hre_ref[...] = jnp.zeros_like(hre_ref)
        him_ref[...] = jnp.zeros_like(him_ref)

    u = u_ref[...]
    tc = u.shape[0] // nb
    st_ref[...] = jnp.dot(u.astype(BF16), bblk_ref[...], preferred_element_type=F32)
    ar = jnp.broadcast_to(are_ref[...], (nb, n))
    ai = jnp.broadcast_to(aim_ref[...], (nb, n))

    def step(t, carry):
        hr, hi = carry
        r0 = pl.multiple_of(t * nb, nb)
        br = st_ref[pl.ds(r0, nb), 0:n]
        bi = st_ref[pl.ds(r0, nb), n:2 * n]
        nr = (ar * hr - ai * hi) + br
        ni = (ar * hi + ai * hr) + bi
        st_ref[pl.ds(r0, nb), 0:n] = nr
        st_ref[pl.ds(r0, nb), n:2 * n] = ni
        return nr, ni

    hr, hi = lax.fori_loop(0, tc, step, (hre_ref[...], him_ref[...]))
    hre_ref[...] = hr
    him_ref[...] = hi
    y = jnp.dot(st_ref[...].astype(BF16), cblk_ref[...], preferred_element_type=F32)
    z = _gelu_tanh(y + d_ref[...] * u)
    gate = jax.nn.sigmoid(jnp.dot(z.astype(BF16), gw_ref[...], preferred_element_type=F32)
                          + gb_ref[...])
    o_ref[...] = z * gate


def _s5(u_tm, a_re, a_im, bblk, cblk, dskip, gw, gb, *, nb, tc):
    rows, sw = u_tm.shape
    n = a_re.shape[1]
    rb = tc * nb
    full = lambda a: pl.BlockSpec(a.shape, lambda i: (0,) * a.ndim)
    return pl.pallas_call(
        functools.partial(_s5_kernel, nb=nb, n=n),
        grid=(rows // rb,),
        in_specs=[pl.BlockSpec((rb, sw), lambda i: (i, 0)), full(a_re), full(a_im),
                  full(bblk), full(cblk), full(dskip), full(gw), full(gb)],
        out_specs=pl.BlockSpec((rb, sw), lambda i: (i, 0)),
        out_shape=jax.ShapeDtypeStruct((rows, sw), F32),
        scratch_shapes=[pltpu.VMEM((rb, 2 * n), F32), pltpu.VMEM((nb, n), F32),
                        pltpu.VMEM((nb, n), F32)],
        compiler_params=_params(("arbitrary",), VMEM_LIMIT_BYTES),
        name="s5_scan",
    )(u_tm, a_re, a_im, bblk, cblk, dskip, gw, gb)


def _merge_kernel(x_ref, att_ref, pool_ref, ssm_ref, og_ref, w_ref, o_ref, *, att_w, pool_w):
    og = og_ref[...]
    p1 = att_w + pool_w
    merged = jnp.concatenate([
        _rmsnorm(att_ref[...], og[:, :att_w]),
        _rmsnorm(pool_ref[...], og[:, att_w:p1]),
        _rmsnorm(ssm_ref[...], og[:, p1:])], axis=1).astype(BF16)
    o_ref[...] = x_ref[...] + jnp.dot(merged, w_ref[...], preferred_element_type=F32)


def _merge(x, att, pool, ssm_tm, og, w, *, ts):
    b, s, d = x.shape
    att_w, pool_w = att.shape[2], pool.shape[2]
    sw = ssm_tm.shape[1] // b
    tok = lambda wd: pl.BlockSpec((None, ts, wd), lambda bi, i: (bi, i, 0))
    return pl.pallas_call(
        functools.partial(_merge_kernel, att_w=att_w, pool_w=pool_w),
        grid=(b, s // ts),
        in_specs=[tok(d), tok(att_w), tok(pool_w),
                  pl.BlockSpec((ts, sw), lambda bi, i: (i, bi)),
                  pl.BlockSpec(og.shape, lambda bi, i: (0, 0)),
                  pl.BlockSpec(w.shape, lambda bi, i: (0, 0))],
        out_specs=tok(d),
        out_shape=jax.ShapeDtypeStruct((b, s, d), F32),
        compiler_params=_params(("arbitrary", "arbitrary"), VMEM_LIMIT_BYTES),
        name="merge_outproj",
    )(x, att, pool, ssm_tm, og, w)


def _topk_rows(a, k, extra=None):
    nrow, nl = a.shape
    row = lax.broadcasted_iota(I32, (nrow, nl), 0).astype(F32)
    out_row = lax.broadcasted_iota(I32, (k, nl), 0)
    vals = jnp.zeros((k, nl), F32)
    sel = jnp.zeros((k, nl), F32)
    for j in range(k):
        m = jnp.max(a, axis=0, keepdims=True)
        ix = jnp.min(jnp.where(a == m, row, float(nrow)), axis=0, keepdims=True)
        hit = row == ix
        if extra is None:
            pick = ix
        else:
            pick = jnp.sum(jnp.where(hit, extra, 0.0), axis=0, keepdims=True)
        vals = jnp.where(out_row == j, m, vals)
        sel = jnp.where(out_row == j, pick, sel)
        a = jnp.where(hit, -jnp.inf, a)
    return vals, sel


def _route_kernel(x_ref, g_ref, wq_ref, keys_ref, h_out, off_out, sh_out, gate_out, q_sc,
                  *, n_half):
    hd = pl.program_id(1)
    qd = 2 * PEER_KEYS

    @pl.when(hd == 0)
    def _():
        h = _rmsnorm(x_ref[...], g_ref[...])
        h_out[...] = h
        q_sc[...] = jnp.dot(h.astype(BF16), wq_ref[...], preferred_element_type=F32)

    q = q_sc[:, pl.ds(pl.multiple_of(hd * qd, qd), qd)].astype(BF16)
    tops = []
    for i in range(2):
        sc = lax.dot_general(keys_ref[i], q[:, i * PEER_KEYS:(i + 1) * PEER_KEYS], _NT,
                             preferred_element_type=F32)
        tops.append(_topk_rows(sc, PEER_TOPK))
    (s1, i1), (s2, i2) = tops
    k = PEER_TOPK
    sub8 = lax.broadcasted_iota(I32, (SUBLANES, s1.shape[1]), 0)
    cands, ecands = [], []
    a = 0
    while a < k and k // (a + 1) > 1:
        nb = k // (a + 1)
        for b0 in range(0, nb, SUBLANES):
            c = s1[a:a + 1, :] + s2[b0:b0 + SUBLANES, :]
            if b0 + SUBLANES > nb:
                c = jnp.where(sub8 < nb - b0, c, -jnp.inf)
            cands.append(c)
            ecands.append(i1[a:a + 1, :] * float(PEER_KEYS) + i2[b0:b0 + SUBLANES, :])
        a += 1
    cands.append(s1[a:, :] + s2[0:1, :])
    ecands.append(i1[a:, :] * float(PEER_KEYS) + i2[0:1, :])
    best, experts = _topk_rows(jnp.concatenate(cands, axis=0), k,
                               extra=jnp.concatenate(ecands, axis=0))
    w = jnp.exp(best - best[0:1, :])
    gate_out[...] = w / jnp.sum(w, axis=0, keepdims=True)
    e = experts.astype(I32)
    hi_half = e < n_half
    off_out[...] = jnp.where(hi_half, e, e - n_half) * SUBLANES
    sh_out[...] = jnp.where(hi_half, 0.0, 16.0)


def _route(x, g, wq, keys, *, tm, n_experts):
    t, d = x.shape
    n_heads = keys.shape[0]
    hk = pl.BlockSpec((None, PEER_TOPK, tm), lambda i, h: (h, 0, i))
    shp = lambda dt: jax.ShapeDtypeStruct((n_heads, PEER_TOPK, t), dt)
    return pl.pallas_call(
        functools.partial(_route_kernel, n_half=n_experts // 2),
        grid=(t // tm, n_heads),
        in_specs=[pl.BlockSpec((tm, d), lambda i, h: (i, 0)),
                  pl.BlockSpec(g.shape, lambda i, h: (0, 0)),
                  pl.BlockSpec(wq.shape, lambda i, h: (0, 0)),
                  pl.BlockSpec((None, 2, PEER_KEYS, keys.shape[3]), lambda i, h: (h, 0, 0, 0))],
        out_specs=[pl.BlockSpec((tm, d), lambda i, h: (i, 0)), hk, hk, hk],
        out_shape=[jax.ShapeDtypeStruct((t, d), F32), shp(I32), shp(F32), shp(F32)],
        scratch_shapes=[pltpu.VMEM((tm, wq.shape[1]), F32)],
        compiler_params=_params(("arbitrary", "arbitrary"), VMEM_LIMIT_BYTES),
        name="peer_route",
    )(x, g, wq, keys)


def _expert_row(tab_ref, off, sh):
    w = tab_ref[pl.ds(pl.multiple_of(off, SUBLANES), SUBLANES), :]
    return lax.bitcast_convert_type((w << sh) & jnp.uint32(0xFFFF0000), F32)


def _fold_rows(vs, sub, rot=0):
    m = len(vs)
    if m == 1:
        return vs[0]
    span = SUBLANES // m
    c1 = _fold_rows(vs[0::2], sub, rot - span)
    c2 = _fold_rows(vs[1::2], sub, rot)
    own = ((sub - rot) & (2 * span - 1)) < span
    return jnp.where(own, c1, c2) + pltpu.roll(jnp.where(own, c2, c1), span, 0)


def _splat_rows(mxu_src, mxu_dst, xlu_src=None, xlu_dst=None, *, tt, hk):
    eye = (lax.broadcasted_iota(I32, (hk, hk), 0) == lax.broadcasted_iota(I32, (hk, hk), 1))
    ones = jnp.ones((hk, hk), BF16)

    def body(t, carry):
        rows = pl.ds(pl.multiple_of(t * hk, hk), hk)
        v = jnp.broadcast_to(mxu_src[pl.ds(t, 1), :], (hk, hk))
        rep = jnp.dot(jnp.where(eye, v, 0.0).astype(BF16), ones, preferred_element_type=F32)
        mxu_dst[rows, :] = rep.astype(mxu_dst.dtype)
        if xlu_src is not None:
            xlu_dst[rows, :] = jnp.broadcast_to(xlu_src[pl.ds(t, 1), :], (hk, hk)).T
        return carry

    lax.fori_loop(0, tt, body, 0, unroll=8)


def _splat(rep_ref, row):
    return jnp.broadcast_to(rep_ref[pl.ds(row, 1), :], (SUBLANES, LANES))


def _peer_u_kernel(*refs, tt, hk):
    off_refs = refs[:SUBLANES]
    sh_ref, h_ref, gate_ref, tab_ref, coef_ref, part_ref, srep_ref, pre_ref = refs[SUBLANES:]
    sub = lax.broadcasted_iota(I32, (SUBLANES, LANES), 0)
    ngrp = hk // SUBLANES
    _splat_rows(sh_ref, srep_ref, tt=tt, hk=hk)

    def tok(t, carry):
        h = h_ref[t]
        base = pl.multiple_of(t * hk, hk)
        for g in range(ngrp):
            e0 = base + g * SUBLANES
            ps = [_expert_row(tab_ref, off_refs[j][t * ngrp + g],
                              _splat(srep_ref, e0 + j).astype(U32)) * h
                  for j in range(SUBLANES)]
            part_ref[pl.ds(e0, SUBLANES), :] = _fold_rows(ps, sub)
        return carry

    lax.fori_loop(0, tt, tok, 0)

    def lane_sums(t, carry):
        p = part_ref[pl.ds(pl.multiple_of(t * hk, hk), hk), :]
        pre_ref[pl.ds(t, 1), :] = jnp.sum(p.T, axis=0, keepdims=True)
        return carry

    lax.fori_loop(0, tt, lane_sums, 0, unroll=8)
    coef_ref[...] = gate_ref[...] * _gelu_tanh(pre_ref[...])


def _peer_v_kernel(*refs, tt, hk, n_acc):
    off_refs = refs[:SUBLANES]
    sh_ref, coef_ref, x_ref, tab_ref, o_ref, crep_ref, srep_ref = refs[SUBLANES:]
    ngrp = hk // SUBLANES
    _splat_rows(sh_ref, srep_ref, coef_ref, crep_ref, tt=tt, hk=hk)

    def tok(t, carry):
        accs = [None] * n_acc
        e0 = t * hk
        for k in range(hk):
            off = off_refs[k % SUBLANES][t * ngrp + k // SUBLANES]
            term = _splat(crep_ref, e0 + k) * _expert_row(tab_ref, off,
                                                          _splat(srep_ref, e0 + k).astype(U32))
            accs[k % n_acc] = term if accs[k % n_acc] is None else accs[k % n_acc] + term
        while len(accs) > 1:
            accs = [accs[j] + accs[j + len(accs) // 2] for j in range(len(accs) // 2)]
        o_ref[t] = x_ref[t] + accs[0]
        return carry

    lax.fori_loop(0, tt, tok, 0)


def _resident(a):
    return pl.BlockSpec(a.shape, lambda i: (0,) * a.ndim, pipeline_mode=pl.Buffered(1))


def _split_offsets(off):
    t, hk = off.shape
    grouped = off.reshape(t, hk // SUBLANES, SUBLANES)
    return [grouped[:, :, j].reshape(-1) for j in range(SUBLANES)]


def _peer_u(off, sh, h3, gates, tab, *, tt):
    t, hk = off.shape
    assert hk == LANES
    smem = pl.BlockSpec((tt * hk // SUBLANES,), lambda i: (i,), memory_space=pltpu.SMEM)
    compact = pl.BlockSpec((tt, hk), lambda i: (i, 0))
    return pl.pallas_call(
        functools.partial(_peer_u_kernel, tt=tt, hk=hk),
        grid=(t // tt,),
        in_specs=[smem] * SUBLANES + [
            compact, pl.BlockSpec((tt, SUBLANES, LANES), lambda i: (i, 0, 0)), compact,
            _resident(tab)],
        out_specs=compact,
        out_shape=jax.ShapeDtypeStruct((t, hk), F32),
        scratch_shapes=[pltpu.VMEM((tt * hk, LANES), F32), pltpu.VMEM((tt * hk, LANES), I32),
                        pltpu.VMEM((tt, hk), F32)],
        compiler_params=_params(("arbitrary",), VMEM_LIMIT_BYTES),
        name="peer_u",
    )(*_split_offsets(off), sh, h3, gates, tab)


def _peer_v(off, sh, coef, x3, tab, *, tt):
    t, hk = off.shape
    assert hk == LANES
    smem = pl.BlockSpec((tt * hk // SUBLANES,), lambda i: (i,), memory_space=pltpu.SMEM)
    tile = pl.BlockSpec((tt, SUBLANES, LANES), lambda i: (i, 0, 0))
    compact = pl.BlockSpec((tt, hk), lambda i: (i, 0))
    return pl.pallas_call(
        functools.partial(_peer_v_kernel, tt=tt, hk=hk, n_acc=4),
        grid=(t // tt,),
        in_specs=[smem] * SUBLANES + [compact, compact, tile, _resident(tab)],
        out_specs=tile,
        out_shape=jax.ShapeDtypeStruct(x3.shape, F32),
        scratch_shapes=[pltpu.VMEM((tt * hk, LANES), F32), pltpu.VMEM((tt * hk, LANES), I32)],
        compiler_params=_params(("arbitrary",), VMEM_LIMIT_BYTES),
        name="peer_v",
    )(*_split_offsets(off), sh, coef, x3, tab)


def _pack_table(tab):
    n, d = tab.shape
    b = lax.bitcast_convert_type(tab.astype(BF16), jnp.uint16).astype(U32)
    w = (b[:n // 2] << 16) | b[n // 2:]
    return w.reshape(n // 2 * (d // LANES), LANES)


def _block_diag(blocks):
    g, r, c = blocks.shape
    eye = jnp.eye(g, dtype=blocks.dtype)
    return (eye[:, None, :, None] * blocks[:, :, None, :]).reshape(g * r, g * c)


def _s5_params(a_re, a_im, log_dt, b_re, b_im, c_re, c_im):
    a = lax.complex(a_re, a_im)
    dt = jnp.exp(log_dt)[:, None]
    a_bar = jnp.exp(a * dt)
    b_bar = ((a_bar - 1.0) / a)[..., None] * lax.complex(b_re, b_im)
    bt = jnp.swapaxes(b_bar, 1, 2)
    bblk = jnp.concatenate([_block_diag(jnp.real(bt)), _block_diag(jnp.imag(bt))], axis=1)
    ct_re = jnp.swapaxes(c_re, 1, 2)
    ct_im = jnp.swapaxes(c_im, 1, 2)
    cblk = jnp.concatenate([_block_diag(ct_re), -_block_diag(ct_im)], axis=0)
    n = a_bar.size
    return (jnp.real(a_bar).reshape(1, n), jnp.imag(a_bar).reshape(1, n),
            bblk.astype(BF16), cblk.astype(BF16))


def _tiles(s, t):
    ts = min(512, s)
    blk = min(512, s)
    tc = min(64, s)
    tm = min(256, t)
    tt = min(64, t)
    return ts, blk, tc, tm, tt


def _layer(x, norm1_g, w_in, f_bias, q_gain, k_gain, pool_w, pool_scale,
           a_re, a_im, log_dt, b_re, b_im, c_re, c_im, ssm_d, glu_w, glu_b,
           out_norm_g, w_out, norm2_g, peer_wq, peer_keys, peer_u, peer_v):
    b, s, d = x.shape
    t = b * s
    n_heads = f_bias.shape[0]
    att_w = n_heads * HEAD_DIM
    pool_width = pool_scale.shape[0]
    ssm_w = ssm_d.shape[0]
    ts, blk, tc, tm, tt = _tiles(s, t)
    row = lambda v: v.reshape(1, -1)

    c0, c1 = 3 * att_w, 3 * att_w + n_heads
    wf = jnp.zeros((d, LANES), F32).at[:, :n_heads].set(w_in[:, c0:c1])
    fb = jnp.zeros((1, LANES), F32).at[0, :n_heads].set(f_bias)
    seg = _block_diag(jnp.full((n_heads, HEAD_DIM, HEAD_DIM), 1.0 / HEAD_DIM, F32)).astype(BF16)
    q, k, v, c_tok, u_pool, u_ssm = _inproj(
        x, row(norm1_g), w_in[:, :c0].astype(BF16), wf,
        w_in[:, c1:c1 + pool_width].astype(BF16), w_in[:, c1 + pool_width:].astype(BF16), fb,
        row(jnp.tile(q_gain, n_heads) * HEAD_DIM ** -0.5), row(jnp.tile(k_gain, n_heads)),
        seg, ts=ts)
    att = _attention(q, k, v, c_tok, jnp.swapaxes(c_tok, 1, 2), blk=blk)
    pool = _pool(u_pool, _block_diag(pool_w).astype(BF16), row(pool_scale), ts=ts)
    are, aim, bblk, cblk = _s5_params(a_re, a_im, log_dt, b_re, b_im, c_re, c_im)
    ssm = _s5(u_ssm.reshape(s * b, ssm_w), are, aim, bblk, cblk, row(ssm_d),
              glu_w.astype(BF16), row(glu_b), nb=b, tc=tc)
    x1 = _merge(x, att, pool, ssm.reshape(s, b * ssm_w), row(out_norm_g),
                w_out.astype(BF16), ts=ts)

    n_experts = peer_u.shape[0]
    hk = peer_keys.shape[0] * PEER_TOPK
    kh = peer_keys.astype(BF16)
    h2, off, sh, gates = _route(x1.reshape(t, d), row(norm2_g), peer_wq.astype(BF16), kh,
                                tm=tm, n_experts=n_experts)
    tok_major = lambda a: a.reshape(hk, t).T
    off, sh, gates = tok_major(off), tok_major(sh), tok_major(gates)
    coef = _peer_u(off, sh, h2.reshape(t, SUBLANES, d // SUBLANES), gates,
                   _pack_table(peer_u), tt=tt)
    x2 = _peer_v(off, sh, coef, x1.reshape(t, SUBLANES, d // SUBLANES),
                 _pack_table(peer_v), tt=tt)
    return x2.reshape(b, s, d)


def kernel(x, norm1_g, w_in, f_bias, q_gain, k_gain, pool_w, pool_scale, ssm_a_re, ssm_a_im,
           ssm_log_dt, ssm_b_re, ssm_b_im, ssm_c_re, ssm_c_im, ssm_d, glu_w, glu_b, out_norm_g,
           w_out, norm2_g, peer_wq, peer_keys, peer_u, peer_v):
    per_layer = (norm1_g, w_in, f_bias, q_gain, k_gain, pool_w, pool_scale, ssm_a_re, ssm_a_im,
                 ssm_log_dt, ssm_b_re, ssm_b_im, ssm_c_re, ssm_c_im, ssm_d, glu_w, glu_b,
                 out_norm_g, w_out, norm2_g, peer_wq, peer_keys, peer_u, peer_v)
    for l in range(norm1_g.shape[0]):
        x = _layer(x, *[p[l] for p in per_layer])
    return x
```

```python
import functools
import math

import jax
import jax.numpy as jnp
from jax import lax
from jax.experimental import pallas as pl
from jax.experimental.pallas import tpu as pltpu

F32 = jnp.float32
BF16 = jnp.bfloat16
U32 = jnp.uint32
I32 = jnp.int32

EPS = 1e-6
HEAD_DIM = 64
POOL_WINDOWS = (2, 4, 8, 16)
POOL_HALO = 16
SSM_STATE = 64
SSM_GROUP_CH = 16
PEER_KEYS = 128
PEER_TOPK = 16

LANES = 128
SUBLANES = 8
VMEM_LIMIT_BYTES = 56 * 1024 * 1024

_NT = (((1,), (1,)), ((), ()))


def _rmsnorm(x, g):
    return x * lax.rsqrt(jnp.mean(x * x, axis=-1, keepdims=True) + EPS) * g


def _gelu_tanh(x):
    c = math.sqrt(2.0 / math.pi)
    return x * (0.5 * (1.0 + jnp.tanh(c * (x + 0.044715 * (x * x * x)))))


def _params(sem, vmem=None):
    return pltpu.CompilerParams(dimension_semantics=sem, vmem_limit_bytes=vmem)


def _inproj_kernel(x_ref, g_ref, wqkv_ref, wf_ref, wpool_ref, wssm_ref, fb_ref,
                   qg_ref, kg_ref, seg_ref,
                   q_out, k_out, v_out, c_out, pool_out, ssm_out, carry_ref,
                   *, att_w, n_heads):
    i = pl.program_id(1)

    @pl.when(i == 0)
    def _():
        carry_ref[...] = jnp.zeros_like(carry_ref)

    x = x_ref[...]
    ts = x.shape[0]
    h = _rmsnorm(x, g_ref[...])
    hb = h.astype(BF16)
    qkv = jnp.dot(hb, wqkv_ref[...], preferred_element_type=F32)
    seg = seg_ref[...]

    def headnorm(t, gain):
        ms = jnp.dot((t * t).astype(BF16), seg, preferred_element_type=F32)
        return t * lax.rsqrt(ms + EPS) * gain

    q_out[...] = headnorm(qkv[:, :att_w], qg_ref[...]).astype(BF16)
    k_out[...] = headnorm(qkv[:, att_w:2 * att_w], kg_ref[...]).astype(BF16)
    v_out[...] = qkv[:, 2 * att_w:].astype(BF16)
    pool_out[...] = jnp.dot(hb, wpool_ref[...], preferred_element_type=F32)
    ssm_out[...] = jnp.dot(hb, wssm_ref[...], preferred_element_type=F32)

    f = jnp.dot(h, wf_ref[...], precision=lax.Precision.HIGHEST,
                preferred_element_type=F32) + fb_ref[...]
    ls = jnp.minimum(f, 0.0) - jnp.log1p(jnp.exp(-jnp.abs(f)))
    r = lax.broadcasted_iota(I32, (ts, ts), 0)
    c = lax.broadcasted_iota(I32, (ts, ts), 1)
    tri = jnp.where(c <= r, 1.0, 0.0).astype(F32)
    cs = jnp.dot(tri, ls, precision=lax.Precision.HIGHEST,
                 preferred_element_type=F32) + carry_ref[0:1, :]
    carry_ref[...] = jnp.broadcast_to(cs[ts - 1:ts, :], carry_ref.shape)
    c_out[...] = cs[:, :n_heads]


def _inproj(x, g, wqkv, wf, wpool, wssm, fb, qg, kg, seg, *, ts):
    b, s, d = x.shape
    att_w = seg.shape[0]
    n_heads = att_w // HEAD_DIM
    pw = wpool.shape[1]
    sw = wssm.shape[1]
    full = lambda a: pl.BlockSpec(a.shape, lambda bi, i: (0,) * a.ndim)
    tok = lambda w: pl.BlockSpec((None, ts, w), lambda bi, i: (bi, i, 0))
    return pl.pallas_call(
        functools.partial(_inproj_kernel, att_w=att_w, n_heads=n_heads),
        grid=(b, s // ts),
        in_specs=[tok(d), full(g), full(wqkv), full(wf), full(wpool), full(wssm),
                  full(fb), full(qg), full(kg), full(seg)],
        out_specs=[tok(att_w), tok(att_w), tok(att_w), tok(n_heads), tok(pw),
                   pl.BlockSpec((ts, sw), lambda bi, i: (i, bi))],
        out_shape=[jax.ShapeDtypeStruct((b, s, att_w), BF16),
                   jax.ShapeDtypeStruct((b, s, att_w), BF16),
                   jax.ShapeDtypeStruct((b, s, att_w), BF16),
                   jax.ShapeDtypeStruct((b, s, n_heads), F32),
                   jax.ShapeDtypeStruct((b, s, pw), F32),
                   jax.ShapeDtypeStruct((s, b * sw), F32)],
        scratch_shapes=[pltpu.VMEM((SUBLANES, LANES), F32)],
        compiler_params=_params(("arbitrary", "arbitrary"), VMEM_LIMIT_BYTES),
        name="inproj",
    )(x, g, wqkv, wf, wpool, wssm, fb, qg, kg, seg)


def _attn_kernel(q_ref, k_ref, v_ref, ct_ref, cr_ref, o_ref, m_sc, l_sc, acc_sc, s_sc, *, blk):
    p = pl.program_id(1)
    i = pl.program_id(2)
    q = q_ref[...]
    lane = lax.broadcasted_iota(I32, (blk, LANES), 1)
    ct = ct_ref[...]
    hl = lax.broadcasted_iota(I32, ct.shape, 1)
    heads = (2 * p, 2 * p + 1)
    qm = (jnp.where(lane < HEAD_DIM, q, jnp.zeros_like(q)),
          jnp.where(lane >= HEAD_DIM, q, jnp.zeros_like(q)))
    cq = tuple(jnp.sum(jnp.where(hl == hd, ct, 0.0), axis=1, keepdims=True) for hd in heads)
    m_sc[...] = jnp.full(m_sc.shape, -jnp.inf, F32)
    l_sc[...] = jnp.zeros(l_sc.shape, F32)
    acc_sc[...] = jnp.zeros(acc_sc.shape, F32)

    def qk(kb):
        kblk = k_ref[pl.ds(pl.multiple_of(kb * blk, blk), blk), :]
        return [lax.dot_general(qm[hh], kblk, _NT, preferred_element_type=F32) for hh in range(2)]

    def step(kb, s_cur, causal):
        start = pl.multiple_of(kb * blk, blk)
        vblk = v_ref[pl.ds(start, blk), :]
        for hh in range(2):
            ck = cr_ref[pl.ds(heads[hh], 1), pl.ds(start, blk)]
            z = (s_cur[hh] + cq[hh]) - ck
            if causal:
                row = lax.broadcasted_iota(I32, (blk, blk), 0)
                col = lax.broadcasted_iota(I32, (blk, blk), 1)
                z = jnp.where(col <= row, z, -jnp.inf)
            m_old = m_sc[hh]
            m_new = jnp.maximum(m_old, jnp.max(z, axis=1, keepdims=True))
            alpha = jnp.exp(m_old - m_new)
            pe = jnp.exp(z - m_new)
            l_sc[hh] = alpha * l_sc[hh] + jnp.sum(pe, axis=1, keepdims=True)
            acc_sc[hh] = alpha * acc_sc[hh] + jnp.dot(pe.astype(BF16), vblk,
                                                      preferred_element_type=F32)
            m_sc[hh] = m_new

    s0 = qk(0)
    s_sc[0] = s0[0]
    s_sc[1] = s0[1]

    def body(kb, carry):
        s_cur = [s_sc[0], s_sc[1]]
        s_next = qk(kb + 1)
        step(kb, s_cur, False)
        s_sc[0] = s_next[0]
        s_sc[1] = s_next[1]
        return carry

    lax.fori_loop(0, i, body, 0)
    step(i, [s_sc[0], s_sc[1]], True)
    o_ref[...] = jnp.where(lane < HEAD_DIM, acc_sc[0] / l_sc[0], acc_sc[1] / l_sc[1])


def _attention(q, k, v, c_tok, c_row, *, blk):
    b, s, w = q.shape
    n_heads = c_tok.shape[2]
    return pl.pallas_call(
        functools.partial(_attn_kernel, blk=blk),
        grid=(b, w // LANES, s // blk),
        in_specs=[pl.BlockSpec((None, blk, LANES), lambda bi, p, i: (bi, i, p)),
                  pl.BlockSpec((None, s, LANES), lambda bi, p, i: (bi, 0, p)),
                  pl.BlockSpec((None, s, LANES), lambda bi, p, i: (bi, 0, p)),
                  pl.BlockSpec((None, blk, n_heads), lambda bi, p, i: (bi, i, 0)),
                  pl.BlockSpec((None, n_heads, s), lambda bi, p, i: (bi, 0, 0))],
        out_specs=pl.BlockSpec((None, blk, LANES), lambda bi, p, i: (bi, i, p)),
        out_shape=jax.ShapeDtypeStruct((b, s, w), F32),
        scratch_shapes=[pltpu.VMEM((2, blk, 1), F32), pltpu.VMEM((2, blk, 1), F32),
                        pltpu.VMEM((2, blk, LANES), F32), pltpu.VMEM((2, blk, blk), F32)],
        compiler_params=_params(("arbitrary", "arbitrary", "arbitrary"), VMEM_LIMIT_BYTES),
        name="fox_attention",
    )(q, k, v, c_tok, c_row)


def _pool_kernel(u_ref, prev_ref, w_ref, sc_ref, o_ref, *, group_dim):
    i = pl.program_id(1)
    cur = u_ref[...]
    ts, pw = cur.shape
    prev = jnp.where(i > 0, prev_ref[...], 0.0)
    ext = jnp.concatenate([prev, cur], axis=0)
    pos = i * ts + lax.broadcasted_iota(I32, (ts, 1), 0)
    grp = lax.broadcasted_iota(I32, (ts, pw), 1) // group_dim
    acc = cur
    pooled = jnp.zeros_like(cur)
    for j in range(1, max(POOL_WINDOWS)):
        acc = acc + ext[POOL_HALO - j:POOL_HALO - j + ts, :]
        win = j + 1
        if win in POOL_WINDOWS:
            cnt = jnp.minimum(pos + 1, win).astype(F32)
            pooled = jnp.where(grp == POOL_WINDOWS.index(win), acc / cnt, pooled)
    pooled = pooled - cur
    o_ref[...] = jnp.dot(pooled.astype(BF16), w_ref[...],
                         preferred_element_type=F32) * sc_ref[...]


def _pool(u, wblk, scale, *, ts):
    b, s, pw = u.shape
    hb = ts // POOL_HALO
    return pl.pallas_call(
        functools.partial(_pool_kernel, group_dim=pw // len(POOL_WINDOWS)),
        grid=(b, s // ts),
        in_specs=[pl.BlockSpec((None, ts, pw), lambda bi, i: (bi, i, 0)),
                  pl.BlockSpec((None, POOL_HALO, pw),
                               lambda bi, i: (bi, jnp.maximum(i * hb - 1, 0), 0)),
                  pl.BlockSpec(wblk.shape, lambda bi, i: (0, 0)),
                  pl.BlockSpec(scale.shape, lambda bi, i: (0, 0))],
        out_specs=pl.BlockSpec((None, ts, pw), lambda bi, i: (bi, i, 0)),
        out_shape=jax.ShapeDtypeStruct((b, s, pw), F32),
        compiler_params=_params(("arbitrary", "arbitrary")),
        name="pool",
    )(u, u, wblk, scale)


def _s5_kernel(u_ref, are_ref, aim_ref, bblk_ref, cblk_ref, d_ref, gw_ref, gb_ref,
               o_ref, st_ref, hre_ref, him_ref, *, nb, n):
    @pl.when(pl.program_id(0) == 0)
    def _():
        hre_ref[...] = jnp.zeros_like(hre_ref)
        him_ref[...] = jnp.zeros_like(him_ref)

    u = u_ref[...]
    tc = u.shape[0] // nb
    st_ref[...] = jnp.dot(u.astype(BF16), bblk_ref[...], preferred_element_type=F32)
    ar = jnp.broadcast_to(are_ref[...], (nb, n))
    ai = jnp.broadcast_to(aim_ref[...], (nb, n))

    def step(t, carry):
        hr, hi = carry
        r0 = pl.multiple_of(t * nb, nb)
        br = st_ref[pl.ds(r0, nb), 0:n]
        bi = st_ref[pl.ds(r0, nb), n:2 * n]
        nr = (ar * hr - ai * hi) + br
        ni = (ar * hi + ai * hr) + bi
        st_ref[pl.ds(r0, nb), 0:n] = nr
        st_ref[pl.ds(r0, nb), n:2 * n] = ni
        return nr, ni

    hr, hi = lax.fori_loop(0, tc, step, (hre_ref[...], him_ref[...]))
    hre_ref[...] = hr
    him_ref[...] = hi
    y = jnp.dot(st_ref[...].astype(BF16), cblk_ref[...], preferred_element_type=F32)
    z = _gelu_tanh(y + d_ref[...] * u)
    gate = jax.nn.sigmoid(jnp.dot(z.astype(BF16), gw_ref[...], preferred_element_type=F32)
                          + gb_ref[...])
    o_ref[...] = z * gate


def _s5(u_tm, a_re, a_im, bblk, cblk, dskip, gw, gb, *, nb, tc):
    rows, sw = u_tm.shape
    n = a_re.shape[1]
    rb = tc * nb
    full = lambda a: pl.BlockSpec(a.shape, lambda i: (0,) * a.ndim)
    return pl.pallas_call(
        functools.partial(_s5_kernel, nb=nb, n=n),
        grid=(rows // rb,),
        in_specs=[pl.BlockSpec((rb, sw), lambda i: (i, 0)), full(a_re), full(a_im),
                  full(bblk), full(cblk), full(dskip), full(gw), full(gb)],
        out_specs=pl.BlockSpec((rb, sw), lambda i: (i, 0)),
        out_shape=jax.ShapeDtypeStruct((rows, sw), F32),
        scratch_shapes=[pltpu.VMEM((rb, 2 * n), F32), pltpu.VMEM((nb, n), F32),
                        pltpu.VMEM((nb, n), F32)],
        compiler_params=_params(("arbitrary",), VMEM_LIMIT_BYTES),
        name="s5_scan",
    )(u_tm, a_re, a_im, bblk, cblk, dskip, gw, gb)


def _merge_kernel(x_ref, att_ref, pool_ref, ssm_ref, og_ref, w_ref, o_ref, *, att_w, pool_w):
    og = og_ref[...]
    p1 = att_w + pool_w
    merged = jnp.concatenate([
        _rmsnorm(att_ref[...], og[:, :att_w]),
        _rmsnorm(pool_ref[...], og[:, att_w:p1]),
        _rmsnorm(ssm_ref[...], og[:, p1:])], axis=1).astype(BF16)
    o_ref[...] = x_ref[...] + jnp.dot(merged, w_ref[...], preferred_element_type=F32)


def _merge(x, att, pool, ssm_tm, og, w, *, ts):
    b, s, d = x.shape
    att_w, pool_w = att.shape[2], pool.shape[2]
    sw = ssm_tm.shape[1] // b
    tok = lambda wd: pl.BlockSpec((None, ts, wd), lambda bi, i: (bi, i, 0))
    return pl.pallas_call(
        functools.partial(_merge_kernel, att_w=att_w, pool_w=pool_w),
        grid=(b, s // ts),
        in_specs=[tok(d), tok(att_w), tok(pool_w),
                  pl.BlockSpec((ts, sw), lambda bi, i: (i, bi)),
                  pl.BlockSpec(og.shape, lambda bi, i: (0, 0)),
                  pl.BlockSpec(w.shape, lambda bi, i: (0, 0))],
        out_specs=tok(d),
        out_shape=jax.ShapeDtypeStruct((b, s, d), F32),
        compiler_params=_params(("arbitrary", "arbitrary"), VMEM_LIMIT_BYTES),
        name="merge_outproj",
    )(x, att, pool, ssm_tm, og, w)


def _topk_rows(a, k, extra=None):
    nrow, nl = a.shape
    row = lax.broadcasted_iota(I32, (nrow, nl), 0).astype(F32)
    out_row = lax.broadcasted_iota(I32, (k, nl), 0)
    vals = jnp.zeros((k, nl), F32)
    sel = jnp.zeros((k, nl), F32)
    for j in range(k):
        m = jnp.max(a, axis=0, keepdims=True)
        ix = jnp.min(jnp.where(a == m, row, float(nrow)), axis=0, keepdims=True)
        hit = row == ix
        if extra is None:
            pick = ix
        else:
            pick = jnp.sum(jnp.where(hit, extra, 0.0), axis=0, keepdims=True)
        vals = jnp.where(out_row == j, m, vals)
        sel = jnp.where(out_row == j, pick, sel)
        a = jnp.where(hit, -jnp.inf, a)
    return vals, sel


def _route_kernel(x_ref, g_ref, wq_ref, keys_ref, h_out, off_out, sh_out, gate_out, q_sc,
                  *, n_half):
    hd = pl.program_id(1)
    qd = 2 * PEER_KEYS

    @pl.when(hd == 0)
    def _():
        h = _rmsnorm(x_ref[...], g_ref[...])
        h_out[...] = h
        q_sc[...] = jnp.dot(h.astype(BF16), wq_ref[...], preferred_element_type=F32)

    q = q_sc[:, pl.ds(pl.multiple_of(hd * qd, qd), qd)].astype(BF16)
    tops = []
    for i in range(2):
        sc = lax.dot_general(keys_ref[i], q[:, i * PEER_KEYS:(i + 1) * PEER_KEYS], _NT,
                             preferred_element_type=F32)
        tops.append(_topk_rows(sc, PEER_TOPK))
    (s1, i1), (s2, i2) = tops
    k = PEER_TOPK
    sub8 = lax.broadcasted_iota(I32, (SUBLANES, s1.shape[1]), 0)
    cands, ecands = [], []
    a = 0
    while a < k and k // (a + 1) > 1:
        nb = k // (a + 1)
        for b0 in range(0, nb, SUBLANES):
            c = s1[a:a + 1, :] + s2[b0:b0 + SUBLANES, :]
            if b0 + SUBLANES > nb:
                c = jnp.where(sub8 < nb - b0, c, -jnp.inf)
            cands.append(c)
            ecands.append(i1[a:a + 1, :] * float(PEER_KEYS) + i2[b0:b0 + SUBLANES, :])
        a += 1
    cands.append(s1[a:, :] + s2[0:1, :])
    ecands.append(i1[a:, :] * float(PEER_KEYS) + i2[0:1, :])
    best, experts = _topk_rows(jnp.concatenate(cands, axis=0), k,
                               extra=jnp.concatenate(ecands, axis=0))
    w = jnp.exp(best - best[0:1, :])
    gate_out[...] = w / jnp.sum(w, axis=0, keepdims=True)
    e = experts.astype(I32)
    hi_half = e < n_half
    off_out[...] = jnp.where(hi_half, e, e - n_half) * SUBLANES
    sh_out[...] = jnp.where(hi_half, 0.0, 16.0)


def _route(x, g, wq, keys, *, tm, n_experts):
    t, d = x.shape
    n_heads = keys.shape[0]
    hk = pl.BlockSpec((None, PEER_TOPK, tm), lambda i, h: (h, 0, i))
    shp = lambda dt: jax.ShapeDtypeStruct((n_heads, PEER_TOPK, t), dt)
    return pl.pallas_call(
        functools.partial(_route_kernel, n_half=n_experts // 2),
        grid=(t // tm, n_heads),
        in_specs=[pl.BlockSpec((tm, d), lambda i, h: (i, 0)),
                  pl.BlockSpec(g.shape, lambda i, h: (0, 0)),
                  pl.BlockSpec(wq.shape, lambda i, h: (0, 0)),
                  pl.BlockSpec((None, 2, PEER_KEYS, keys.shape[3]), lambda i, h: (h, 0, 0, 0))],
        out_specs=[pl.BlockSpec((tm, d), lambda i, h: (i, 0)), hk, hk, hk],
        out_shape=[jax.ShapeDtypeStruct((t, d), F32), shp(I32), shp(F32), shp(F32)],
        scratch_shapes=[pltpu.VMEM((tm, wq.shape[1]), F32)],
        compiler_params=_params(("arbitrary", "arbitrary"), VMEM_LIMIT_BYTES),
        name="peer_route",
    )(x, g, wq, keys)


def _expert_row(tab_ref, off, sh):
    w = tab_ref[pl.ds(pl.multiple_of(off, SUBLANES), SUBLANES), :]
    return lax.bitcast_convert_type((w << sh) & jnp.uint32(0xFFFF0000), F32)


def _fold_rows(vs, sub, rot=0):
    m = len(vs)
    if m == 1:
        return vs[0]
    span = SUBLANES // m
    c1 = _fold_rows(vs[0::2], sub, rot - span)
    c2 = _fold_rows(vs[1::2], sub, rot)
    own = ((sub - rot) & (2 * span - 1)) < span
    return jnp.where(own, c1, c2) + pltpu.roll(jnp.where(own, c2, c1), span, 0)


def _splat_rows(mxu_src, mxu_dst, xlu_src=None, xlu_dst=None, *, tt, hk, also=None):
    eye = (lax.broadcasted_iota(I32, (hk, hk), 0) == lax.broadcasted_iota(I32, (hk, hk), 1))
    ones = jnp.ones((hk, hk), BF16)

    def body(t, carry):
        rows = pl.ds(pl.multiple_of(t * hk, hk), hk)
        v = jnp.broadcast_to(mxu_src[pl.ds(t, 1), :], (hk, hk))
        rep = jnp.dot(jnp.where(eye, v, 0.0).astype(BF16), ones, preferred_element_type=F32)
        mxu_dst[rows, :] = rep.astype(mxu_dst.dtype)
        if xlu_src is not None:
            xlu_dst[rows, :] = jnp.broadcast_to(xlu_src[pl.ds(t, 1), :], (hk, hk)).T
        if also is not None:
            also(t)
        return carry

    lax.fori_loop(0, tt, body, 0, unroll=8)


def _splat(rep_ref, row):
    return jnp.broadcast_to(rep_ref[pl.ds(row, 1), :], (SUBLANES, LANES))


def _peer_u_kernel(*refs, tt, hk):
    off_refs = refs[:SUBLANES]
    sh_ref, shn_ref, h_ref, gate_ref, tab_ref, coef_ref, part_ref, srep_ref, pre_ref = refs[SUBLANES:]
    sub = lax.broadcasted_iota(I32, (SUBLANES, LANES), 0)
    ngrp = hk // SUBLANES

    @pl.when(pl.program_id(0) == 0)
    def _():
        _splat_rows(sh_ref, srep_ref, tt=tt, hk=hk)

    def tok(t, carry):
        h = h_ref[t]
        base = pl.multiple_of(t * hk, hk)
        for g in range(ngrp):
            e0 = base + g * SUBLANES
            ps = [_expert_row(tab_ref, off_refs[j][t * ngrp + g],
                              _splat(srep_ref, e0 + j).astype(U32)) * h
                  for j in range(SUBLANES)]
            part_ref[pl.ds(e0, SUBLANES), :] = _fold_rows(ps, sub)
        return carry

    lax.fori_loop(0, tt, tok, 0)

    def lane_sums(t):
        p = part_ref[pl.ds(pl.multiple_of(t * hk, hk), hk), :]
        pre_ref[pl.ds(t, 1), :] = jnp.sum(p.T, axis=0, keepdims=True)

    _splat_rows(shn_ref, srep_ref, tt=tt, hk=hk, also=lane_sums)
    coef_ref[...] = gate_ref[...] * _gelu_tanh(pre_ref[...])


def _peer_v_kernel(*refs, tt, hk, n_acc):
    off_refs = refs[:SUBLANES]
    sh_ref, coef_ref, x_ref, tab_ref, o_ref, crep_ref, srep_ref = refs[SUBLANES:]
    ngrp = hk // SUBLANES
    _splat_rows(sh_ref, srep_ref, coef_ref, crep_ref, tt=tt, hk=hk)

    def tok(t, carry):
        accs = [None] * n_acc
        e0 = t * hk
        for k in range(hk):
            off = off_refs[k % SUBLANES][t * ngrp + k // SUBLANES]
            term = _splat(crep_ref, e0 + k) * _expert_row(tab_ref, off,
                                                          _splat(srep_ref, e0 + k).astype(U32))
            accs[k % n_acc] = term if accs[k % n_acc] is None else accs[k % n_acc] + term
        while len(accs) > 1:
            accs = [accs[j] + accs[j + len(accs) // 2] for j in range(len(accs) // 2)]
        o_ref[t] = x_ref[t] + accs[0]
        return carry

    lax.fori_loop(0, tt, tok, 0)


def _resident(a):
    return pl.BlockSpec(a.shape, lambda i: (0,) * a.ndim, pipeline_mode=pl.Buffered(1))


def _split_offsets(off):
    t, hk = off.shape
    grouped = off.reshape(t, hk // SUBLANES, SUBLANES)
    return [grouped[:, :, j].reshape(-1) for j in range(SUBLANES)]


def _peer_u(off, sh, h, gates, tab, *, tt):
    t, hk = off.shape
    assert hk == LANES and h.shape[1:] == (SUBLANES, LANES)
    smem = pl.BlockSpec((tt * hk // SUBLANES,), lambda i: (i,), memory_space=pltpu.SMEM)
    compact = pl.BlockSpec((tt, hk), lambda i: (i, 0))
    last = t // tt - 1
    nxt = pl.BlockSpec((tt, hk), lambda i: (jnp.minimum(i + 1, last), 0))
    return pl.pallas_call(
        functools.partial(_peer_u_kernel, tt=tt, hk=hk),
        grid=(t // tt,),
        in_specs=[smem] * SUBLANES + [
            compact, nxt, pl.BlockSpec((tt, SUBLANES, LANES), lambda i: (i, 0, 0)), compact,
            _resident(tab)],
        out_specs=compact,
        out_shape=jax.ShapeDtypeStruct((t, hk), F32),
        scratch_shapes=[pltpu.VMEM((tt * hk, LANES), F32), pltpu.VMEM((tt * hk, LANES), I32),
                        pltpu.VMEM((tt, hk), F32)],
        compiler_params=_params(("arbitrary",), VMEM_LIMIT_BYTES),
        name="peer_u",
    )(*_split_offsets(off), sh, sh, h, gates, tab)


def _peer_v(off, sh, coef, x, tab, *, tt):
    t, hk = off.shape
    assert hk == LANES and x.shape[1:] == (SUBLANES, LANES)
    smem = pl.BlockSpec((tt * hk // SUBLANES,), lambda i: (i,), memory_space=pltpu.SMEM)
    tile = pl.BlockSpec((tt, SUBLANES, LANES), lambda i: (i, 0, 0))
    compact = pl.BlockSpec((tt, hk), lambda i: (i, 0))
    return pl.pallas_call(
        functools.partial(_peer_v_kernel, tt=tt, hk=hk, n_acc=4),
        grid=(t // tt,),
        in_specs=[smem] * SUBLANES + [compact, compact, tile, _resident(tab)],
        out_specs=tile,
        out_shape=jax.ShapeDtypeStruct(x.shape, F32),
        scratch_shapes=[pltpu.VMEM((tt * hk, LANES), F32), pltpu.VMEM((tt * hk, LANES), I32)],
        compiler_params=_params(("arbitrary",), VMEM_LIMIT_BYTES),
        name="peer_v",
    )(*_split_offsets(off), sh, coef, x, tab)


def _pack_table(tab):
    n, d = tab.shape
    b = lax.bitcast_convert_type(tab.astype(BF16), jnp.uint16).astype(U32)
    w = (b[:n // 2] << 16) | b[n // 2:]
    return w.reshape(n // 2 * (d // LANES), LANES)


def _block_diag(blocks):
    g, r, c = blocks.shape
    eye = jnp.eye(g, dtype=blocks.dtype)
    return (eye[:, None, :, None] * blocks[:, :, None, :]).reshape(g * r, g * c)


def _s5_params(a_re, a_im, log_dt, b_re, b_im, c_re, c_im):
    a = lax.complex(a_re, a_im)
    dt = jnp.exp(log_dt)[:, None]
    a_bar = jnp.exp(a * dt)
    b_bar = ((a_bar - 1.0) / a)[..., None] * lax.complex(b_re, b_im)
    bt = jnp.swapaxes(b_bar, 1, 2)
    bblk = jnp.concatenate([_block_diag(jnp.real(bt)), _block_diag(jnp.imag(bt))], axis=1)
    ct_re = jnp.swapaxes(c_re, 1, 2)
    ct_im = jnp.swapaxes(c_im, 1, 2)
    cblk = jnp.concatenate([_block_diag(ct_re), -_block_diag(ct_im)], axis=0)
    n = a_bar.size
    return (jnp.real(a_bar).reshape(1, n), jnp.imag(a_bar).reshape(1, n),
            bblk.astype(BF16), cblk.astype(BF16))


def _tiles(s, t):
    ts = min(512, s)
    blk = min(512, s)
    tc = min(64, s)
    tm = min(512, t)
    tt = min(128, t)
    return ts, blk, tc, tm, tt


def _layer(x, norm1_g, w_in, f_bias, q_gain, k_gain, pool_w, pool_scale,
           a_re, a_im, log_dt, b_re, b_im, c_re, c_im, ssm_d, glu_w, glu_b,
           out_norm_g, w_out, norm2_g, peer_wq, peer_keys, peer_u, peer_v):
    b, s, d = x.shape
    t = b * s
    n_heads = f_bias.shape[0]
    att_w = n_heads * HEAD_DIM
    pool_width = pool_scale.shape[0]
    ssm_w = ssm_d.shape[0]
    ts, blk, tc, tm, tt = _tiles(s, t)
    row = lambda v: v.reshape(1, -1)

    c0, c1 = 3 * att_w, 3 * att_w + n_heads
    wf = jnp.zeros((d, LANES), F32).at[:, :n_heads].set(w_in[:, c0:c1])
    fb = jnp.zeros((1, LANES), F32).at[0, :n_heads].set(f_bias)
    seg = _block_diag(jnp.full((n_heads, HEAD_DIM, HEAD_DIM), 1.0 / HEAD_DIM, F32)).astype(BF16)
    q, k, v, c_tok, u_pool, u_ssm = _inproj(
        x, row(norm1_g), w_in[:, :c0].astype(BF16), wf,
        w_in[:, c1:c1 + pool_width].astype(BF16), w_in[:, c1 + pool_width:].astype(BF16), fb,
        row(jnp.tile(q_gain, n_heads) * HEAD_DIM ** -0.5), row(jnp.tile(k_gain, n_heads)),
        seg, ts=ts)
    att = _attention(q, k, v, c_tok, jnp.swapaxes(c_tok, 1, 2), blk=blk)
    pool = _pool(u_pool, _block_diag(pool_w).astype(BF16), row(pool_scale), ts=ts)
    are, aim, bblk, cblk = _s5_params(a_re, a_im, log_dt, b_re, b_im, c_re, c_im)
    ssm = _s5(u_ssm.reshape(s * b, ssm_w), are, aim, bblk, cblk, row(ssm_d),
              glu_w.astype(BF16), row(glu_b), nb=b, tc=tc)
    x1 = _merge(x, att, pool, ssm.reshape(s, b * ssm_w), row(out_norm_g),
                w_out.astype(BF16), ts=ts)

    n_experts = peer_u.shape[0]
    hk = peer_keys.shape[0] * PEER_TOPK
    kh = peer_keys.astype(BF16)
    h2, off, sh, gates = _route(x1.reshape(t, d), row(norm2_g), peer_wq.astype(BF16), kh,
                                tm=tm, n_experts=n_experts)
    tok_major = lambda a: a.reshape(hk, t).T
    off, sh, gates = tok_major(off), tok_major(sh), tok_major(gates)
    tiled = lambda a: a.reshape(t, SUBLANES, d // SUBLANES)
    coef = _peer_u(off, sh, tiled(h2), gates, _pack_table(peer_u), tt=tt)
    x2 = _peer_v(off, sh, coef, tiled(x1), _pack_table(peer_v), tt=tt)
    return x2.reshape(b, s, d)


def kernel(x, norm1_g, w_in, f_bias, q_gain, k_gain, pool_w, pool_scale, ssm_a_re, ssm_a_im,
           ssm_log_dt, ssm_b_re, ssm_b_im, ssm_c_re, ssm_c_im, ssm_d, glu_w, glu_b, out_norm_g,
           w_out, norm2_g, peer_wq, peer_keys, peer_u, peer_v):
    per_layer = (norm1_g, w_in, f_bias, q_gain, k_gain, pool_w, pool_scale, ssm_a_re, ssm_a_im,
                 ssm_log_dt, ssm_b_re, ssm_b_im, ssm_c_re, ssm_c_im, ssm_d, glu_w, glu_b,
                 out_norm_g, w_out, norm2_g, peer_wq, peer_keys, peer_u, peer_v)
    for l in range(norm1_g.shape[0]):
        x = _layer(x, *[p[l] for p in per_layer])
    return x
```

```python
import functools
import math

import jax
import jax.numpy as jnp
from jax import lax
from jax.experimental import pallas as pl
from jax.experimental.pallas import tpu as pltpu

F32 = jnp.float32
BF16 = jnp.bfloat16
U32 = jnp.uint32
I32 = jnp.int32

EPS = 1e-6
HEAD_DIM = 64
POOL_WINDOWS = (2, 4, 8, 16)
POOL_HALO = 16
SSM_STATE = 64
SSM_GROUP_CH = 16
PEER_KEYS = 128
PEER_TOPK = 16

LANES = 128
SUBLANES = 8
VMEM_LIMIT_BYTES = 56 * 1024 * 1024

_NT = (((1,), (1,)), ((), ()))


def _rmsnorm(x, g):
    return x * lax.rsqrt(jnp.mean(x * x, axis=-1, keepdims=True) + EPS) * g


def _gelu_tanh(x):
    c = math.sqrt(2.0 / math.pi)
    return x * (0.5 * (1.0 + jnp.tanh(c * (x + 0.044715 * (x * x * x)))))


def _params(sem, vmem=None):
    return pltpu.CompilerParams(dimension_semantics=sem, vmem_limit_bytes=vmem)


def _load_tiled(ref, n):
    return jnp.concatenate([ref[pl.ds(s, n, stride=SUBLANES), :] for s in range(SUBLANES)], axis=1)


def _store_tiled(ref, val, n):
    for s in range(SUBLANES):
        ref[pl.ds(s, n, stride=SUBLANES), :] = val[:, s * LANES:(s + 1) * LANES]


def _inproj_kernel(x_ref, g_ref, wqkv_ref, wf_ref, wpool_ref, wssm_ref, fb_ref,
                   qg_ref, kg_ref, seg_ref,
                   q_out, k_out, v_out, c_out, pool_out, ssm_out, carry_ref,
                   *, att_w, n_heads):
    i = pl.program_id(1)

    @pl.when(i == 0)
    def _():
        carry_ref[...] = jnp.zeros_like(carry_ref)

    ts = x_ref.shape[0] // SUBLANES
    x = _load_tiled(x_ref, ts)
    h = _rmsnorm(x, g_ref[...])
    hb = h.astype(BF16)
    qkv = jnp.dot(hb, wqkv_ref[...], preferred_element_type=F32)
    seg = seg_ref[...]

    def headnorm(t, gain):
        ms = jnp.dot((t * t).astype(BF16), seg, preferred_element_type=F32)
        return t * lax.rsqrt(ms + EPS) * gain

    q_out[...] = headnorm(qkv[:, :att_w], qg_ref[...]).astype(BF16)
    k_out[...] = headnorm(qkv[:, att_w:2 * att_w], kg_ref[...]).astype(BF16)
    v_out[...] = qkv[:, 2 * att_w:].astype(BF16)
    pool_out[...] = jnp.dot(hb, wpool_ref[...], preferred_element_type=F32)
    ssm_out[...] = jnp.dot(hb, wssm_ref[...], preferred_element_type=F32)

    f = jnp.dot(h, wf_ref[...], precision=lax.Precision.HIGHEST,
                preferred_element_type=F32) + fb_ref[...]
    ls = jnp.minimum(f, 0.0) - jnp.log1p(jnp.exp(-jnp.abs(f)))
    r = lax.broadcasted_iota(I32, (ts, ts), 0)
    c = lax.broadcasted_iota(I32, (ts, ts), 1)
    tri = jnp.where(c <= r, 1.0, 0.0).astype(F32)
    cs = jnp.dot(tri, ls, precision=lax.Precision.HIGHEST,
                 preferred_element_type=F32) + carry_ref[0:1, :]
    carry_ref[...] = jnp.broadcast_to(cs[ts - 1:ts, :], carry_ref.shape)
    c_out[...] = cs[:, :n_heads]


def _tiled_spec(ts):
    return pl.BlockSpec((None, ts * SUBLANES, LANES), lambda bi, i: (bi, i, 0))


def _inproj(x, g, wqkv, wf, wpool, wssm, fb, qg, kg, seg, *, ts):
    b, s = x.shape[0], x.shape[1] // SUBLANES
    att_w = seg.shape[0]
    n_heads = att_w // HEAD_DIM
    pw = wpool.shape[1]
    sw = wssm.shape[1]
    full = lambda a: pl.BlockSpec(a.shape, lambda bi, i: (0,) * a.ndim)
    tok = lambda w: pl.BlockSpec((None, ts, w), lambda bi, i: (bi, i, 0))
    return pl.pallas_call(
        functools.partial(_inproj_kernel, att_w=att_w, n_heads=n_heads),
        grid=(b, s // ts),
        in_specs=[_tiled_spec(ts), full(g), full(wqkv), full(wf), full(wpool), full(wssm),
                  full(fb), full(qg), full(kg), full(seg)],
        out_specs=[tok(att_w), tok(att_w), tok(att_w), tok(n_heads), tok(pw),
                   pl.BlockSpec((ts, sw), lambda bi, i: (i, bi))],
        out_shape=[jax.ShapeDtypeStruct((b, s, att_w), BF16),
                   jax.ShapeDtypeStruct((b, s, att_w), BF16),
                   jax.ShapeDtypeStruct((b, s, att_w), BF16),
                   jax.ShapeDtypeStruct((b, s, n_heads), F32),
                   jax.ShapeDtypeStruct((b, s, pw), F32),
                   jax.ShapeDtypeStruct((s, b * sw), F32)],
        scratch_shapes=[pltpu.VMEM((SUBLANES, LANES), F32)],
        compiler_params=_params(("arbitrary", "arbitrary"), VMEM_LIMIT_BYTES),
        name="inproj",
    )(x, g, wqkv, wf, wpool, wssm, fb, qg, kg, seg)


def _attn_kernel(q_ref, k_ref, v_ref, ct_ref, cr_ref, o_ref, m_sc, l_sc, acc_sc, s_sc, *, blk):
    p = pl.program_id(1)
    i = pl.program_id(2)
    q = q_ref[...]
    lane = lax.broadcasted_iota(I32, (blk, LANES), 1)
    ct = ct_ref[...]
    hl = lax.broadcasted_iota(I32, ct.shape, 1)
    heads = (2 * p, 2 * p + 1)
    qm = (jnp.where(lane < HEAD_DIM, q, jnp.zeros_like(q)),
          jnp.where(lane >= HEAD_DIM, q, jnp.zeros_like(q)))
    cq = tuple(jnp.sum(jnp.where(hl == hd, ct, 0.0), axis=1, keepdims=True) for hd in heads)
    m_sc[...] = jnp.full(m_sc.shape, -jnp.inf, F32)
    l_sc[...] = jnp.zeros(l_sc.shape, F32)
    acc_sc[...] = jnp.zeros(acc_sc.shape, F32)

    def qk(kb):
        kblk = k_ref[pl.ds(pl.multiple_of(kb * blk, blk), blk), :]
        return [lax.dot_general(qm[hh], kblk, _NT, preferred_element_type=F32) for hh in range(2)]

    def step(kb, s_cur, causal):
        start = pl.multiple_of(kb * blk, blk)
        vblk = v_ref[pl.ds(start, blk), :]
        for hh in range(2):
            ck = cr_ref[pl.ds(heads[hh], 1), pl.ds(start, blk)]
            z = (s_cur[hh] + cq[hh]) - ck
            if causal:
                row = lax.broadcasted_iota(I32, (blk, blk), 0)
                col = lax.broadcasted_iota(I32, (blk, blk), 1)
                z = jnp.where(col <= row, z, -jnp.inf)
            m_old = m_sc[hh]
            m_new = jnp.maximum(m_old, jnp.max(z, axis=1, keepdims=True))
            alpha = jnp.exp(m_old - m_new)
            pe = jnp.exp(z - m_new)
            l_sc[hh] = alpha * l_sc[hh] + jnp.sum(pe, axis=1, keepdims=True)
            acc_sc[hh] = alpha * acc_sc[hh] + jnp.dot(pe.astype(BF16), vblk,
                                                      preferred_element_type=F32)
            m_sc[hh] = m_new

    s0 = qk(0)
    s_sc[0] = s0[0]
    s_sc[1] = s0[1]

    def body(kb, carry):
        s_cur = [s_sc[0], s_sc[1]]
        s_next = qk(kb + 1)
        step(kb, s_cur, False)
        s_sc[0] = s_next[0]
        s_sc[1] = s_next[1]
        return carry

    lax.fori_loop(0, i, body, 0)
    step(i, [s_sc[0], s_sc[1]], True)
    o_ref[...] = jnp.where(lane < HEAD_DIM, acc_sc[0] / l_sc[0], acc_sc[1] / l_sc[1])


def _attention(q, k, v, c_tok, c_row, *, blk):
    b, s, w = q.shape
    n_heads = c_tok.shape[2]
    return pl.pallas_call(
        functools.partial(_attn_kernel, blk=blk),
        grid=(b, w // LANES, s // blk),
        in_specs=[pl.BlockSpec((None, blk, LANES), lambda bi, p, i: (bi, i, p)),
                  pl.BlockSpec((None, s, LANES), lambda bi, p, i: (bi, 0, p)),
                  pl.BlockSpec((None, s, LANES), lambda bi, p, i: (bi, 0, p)),
                  pl.BlockSpec((None, blk, n_heads), lambda bi, p, i: (bi, i, 0)),
                  pl.BlockSpec((None, n_heads, s), lambda bi, p, i: (bi, 0, 0))],
        out_specs=pl.BlockSpec((None, blk, LANES), lambda bi, p, i: (bi, i, p)),
        out_shape=jax.ShapeDtypeStruct((b, s, w), F32),
        scratch_shapes=[pltpu.VMEM((2, blk, 1), F32), pltpu.VMEM((2, blk, 1), F32),
                        pltpu.VMEM((2, blk, LANES), F32), pltpu.VMEM((2, blk, blk), F32)],
        compiler_params=_params(("arbitrary", "arbitrary", "arbitrary"), VMEM_LIMIT_BYTES),
        name="fox_attention",
    )(q, k, v, c_tok, c_row)


def _pool_kernel(u_ref, prev_ref, w_ref, sc_ref, o_ref, *, group_dim):
    i = pl.program_id(1)
    cur = u_ref[...]
    ts, pw = cur.shape
    prev = jnp.where(i > 0, prev_ref[...], 0.0)
    ext = jnp.concatenate([prev, cur], axis=0)
    pos = i * ts + lax.broadcasted_iota(I32, (ts, 1), 0)
    grp = lax.broadcasted_iota(I32, (ts, pw), 1) // group_dim
    acc = cur
    pooled = jnp.zeros_like(cur)
    for j in range(1, max(POOL_WINDOWS)):
        acc = acc + ext[POOL_HALO - j:POOL_HALO - j + ts, :]
        win = j + 1
        if win in POOL_WINDOWS:
            cnt = jnp.minimum(pos + 1, win).astype(F32)
            pooled = jnp.where(grp == POOL_WINDOWS.index(win), acc / cnt, pooled)
    pooled = pooled - cur
    o_ref[...] = jnp.dot(pooled.astype(BF16), w_ref[...],
                         preferred_element_type=F32) * sc_ref[...]


def _pool(u, wblk, scale, *, ts):
    b, s, pw = u.shape
    hb = ts // POOL_HALO
    return pl.pallas_call(
        functools.partial(_pool_kernel, group_dim=pw // len(POOL_WINDOWS)),
        grid=(b, s // ts),
        in_specs=[pl.BlockSpec((None, ts, pw), lambda bi, i: (bi, i, 0)),
                  pl.BlockSpec((None, POOL_HALO, pw),
                               lambda bi, i: (bi, jnp.maximum(i * hb - 1, 0), 0)),
                  pl.BlockSpec(wblk.shape, lambda bi, i: (0, 0)),
                  pl.BlockSpec(scale.shape, lambda bi, i: (0, 0))],
        out_specs=pl.BlockSpec((None, ts, pw), lambda bi, i: (bi, i, 0)),
        out_shape=jax.ShapeDtypeStruct((b, s, pw), F32),
        compiler_params=_params(("arbitrary", "arbitrary")),
        name="pool",
    )(u, u, wblk, scale)


def _s5_kernel(u_ref, are_ref, aim_ref, bblk_ref, cblk_ref, d_ref, gw_ref, gb_ref,
               o_ref, st_ref, hre_ref, him_ref, *, nb, n):
    @pl.when(pl.program_id(0) == 0)
    def _():
        hre_ref[...] = jnp.zeros_like(hre_ref)
        him_ref[...] = jnp.zeros_like(him_ref)

    u = u_ref[...]
    tc = u.shape[0] // nb
    st_ref[...] = jnp.dot(u.astype(BF16), bblk_ref[...], preferred_element_type=F32)
    ar = jnp.broadcast_to(are_ref[...], (nb, n))
    ai = jnp.broadcast_to(aim_ref[...], (nb, n))

    def step(t, carry):
        hr, hi = carry
        r0 = pl.multiple_of(t * nb, nb)
        br = st_ref[pl.ds(r0, nb), 0:n]
        bi = st_ref[pl.ds(r0, nb), n:2 * n]
        nr = (ar * hr - ai * hi) + br
        ni = (ar * hi + ai * hr) + bi
        st_ref[pl.ds(r0, nb), 0:n] = nr
        st_ref[pl.ds(r0, nb), n:2 * n] = ni
        return nr, ni

    hr, hi = lax.fori_loop(0, tc, step, (hre_ref[...], him_ref[...]))
    hre_ref[...] = hr
    him_ref[...] = hi
    y = jnp.dot(st_ref[...].astype(BF16), cblk_ref[...], preferred_element_type=F32)
    z = _gelu_tanh(y + d_ref[...] * u)
    gate = jax.nn.sigmoid(jnp.dot(z.astype(BF16), gw_ref[...], preferred_element_type=F32)
                          + gb_ref[...])
    o_ref[...] = z * gate


def _s5(u_tm, a_re, a_im, bblk, cblk, dskip, gw, gb, *, nb, tc):
    rows, sw = u_tm.shape
    n = a_re.shape[1]
    rb = tc * nb
    full = lambda a: pl.BlockSpec(a.shape, lambda i: (0,) * a.ndim)
    return pl.pallas_call(
        functools.partial(_s5_kernel, nb=nb, n=n),
        grid=(rows // rb,),
        in_specs=[pl.BlockSpec((rb, sw), lambda i: (i, 0)), full(a_re), full(a_im),
                  full(bblk), full(cblk), full(dskip), full(gw), full(gb)],
        out_specs=pl.BlockSpec((rb, sw), lambda i: (i, 0)),
        out_shape=jax.ShapeDtypeStruct((rows, sw), F32),
        scratch_shapes=[pltpu.VMEM((rb, 2 * n), F32), pltpu.VMEM((nb, n), F32),
                        pltpu.VMEM((nb, n), F32)],
        compiler_params=_params(("arbitrary",), VMEM_LIMIT_BYTES),
        name="s5_scan",
    )(u_tm, a_re, a_im, bblk, cblk, dskip, gw, gb)


def _merge_kernel(x_ref, att_ref, pool_ref, ssm_ref, og_ref, w_ref, o_ref, *, att_w, pool_w):
    og = og_ref[...]
    p1 = att_w + pool_w
    merged = jnp.concatenate([
        _rmsnorm(att_ref[...], og[:, :att_w]),
        _rmsnorm(pool_ref[...], og[:, att_w:p1]),
        _rmsnorm(ssm_ref[...], og[:, p1:])], axis=1).astype(BF16)
    ts = merged.shape[0]
    out = _load_tiled(x_ref, ts) + jnp.dot(merged, w_ref[...], preferred_element_type=F32)
    _store_tiled(o_ref, out, ts)


def _merge(x, att, pool, ssm_tm, og, w, *, ts):
    b, s = x.shape[0], x.shape[1] // SUBLANES
    att_w, pool_w = att.shape[2], pool.shape[2]
    sw = ssm_tm.shape[1] // b
    tok = lambda wd: pl.BlockSpec((None, ts, wd), lambda bi, i: (bi, i, 0))
    return pl.pallas_call(
        functools.partial(_merge_kernel, att_w=att_w, pool_w=pool_w),
        grid=(b, s // ts),
        in_specs=[_tiled_spec(ts), tok(att_w), tok(pool_w),
                  pl.BlockSpec((ts, sw), lambda bi, i: (i, bi)),
                  pl.BlockSpec(og.shape, lambda bi, i: (0, 0)),
                  pl.BlockSpec(w.shape, lambda bi, i: (0, 0))],
        out_specs=_tiled_spec(ts),
        out_shape=jax.ShapeDtypeStruct(x.shape, F32),
        compiler_params=_params(("arbitrary", "arbitrary"), VMEM_LIMIT_BYTES),
        name="merge_outproj",
    )(x, att, pool, ssm_tm, og, w)


def _topk_rows(a, k, extra=None):
    nrow, nl = a.shape
    row = lax.broadcasted_iota(I32, (nrow, nl), 0).astype(F32)
    out_row = lax.broadcasted_iota(I32, (k, nl), 0)
    vals = jnp.zeros((k, nl), F32)
    sel = jnp.zeros((k, nl), F32)
    for j in range(k):
        m = jnp.max(a, axis=0, keepdims=True)
        ix = jnp.min(jnp.where(a == m, row, float(nrow)), axis=0, keepdims=True)
        hit = row == ix
        if extra is None:
            pick = ix
        else:
            pick = jnp.sum(jnp.where(hit, extra, 0.0), axis=0, keepdims=True)
        vals = jnp.where(out_row == j, m, vals)
        sel = jnp.where(out_row == j, pick, sel)
        a = jnp.where(hit, -jnp.inf, a)
    return vals, sel


def _route_kernel(x_ref, g_ref, wq_ref, keys_ref, h_out, off_out, sh_out, gate_out, q_sc,
                  *, n_half):
    hd = pl.program_id(1)
    qd = 2 * PEER_KEYS

    @pl.when(hd == 0)
    def _():
        tm = q_sc.shape[0]
        h = _rmsnorm(_load_tiled(x_ref, tm), g_ref[...])
        _store_tiled(h_out, h, tm)
        q_sc[...] = jnp.dot(h.astype(BF16), wq_ref[...], preferred_element_type=F32)

    q = q_sc[:, pl.ds(pl.multiple_of(hd * qd, qd), qd)].astype(BF16)
    tops = []
    for i in range(2):
        sc = lax.dot_general(keys_ref[i], q[:, i * PEER_KEYS:(i + 1) * PEER_KEYS], _NT,
                             preferred_element_type=F32)
        tops.append(_topk_rows(sc, PEER_TOPK))
    (s1, i1), (s2, i2) = tops
    k = PEER_TOPK
    sub8 = lax.broadcasted_iota(I32, (SUBLANES, s1.shape[1]), 0)
    cands, ecands = [], []
    a = 0
    while a < k and k // (a + 1) > 1:
        nb = k // (a + 1)
        for b0 in range(0, nb, SUBLANES):
            c = s1[a:a + 1, :] + s2[b0:b0 + SUBLANES, :]
            if b0 + SUBLANES > nb:
                c = jnp.where(sub8 < nb - b0, c, -jnp.inf)
            cands.append(c)
            ecands.append(i1[a:a + 1, :] * float(PEER_KEYS) + i2[b0:b0 + SUBLANES, :])
        a += 1
    cands.append(s1[a:, :] + s2[0:1, :])
    ecands.append(i1[a:, :] * float(PEER_KEYS) + i2[0:1, :])
    best, experts = _topk_rows(jnp.concatenate(cands, axis=0), k,
                               extra=jnp.concatenate(ecands, axis=0))
    w = jnp.exp(best - best[0:1, :])
    gate_out[...] = w / jnp.sum(w, axis=0, keepdims=True)
    e = experts.astype(I32)
    hi_half = e < n_half
    off_out[...] = jnp.where(hi_half, e, e - n_half) * SUBLANES
    sh_out[...] = jnp.where(hi_half, 0.0, 16.0)


def _route(x, g, wq, keys, *, tm, n_experts):
    t = x.shape[0] // SUBLANES
    n_heads = keys.shape[0]
    hk = pl.BlockSpec((None, PEER_TOPK, tm), lambda i, h: (h, 0, i))
    shp = lambda dt: jax.ShapeDtypeStruct((n_heads, PEER_TOPK, t), dt)
    tiled = pl.BlockSpec((tm * SUBLANES, LANES), lambda i, h: (i, 0))
    return pl.pallas_call(
        functools.partial(_route_kernel, n_half=n_experts // 2),
        grid=(t // tm, n_heads),
        in_specs=[tiled,
                  pl.BlockSpec(g.shape, lambda i, h: (0, 0)),
                  pl.BlockSpec(wq.shape, lambda i, h: (0, 0)),
                  pl.BlockSpec((None, 2, PEER_KEYS, keys.shape[3]), lambda i, h: (h, 0, 0, 0))],
        out_specs=[tiled, hk, hk, hk],
        out_shape=[jax.ShapeDtypeStruct(x.shape, F32), shp(I32), shp(F32), shp(F32)],
        scratch_shapes=[pltpu.VMEM((tm, wq.shape[1]), F32)],
        compiler_params=_params(("arbitrary", "arbitrary"), VMEM_LIMIT_BYTES),
        name="peer_route",
    )(x, g, wq, keys)


def _expert_row(tab_ref, off, sh):
    w = tab_ref[pl.ds(pl.multiple_of(off, SUBLANES), SUBLANES), :]
    return lax.bitcast_convert_type((w << sh) & jnp.uint32(0xFFFF0000), F32)


def _fold_rows(vs, sub, rot=0):
    m = len(vs)
    if m == 1:
        return vs[0]
    span = SUBLANES // m
    c1 = _fold_rows(vs[0::2], sub, rot - span)
    c2 = _fold_rows(vs[1::2], sub, rot)
    own = ((sub - rot) & (2 * span - 1)) < span
    return jnp.where(own, c1, c2) + pltpu.roll(jnp.where(own, c2, c1), span, 0)


def _splat_rows(mxu_src, mxu_dst, xlu_src=None, xlu_dst=None, *, tt, hk, also=None):
    eye = (lax.broadcasted_iota(I32, (hk, hk), 0) == lax.broadcasted_iota(I32, (hk, hk), 1))
    ones = jnp.ones((hk, hk), BF16)

    def body(t, carry):
        rows = pl.ds(pl.multiple_of(t * hk, hk), hk)
        v = jnp.broadcast_to(mxu_src[pl.ds(t, 1), :], (hk, hk))
        rep = jnp.dot(jnp.where(eye, v, 0.0).astype(BF16), ones, preferred_element_type=F32)
        mxu_dst[rows, :] = rep.astype(mxu_dst.dtype)
        if xlu_src is not None:
            xlu_dst[rows, :] = jnp.broadcast_to(xlu_src[pl.ds(t, 1), :], (hk, hk)).T
        if also is not None:
            also(t)
        return carry

    lax.fori_loop(0, tt, body, 0, unroll=8)


def _splat(rep_ref, row):
    return jnp.broadcast_to(rep_ref[pl.ds(row, 1), :], (SUBLANES, LANES))


def _peer_u_kernel(*refs, tt, hk):
    n_off = tt * hk // SUBLANES
    off_refs = [refs[0].at[pl.ds(j * n_off, n_off)] for j in range(SUBLANES)]
    sh_ref, shn_ref, h_ref, gate_ref, tab_ref, coef_ref, part_ref, srep_ref, pre_ref = refs[1:]
    sub = lax.broadcasted_iota(I32, (SUBLANES, LANES), 0)
    ngrp = hk // SUBLANES

    @pl.when(pl.program_id(0) == 0)
    def _():
        _splat_rows(sh_ref, srep_ref, tt=tt, hk=hk)

    def tok(t, carry):
        h = h_ref[t]
        base = pl.multiple_of(t * hk, hk)
        for g in range(ngrp):
            e0 = base + g * SUBLANES
            ps = [_expert_row(tab_ref, off_refs[j][t * ngrp + g],
                              _splat(srep_ref, e0 + j).astype(U32)) * h
                  for j in range(SUBLANES)]
            part_ref[pl.ds(e0, SUBLANES), :] = _fold_rows(ps, sub)
        return carry

    lax.fori_loop(0, tt, tok, 0)

    def lane_sums(t):
        p = part_ref[pl.ds(pl.multiple_of(t * hk, hk), hk), :]
        pre_ref[pl.ds(t, 1), :] = jnp.sum(p.T, axis=0, keepdims=True)

    _splat_rows(shn_ref, srep_ref, tt=tt, hk=hk, also=lane_sums)
    coef_ref[...] = gate_ref[...] * _gelu_tanh(pre_ref[...])


def _peer_v_kernel(*refs, tt, hk, n_acc):
    n_off = tt * hk // SUBLANES
    off_refs = [refs[0].at[pl.ds(j * n_off, n_off)] for j in range(SUBLANES)]
    sh_ref, coef_ref, x_ref, tab_ref, o_ref, crep_ref, srep_ref = refs[1:]
    ngrp = hk // SUBLANES
    _splat_rows(sh_ref, srep_ref, coef_ref, crep_ref, tt=tt, hk=hk)

    def tok(t, carry):
        accs = [None] * n_acc
        e0 = t * hk
        for k in range(hk):
            off = off_refs[k % SUBLANES][t * ngrp + k // SUBLANES]
            term = _splat(crep_ref, e0 + k) * _expert_row(tab_ref, off,
                                                          _splat(srep_ref, e0 + k).astype(U32))
            accs[k % n_acc] = term if accs[k % n_acc] is None else accs[k % n_acc] + term
        while len(accs) > 1:
            accs = [accs[j] + accs[j + len(accs) // 2] for j in range(len(accs) // 2)]
        o_ref[t] = x_ref[t] + accs[0]
        return carry

    lax.fori_loop(0, tt, tok, 0)


def _resident(a):
    return pl.BlockSpec(a.shape, lambda i: (0,) * a.ndim, pipeline_mode=pl.Buffered(1))


def _split_offsets(off, tt):
    t, hk = off.shape
    grouped = off.reshape(t // tt, tt, hk // SUBLANES, SUBLANES)
    return jnp.transpose(grouped, (0, 3, 1, 2)).reshape(-1)


def _peer_u(off, sh, h, gates, tab, *, tt):
    t, hk = off.shape
    assert hk == LANES and h.shape[1:] == (SUBLANES, LANES)
    smem = pl.BlockSpec((tt * hk,), lambda i: (i,), memory_space=pltpu.SMEM)
    compact = pl.BlockSpec((tt, hk), lambda i: (i, 0))
    last = t // tt - 1
    nxt = pl.BlockSpec((tt, hk), lambda i: (jnp.minimum(i + 1, last), 0))
    return pl.pallas_call(
        functools.partial(_peer_u_kernel, tt=tt, hk=hk),
        grid=(t // tt,),
        in_specs=[
            smem, compact, nxt, pl.BlockSpec((tt, SUBLANES, LANES), lambda i: (i, 0, 0)), compact,
            _resident(tab)],
        out_specs=compact,
        out_shape=jax.ShapeDtypeStruct((t, hk), F32),
        scratch_shapes=[pltpu.VMEM((tt * hk, LANES), F32), pltpu.VMEM((tt * hk, LANES), I32),
                        pltpu.VMEM((tt, hk), F32)],
        compiler_params=_params(("arbitrary",), VMEM_LIMIT_BYTES),
        name="peer_u",
    )(_split_offsets(off, tt), sh, sh, h, gates, tab)


def _peer_v(off, sh, coef, x, tab, *, tt):
    t, hk = off.shape
    assert hk == LANES and x.shape[1:] == (SUBLANES, LANES)
    smem = pl.BlockSpec((tt * hk,), lambda i: (i,), memory_space=pltpu.SMEM)
    tile = pl.BlockSpec((tt, SUBLANES, LANES), lambda i: (i, 0, 0))
    compact = pl.BlockSpec((tt, hk), lambda i: (i, 0))
    return pl.pallas_call(
        functools.partial(_peer_v_kernel, tt=tt, hk=hk, n_acc=4),
        grid=(t // tt,),
        in_specs=[smem, compact, compact, tile, _resident(tab)],
        out_specs=tile,
        out_shape=jax.ShapeDtypeStruct(x.shape, F32),
        scratch_shapes=[pltpu.VMEM((tt * hk, LANES), F32), pltpu.VMEM((tt * hk, LANES), I32)],
        compiler_params=_params(("arbitrary",), VMEM_LIMIT_BYTES),
        name="peer_v",
    )(_split_offsets(off, tt), sh, coef, x, tab)


def _pack_kernel(hi_ref, lo_ref, o_ref):
    def bf16_bits(x):
        bits = lax.bitcast_convert_type(x, U32)
        r = bits + (jnp.uint32(0x7FFF) + ((bits >> 16) & jnp.uint32(1)))
        return jnp.where(x != x, jnp.uint32(0x7FC00000), r)

    w = (bf16_bits(hi_ref[...]) & jnp.uint32(0xFFFF0000)) | (bf16_bits(lo_ref[...]) >> 16)
    _store_tiled(o_ref, w, w.shape[0])


def _pack_table(tab, rows=256):
    n, d = tab.shape
    assert d == SUBLANES * LANES and (n // 2) % rows == 0
    nblk = n // 2 // rows
    return pl.pallas_call(
        _pack_kernel,
        grid=(nblk,),
        in_specs=[pl.BlockSpec((rows, d), lambda i: (i, 0)),
                  pl.BlockSpec((rows, d), lambda i: (i + nblk, 0))],
        out_specs=pl.BlockSpec((rows * SUBLANES, LANES), lambda i: (i, 0)),
        out_shape=jax.ShapeDtypeStruct((n // 2 * SUBLANES, LANES), U32),
        compiler_params=_params(("arbitrary",)),
        name="pack_table",
    )(tab, tab)


def _block_diag(blocks):
    g, r, c = blocks.shape
    eye = jnp.eye(g, dtype=blocks.dtype)
    return (eye[:, None, :, None] * blocks[:, :, None, :]).reshape(g * r, g * c)


def _s5_params(a_re, a_im, log_dt, b_re, b_im, c_re, c_im):
    a = lax.complex(a_re, a_im)
    dt = jnp.exp(log_dt)[:, None]
    a_bar = jnp.exp(a * dt)
    b_bar = ((a_bar - 1.0) / a)[..., None] * lax.complex(b_re, b_im)
    bt = jnp.swapaxes(b_bar, 1, 2)
    bblk = jnp.concatenate([_block_diag(jnp.real(bt)), _block_diag(jnp.imag(bt))], axis=1)
    ct_re = jnp.swapaxes(c_re, 1, 2)
    ct_im = jnp.swapaxes(c_im, 1, 2)
    cblk = jnp.concatenate([_block_diag(ct_re), -_block_diag(ct_im)], axis=0)
    n = a_bar.size
    return (jnp.real(a_bar).reshape(1, n), jnp.imag(a_bar).reshape(1, n),
            bblk.astype(BF16), cblk.astype(BF16))


def _tiles(s, t):
    ts = min(512, s)
    blk = min(512, s)
    tc = min(64, s)
    tm = min(512, t)
    tt = min(128, t)
    return ts, blk, tc, tm, tt


def _layer(x, norm1_g, w_in, f_bias, q_gain, k_gain, pool_w, pool_scale,
           a_re, a_im, log_dt, b_re, b_im, c_re, c_im, ssm_d, glu_w, glu_b,
           out_norm_g, w_out, norm2_g, peer_wq, peer_keys, peer_u, peer_v):
    b, s, d = x.shape[0], x.shape[1] // SUBLANES, SUBLANES * LANES
    t = b * s
    n_heads = f_bias.shape[0]
    att_w = n_heads * HEAD_DIM
    pool_width = pool_scale.shape[0]
    ssm_w = ssm_d.shape[0]
    ts, blk, tc, tm, tt = _tiles(s, t)
    row = lambda v: v.reshape(1, -1)

    c0, c1 = 3 * att_w, 3 * att_w + n_heads
    wf = jnp.zeros((d, LANES), F32).at[:, :n_heads].set(w_in[:, c0:c1])
    fb = jnp.zeros((1, LANES), F32).at[0, :n_heads].set(f_bias)
    seg = _block_diag(jnp.full((n_heads, HEAD_DIM, HEAD_DIM), 1.0 / HEAD_DIM, F32)).astype(BF16)
    q, k, v, c_tok, u_pool, u_ssm = _inproj(
        x, row(norm1_g), w_in[:, :c0].astype(BF16), wf,
        w_in[:, c1:c1 + pool_width].astype(BF16), w_in[:, c1 + pool_width:].astype(BF16), fb,
        row(jnp.tile(q_gain, n_heads) * HEAD_DIM ** -0.5), row(jnp.tile(k_gain, n_heads)),
        seg, ts=ts)
    att = _attention(q, k, v, c_tok, jnp.swapaxes(c_tok, 1, 2), blk=blk)
    pool = _pool(u_pool, _block_diag(pool_w).astype(BF16), row(pool_scale), ts=ts)
    are, aim, bblk, cblk = _s5_params(a_re, a_im, log_dt, b_re, b_im, c_re, c_im)
    ssm = _s5(u_ssm.reshape(s * b, ssm_w), are, aim, bblk, cblk, row(ssm_d),
              glu_w.astype(BF16), row(glu_b), nb=b, tc=tc)
    x1 = _merge(x, att, pool, ssm.reshape(s, b * ssm_w), row(out_norm_g),
                w_out.astype(BF16), ts=ts)

    n_experts = peer_u.shape[0]
    hk = peer_keys.shape[0] * PEER_TOPK
    kh = peer_keys.astype(BF16)
    h2, off, sh, gates = _route(x1.reshape(t * SUBLANES, LANES), row(norm2_g),
                                peer_wq.astype(BF16), kh, tm=tm, n_experts=n_experts)
    tok_major = lambda a: a.reshape(hk, t).T
    off, sh, gates = tok_major(off), tok_major(sh), tok_major(gates)
    tiles = lambda a: a.reshape(t, SUBLANES, LANES)
    coef = _peer_u(off, sh, tiles(h2), gates, _pack_table(peer_u), tt=tt)
    x2 = _peer_v(off, sh, coef, tiles(x1), _pack_table(peer_v), tt=tt)
    return x2.reshape(b, s * SUBLANES, LANES)


def kernel(x, norm1_g, w_in, f_bias, q_gain, k_gain, pool_w, pool_scale, ssm_a_re, ssm_a_im,
           ssm_log_dt, ssm_b_re, ssm_b_im, ssm_c_re, ssm_c_im, ssm_d, glu_w, glu_b, out_norm_g,
           w_out, norm2_g, peer_wq, peer_keys, peer_u, peer_v):
    per_layer = (norm1_g, w_in, f_bias, q_gain, k_gain, pool_w, pool_scale, ssm_a_re, ssm_a_im,
                 ssm_log_dt, ssm_b_re, ssm_b_im, ssm_c_re, ssm_c_im, ssm_d, glu_w, glu_b,
                 out_norm_g, w_out, norm2_g, peer_wq, peer_keys, peer_u, peer_v)
    b, s, d = x.shape
    assert d == SUBLANES * LANES
    x = x.reshape(b, s * SUBLANES, LANES)
    for l in range(norm1_g.shape[0]):
        x = _layer(x, *[p[l] for p in per_layer])
    return x.reshape(b, s, d)
```

```python
import functools
import math

import jax
import jax.numpy as jnp
from jax import lax
from jax.experimental import pallas as pl
from jax.experimental.pallas import tpu as pltpu

F32 = jnp.float32
BF16 = jnp.bfloat16
U32 = jnp.uint32
I32 = jnp.int32

EPS = 1e-6
HEAD_DIM = 64
POOL_WINDOWS = (2, 4, 8, 16)
POOL_HALO = 16
SSM_STATE = 64
SSM_GROUP_CH = 16
PEER_KEYS = 128
PEER_TOPK = 16

LANES = 128
SUBLANES = 8
VMEM_LIMIT_BYTES = 56 * 1024 * 1024

_NT = (((1,), (1,)), ((), ()))


def _rmsnorm(x, g):
    return x * lax.rsqrt(jnp.mean(x * x, axis=-1, keepdims=True) + EPS) * g


def _gelu_tanh(x):
    c = math.sqrt(2.0 / math.pi)
    return x * (0.5 * (1.0 + jnp.tanh(c * (x + 0.044715 * (x * x * x)))))


def _params(sem, vmem=None):
    return pltpu.CompilerParams(dimension_semantics=sem, vmem_limit_bytes=vmem)


def _load_tiled(ref, n):
    return jnp.concatenate([ref[pl.ds(s, n, stride=SUBLANES), :] for s in range(SUBLANES)], axis=1)


def _store_tiled(ref, val, n):
    for s in range(SUBLANES):
        ref[pl.ds(s, n, stride=SUBLANES), :] = val[:, s * LANES:(s + 1) * LANES]


def _inproj_kernel(x_ref, g_ref, wqkv_ref, wf_ref, wpool_ref, wssm_ref, fb_ref,
                   qg_ref, kg_ref, seg_ref,
                   q_out, k_out, v_out, c_out, pool_out, ssm_out, carry_ref,
                   *, att_w, n_heads):
    i = pl.program_id(1)

    @pl.when(i == 0)
    def _():
        carry_ref[...] = jnp.zeros_like(carry_ref)

    ts = x_ref.shape[0] // SUBLANES
    x = _load_tiled(x_ref, ts)
    h = _rmsnorm(x, g_ref[...])
    hb = h.astype(BF16)
    qkv = jnp.dot(hb, wqkv_ref[...], preferred_element_type=F32)
    seg = seg_ref[...]

    def headnorm(t, gain):
        ms = jnp.dot((t * t).astype(BF16), seg, preferred_element_type=F32)
        return t * lax.rsqrt(ms + EPS) * gain

    q_out[...] = headnorm(qkv[:, :att_w], qg_ref[...]).astype(BF16)
    k_out[...] = headnorm(qkv[:, att_w:2 * att_w], kg_ref[...]).astype(BF16)
    v_out[...] = qkv[:, 2 * att_w:].astype(BF16)
    pool_out[...] = jnp.dot(hb, wpool_ref[...], preferred_element_type=F32)
    ssm_out[...] = jnp.dot(hb, wssm_ref[...], preferred_element_type=F32)

    f = jnp.dot(h, wf_ref[...], precision=lax.Precision.HIGHEST,
                preferred_element_type=F32) + fb_ref[...]
    ls = jnp.minimum(f, 0.0) - jnp.log1p(jnp.exp(-jnp.abs(f)))
    r = lax.broadcasted_iota(I32, (ts, ts), 0)
    c = lax.broadcasted_iota(I32, (ts, ts), 1)
    tri = jnp.where(c <= r, 1.0, 0.0).astype(F32)
    cs = jnp.dot(tri, ls, precision=lax.Precision.HIGHEST,
                 preferred_element_type=F32) + carry_ref[0:1, :]
    carry_ref[...] = jnp.broadcast_to(cs[ts - 1:ts, :], carry_ref.shape)
    c_out[...] = cs[:, :n_heads]


def _tiled_spec(ts):
    return pl.BlockSpec((None, ts * SUBLANES, LANES), lambda bi, i: (bi, i, 0))


def _inproj(x, g, wqkv, wf, wpool, wssm, fb, qg, kg, seg, *, ts):
    b, s = x.shape[0], x.shape[1] // SUBLANES
    att_w = seg.shape[0]
    n_heads = att_w // HEAD_DIM
    pw = wpool.shape[1]
    sw = wssm.shape[1]
    full = lambda a: pl.BlockSpec(a.shape, lambda bi, i: (0,) * a.ndim)
    tok = lambda w: pl.BlockSpec((None, ts, w), lambda bi, i: (bi, i, 0))
    return pl.pallas_call(
        functools.partial(_inproj_kernel, att_w=att_w, n_heads=n_heads),
        grid=(b, s // ts),
        in_specs=[_tiled_spec(ts), full(g), full(wqkv), full(wf), full(wpool), full(wssm),
                  full(fb), full(qg), full(kg), full(seg)],
        out_specs=[tok(att_w), tok(att_w), tok(att_w), tok(n_heads), tok(pw),
                   pl.BlockSpec((ts, sw), lambda bi, i: (i, bi))],
        out_shape=[jax.ShapeDtypeStruct((b, s, att_w), BF16),
                   jax.ShapeDtypeStruct((b, s, att_w), BF16),
                   jax.ShapeDtypeStruct((b, s, att_w), BF16),
                   jax.ShapeDtypeStruct((b, s, n_heads), F32),
                   jax.ShapeDtypeStruct((b, s, pw), F32),
                   jax.ShapeDtypeStruct((s, b * sw), F32)],
        scratch_shapes=[pltpu.VMEM((SUBLANES, LANES), F32)],
        compiler_params=_params(("arbitrary", "arbitrary"), VMEM_LIMIT_BYTES),
        name="inproj",
    )(x, g, wqkv, wf, wpool, wssm, fb, qg, kg, seg)


def _attn_kernel(q_ref, k_ref, v_ref, ct_ref, cr_ref, o_ref, m_sc, l_sc, acc_sc, s_sc, *, blk):
    p = pl.program_id(1)
    i = pl.program_id(2)
    q = q_ref[...]
    lane = lax.broadcasted_iota(I32, (blk, LANES), 1)
    ct = ct_ref[...]
    hl = lax.broadcasted_iota(I32, ct.shape, 1)
    heads = (2 * p, 2 * p + 1)
    qm = (jnp.where(lane < HEAD_DIM, q, jnp.zeros_like(q)),
          jnp.where(lane >= HEAD_DIM, q, jnp.zeros_like(q)))
    cq = tuple(jnp.sum(jnp.where(hl == hd, ct, 0.0), axis=1, keepdims=True) for hd in heads)
    m_sc[...] = jnp.full(m_sc.shape, -jnp.inf, F32)
    l_sc[...] = jnp.zeros(l_sc.shape, F32)
    acc_sc[...] = jnp.zeros(acc_sc.shape, F32)

    def qk(kb):
        kblk = k_ref[pl.ds(pl.multiple_of(kb * blk, blk), blk), :]
        return [lax.dot_general(qm[hh], kblk, _NT, preferred_element_type=F32) for hh in range(2)]

    def step(kb, s_cur, causal):
        start = pl.multiple_of(kb * blk, blk)
        vblk = v_ref[pl.ds(start, blk), :]
        for hh in range(2):
            ck = cr_ref[pl.ds(heads[hh], 1), pl.ds(start, blk)]
            z = (s_cur[hh] + cq[hh]) - ck
            if causal:
                row = lax.broadcasted_iota(I32, (blk, blk), 0)
                col = lax.broadcasted_iota(I32, (blk, blk), 1)
                z = jnp.where(col <= row, z, -jnp.inf)
            m_old = m_sc[hh]
            m_new = jnp.maximum(m_old, jnp.max(z, axis=1, keepdims=True))
            alpha = jnp.exp(m_old - m_new)
            pe = jnp.exp(z - m_new)
            l_sc[hh] = alpha * l_sc[hh] + jnp.sum(pe, axis=1, keepdims=True)
            acc_sc[hh] = alpha * acc_sc[hh] + jnp.dot(pe.astype(BF16), vblk,
                                                      preferred_element_type=F32)
            m_sc[hh] = m_new

    s0 = qk(0)
    s_sc[0] = s0[0]
    s_sc[1] = s0[1]

    def body(kb, carry):
        s_cur = [s_sc[0], s_sc[1]]
        s_next = qk(kb + 1)
        step(kb, s_cur, False)
        s_sc[0] = s_next[0]
        s_sc[1] = s_next[1]
        return carry

    lax.fori_loop(0, i, body, 0)
    step(i, [s_sc[0], s_sc[1]], True)
    o_ref[...] = jnp.where(lane < HEAD_DIM, acc_sc[0] / l_sc[0], acc_sc[1] / l_sc[1])


def _attention(q, k, v, c_tok, c_row, *, blk):
    b, s, w = q.shape
    n_heads = c_tok.shape[2]
    return pl.pallas_call(
        functools.partial(_attn_kernel, blk=blk),
        grid=(b, w // LANES, s // blk),
        in_specs=[pl.BlockSpec((None, blk, LANES), lambda bi, p, i: (bi, i, p)),
                  pl.BlockSpec((None, s, LANES), lambda bi, p, i: (bi, 0, p)),
                  pl.BlockSpec((None, s, LANES), lambda bi, p, i: (bi, 0, p)),
                  pl.BlockSpec((None, blk, n_heads), lambda bi, p, i: (bi, i, 0)),
                  pl.BlockSpec((None, n_heads, s), lambda bi, p, i: (bi, 0, 0))],
        out_specs=pl.BlockSpec((None, blk, LANES), lambda bi, p, i: (bi, i, p)),
        out_shape=jax.ShapeDtypeStruct((b, s, w), F32),
        scratch_shapes=[pltpu.VMEM((2, blk, 1), F32), pltpu.VMEM((2, blk, 1), F32),
                        pltpu.VMEM((2, blk, LANES), F32), pltpu.VMEM((2, blk, blk), F32)],
        compiler_params=_params(("arbitrary", "arbitrary", "arbitrary"), VMEM_LIMIT_BYTES),
        name="fox_attention",
    )(q, k, v, c_tok, c_row)


def _pool_kernel(u_ref, prev_ref, w_ref, sc_ref, o_ref, *, group_dim):
    i = pl.program_id(1)
    cur = u_ref[...]
    ts, pw = cur.shape
    prev = jnp.where(i > 0, prev_ref[...], 0.0)
    ext = jnp.concatenate([prev, cur], axis=0)
    pos = i * ts + lax.broadcasted_iota(I32, (ts, 1), 0)
    grp = lax.broadcasted_iota(I32, (ts, pw), 1) // group_dim
    acc = cur
    pooled = jnp.zeros_like(cur)
    for j in range(1, max(POOL_WINDOWS)):
        acc = acc + ext[POOL_HALO - j:POOL_HALO - j + ts, :]
        win = j + 1
        if win in POOL_WINDOWS:
            cnt = jnp.minimum(pos + 1, win).astype(F32)
            pooled = jnp.where(grp == POOL_WINDOWS.index(win), acc / cnt, pooled)
    pooled = pooled - cur
    o_ref[...] = jnp.dot(pooled.astype(BF16), w_ref[...],
                         preferred_element_type=F32) * sc_ref[...]


def _pool(u, wblk, scale, *, ts):
    b, s, pw = u.shape
    hb = ts // POOL_HALO
    return pl.pallas_call(
        functools.partial(_pool_kernel, group_dim=pw // len(POOL_WINDOWS)),
        grid=(b, s // ts),
        in_specs=[pl.BlockSpec((None, ts, pw), lambda bi, i: (bi, i, 0)),
                  pl.BlockSpec((None, POOL_HALO, pw),
                               lambda bi, i: (bi, jnp.maximum(i * hb - 1, 0), 0)),
                  pl.BlockSpec(wblk.shape, lambda bi, i: (0, 0)),
                  pl.BlockSpec(scale.shape, lambda bi, i: (0, 0))],
        out_specs=pl.BlockSpec((None, ts, pw), lambda bi, i: (bi, i, 0)),
        out_shape=jax.ShapeDtypeStruct((b, s, pw), F32),
        compiler_params=_params(("arbitrary", "arbitrary")),
        name="pool",
    )(u, u, wblk, scale)


def _s5_kernel(u_ref, are_ref, aim_ref, bblk_ref, cblk_ref, d_ref, gw_ref, gb_ref,
               o_ref, st_ref, hre_ref, him_ref, *, nb, n):
    @pl.when(pl.program_id(0) == 0)
    def _():
        hre_ref[...] = jnp.zeros_like(hre_ref)
        him_ref[...] = jnp.zeros_like(him_ref)

    u = u_ref[...]
    tc = u.shape[0] // nb
    st_ref[...] = jnp.dot(u.astype(BF16), bblk_ref[...], preferred_element_type=F32)
    ar = jnp.broadcast_to(are_ref[...], (nb, n))
    ai = jnp.broadcast_to(aim_ref[...], (nb, n))

    def step(t, carry):
        hr, hi = carry
        r0 = pl.multiple_of(t * nb, nb)
        br = st_ref[pl.ds(r0, nb), 0:n]
        bi = st_ref[pl.ds(r0, nb), n:2 * n]
        nr = (ar * hr - ai * hi) + br
        ni = (ar * hi + ai * hr) + bi
        st_ref[pl.ds(r0, nb), 0:n] = nr
        st_ref[pl.ds(r0, nb), n:2 * n] = ni
        return nr, ni

    hr, hi = lax.fori_loop(0, tc, step, (hre_ref[...], him_ref[...]))
    hre_ref[...] = hr
    him_ref[...] = hi
    y = jnp.dot(st_ref[...].astype(BF16), cblk_ref[...], preferred_element_type=F32)
    z = _gelu_tanh(y + d_ref[...] * u)
    gate = jax.nn.sigmoid(jnp.dot(z.astype(BF16), gw_ref[...], preferred_element_type=F32)
                          + gb_ref[...])
    o_ref[...] = z * gate


def _s5(u_tm, a_re, a_im, bblk, cblk, dskip, gw, gb, *, nb, tc):
    rows, sw = u_tm.shape
    n = a_re.shape[1]
    rb = tc * nb
    full = lambda a: pl.BlockSpec(a.shape, lambda i: (0,) * a.ndim)
    return pl.pallas_call(
        functools.partial(_s5_kernel, nb=nb, n=n),
        grid=(rows // rb,),
        in_specs=[pl.BlockSpec((rb, sw), lambda i: (i, 0)), full(a_re), full(a_im),
                  full(bblk), full(cblk), full(dskip), full(gw), full(gb)],
        out_specs=pl.BlockSpec((rb, sw), lambda i: (i, 0)),
        out_shape=jax.ShapeDtypeStruct((rows, sw), F32),
        scratch_shapes=[pltpu.VMEM((rb, 2 * n), F32), pltpu.VMEM((nb, n), F32),
                        pltpu.VMEM((nb, n), F32)],
        compiler_params=_params(("arbitrary",), VMEM_LIMIT_BYTES),
        name="s5_scan",
    )(u_tm, a_re, a_im, bblk, cblk, dskip, gw, gb)


def _merge_kernel(x_ref, att_ref, pool_ref, ssm_ref, og_ref, w_ref, o_ref, *, att_w, pool_w):
    og = og_ref[...]
    p1 = att_w + pool_w
    merged = jnp.concatenate([
        _rmsnorm(att_ref[...], og[:, :att_w]),
        _rmsnorm(pool_ref[...], og[:, att_w:p1]),
        _rmsnorm(ssm_ref[...], og[:, p1:])], axis=1).astype(BF16)
    ts = merged.shape[0]
    out = _load_tiled(x_ref, ts) + jnp.dot(merged, w_ref[...], preferred_element_type=F32)
    _store_tiled(o_ref, out, ts)


def _merge(x, att, pool, ssm_tm, og, w, *, ts):
    b, s = x.shape[0], x.shape[1] // SUBLANES
    att_w, pool_w = att.shape[2], pool.shape[2]
    sw = ssm_tm.shape[1] // b
    tok = lambda wd: pl.BlockSpec((None, ts, wd), lambda bi, i: (bi, i, 0))
    return pl.pallas_call(
        functools.partial(_merge_kernel, att_w=att_w, pool_w=pool_w),
        grid=(b, s // ts),
        in_specs=[_tiled_spec(ts), tok(att_w), tok(pool_w),
                  pl.BlockSpec((ts, sw), lambda bi, i: (i, bi)),
                  pl.BlockSpec(og.shape, lambda bi, i: (0, 0)),
                  pl.BlockSpec(w.shape, lambda bi, i: (0, 0))],
        out_specs=_tiled_spec(ts),
        out_shape=jax.ShapeDtypeStruct(x.shape, F32),
        compiler_params=_params(("arbitrary", "arbitrary"), VMEM_LIMIT_BYTES),
        name="merge_outproj",
    )(x, att, pool, ssm_tm, og, w)


def _topk_rows(a, k, extra=None):
    nrow, nl = a.shape
    row = lax.broadcasted_iota(I32, (nrow, nl), 0).astype(F32)
    out_row = lax.broadcasted_iota(I32, (k, nl), 0)
    vals = jnp.zeros((k, nl), F32)
    sel = jnp.zeros((k, nl), F32)
    for j in range(k):
        m = jnp.max(a, axis=0, keepdims=True)
        ix = jnp.min(jnp.where(a == m, row, float(nrow)), axis=0, keepdims=True)
        hit = row == ix
        if extra is None:
            pick = ix
        else:
            pick = jnp.sum(jnp.where(hit, extra, 0.0), axis=0, keepdims=True)
        vals = jnp.where(out_row == j, m, vals)
        sel = jnp.where(out_row == j, pick, sel)
        a = jnp.where(hit, -jnp.inf, a)
    return vals, sel


def _route_kernel(x_ref, g_ref, wq_ref, keys_ref, h_out, off_out, sh_out, gate_out, q_sc,
                  *, n_half):
    hd = pl.program_id(1)
    qd = 2 * PEER_KEYS

    @pl.when(hd == 0)
    def _():
        tm = q_sc.shape[0]
        h = _rmsnorm(_load_tiled(x_ref, tm), g_ref[...])
        _store_tiled(h_out, h, tm)
        q_sc[...] = jnp.dot(h.astype(BF16), wq_ref[...], preferred_element_type=F32)

    q = q_sc[:, pl.ds(pl.multiple_of(hd * qd, qd), qd)].astype(BF16)
    tops = []
    for i in range(2):
        sc = lax.dot_general(keys_ref[i], q[:, i * PEER_KEYS:(i + 1) * PEER_KEYS], _NT,
                             preferred_element_type=F32)
        tops.append(_topk_rows(sc, PEER_TOPK))
    (s1, i1), (s2, i2) = tops
    k = PEER_TOPK
    sub8 = lax.broadcasted_iota(I32, (SUBLANES, s1.shape[1]), 0)
    cands, ecands = [], []
    a = 0
    while a < k and k // (a + 1) > 1:
        nb = k // (a + 1)
        for b0 in range(0, nb, SUBLANES):
            c = s1[a:a + 1, :] + s2[b0:b0 + SUBLANES, :]
            if b0 + SUBLANES > nb:
                c = jnp.where(sub8 < nb - b0, c, -jnp.inf)
            cands.append(c)
            ecands.append(i1[a:a + 1, :] * float(PEER_KEYS) + i2[b0:b0 + SUBLANES, :])
        a += 1
    cands.append(s1[a:, :] + s2[0:1, :])
    ecands.append(i1[a:, :] * float(PEER_KEYS) + i2[0:1, :])
    best, experts = _topk_rows(jnp.concatenate(cands, axis=0), k,
                               extra=jnp.concatenate(ecands, axis=0))
    w = jnp.exp(best - best[0:1, :])
    gate_out[...] = w / jnp.sum(w, axis=0, keepdims=True)
    e = experts.astype(I32)
    hi_half = e < n_half
    off_out[...] = jnp.where(hi_half, e, e - n_half) * SUBLANES
    sh_out[...] = jnp.where(hi_half, 0.0, 16.0)


def _route(x, g, wq, keys, *, tm, n_experts):
    t = x.shape[0] // SUBLANES
    n_heads = keys.shape[0]
    hk = pl.BlockSpec((None, PEER_TOPK, tm), lambda i, h: (h, 0, i))
    shp = lambda dt: jax.ShapeDtypeStruct((n_heads, PEER_TOPK, t), dt)
    tiled = pl.BlockSpec((tm * SUBLANES, LANES), lambda i, h: (i, 0))
    return pl.pallas_call(
        functools.partial(_route_kernel, n_half=n_experts // 2),
        grid=(t // tm, n_heads),
        in_specs=[tiled,
                  pl.BlockSpec(g.shape, lambda i, h: (0, 0)),
                  pl.BlockSpec(wq.shape, lambda i, h: (0, 0)),
                  pl.BlockSpec((None, 2, PEER_KEYS, keys.shape[3]), lambda i, h: (h, 0, 0, 0))],
        out_specs=[tiled, hk, hk, hk],
        out_shape=[jax.ShapeDtypeStruct(x.shape, F32), shp(I32), shp(F32), shp(F32)],
        scratch_shapes=[pltpu.VMEM((tm, wq.shape[1]), F32)],
        compiler_params=_params(("arbitrary", "arbitrary"), VMEM_LIMIT_BYTES),
        name="peer_route",
    )(x, g, wq, keys)


def _expert_row(tab_ref, off, sh):
    w = tab_ref[pl.ds(pl.multiple_of(off, SUBLANES), SUBLANES), :]
    return lax.bitcast_convert_type((w << sh) & jnp.uint32(0xFFFF0000), F32)


def _fold_rows(vs, sub, rot=0):
    m = len(vs)
    if m == 1:
        return vs[0]
    span = SUBLANES // m
    c1 = _fold_rows(vs[0::2], sub, rot - span)
    c2 = _fold_rows(vs[1::2], sub, rot)
    own = ((sub - rot) & (2 * span - 1)) < span
    return jnp.where(own, c1, c2) + pltpu.roll(jnp.where(own, c2, c1), span, 0)


def _splat_rows(mxu_src, mxu_dst, xlu_src=None, xlu_dst=None, *, tt, hk, also=None):
    eye = (lax.broadcasted_iota(I32, (hk, hk), 0) == lax.broadcasted_iota(I32, (hk, hk), 1))
    ones = jnp.ones((hk, hk), BF16)

    def body(t, carry):
        v = jnp.broadcast_to(mxu_src[pl.ds(t, 1), :], (hk, hk))
        rep = jnp.dot(jnp.where(eye, v, 0.0).astype(BF16), ones, preferred_element_type=F32)
        mxu_dst(t, rep.astype(I32))
        if xlu_src is not None:
            xlu_dst(t, jnp.broadcast_to(xlu_src[pl.ds(t, 1), :], (hk, hk)).T)
        if also is not None:
            also(t)
        return carry

    lax.fori_loop(0, tt, body, 0, unroll=8)


def _splat(rep_ref, row):
    return jnp.broadcast_to(rep_ref[pl.ds(row, 1), :], (SUBLANES, LANES))


def _peer_u_kernel(*refs, tt, hk):
    n_off = tt * hk // SUBLANES
    off_refs = [refs[0].at[pl.ds(j * n_off, n_off)] for j in range(SUBLANES)]
    sh_ref, shn_ref, h_ref, gate_ref, tab_ref, coef_ref, part_ref, srep_ref, pre_ref = refs[1:]
    sub = lax.broadcasted_iota(I32, (SUBLANES, LANES), 0)
    ngrp = hk // SUBLANES

    def store_srep(t, block):
        srep_ref[pl.ds(pl.multiple_of(t * hk, hk), hk), :] = block

    @pl.when(pl.program_id(0) == 0)
    def _():
        _splat_rows(sh_ref, store_srep, tt=tt, hk=hk)

    def tok(t, carry):
        h = h_ref[t]
        base = pl.multiple_of(t * hk, hk)
        for g in range(ngrp):
            e0 = base + g * SUBLANES
            ps = [_expert_row(tab_ref, off_refs[j][t * ngrp + g],
                              _splat(srep_ref, e0 + j).astype(U32)) * h
                  for j in range(SUBLANES)]
            part_ref[pl.ds(e0, SUBLANES), :] = _fold_rows(ps, sub)
        return carry

    lax.fori_loop(0, tt, tok, 0)

    def lane_sums(t):
        p = part_ref[pl.ds(pl.multiple_of(t * hk, hk), hk), :]
        pre_ref[pl.ds(t, 1), :] = jnp.sum(p.T, axis=0, keepdims=True)

    _splat_rows(shn_ref, store_srep, tt=tt, hk=hk, also=lane_sums)
    coef_ref[...] = gate_ref[...] * _gelu_tanh(pre_ref[...])


def _peer_v_kernel(*refs, tt, hk, n_acc):
    n_off = tt * hk // SUBLANES
    off_refs = [refs[0].at[pl.ds(j * n_off, n_off)] for j in range(SUBLANES)]
    sh_ref, coef_ref, x_ref, tab_ref, o_ref, rep_ref = refs[1:]
    ngrp = hk // SUBLANES

    def store_coef(t, block):
        rep_ref[pl.ds(2 * t * hk, hk, stride=2), :] = block

    def store_shift(t, block):
        rep_ref[pl.ds(2 * t * hk + 1, hk, stride=2), :] = lax.bitcast_convert_type(block, F32)

    _splat_rows(sh_ref, store_shift, coef_ref, store_coef, tt=tt, hk=hk)

    def tok(t, carry):
        accs = [None] * n_acc
        e0 = 2 * t * hk
        for k in range(hk):
            off = off_refs[k % SUBLANES][t * ngrp + k // SUBLANES]
            sv = lax.bitcast_convert_type(_splat(rep_ref, e0 + 2 * k + 1), U32)
            term = _splat(rep_ref, e0 + 2 * k) * _expert_row(tab_ref, off, sv)
            accs[k % n_acc] = term if accs[k % n_acc] is None else accs[k % n_acc] + term
        while len(accs) > 1:
            accs = [accs[j] + accs[j + len(accs) // 2] for j in range(len(accs) // 2)]
        o_ref[t] = x_ref[t] + accs[0]
        return carry

    lax.fori_loop(0, tt, tok, 0)


def _resident(a):
    return pl.BlockSpec(a.shape, lambda i: (0,) * a.ndim, pipeline_mode=pl.Buffered(1))


def _split_offsets(off, tt):
    t, hk = off.shape
    grouped = off.reshape(t // tt, tt, hk // SUBLANES, SUBLANES)
    return jnp.transpose(grouped, (0, 3, 1, 2)).reshape(-1)


def _peer_u(off, sh, h, gates, tab, *, tt):
    t, hk = off.shape
    assert hk == LANES and h.shape[1:] == (SUBLANES, LANES)
    smem = pl.BlockSpec((tt * hk,), lambda i: (i,), memory_space=pltpu.SMEM)
    compact = pl.BlockSpec((tt, hk), lambda i: (i, 0))
    last = t // tt - 1
    nxt = pl.BlockSpec((tt, hk), lambda i: (jnp.minimum(i + 1, last), 0))
    return pl.pallas_call(
        functools.partial(_peer_u_kernel, tt=tt, hk=hk),
        grid=(t // tt,),
        in_specs=[
            smem, compact, nxt, pl.BlockSpec((tt, SUBLANES, LANES), lambda i: (i, 0, 0)), compact,
            _resident(tab)],
        out_specs=compact,
        out_shape=jax.ShapeDtypeStruct((t, hk), F32),
        scratch_shapes=[pltpu.VMEM((tt * hk, LANES), F32), pltpu.VMEM((tt * hk, LANES), I32),
                        pltpu.VMEM((tt, hk), F32)],
        compiler_params=_params(("arbitrary",), VMEM_LIMIT_BYTES),
        name="peer_u",
    )(_split_offsets(off, tt), sh, sh, h, gates, tab)


def _peer_v(off, sh, coef, x, tab, *, tt):
    t, hk = off.shape
    assert hk == LANES and x.shape[1:] == (SUBLANES, LANES)
    smem = pl.BlockSpec((tt * hk,), lambda i: (i,), memory_space=pltpu.SMEM)
    tile = pl.BlockSpec((tt, SUBLANES, LANES), lambda i: (i, 0, 0))
    compact = pl.BlockSpec((tt, hk), lambda i: (i, 0))
    return pl.pallas_call(
        functools.partial(_peer_v_kernel, tt=tt, hk=hk, n_acc=4),
        grid=(t // tt,),
        in_specs=[smem, compact, compact, tile, _resident(tab)],
        out_specs=tile,
        out_shape=jax.ShapeDtypeStruct(x.shape, F32),
        scratch_shapes=[pltpu.VMEM((2 * tt * hk, LANES), F32)],
        compiler_params=_params(("arbitrary",), VMEM_LIMIT_BYTES),
        name="peer_v",
    )(_split_offsets(off, tt), sh, coef, x, tab)


def _pack_kernel(hi_ref, lo_ref, o_ref):
    def bf16_bits(x):
        bits = lax.bitcast_convert_type(x, U32)
        r = bits + (jnp.uint32(0x7FFF) + ((bits >> 16) & jnp.uint32(1)))
        return jnp.where(x != x, jnp.uint32(0x7FC00000), r)

    w = (bf16_bits(hi_ref[...]) & jnp.uint32(0xFFFF0000)) | (bf16_bits(lo_ref[...]) >> 16)
    _store_tiled(o_ref, w, w.shape[0])


def _pack_table(tab, rows=256):
    n, d = tab.shape
    assert d == SUBLANES * LANES and (n // 2) % rows == 0
    nblk = n // 2 // rows
    return pl.pallas_call(
        _pack_kernel,
        grid=(nblk,),
        in_specs=[pl.BlockSpec((rows, d), lambda i: (i, 0)),
                  pl.BlockSpec((rows, d), lambda i: (i + nblk, 0))],
        out_specs=pl.BlockSpec((rows * SUBLANES, LANES), lambda i: (i, 0)),
        out_shape=jax.ShapeDtypeStruct((n // 2 * SUBLANES, LANES), U32),
        compiler_params=_params(("arbitrary",)),
        name="pack_table",
    )(tab, tab)


def _block_diag(blocks):
    g, r, c = blocks.shape
    eye = jnp.eye(g, dtype=blocks.dtype)
    return (eye[:, None, :, None] * blocks[:, :, None, :]).reshape(g * r, g * c)


def _s5_params(a_re, a_im, log_dt, b_re, b_im, c_re, c_im):
    a = lax.complex(a_re, a_im)
    dt = jnp.exp(log_dt)[:, None]
    a_bar = jnp.exp(a * dt)
    b_bar = ((a_bar - 1.0) / a)[..., None] * lax.complex(b_re, b_im)
    bt = jnp.swapaxes(b_bar, 1, 2)
    bblk = jnp.concatenate([_block_diag(jnp.real(bt)), _block_diag(jnp.imag(bt))], axis=1)
    ct_re = jnp.swapaxes(c_re, 1, 2)
    ct_im = jnp.swapaxes(c_im, 1, 2)
    cblk = jnp.concatenate([_block_diag(ct_re), -_block_diag(ct_im)], axis=0)
    n = a_bar.size
    return (jnp.real(a_bar).reshape(1, n), jnp.imag(a_bar).reshape(1, n),
            bblk.astype(BF16), cblk.astype(BF16))


def _tiles(s, t):
    ts = min(512, s)
    blk = min(512, s)
    tc = min(64, s)
    tm = min(512, t)
    tt = min(128, t)
    return ts, blk, tc, tm, tt


def _layer(x, norm1_g, w_in, f_bias, q_gain, k_gain, pool_w, pool_scale,
           a_re, a_im, log_dt, b_re, b_im, c_re, c_im, ssm_d, glu_w, glu_b,
           out_norm_g, w_out, norm2_g, peer_wq, peer_keys, peer_u, peer_v):
    b, s, d = x.shape[0], x.shape[1] // SUBLANES, SUBLANES * LANES
    t = b * s
    n_heads = f_bias.shape[0]
    att_w = n_heads * HEAD_DIM
    pool_width = pool_scale.shape[0]
    ssm_w = ssm_d.shape[0]
    ts, blk, tc, tm, tt = _tiles(s, t)
    row = lambda v: v.reshape(1, -1)

    c0, c1 = 3 * att_w, 3 * att_w + n_heads
    wf = jnp.zeros((d, LANES), F32).at[:, :n_heads].set(w_in[:, c0:c1])
    fb = jnp.zeros((1, LANES), F32).at[0, :n_heads].set(f_bias)
    seg = _block_diag(jnp.full((n_heads, HEAD_DIM, HEAD_DIM), 1.0 / HEAD_DIM, F32)).astype(BF16)
    q, k, v, c_tok, u_pool, u_ssm = _inproj(
        x, row(norm1_g), w_in[:, :c0].astype(BF16), wf,
        w_in[:, c1:c1 + pool_width].astype(BF16), w_in[:, c1 + pool_width:].astype(BF16), fb,
        row(jnp.tile(q_gain, n_heads) * HEAD_DIM ** -0.5), row(jnp.tile(k_gain, n_heads)),
        seg, ts=ts)
    att = _attention(q, k, v, c_tok, jnp.swapaxes(c_tok, 1, 2), blk=blk)
    pool = _pool(u_pool, _block_diag(pool_w).astype(BF16), row(pool_scale), ts=ts)
    are, aim, bblk, cblk = _s5_params(a_re, a_im, log_dt, b_re, b_im, c_re, c_im)
    ssm = _s5(u_ssm.reshape(s * b, ssm_w), are, aim, bblk, cblk, row(ssm_d),
              glu_w.astype(BF16), row(glu_b), nb=b, tc=tc)
    x1 = _merge(x, att, pool, ssm.reshape(s, b * ssm_w), row(out_norm_g),
                w_out.astype(BF16), ts=ts)

    n_experts = peer_u.shape[0]
    hk = peer_keys.shape[0] * PEER_TOPK
    kh = peer_keys.astype(BF16)
    h2, off, sh, gates = _route(x1.reshape(t * SUBLANES, LANES), row(norm2_g),
                                peer_wq.astype(BF16), kh, tm=tm, n_experts=n_experts)
    tok_major = lambda a: a.reshape(hk, t).T
    off, sh, gates = tok_major(off), tok_major(sh), tok_major(gates)
    tiles = lambda a: a.reshape(t, SUBLANES, LANES)
    coef = _peer_u(off, sh, tiles(h2), gates, _pack_table(peer_u), tt=tt)
    x2 = _peer_v(off, sh, coef, tiles(x1), _pack_table(peer_v), tt=tt)
    return x2.reshape(b, s * SUBLANES, LANES)


def kernel(x, norm1_g, w_in, f_bias, q_gain, k_gain, pool_w, pool_scale, ssm_a_re, ssm_a_im,
           ssm_log_dt, ssm_b_re, ssm_b_im, ssm_c_re, ssm_c_im, ssm_d, glu_w, glu_b, out_norm_g,
           w_out, norm2_g, peer_wq, peer_keys, peer_u, peer_v):
    per_layer = (norm1_g, w_in, f_bias, q_gain, k_gain, pool_w, pool_scale, ssm_a_re, ssm_a_im,
                 ssm_log_dt, ssm_b_re, ssm_b_im, ssm_c_re, ssm_c_im, ssm_d, glu_w, glu_b,
                 out_norm_g, w_out, norm2_g, peer_wq, peer_keys, peer_u, peer_v)
    b, s, d = x.shape
    assert d == SUBLANES * LANES
    x = x.reshape(b, s * SUBLANES, LANES)
    for l in range(norm1_g.shape[0]):
        x = _layer(x, *[p[l] for p in per_layer])
    return x.reshape(b, s, d)
```

```python
import functools
import math

import jax
import jax.numpy as jnp
from jax import lax
from jax.experimental import pallas as pl
from jax.experimental.pallas import tpu as pltpu

F32 = jnp.float32
BF16 = jnp.bfloat16
U32 = jnp.uint32
I32 = jnp.int32

EPS = 1e-6
HEAD_DIM = 64
POOL_WINDOWS = (2, 4, 8, 16)
POOL_HALO = 16
SSM_STATE = 64
SSM_GROUP_CH = 16
PEER_KEYS = 128
PEER_TOPK = 16
U_EDGES_PER_DOT = 32

LANES = 128
SUBLANES = 8
VMEM_LIMIT_BYTES = 56 * 1024 * 1024

_NT = (((1,), (1,)), ((), ()))


def _rmsnorm(x, g):
    return x * lax.rsqrt(jnp.mean(x * x, axis=-1, keepdims=True) + EPS) * g


def _gelu_tanh(x):
    c = math.sqrt(2.0 / math.pi)
    return x * (0.5 * (1.0 + jnp.tanh(c * (x + 0.044715 * (x * x * x)))))


def _params(sem, vmem=None):
    return pltpu.CompilerParams(dimension_semantics=sem, vmem_limit_bytes=vmem)


def _load_tiled(ref, n):
    return jnp.concatenate([ref[pl.ds(s, n, stride=SUBLANES), :] for s in range(SUBLANES)], axis=1)


def _store_tiled(ref, val, n):
    for s in range(SUBLANES):
        ref[pl.ds(s, n, stride=SUBLANES), :] = val[:, s * LANES:(s + 1) * LANES]


def _inproj_kernel(x_ref, g_ref, wqkv_ref, wf_ref, wpool_ref, wssm_ref, fb_ref,
                   qg_ref, kg_ref, seg_ref,
                   q_out, k_out, v_out, c_out, pool_out, ssm_out, carry_ref,
                   *, att_w, n_heads):
    i = pl.program_id(1)

    @pl.when(i == 0)
    def _():
        carry_ref[...] = jnp.zeros_like(carry_ref)

    ts = x_ref.shape[0] // SUBLANES
    x = _load_tiled(x_ref, ts)
    h = _rmsnorm(x, g_ref[...])
    hb = h.astype(BF16)
    qkv = jnp.dot(hb, wqkv_ref[...], preferred_element_type=F32)
    seg = seg_ref[...]

    def headnorm(t, gain):
        ms = jnp.dot((t * t).astype(BF16), seg, preferred_element_type=F32)
        return t * lax.rsqrt(ms + EPS) * gain

    q_out[...] = headnorm(qkv[:, :att_w], qg_ref[...]).astype(BF16)
    k_out[...] = headnorm(qkv[:, att_w:2 * att_w], kg_ref[...]).astype(BF16)
    v_out[...] = qkv[:, 2 * att_w:].astype(BF16)
    pool_out[...] = jnp.dot(hb, wpool_ref[...], preferred_element_type=F32)
    ssm_out[...] = jnp.dot(hb, wssm_ref[...], preferred_element_type=F32)

    f = jnp.dot(h, wf_ref[...], precision=lax.Precision.HIGHEST,
                preferred_element_type=F32) + fb_ref[...]
    ls = jnp.minimum(f, 0.0) - jnp.log1p(jnp.exp(-jnp.abs(f)))
    r = lax.broadcasted_iota(I32, (ts, ts), 0)
    c = lax.broadcasted_iota(I32, (ts, ts), 1)
    tri = jnp.where(c <= r, 1.0, 0.0).astype(F32)
    cs = jnp.dot(tri, ls, precision=lax.Precision.HIGHEST,
                 preferred_element_type=F32) + carry_ref[0:1, :]
    carry_ref[...] = jnp.broadcast_to(cs[ts - 1:ts, :], carry_ref.shape)
    c_out[...] = cs[:, :n_heads]


def _tiled_spec(ts):
    return pl.BlockSpec((None, ts * SUBLANES, LANES), lambda bi, i: (bi, i, 0))


def _inproj(x, g, wqkv, wf, wpool, wssm, fb, qg, kg, seg, *, ts):
    b, s = x.shape[0], x.shape[1] // SUBLANES
    att_w = seg.shape[0]
    n_heads = att_w // HEAD_DIM
    pw = wpool.shape[1]
    sw = wssm.shape[1]
    full = lambda a: pl.BlockSpec(a.shape, lambda bi, i: (0,) * a.ndim)
    tok = lambda w: pl.BlockSpec((None, ts, w), lambda bi, i: (bi, i, 0))
    return pl.pallas_call(
        functools.partial(_inproj_kernel, att_w=att_w, n_heads=n_heads),
        grid=(b, s // ts),
        in_specs=[_tiled_spec(ts), full(g), full(wqkv), full(wf), full(wpool), full(wssm),
                  full(fb), full(qg), full(kg), full(seg)],
        out_specs=[tok(att_w), tok(att_w), tok(att_w), tok(n_heads), tok(pw),
                   pl.BlockSpec((ts, sw), lambda bi, i: (i, bi))],
        out_shape=[jax.ShapeDtypeStruct((b, s, att_w), BF16),
                   jax.ShapeDtypeStruct((b, s, att_w), BF16),
                   jax.ShapeDtypeStruct((b, s, att_w), BF16),
                   jax.ShapeDtypeStruct((b, s, n_heads), F32),
                   jax.ShapeDtypeStruct((b, s, pw), F32),
                   jax.ShapeDtypeStruct((s, b * sw), F32)],
        scratch_shapes=[pltpu.VMEM((SUBLANES, LANES), F32)],
        compiler_params=_params(("arbitrary", "arbitrary"), VMEM_LIMIT_BYTES),
        name="inproj",
    )(x, g, wqkv, wf, wpool, wssm, fb, qg, kg, seg)


def _attn_kernel(q_ref, k_ref, v_ref, ct_ref, cr_ref, o_ref, m_sc, l_sc, acc_sc, s_sc, *, blk):
    p = pl.program_id(1)
    i = pl.program_id(2)
    q = q_ref[...]
    lane = lax.broadcasted_iota(I32, (blk, LANES), 1)
    ct = ct_ref[...]
    hl = lax.broadcasted_iota(I32, ct.shape, 1)
    heads = (2 * p, 2 * p + 1)
    qm = (jnp.where(lane < HEAD_DIM, q, jnp.zeros_like(q)),
          jnp.where(lane >= HEAD_DIM, q, jnp.zeros_like(q)))
    cq = tuple(jnp.sum(jnp.where(hl == hd, ct, 0.0), axis=1, keepdims=True) for hd in heads)
    m_sc[...] = jnp.full(m_sc.shape, -jnp.inf, F32)
    l_sc[...] = jnp.zeros(l_sc.shape, F32)
    acc_sc[...] = jnp.zeros(acc_sc.shape, F32)

    def qk(kb):
        kblk = k_ref[pl.ds(pl.multiple_of(kb * blk, blk), blk), :]
        return [lax.dot_general(qm[hh], kblk, _NT, preferred_element_type=F32) for hh in range(2)]

    def step(kb, s_cur, causal):
        start = pl.multiple_of(kb * blk, blk)
        vblk = v_ref[pl.ds(start, blk), :]
        for hh in range(2):
            ck = cr_ref[pl.ds(heads[hh], 1), pl.ds(start, blk)]
            z = (s_cur[hh] + cq[hh]) - ck
            if causal:
                row = lax.broadcasted_iota(I32, (blk, blk), 0)
                col = lax.broadcasted_iota(I32, (blk, blk), 1)
                z = jnp.where(col <= row, z, -jnp.inf)
            m_old = m_sc[hh]
            m_new = jnp.maximum(m_old, jnp.max(z, axis=1, keepdims=True))
            alpha = jnp.exp(m_old - m_new)
            pe = jnp.exp(z - m_new)
            l_sc[hh] = alpha * l_sc[hh] + jnp.sum(pe, axis=1, keepdims=True)
            acc_sc[hh] = alpha * acc_sc[hh] + jnp.dot(pe.astype(BF16), vblk,
                                                      preferred_element_type=F32)
            m_sc[hh] = m_new

    s0 = qk(0)
    s_sc[0] = s0[0]
    s_sc[1] = s0[1]

    def body(kb, carry):
        s_cur = [s_sc[0], s_sc[1]]
        s_next = qk(kb + 1)
        step(kb, s_cur, False)
        s_sc[0] = s_next[0]
        s_sc[1] = s_next[1]
        return carry

    lax.fori_loop(0, i, body, 0)
    step(i, [s_sc[0], s_sc[1]], True)
    o_ref[...] = jnp.where(lane < HEAD_DIM, acc_sc[0] / l_sc[0], acc_sc[1] / l_sc[1])


def _attention(q, k, v, c_tok, c_row, *, blk):
    b, s, w = q.shape
    n_heads = c_tok.shape[2]
    return pl.pallas_call(
        functools.partial(_attn_kernel, blk=blk),
        grid=(b, w // LANES, s // blk),
        in_specs=[pl.BlockSpec((None, blk, LANES), lambda bi, p, i: (bi, i, p)),
                  pl.BlockSpec((None, s, LANES), lambda bi, p, i: (bi, 0, p)),
                  pl.BlockSpec((None, s, LANES), lambda bi, p, i: (bi, 0, p)),
                  pl.BlockSpec((None, blk, n_heads), lambda bi, p, i: (bi, i, 0)),
                  pl.BlockSpec((None, n_heads, s), lambda bi, p, i: (bi, 0, 0))],
        out_specs=pl.BlockSpec((None, blk, LANES), lambda bi, p, i: (bi, i, p)),
        out_shape=jax.ShapeDtypeStruct((b, s, w), F32),
        scratch_shapes=[pltpu.VMEM((2, blk, 1), F32), pltpu.VMEM((2, blk, 1), F32),
                        pltpu.VMEM((2, blk, LANES), F32), pltpu.VMEM((2, blk, blk), F32)],
        compiler_params=_params(("arbitrary", "arbitrary", "arbitrary"), VMEM_LIMIT_BYTES),
        name="fox_attention",
    )(q, k, v, c_tok, c_row)


def _pool_kernel(u_ref, prev_ref, w_ref, sc_ref, o_ref, *, group_dim):
    i = pl.program_id(1)
    cur = u_ref[...]
    ts, pw = cur.shape
    prev = jnp.where(i > 0, prev_ref[...], 0.0)
    ext = jnp.concatenate([prev, cur], axis=0)
    pos = i * ts + lax.broadcasted_iota(I32, (ts, 1), 0)
    grp = lax.broadcasted_iota(I32, (ts, pw), 1) // group_dim
    acc = cur
    pooled = jnp.zeros_like(cur)
    for j in range(1, max(POOL_WINDOWS)):
        acc = acc + ext[POOL_HALO - j:POOL_HALO - j + ts, :]
        win = j + 1
        if win in POOL_WINDOWS:
            cnt = jnp.minimum(pos + 1, win).astype(F32)
            pooled = jnp.where(grp == POOL_WINDOWS.index(win), acc / cnt, pooled)
    pooled = pooled - cur
    o_ref[...] = jnp.dot(pooled.astype(BF16), w_ref[...],
                         preferred_element_type=F32) * sc_ref[...]


def _pool(u, wblk, scale, *, ts):
    b, s, pw = u.shape
    hb = ts // POOL_HALO
    return pl.pallas_call(
        functools.partial(_pool_kernel, group_dim=pw // len(POOL_WINDOWS)),
        grid=(b, s // ts),
        in_specs=[pl.BlockSpec((None, ts, pw), lambda bi, i: (bi, i, 0)),
                  pl.BlockSpec((None, POOL_HALO, pw),
                               lambda bi, i: (bi, jnp.maximum(i * hb - 1, 0), 0)),
                  pl.BlockSpec(wblk.shape, lambda bi, i: (0, 0)),
                  pl.BlockSpec(scale.shape, lambda bi, i: (0, 0))],
        out_specs=pl.BlockSpec((None, ts, pw), lambda bi, i: (bi, i, 0)),
        out_shape=jax.ShapeDtypeStruct((b, s, pw), F32),
        compiler_params=_params(("arbitrary", "arbitrary")),
        name="pool",
    )(u, u, wblk, scale)


def _s5_kernel(u_ref, are_ref, aim_ref, bblk_ref, cblk_ref, d_ref, gw_ref, gb_ref,
               o_ref, st_ref, hre_ref, him_ref, *, nb, n):
    @pl.when(pl.program_id(0) == 0)
    def _():
        hre_ref[...] = jnp.zeros_like(hre_ref)
        him_ref[...] = jnp.zeros_like(him_ref)

    u = u_ref[...]
    tc = u.shape[0] // nb
    st_ref[...] = jnp.dot(u.astype(BF16), bblk_ref[...], preferred_element_type=F32)
    ar = jnp.broadcast_to(are_ref[...], (nb, n))
    ai = jnp.broadcast_to(aim_ref[...], (nb, n))

    def step(t, carry):
        hr, hi = carry
        r0 = pl.multiple_of(t * nb, nb)
        br = st_ref[pl.ds(r0, nb), 0:n]
        bi = st_ref[pl.ds(r0, nb), n:2 * n]
        nr = (ar * hr - ai * hi) + br
        ni = (ar * hi + ai * hr) + bi
        st_ref[pl.ds(r0, nb), 0:n] = nr
        st_ref[pl.ds(r0, nb), n:2 * n] = ni
        return nr, ni

    hr, hi = lax.fori_loop(0, tc, step, (hre_ref[...], him_ref[...]))
    hre_ref[...] = hr
    him_ref[...] = hi
    y = jnp.dot(st_ref[...].astype(BF16), cblk_ref[...], preferred_element_type=F32)
    z = _gelu_tanh(y + d_ref[...] * u)
    gate = jax.nn.sigmoid(jnp.dot(z.astype(BF16), gw_ref[...], preferred_element_type=F32)
                          + gb_ref[...])
    o_ref[...] = z * gate


def _s5(u_tm, a_re, a_im, bblk, cblk, dskip, gw, gb, *, nb, tc):
    rows, sw = u_tm.shape
    n = a_re.shape[1]
    rb = tc * nb
    full = lambda a: pl.BlockSpec(a.shape, lambda i: (0,) * a.ndim)
    return pl.pallas_call(
        functools.partial(_s5_kernel, nb=nb, n=n),
        grid=(rows // rb,),
        in_specs=[pl.BlockSpec((rb, sw), lambda i: (i, 0)), full(a_re), full(a_im),
                  full(bblk), full(cblk), full(dskip), full(gw), full(gb)],
        out_specs=pl.BlockSpec((rb, sw), lambda i: (i, 0)),
        out_shape=jax.ShapeDtypeStruct((rows, sw), F32),
        scratch_shapes=[pltpu.VMEM((rb, 2 * n), F32), pltpu.VMEM((nb, n), F32),
                        pltpu.VMEM((nb, n), F32)],
        compiler_params=_params(("arbitrary",), VMEM_LIMIT_BYTES),
        name="s5_scan",
    )(u_tm, a_re, a_im, bblk, cblk, dskip, gw, gb)


def _merge_kernel(x_ref, att_ref, pool_ref, ssm_ref, og_ref, w_ref, o_ref, *, att_w, pool_w):
    og = og_ref[...]
    p1 = att_w + pool_w
    merged = jnp.concatenate([
        _rmsnorm(att_ref[...], og[:, :att_w]),
        _rmsnorm(pool_ref[...], og[:, att_w:p1]),
        _rmsnorm(ssm_ref[...], og[:, p1:])], axis=1).astype(BF16)
    ts = merged.shape[0]
    out = _load_tiled(x_ref, ts) + jnp.dot(merged, w_ref[...], preferred_element_type=F32)
    _store_tiled(o_ref, out, ts)


def _merge(x, att, pool, ssm_tm, og, w, *, ts):
    b, s = x.shape[0], x.shape[1] // SUBLANES
    att_w, pool_w = att.shape[2], pool.shape[2]
    sw = ssm_tm.shape[1] // b
    tok = lambda wd: pl.BlockSpec((None, ts, wd), lambda bi, i: (bi, i, 0))
    return pl.pallas_call(
        functools.partial(_merge_kernel, att_w=att_w, pool_w=pool_w),
        grid=(b, s // ts),
        in_specs=[_tiled_spec(ts), tok(att_w), tok(pool_w),
                  pl.BlockSpec((ts, sw), lambda bi, i: (i, bi)),
                  pl.BlockSpec(og.shape, lambda bi, i: (0, 0)),
                  pl.BlockSpec(w.shape, lambda bi, i: (0, 0))],
        out_specs=_tiled_spec(ts),
        out_shape=jax.ShapeDtypeStruct(x.shape, F32),
        compiler_params=_params(("arbitrary", "arbitrary"), VMEM_LIMIT_BYTES),
        name="merge_outproj",
    )(x, att, pool, ssm_tm, og, w)


def _topk_rows(a, k, extra=None):
    nrow, nl = a.shape
    row = lax.broadcasted_iota(I32, (nrow, nl), 0).astype(F32)
    out_row = lax.broadcasted_iota(I32, (k, nl), 0)
    vals = jnp.zeros((k, nl), F32)
    sel = jnp.zeros((k, nl), F32)
    for j in range(k):
        m = jnp.max(a, axis=0, keepdims=True)
        ix = jnp.min(jnp.where(a == m, row, float(nrow)), axis=0, keepdims=True)
        hit = row == ix
        if extra is None:
            pick = ix
        else:
            pick = jnp.sum(jnp.where(hit, extra, 0.0), axis=0, keepdims=True)
        vals = jnp.where(out_row == j, m, vals)
        sel = jnp.where(out_row == j, pick, sel)
        a = jnp.where(hit, -jnp.inf, a)
    return vals, sel


def _route_kernel(x_ref, g_ref, wq_ref, keys_ref, h_out, off_out, sh_out, gate_out, q_sc,
                  *, n_half):
    hd = pl.program_id(1)
    qd = 2 * PEER_KEYS

    @pl.when(hd == 0)
    def _():
        tm = q_sc.shape[0]
        h = _rmsnorm(_load_tiled(x_ref, tm), g_ref[...])
        _store_tiled(h_out, h, tm)
        q_sc[...] = jnp.dot(h.astype(BF16), wq_ref[...], preferred_element_type=F32)

    q = q_sc[:, pl.ds(pl.multiple_of(hd * qd, qd), qd)].astype(BF16)
    tops = []
    for i in range(2):
        sc = lax.dot_general(keys_ref[i], q[:, i * PEER_KEYS:(i + 1) * PEER_KEYS], _NT,
                             preferred_element_type=F32)
        tops.append(_topk_rows(sc, PEER_TOPK))
    (s1, i1), (s2, i2) = tops
    k = PEER_TOPK
    sub8 = lax.broadcasted_iota(I32, (SUBLANES, s1.shape[1]), 0)
    cands, ecands = [], []
    a = 0
    while a < k and k // (a + 1) > 1:
        nb = k // (a + 1)
        for b0 in range(0, nb, SUBLANES):
            c = s1[a:a + 1, :] + s2[b0:b0 + SUBLANES, :]
            if b0 + SUBLANES > nb:
                c = jnp.where(sub8 < nb - b0, c, -jnp.inf)
            cands.append(c)
            ecands.append(i1[a:a + 1, :] * float(PEER_KEYS) + i2[b0:b0 + SUBLANES, :])
        a += 1
    cands.append(s1[a:, :] + s2[0:1, :])
    ecands.append(i1[a:, :] * float(PEER_KEYS) + i2[0:1, :])
    best, experts = _topk_rows(jnp.concatenate(cands, axis=0), k,
                               extra=jnp.concatenate(ecands, axis=0))
    w = jnp.exp(best - best[0:1, :])
    gate_out[...] = w / jnp.sum(w, axis=0, keepdims=True)
    e = experts.astype(I32)
    hi_half = e < n_half
    off_out[...] = jnp.where(hi_half, e, e - n_half) * SUBLANES
    sh_out[...] = jnp.where(hi_half, 0.0, 16.0)


def _route(x, g, wq, keys, *, tm, n_experts):
    t = x.shape[0] // SUBLANES
    n_heads = keys.shape[0]
    hk = pl.BlockSpec((None, PEER_TOPK, tm), lambda i, h: (h, 0, i))
    shp = lambda dt: jax.ShapeDtypeStruct((n_heads, PEER_TOPK, t), dt)
    tiled = pl.BlockSpec((tm * SUBLANES, LANES), lambda i, h: (i, 0))
    return pl.pallas_call(
        functools.partial(_route_kernel, n_half=n_experts // 2),
        grid=(t // tm, n_heads),
        in_specs=[tiled,
                  pl.BlockSpec(g.shape, lambda i, h: (0, 0)),
                  pl.BlockSpec(wq.shape, lambda i, h: (0, 0)),
                  pl.BlockSpec((None, 2, PEER_KEYS, keys.shape[3]), lambda i, h: (h, 0, 0, 0))],
        out_specs=[tiled, hk, hk, hk],
        out_shape=[jax.ShapeDtypeStruct(x.shape, F32), shp(I32), shp(F32), shp(F32)],
        scratch_shapes=[pltpu.VMEM((tm, wq.shape[1]), F32)],
        compiler_params=_params(("arbitrary", "arbitrary"), VMEM_LIMIT_BYTES),
        name="peer_route",
    )(x, g, wq, keys)


def _expert_row(tab_ref, off, sh):
    w = tab_ref[pl.ds(pl.multiple_of(off, SUBLANES), SUBLANES), :]
    return lax.bitcast_convert_type((w << sh) & jnp.uint32(0xFFFF0000), F32)


def _splat_rows(mxu_src, mxu_dst, xlu_src=None, xlu_dst=None, *, tt, hk, also=None):
    eye = (lax.broadcasted_iota(I32, (hk, hk), 0) == lax.broadcasted_iota(I32, (hk, hk), 1))
    ones = jnp.ones((hk, hk), BF16)

    def body(t, carry):
        v = jnp.broadcast_to(mxu_src[pl.ds(t, 1), :], (hk, hk))
        rep = jnp.dot(jnp.where(eye, v, 0.0).astype(BF16), ones, preferred_element_type=F32)
        mxu_dst(t, rep.astype(I32))
        if xlu_src is not None:
            xlu_dst(t, jnp.broadcast_to(xlu_src[pl.ds(t, 1), :], (hk, hk)).T)
        if also is not None:
            also(t)
        return carry

    lax.fori_loop(0, tt, body, 0, unroll=8)


def _splat(rep_ref, row):
    return jnp.broadcast_to(rep_ref[pl.ds(row, 1), :], (SUBLANES, LANES))


def _peer_u_kernel(*refs, tt, hk):
    n_off = tt * hk // SUBLANES
    off_refs = [refs[0].at[pl.ds(j * n_off, n_off)] for j in range(SUBLANES)]
    sh_ref, shn_ref, h_ref, gate_ref, tab_ref, coef_ref, part_ref, srep_ref, pre_ref = refs[1:]
    ngrp = hk // SUBLANES

    def store_srep(t, block):
        srep_ref[pl.ds(pl.multiple_of(t * hk, hk), hk), :] = block

    @pl.when(pl.program_id(0) == 0)
    def _():
        _splat_rows(sh_ref, store_srep, tt=tt, hk=hk)

    chunk = U_EDGES_PER_DOT
    sel = (lax.broadcasted_iota(I32, (chunk, chunk * SUBLANES), 1) // SUBLANES
           == lax.broadcasted_iota(I32, (chunk, chunk * SUBLANES), 0)).astype(BF16)

    def tok(t, carry):
        h = h_ref[t]
        base = pl.multiple_of(t * hk, hk)
        for c in range(hk // chunk):
            e0 = base + c * chunk
            ps = [_expert_row(tab_ref, off_refs[k % SUBLANES][t * ngrp + (c * chunk + k) // SUBLANES],
                              _splat(srep_ref, e0 + k).astype(U32)) * h
                  for k in range(chunk)]
            stacked = jnp.concatenate(ps, axis=0).astype(BF16)
            part_ref[pl.ds(e0, chunk), :] = jnp.dot(sel, stacked, preferred_element_type=F32)
        return carry

    lax.fori_loop(0, tt, tok, 0, unroll=8)

    def lane_sums(t):
        p = part_ref[pl.ds(pl.multiple_of(t * hk, hk), hk), :]
        pre_ref[pl.ds(t, 1), :] = jnp.sum(p.T, axis=0, keepdims=True)

    _splat_rows(shn_ref, store_srep, tt=tt, hk=hk, also=lane_sums)
    coef_ref[...] = gate_ref[...] * _gelu_tanh(pre_ref[...])


def _peer_v_kernel(*refs, tt, hk, n_acc):
    n_off = tt * hk // SUBLANES
    off_refs = [refs[0].at[pl.ds(j * n_off, n_off)] for j in range(SUBLANES)]
    sh_ref, coef_ref, x_ref, tab_ref, o_ref, rep_ref = refs[1:]
    ngrp = hk // SUBLANES

    def store_coef(t, block):
        rep_ref[pl.ds(2 * t * hk, hk, stride=2), :] = block

    def store_shift(t, block):
        rep_ref[pl.ds(2 * t * hk + 1, hk, stride=2), :] = lax.bitcast_convert_type(block, F32)

    _splat_rows(sh_ref, store_shift, coef_ref, store_coef, tt=tt, hk=hk)

    def tok(t, carry):
        accs = [None] * n_acc
        e0 = 2 * t * hk
        for k in range(hk):
            off = off_refs[k % SUBLANES][t * ngrp + k // SUBLANES]
            sv = lax.bitcast_convert_type(_splat(rep_ref, e0 + 2 * k + 1), U32)
            term = _splat(rep_ref, e0 + 2 * k) * _expert_row(tab_ref, off, sv)
            accs[k % n_acc] = term if accs[k % n_acc] is None else accs[k % n_acc] + term
        while len(accs) > 1:
            accs = [accs[j] + accs[j + len(accs) // 2] for j in range(len(accs) // 2)]
        o_ref[t] = x_ref[t] + accs[0]
        return carry

    lax.fori_loop(0, tt, tok, 0, unroll=2)


def _resident(a):
    return pl.BlockSpec(a.shape, lambda i: (0,) * a.ndim, pipeline_mode=pl.Buffered(1))


def _split_offsets(off, tt):
    t, hk = off.shape
    grouped = off.reshape(t // tt, tt, hk // SUBLANES, SUBLANES)
    return jnp.transpose(grouped, (0, 3, 1, 2)).reshape(-1)


def _peer_u(off, sh, h, gates, tab, *, tt):
    t, hk = off.shape
    assert hk == LANES and h.shape[1:] == (SUBLANES, LANES)
    smem = pl.BlockSpec((tt * hk,), lambda i: (i,), memory_space=pltpu.SMEM)
    compact = pl.BlockSpec((tt, hk), lambda i: (i, 0))
    last = t // tt - 1
    nxt = pl.BlockSpec((tt, hk), lambda i: (jnp.minimum(i + 1, last), 0))
    return pl.pallas_call(
        functools.partial(_peer_u_kernel, tt=tt, hk=hk),
        grid=(t // tt,),
        in_specs=[
            smem, compact, nxt, pl.BlockSpec((tt, SUBLANES, LANES), lambda i: (i, 0, 0)), compact,
            _resident(tab)],
        out_specs=compact,
        out_shape=jax.ShapeDtypeStruct((t, hk), F32),
        scratch_shapes=[pltpu.VMEM((tt * hk, LANES), F32), pltpu.VMEM((tt * hk, LANES), I32),
                        pltpu.VMEM((tt, hk), F32)],
        compiler_params=_params(("arbitrary",), VMEM_LIMIT_BYTES),
        name="peer_u",
    )(_split_offsets(off, tt), sh, sh, h, gates, tab)


def _peer_v(off, sh, coef, x, tab, *, tt):
    t, hk = off.shape
    assert hk == LANES and x.shape[1:] == (SUBLANES, LANES)
    smem = pl.BlockSpec((tt * hk,), lambda i: (i,), memory_space=pltpu.SMEM)
    tile = pl.BlockSpec((tt, SUBLANES, LANES), lambda i: (i, 0, 0))
    compact = pl.BlockSpec((tt, hk), lambda i: (i, 0))
    return pl.pallas_call(
        functools.partial(_peer_v_kernel, tt=tt, hk=hk, n_acc=4),
        grid=(t // tt,),
        in_specs=[smem, compact, compact, tile, _resident(tab)],
        out_specs=tile,
        out_shape=jax.ShapeDtypeStruct(x.shape, F32),
        scratch_shapes=[pltpu.VMEM((2 * tt * hk, LANES), F32)],
        compiler_params=_params(("arbitrary",), VMEM_LIMIT_BYTES),
        name="peer_v",
    )(_split_offsets(off, tt), sh, coef, x, tab)


def _pack_kernel(hi_ref, lo_ref, o_ref):
    def bf16_bits(x):
        bits = lax.bitcast_convert_type(x, U32)
        r = bits + (jnp.uint32(0x7FFF) + ((bits >> 16) & jnp.uint32(1)))
        return jnp.where(x != x, jnp.uint32(0x7FC00000), r)

    w = (bf16_bits(hi_ref[...]) & jnp.uint32(0xFFFF0000)) | (bf16_bits(lo_ref[...]) >> 16)
    _store_tiled(o_ref, w, w.shape[0])


def _pack_table(tab, rows=256):
    n, d = tab.shape
    assert d == SUBLANES * LANES and (n // 2) % rows == 0
    nblk = n // 2 // rows
    return pl.pallas_call(
        _pack_kernel,
        grid=(nblk,),
        in_specs=[pl.BlockSpec((rows, d), lambda i: (i, 0)),
                  pl.BlockSpec((rows, d), lambda i: (i + nblk, 0))],
        out_specs=pl.BlockSpec((rows * SUBLANES, LANES), lambda i: (i, 0)),
        out_shape=jax.ShapeDtypeStruct((n // 2 * SUBLANES, LANES), U32),
        compiler_params=_params(("arbitrary",)),
        name="pack_table",
    )(tab, tab)


def _block_diag(blocks):
    g, r, c = blocks.shape
    eye = jnp.eye(g, dtype=blocks.dtype)
    return (eye[:, None, :, None] * blocks[:, :, None, :]).reshape(g * r, g * c)


def _s5_params(a_re, a_im, log_dt, b_re, b_im, c_re, c_im):
    a = lax.complex(a_re, a_im)
    dt = jnp.exp(log_dt)[:, None]
    a_bar = jnp.exp(a * dt)
    b_bar = ((a_bar - 1.0) / a)[..., None] * lax.complex(b_re, b_im)
    bt = jnp.swapaxes(b_bar, 1, 2)
    bblk = jnp.concatenate([_block_diag(jnp.real(bt)), _block_diag(jnp.imag(bt))], axis=1)
    ct_re = jnp.swapaxes(c_re, 1, 2)
    ct_im = jnp.swapaxes(c_im, 1, 2)
    cblk = jnp.concatenate([_block_diag(ct_re), -_block_diag(ct_im)], axis=0)
    n = a_bar.size
    return (jnp.real(a_bar).reshape(1, n), jnp.imag(a_bar).reshape(1, n),
            bblk.astype(BF16), cblk.astype(BF16))


def _tiles(s, t):
    ts = min(512, s)
    blk = min(512, s)
    tc = min(64, s)
    tm = min(512, t)
    tt = min(128, t)
    return ts, blk, tc, tm, tt


def _layer(x, norm1_g, w_in, f_bias, q_gain, k_gain, pool_w, pool_scale,
           a_re, a_im, log_dt, b_re, b_im, c_re, c_im, ssm_d, glu_w, glu_b,
           out_norm_g, w_out, norm2_g, peer_wq, peer_keys, peer_u, peer_v):
    b, s, d = x.shape[0], x.shape[1] // SUBLANES, SUBLANES * LANES
    t = b * s
    n_heads = f_bias.shape[0]
    att_w = n_heads * HEAD_DIM
    pool_width = pool_scale.shape[0]
    ssm_w = ssm_d.shape[0]
    ts, blk, tc, tm, tt = _tiles(s, t)
    row = lambda v: v.reshape(1, -1)

    c0, c1 = 3 * att_w, 3 * att_w + n_heads
    wf = jnp.zeros((d, LANES), F32).at[:, :n_heads].set(w_in[:, c0:c1])
    fb = jnp.zeros((1, LANES), F32).at[0, :n_heads].set(f_bias)
    seg = _block_diag(jnp.full((n_heads, HEAD_DIM, HEAD_DIM), 1.0 / HEAD_DIM, F32)).astype(BF16)
    q, k, v, c_tok, u_pool, u_ssm = _inproj(
        x, row(norm1_g), w_in[:, :c0].astype(BF16), wf,
        w_in[:, c1:c1 + pool_width].astype(BF16), w_in[:, c1 + pool_width:].astype(BF16), fb,
        row(jnp.tile(q_gain, n_heads) * HEAD_DIM ** -0.5), row(jnp.tile(k_gain, n_heads)),
        seg, ts=ts)
    att = _attention(q, k, v, c_tok, jnp.swapaxes(c_tok, 1, 2), blk=blk)
    pool = _pool(u_pool, _block_diag(pool_w).astype(BF16), row(pool_scale), ts=ts)
    are, aim, bblk, cblk = _s5_params(a_re, a_im, log_dt, b_re, b_im, c_re, c_im)
    ssm = _s5(u_ssm.reshape(s * b, ssm_w), are, aim, bblk, cblk, row(ssm_d),
              glu_w.astype(BF16), row(glu_b), nb=b, tc=tc)
    x1 = _merge(x, att, pool, ssm.reshape(s, b * ssm_w), row(out_norm_g),
                w_out.astype(BF16), ts=ts)

    n_experts = peer_u.shape[0]
    hk = peer_keys.shape[0] * PEER_TOPK
    kh = peer_keys.astype(BF16)
    h2, off, sh, gates = _route(x1.reshape(t * SUBLANES, LANES), row(norm2_g),
                                peer_wq.astype(BF16), kh, tm=tm, n_experts=n_experts)
    tok_major = lambda a: a.reshape(hk, t).T
    off, sh, gates = tok_major(off), tok_major(sh), tok_major(gates)
    tiles = lambda a: a.reshape(t, SUBLANES, LANES)
    coef = _peer_u(off, sh, tiles(h2), gates, _pack_table(peer_u), tt=tt)
    x2 = _peer_v(off, sh, coef, tiles(x1), _pack_table(peer_v), tt=tt)
    return x2.reshape(b, s * SUBLANES, LANES)


def kernel(x, norm1_g, w_in, f_bias, q_gain, k_gain, pool_w, pool_scale, ssm_a_re, ssm_a_im,
           ssm_log_dt, ssm_b_re, ssm_b_im, ssm_c_re, ssm_c_im, ssm_d, glu_w, glu_b, out_norm_g,
           w_out, norm2_g, peer_wq, peer_keys, peer_u, peer_v):
    per_layer = (norm1_g, w_in, f_bias, q_gain, k_gain, pool_w, pool_scale, ssm_a_re, ssm_a_im,
                 ssm_log_dt, ssm_b_re, ssm_b_im, ssm_c_re, ssm_c_im, ssm_d, glu_w, glu_b,
                 out_norm_g, w_out, norm2_g, peer_wq, peer_keys, peer_u, peer_v)
    b, s, d = x.shape
    assert d == SUBLANES * LANES
    x = x.reshape(b, s * SUBLANES, LANES)
    for l in range(norm1_g.shape[0]):
        x = _layer(x, *[p[l] for p in per_layer])
    return x.reshape(b, s, d)
```

```python
import functools
import math

import jax
import jax.numpy as jnp
from jax import lax
from jax.experimental import pallas as pl
from jax.experimental.pallas import tpu as pltpu

F32 = jnp.float32
BF16 = jnp.bfloat16
U32 = jnp.uint32
I32 = jnp.int32

EPS = 1e-6
HEAD_DIM = 64
POOL_WINDOWS = (2, 4, 8, 16)
POOL_HALO = 16
SSM_STATE = 64
SSM_GROUP_CH = 16
PEER_KEYS = 128
PEER_TOPK = 16
U_EDGES_PER_DOT = 32

LANES = 128
SUBLANES = 8
VMEM_LIMIT_BYTES = 56 * 1024 * 1024

_NT = (((1,), (1,)), ((), ()))


def _rmsnorm(x, g):
    return x * lax.rsqrt(jnp.mean(x * x, axis=-1, keepdims=True) + EPS) * g


def _gelu_tanh(x):
    c = math.sqrt(2.0 / math.pi)
    return x * (0.5 * (1.0 + jnp.tanh(c * (x + 0.044715 * (x * x * x)))))


def _params(sem, vmem=None):
    return pltpu.CompilerParams(dimension_semantics=sem, vmem_limit_bytes=vmem)


def _load_tiled(ref, n):
    return jnp.concatenate([ref[pl.ds(s, n, stride=SUBLANES), :] for s in range(SUBLANES)], axis=1)


def _store_tiled(ref, val, n):
    for s in range(SUBLANES):
        ref[pl.ds(s, n, stride=SUBLANES), :] = val[:, s * LANES:(s + 1) * LANES]


def _inproj_kernel(x_ref, g_ref, wqkv_ref, wf_ref, wpool_ref, wssm_ref, fb_ref,
                   qg_ref, kg_ref, seg_ref,
                   q_out, k_out, v_out, c_out, pool_out, ssm_out, carry_ref,
                   *, att_w, n_heads):
    i = pl.program_id(1)

    @pl.when(i == 0)
    def _():
        carry_ref[...] = jnp.zeros_like(carry_ref)

    ts = x_ref.shape[0] // SUBLANES
    x = _load_tiled(x_ref, ts)
    h = _rmsnorm(x, g_ref[...])
    hb = h.astype(BF16)
    qkv = jnp.dot(hb, wqkv_ref[...], preferred_element_type=F32)
    seg = seg_ref[...]

    def headnorm(t, gain):
        ms = jnp.dot((t * t).astype(BF16), seg, preferred_element_type=F32)
        return t * lax.rsqrt(ms + EPS) * gain

    q_out[...] = headnorm(qkv[:, :att_w], qg_ref[...]).astype(BF16)
    k_out[...] = headnorm(qkv[:, att_w:2 * att_w], kg_ref[...]).astype(BF16)
    v_out[...] = qkv[:, 2 * att_w:].astype(BF16)
    pool_out[...] = jnp.dot(hb, wpool_ref[...], preferred_element_type=F32)
    ssm_out[...] = jnp.dot(hb, wssm_ref[...], preferred_element_type=F32)

    f = jnp.dot(h, wf_ref[...], precision=lax.Precision.HIGHEST,
                preferred_element_type=F32) + fb_ref[...]
    ls = jnp.minimum(f, 0.0) - jnp.log1p(jnp.exp(-jnp.abs(f)))
    r = lax.broadcasted_iota(I32, (ts, ts), 0)
    c = lax.broadcasted_iota(I32, (ts, ts), 1)
    tri = jnp.where(c <= r, 1.0, 0.0).astype(F32)
    cs = jnp.dot(tri, ls, precision=lax.Precision.HIGHEST,
                 preferred_element_type=F32) + carry_ref[0:1, :]
    carry_ref[...] = jnp.broadcast_to(cs[ts - 1:ts, :], carry_ref.shape)
    c_out[...] = cs[:, :n_heads]


def _tiled_spec(ts):
    return pl.BlockSpec((None, ts * SUBLANES, LANES), lambda bi, i: (bi, i, 0))


def _inproj(x, g, wqkv, wf, wpool, wssm, fb, qg, kg, seg, *, ts):
    b, s = x.shape[0], x.shape[1] // SUBLANES
    att_w = seg.shape[0]
    n_heads = att_w // HEAD_DIM
    pw = wpool.shape[1]
    sw = wssm.shape[1]
    full = lambda a: pl.BlockSpec(a.shape, lambda bi, i: (0,) * a.ndim)
    tok = lambda w: pl.BlockSpec((None, ts, w), lambda bi, i: (bi, i, 0))
    return pl.pallas_call(
        functools.partial(_inproj_kernel, att_w=att_w, n_heads=n_heads),
        grid=(b, s // ts),
        in_specs=[_tiled_spec(ts), full(g), full(wqkv), full(wf), full(wpool), full(wssm),
                  full(fb), full(qg), full(kg), full(seg)],
        out_specs=[tok(att_w), tok(att_w), tok(att_w), tok(n_heads), tok(pw),
                   pl.BlockSpec((ts, sw), lambda bi, i: (i, bi))],
        out_shape=[jax.ShapeDtypeStruct((b, s, att_w), BF16),
                   jax.ShapeDtypeStruct((b, s, att_w), BF16),
                   jax.ShapeDtypeStruct((b, s, att_w), BF16),
                   jax.ShapeDtypeStruct((b, s, n_heads), F32),
                   jax.ShapeDtypeStruct((b, s, pw), F32),
                   jax.ShapeDtypeStruct((s, b * sw), F32)],
        scratch_shapes=[pltpu.VMEM((SUBLANES, LANES), F32)],
        compiler_params=_params(("arbitrary", "arbitrary"), VMEM_LIMIT_BYTES),
        name="inproj",
    )(x, g, wqkv, wf, wpool, wssm, fb, qg, kg, seg)


def _attn_kernel(q_ref, k_ref, v_ref, ct_ref, cr_ref, o_ref, m_sc, l_sc, acc_sc, s_sc, *, blk):
    p = pl.program_id(1)
    i = pl.program_id(2)
    q = q_ref[...]
    lane = lax.broadcasted_iota(I32, (blk, LANES), 1)
    ct = ct_ref[...]
    hl = lax.broadcasted_iota(I32, ct.shape, 1)
    heads = (2 * p, 2 * p + 1)
    qm = (jnp.where(lane < HEAD_DIM, q, jnp.zeros_like(q)),
          jnp.where(lane >= HEAD_DIM, q, jnp.zeros_like(q)))
    cq = tuple(jnp.sum(jnp.where(hl == hd, ct, 0.0), axis=1, keepdims=True) for hd in heads)
    m_sc[...] = jnp.full(m_sc.shape, -jnp.inf, F32)
    l_sc[...] = jnp.zeros(l_sc.shape, F32)
    acc_sc[...] = jnp.zeros(acc_sc.shape, F32)

    def qk(kb):
        kblk = k_ref[pl.ds(pl.multiple_of(kb * blk, blk), blk), :]
        return [lax.dot_general(qm[hh], kblk, _NT, preferred_element_type=F32) for hh in range(2)]

    def step(kb, s_cur, causal):
        start = pl.multiple_of(kb * blk, blk)
        vblk = v_ref[pl.ds(start, blk), :]
        for hh in range(2):
            ck = cr_ref[pl.ds(heads[hh], 1), pl.ds(start, blk)]
            z = (s_cur[hh] + cq[hh]) - ck
            if causal:
                row = lax.broadcasted_iota(I32, (blk, blk), 0)
                col = lax.broadcasted_iota(I32, (blk, blk), 1)
                z = jnp.where(col <= row, z, -jnp.inf)
            m_old = m_sc[hh]
            m_new = jnp.maximum(m_old, jnp.max(z, axis=1, keepdims=True))
            alpha = jnp.exp(m_old - m_new)
            pe = jnp.exp(z - m_new)
            l_sc[hh] = alpha * l_sc[hh] + jnp.sum(pe, axis=1, keepdims=True)
            acc_sc[hh] = alpha * acc_sc[hh] + jnp.dot(pe.astype(BF16), vblk,
                                                      preferred_element_type=F32)
            m_sc[hh] = m_new

    s0 = qk(0)
    s_sc[0] = s0[0]
    s_sc[1] = s0[1]

    def body(kb, carry):
        s_cur = [s_sc[0], s_sc[1]]
        s_next = qk(kb + 1)
        step(kb, s_cur, False)
        s_sc[0] = s_next[0]
        s_sc[1] = s_next[1]
        return carry

    lax.fori_loop(0, i, body, 0)
    step(i, [s_sc[0], s_sc[1]], True)
    o_ref[...] = jnp.where(lane < HEAD_DIM, acc_sc[0] / l_sc[0], acc_sc[1] / l_sc[1])


def _attention(q, k, v, c_tok, c_row, *, blk):
    b, s, w = q.shape
    n_heads = c_tok.shape[2]
    return pl.pallas_call(
        functools.partial(_attn_kernel, blk=blk),
        grid=(b, w // LANES, s // blk),
        in_specs=[pl.BlockSpec((None, blk, LANES), lambda bi, p, i: (bi, i, p)),
                  pl.BlockSpec((None, s, LANES), lambda bi, p, i: (bi, 0, p)),
                  pl.BlockSpec((None, s, LANES), lambda bi, p, i: (bi, 0, p)),
                  pl.BlockSpec((None, blk, n_heads), lambda bi, p, i: (bi, i, 0)),
                  pl.BlockSpec((None, n_heads, s), lambda bi, p, i: (bi, 0, 0))],
        out_specs=pl.BlockSpec((None, blk, LANES), lambda bi, p, i: (bi, i, p)),
        out_shape=jax.ShapeDtypeStruct((b, s, w), F32),
        scratch_shapes=[pltpu.VMEM((2, blk, 1), F32), pltpu.VMEM((2, blk, 1), F32),
                        pltpu.VMEM((2, blk, LANES), F32), pltpu.VMEM((2, blk, blk), F32)],
        compiler_params=_params(("arbitrary", "arbitrary", "arbitrary"), VMEM_LIMIT_BYTES),
        name="fox_attention",
    )(q, k, v, c_tok, c_row)


def _pool_kernel(u_ref, prev_ref, w_ref, sc_ref, o_ref, *, group_dim):
    i = pl.program_id(1)
    cur = u_ref[...]
    ts, pw = cur.shape
    prev = jnp.where(i > 0, prev_ref[...], 0.0)
    ext = jnp.concatenate([prev, cur], axis=0)
    pos = i * ts + lax.broadcasted_iota(I32, (ts, 1), 0)
    grp = lax.broadcasted_iota(I32, (ts, pw), 1) // group_dim
    acc = cur
    pooled = jnp.zeros_like(cur)
    for j in range(1, max(POOL_WINDOWS)):
        acc = acc + ext[POOL_HALO - j:POOL_HALO - j + ts, :]
        win = j + 1
        if win in POOL_WINDOWS:
            cnt = jnp.minimum(pos + 1, win).astype(F32)
            pooled = jnp.where(grp == POOL_WINDOWS.index(win), acc / cnt, pooled)
    pooled = pooled - cur
    o_ref[...] = jnp.dot(pooled.astype(BF16), w_ref[...],
                         preferred_element_type=F32) * sc_ref[...]


def _pool(u, wblk, scale, *, ts):
    b, s, pw = u.shape
    hb = ts // POOL_HALO
    return pl.pallas_call(
        functools.partial(_pool_kernel, group_dim=pw // len(POOL_WINDOWS)),
        grid=(b, s // ts),
        in_specs=[pl.BlockSpec((None, ts, pw), lambda bi, i: (bi, i, 0)),
                  pl.BlockSpec((None, POOL_HALO, pw),
                               lambda bi, i: (bi, jnp.maximum(i * hb - 1, 0), 0)),
                  pl.BlockSpec(wblk.shape, lambda bi, i: (0, 0)),
                  pl.BlockSpec(scale.shape, lambda bi, i: (0, 0))],
        out_specs=pl.BlockSpec((None, ts, pw), lambda bi, i: (bi, i, 0)),
        out_shape=jax.ShapeDtypeStruct((b, s, pw), F32),
        compiler_params=_params(("arbitrary", "arbitrary")),
        name="pool",
    )(u, u, wblk, scale)


def _s5_kernel(u_ref, are_ref, aim_ref, bblk_ref, cblk_ref, d_ref, gw_ref, gb_ref,
               o_ref, st_ref, hre_ref, him_ref, *, nb, n):
    @pl.when(pl.program_id(0) == 0)
    def _():
        hre_ref[...] = jnp.zeros_like(hre_ref)
        him_ref[...] = jnp.zeros_like(him_ref)

    u = u_ref[...]
    tc = u.shape[0] // nb
    st_ref[...] = jnp.dot(u.astype(BF16), bblk_ref[...], preferred_element_type=F32)
    ar = jnp.broadcast_to(are_ref[...], (nb, n))
    ai = jnp.broadcast_to(aim_ref[...], (nb, n))

    def step(t, carry):
        hr, hi = carry
        r0 = pl.multiple_of(t * nb, nb)
        br = st_ref[pl.ds(r0, nb), 0:n]
        bi = st_ref[pl.ds(r0, nb), n:2 * n]
        nr = (ar * hr - ai * hi) + br
        ni = (ar * hi + ai * hr) + bi
        st_ref[pl.ds(r0, nb), 0:n] = nr
        st_ref[pl.ds(r0, nb), n:2 * n] = ni
        return nr, ni

    hr, hi = lax.fori_loop(0, tc, step, (hre_ref[...], him_ref[...]))
    hre_ref[...] = hr
    him_ref[...] = hi
    y = jnp.dot(st_ref[...].astype(BF16), cblk_ref[...], preferred_element_type=F32)
    z = _gelu_tanh(y + d_ref[...] * u)
    gate = jax.nn.sigmoid(jnp.dot(z.astype(BF16), gw_ref[...], preferred_element_type=F32)
                          + gb_ref[...])
    o_ref[...] = z * gate


def _s5(u_tm, a_re, a_im, bblk, cblk, dskip, gw, gb, *, nb, tc):
    rows, sw = u_tm.shape
    n = a_re.shape[1]
    rb = tc * nb
    full = lambda a: pl.BlockSpec(a.shape, lambda i: (0,) * a.ndim)
    return pl.pallas_call(
        functools.partial(_s5_kernel, nb=nb, n=n),
        grid=(rows // rb,),
        in_specs=[pl.BlockSpec((rb, sw), lambda i: (i, 0)), full(a_re), full(a_im),
                  full(bblk), full(cblk), full(dskip), full(gw), full(gb)],
        out_specs=pl.BlockSpec((rb, sw), lambda i: (i, 0)),
        out_shape=jax.ShapeDtypeStruct((rows, sw), F32),
        scratch_shapes=[pltpu.VMEM((rb, 2 * n), F32), pltpu.VMEM((nb, n), F32),
                        pltpu.VMEM((nb, n), F32)],
        compiler_params=_params(("arbitrary",), VMEM_LIMIT_BYTES),
        name="s5_scan",
    )(u_tm, a_re, a_im, bblk, cblk, dskip, gw, gb)


def _merge_kernel(x_ref, att_ref, pool_ref, ssm_ref, og_ref, w_ref, o_ref, *, att_w, pool_w):
    og = og_ref[...]
    p1 = att_w + pool_w
    merged = jnp.concatenate([
        _rmsnorm(att_ref[...], og[:, :att_w]),
        _rmsnorm(pool_ref[...], og[:, att_w:p1]),
        _rmsnorm(ssm_ref[...], og[:, p1:])], axis=1).astype(BF16)
    ts = merged.shape[0]
    out = _load_tiled(x_ref, ts) + jnp.dot(merged, w_ref[...], preferred_element_type=F32)
    _store_tiled(o_ref, out, ts)


def _merge(x, att, pool, ssm_tm, og, w, *, ts):
    b, s = x.shape[0], x.shape[1] // SUBLANES
    att_w, pool_w = att.shape[2], pool.shape[2]
    sw = ssm_tm.shape[1] // b
    tok = lambda wd: pl.BlockSpec((None, ts, wd), lambda bi, i: (bi, i, 0))
    return pl.pallas_call(
        functools.partial(_merge_kernel, att_w=att_w, pool_w=pool_w),
        grid=(b, s // ts),
        in_specs=[_tiled_spec(ts), tok(att_w), tok(pool_w),
                  pl.BlockSpec((ts, sw), lambda bi, i: (i, bi)),
                  pl.BlockSpec(og.shape, lambda bi, i: (0, 0)),
                  pl.BlockSpec(w.shape, lambda bi, i: (0, 0))],
        out_specs=_tiled_spec(ts),
        out_shape=jax.ShapeDtypeStruct(x.shape, F32),
        compiler_params=_params(("arbitrary", "arbitrary"), VMEM_LIMIT_BYTES),
        name="merge_outproj",
    )(x, att, pool, ssm_tm, og, w)


def _topk_rows(a, k, extra=None):
    nrow, nl = a.shape
    row = lax.broadcasted_iota(I32, (nrow, nl), 0).astype(F32)
    out_row = lax.broadcasted_iota(I32, (k, nl), 0)
    vals = jnp.zeros((k, nl), F32)
    sel = jnp.zeros((k, nl), F32)
    for j in range(k):
        m = jnp.max(a, axis=0, keepdims=True)
        ix = jnp.min(jnp.where(a == m, row, float(nrow)), axis=0, keepdims=True)
        hit = row == ix
        if extra is None:
            pick = ix
        else:
            pick = jnp.sum(jnp.where(hit, extra, 0.0), axis=0, keepdims=True)
        vals = jnp.where(out_row == j, m, vals)
        sel = jnp.where(out_row == j, pick, sel)
        a = jnp.where(hit, -jnp.inf, a)
    return vals, sel


def _route_kernel(x_ref, g_ref, wq_ref, keys_ref, h_out, off_out, sh_out, gate_out, q_sc,
                  *, n_half):
    hd = pl.program_id(1)
    qd = 2 * PEER_KEYS

    @pl.when(hd == 0)
    def _():
        tm = q_sc.shape[0]
        h = _rmsnorm(_load_tiled(x_ref, tm), g_ref[...])
        _store_tiled(h_out, h, tm)
        q_sc[...] = jnp.dot(h.astype(BF16), wq_ref[...], preferred_element_type=F32)

    q = q_sc[:, pl.ds(pl.multiple_of(hd * qd, qd), qd)].astype(BF16)
    tops = []
    for i in range(2):
        sc = lax.dot_general(keys_ref[i], q[:, i * PEER_KEYS:(i + 1) * PEER_KEYS], _NT,
                             preferred_element_type=F32)
        tops.append(_topk_rows(sc, PEER_TOPK))
    (s1, i1), (s2, i2) = tops
    k = PEER_TOPK
    sub8 = lax.broadcasted_iota(I32, (SUBLANES, s1.shape[1]), 0)
    cands, ecands = [], []
    a = 0
    while a < k and k // (a + 1) > 1:
        nb = k // (a + 1)
        for b0 in range(0, nb, SUBLANES):
            c = s1[a:a + 1, :] + s2[b0:b0 + SUBLANES, :]
            if b0 + SUBLANES > nb:
                c = jnp.where(sub8 < nb - b0, c, -jnp.inf)
            cands.append(c)
            ecands.append(i1[a:a + 1, :] * float(PEER_KEYS) + i2[b0:b0 + SUBLANES, :])
        a += 1
    cands.append(s1[a:, :] + s2[0:1, :])
    ecands.append(i1[a:, :] * float(PEER_KEYS) + i2[0:1, :])
    best, experts = _topk_rows(jnp.concatenate(cands, axis=0), k,
                               extra=jnp.concatenate(ecands, axis=0))
    w = jnp.exp(best - best[0:1, :])
    gate_out[...] = w / jnp.sum(w, axis=0, keepdims=True)
    e = experts.astype(I32)
    hi_half = e < n_half
    off_out[...] = jnp.where(hi_half, e, e - n_half) * SUBLANES
    sh_out[...] = jnp.where(hi_half, 0.0, 16.0)


def _route(x, g, wq, keys, *, tm, n_experts):
    t = x.shape[0] // SUBLANES
    n_heads = keys.shape[0]
    hk = pl.BlockSpec((None, PEER_TOPK, tm), lambda i, h: (h, 0, i))
    shp = lambda dt: jax.ShapeDtypeStruct((n_heads, PEER_TOPK, t), dt)
    tiled = pl.BlockSpec((tm * SUBLANES, LANES), lambda i, h: (i, 0))
    return pl.pallas_call(
        functools.partial(_route_kernel, n_half=n_experts // 2),
        grid=(t // tm, n_heads),
        in_specs=[tiled,
                  pl.BlockSpec(g.shape, lambda i, h: (0, 0)),
                  pl.BlockSpec(wq.shape, lambda i, h: (0, 0)),
                  pl.BlockSpec((None, 2, PEER_KEYS, keys.shape[3]), lambda i, h: (h, 0, 0, 0))],
        out_specs=[tiled, hk, hk, hk],
        out_shape=[jax.ShapeDtypeStruct(x.shape, F32), shp(I32), shp(F32), shp(F32)],
        scratch_shapes=[pltpu.VMEM((tm, wq.shape[1]), F32)],
        compiler_params=_params(("arbitrary", "arbitrary"), VMEM_LIMIT_BYTES),
        name="peer_route",
    )(x, g, wq, keys)


def _expert_row(tab_ref, off, sh):
    w = tab_ref[pl.ds(pl.multiple_of(off, SUBLANES), SUBLANES), :]
    return lax.bitcast_convert_type((w << sh) & jnp.uint32(0xFFFF0000), F32)


def _splat_rows(mxu_src, mxu_dst, xlu_src=None, xlu_dst=None, *, tt, hk, also=None):
    eye = (lax.broadcasted_iota(I32, (hk, hk), 0) == lax.broadcasted_iota(I32, (hk, hk), 1))
    ones = jnp.ones((hk, hk), BF16)

    def body(t, carry):
        v = jnp.broadcast_to(mxu_src[pl.ds(t, 1), :], (hk, hk))
        rep = jnp.dot(jnp.where(eye, v, 0.0).astype(BF16), ones, preferred_element_type=F32)
        mxu_dst(t, rep.astype(I32))
        if xlu_src is not None:
            xlu_dst(t, jnp.broadcast_to(xlu_src[pl.ds(t, 1), :], (hk, hk)).T)
        if also is not None:
            also(t)
        return carry

    lax.fori_loop(0, tt, body, 0, unroll=8)


def _splat(rep_ref, row):
    return jnp.broadcast_to(rep_ref[pl.ds(row, 1), :], (SUBLANES, LANES))


def _peer_u_kernel(*refs, tt, hk):
    n_off = tt * hk // SUBLANES
    off_refs = [refs[0].at[pl.ds(j * n_off, n_off)] for j in range(SUBLANES)]
    sh_ref, shn_ref, h_ref, gate_ref, tab_ref, coef_ref, part_ref, srep_ref, pre_ref = refs[1:]
    ngrp = hk // SUBLANES

    def store_srep(t, block):
        srep_ref[pl.ds(pl.multiple_of(t * hk, hk), hk), :] = block

    @pl.when(pl.program_id(0) == 0)
    def _():
        _splat_rows(sh_ref, store_srep, tt=tt, hk=hk)

    chunk = U_EDGES_PER_DOT
    sel = (lax.broadcasted_iota(I32, (chunk, chunk * SUBLANES), 1) // SUBLANES
           == lax.broadcasted_iota(I32, (chunk, chunk * SUBLANES), 0)).astype(BF16)

    def tok(t, carry):
        h = h_ref[t]
        base = pl.multiple_of(t * hk, hk)
        for c in range(hk // chunk):
            e0 = base + c * chunk
            ps = [_expert_row(tab_ref, off_refs[k % SUBLANES][t * ngrp + (c * chunk + k) // SUBLANES],
                              _splat(srep_ref, e0 + k).astype(U32)) * h
                  for k in range(chunk)]
            stacked = jnp.concatenate(ps, axis=0).astype(BF16)
            part_ref[pl.ds(e0, chunk), :] = jnp.dot(sel, stacked, preferred_element_type=F32)
        return carry

    lax.fori_loop(0, tt, tok, 0, unroll=8)

    def lane_sums(t):
        p = part_ref[pl.ds(pl.multiple_of(t * hk, hk), hk), :]
        pre_ref[pl.ds(t, 1), :] = jnp.sum(p.T, axis=0, keepdims=True)

    _splat_rows(shn_ref, store_srep, tt=tt, hk=hk, also=lane_sums)
    coef_ref[...] = gate_ref[...] * _gelu_tanh(pre_ref[...])


def _peer_v_kernel(*refs, tt, hk, n_acc):
    half = hk // 2
    n_off = tt * half // SUBLANES
    off_refs = [refs[0].at[pl.ds(j * n_off, n_off)] for j in range(SUBLANES)]
    sh_ref, coef_ref, x_ref, tab_ref, o_ref, rep_ref = refs[1:]
    ngrp = half // SUBLANES

    def store_coef(t, block):
        rep_ref[pl.ds(2 * t * hk, hk, stride=2), :] = block

    def store_shift(t, block):
        rep_ref[pl.ds(2 * t * hk + 1, hk, stride=2), :] = lax.bitcast_convert_type(block, F32)

    _splat_rows(sh_ref, store_shift, coef_ref, store_coef, tt=tt, hk=hk)

    def tok(t, carry):
        accs = [None] * n_acc
        e0 = 2 * t * hk
        for k in range(half):
            w = off_refs[k % SUBLANES][t * ngrp + k // SUBLANES]
            for kk, off in ((k, w & 0xFFFF), (k + half, lax.shift_right_logical(w, 16))):
                sv = lax.bitcast_convert_type(_splat(rep_ref, e0 + 2 * kk + 1), U32)
                term = _splat(rep_ref, e0 + 2 * kk) * _expert_row(tab_ref, off, sv)
                a = kk % n_acc
                accs[a] = term if accs[a] is None else accs[a] + term
        while len(accs) > 1:
            accs = [accs[j] + accs[j + len(accs) // 2] for j in range(len(accs) // 2)]
        o_ref[t] = x_ref[t] + accs[0]
        return carry

    lax.fori_loop(0, tt, tok, 0, unroll=2)


def _resident(a):
    return pl.BlockSpec(a.shape, lambda i: (0,) * a.ndim, pipeline_mode=pl.Buffered(1))


def _split_offsets(off, tt):
    t, hk = off.shape
    grouped = off.reshape(t // tt, tt, hk // SUBLANES, SUBLANES)
    return jnp.transpose(grouped, (0, 3, 1, 2)).reshape(-1)


def _peer_u(off, sh, h, gates, tab, *, tt):
    t, hk = off.shape
    assert hk == LANES and h.shape[1:] == (SUBLANES, LANES)
    smem = pl.BlockSpec((tt * hk,), lambda i: (i,), memory_space=pltpu.SMEM)
    compact = pl.BlockSpec((tt, hk), lambda i: (i, 0))
    last = t // tt - 1
    nxt = pl.BlockSpec((tt, hk), lambda i: (jnp.minimum(i + 1, last), 0))
    return pl.pallas_call(
        functools.partial(_peer_u_kernel, tt=tt, hk=hk),
        grid=(t // tt,),
        in_specs=[
            smem, compact, nxt, pl.BlockSpec((tt, SUBLANES, LANES), lambda i: (i, 0, 0)), compact,
            _resident(tab)],
        out_specs=compact,
        out_shape=jax.ShapeDtypeStruct((t, hk), F32),
        scratch_shapes=[pltpu.VMEM((tt * hk, LANES), F32), pltpu.VMEM((tt * hk, LANES), I32),
                        pltpu.VMEM((tt, hk), F32)],
        compiler_params=_params(("arbitrary",), VMEM_LIMIT_BYTES),
        name="peer_u",
    )(_split_offsets(off, tt), sh, sh, h, gates, tab)


def _peer_v(off, sh, coef, x, tab, *, tt):
    t, hk = off.shape
    assert hk == LANES and x.shape[1:] == (SUBLANES, LANES)
    smem = pl.BlockSpec((tt * hk // 2,), lambda i: (i,), memory_space=pltpu.SMEM)
    tile = pl.BlockSpec((tt, SUBLANES, LANES), lambda i: (i, 0, 0))
    packed = off[:, :hk // 2] | (off[:, hk // 2:] << 16)
    compact = pl.BlockSpec((tt, hk), lambda i: (i, 0))
    return pl.pallas_call(
        functools.partial(_peer_v_kernel, tt=tt, hk=hk, n_acc=4),
        grid=(t // tt,),
        in_specs=[smem, compact, compact, tile, _resident(tab)],
        out_specs=tile,
        out_shape=jax.ShapeDtypeStruct(x.shape, F32),
        scratch_shapes=[pltpu.VMEM((2 * tt * hk, LANES), F32)],
        compiler_params=_params(("arbitrary",), VMEM_LIMIT_BYTES),
        name="peer_v",
    )(_split_offsets(packed, tt), sh, coef, x, tab)


def _pack_kernel(hi_ref, lo_ref, o_ref):
    def bf16_bits(x):
        bits = lax.bitcast_convert_type(x, U32)
        r = bits + (jnp.uint32(0x7FFF) + ((bits >> 16) & jnp.uint32(1)))
        return jnp.where(x != x, jnp.uint32(0x7FC00000), r)

    w = (bf16_bits(hi_ref[...]) & jnp.uint32(0xFFFF0000)) | (bf16_bits(lo_ref[...]) >> 16)
    _store_tiled(o_ref, w, w.shape[0])


def _pack_table(tab, rows=256):
    n, d = tab.shape
    assert d == SUBLANES * LANES and (n // 2) % rows == 0
    nblk = n // 2 // rows
    return pl.pallas_call(
        _pack_kernel,
        grid=(nblk,),
        in_specs=[pl.BlockSpec((rows, d), lambda i: (i, 0)),
                  pl.BlockSpec((rows, d), lambda i: (i + nblk, 0))],
        out_specs=pl.BlockSpec((rows * SUBLANES, LANES), lambda i: (i, 0)),
        out_shape=jax.ShapeDtypeStruct((n // 2 * SUBLANES, LANES), U32),
        compiler_params=_params(("arbitrary",)),
        name="pack_table",
    )(tab, tab)


def _block_diag(blocks):
    g, r, c = blocks.shape
    eye = jnp.eye(g, dtype=blocks.dtype)
    return (eye[:, None, :, None] * blocks[:, :, None, :]).reshape(g * r, g * c)


def _s5_params(a_re, a_im, log_dt, b_re, b_im, c_re, c_im):
    a = lax.complex(a_re, a_im)
    dt = jnp.exp(log_dt)[:, None]
    a_bar = jnp.exp(a * dt)
    b_bar = ((a_bar - 1.0) / a)[..., None] * lax.complex(b_re, b_im)
    bt = jnp.swapaxes(b_bar, 1, 2)
    bblk = jnp.concatenate([_block_diag(jnp.real(bt)), _block_diag(jnp.imag(bt))], axis=1)
    ct_re = jnp.swapaxes(c_re, 1, 2)
    ct_im = jnp.swapaxes(c_im, 1, 2)
    cblk = jnp.concatenate([_block_diag(ct_re), -_block_diag(ct_im)], axis=0)
    n = a_bar.size
    return (jnp.real(a_bar).reshape(1, n), jnp.imag(a_bar).reshape(1, n),
            bblk.astype(BF16), cblk.astype(BF16))


def _tiles(s, t):
    ts = min(512, s)
    blk = min(512, s)
    tc = min(64, s)
    tm = min(512, t)
    tt = min(128, t)
    return ts, blk, tc, tm, tt


def _layer(x, norm1_g, w_in, f_bias, q_gain, k_gain, pool_w, pool_scale,
           a_re, a_im, log_dt, b_re, b_im, c_re, c_im, ssm_d, glu_w, glu_b,
           out_norm_g, w_out, norm2_g, peer_wq, peer_keys, peer_u, peer_v):
    b, s, d = x.shape[0], x.shape[1] // SUBLANES, SUBLANES * LANES
    t = b * s
    n_heads = f_bias.shape[0]
    att_w = n_heads * HEAD_DIM
    pool_width = pool_scale.shape[0]
    ssm_w = ssm_d.shape[0]
    ts, blk, tc, tm, tt = _tiles(s, t)
    row = lambda v: v.reshape(1, -1)

    c0, c1 = 3 * att_w, 3 * att_w + n_heads
    wf = jnp.zeros((d, LANES), F32).at[:, :n_heads].set(w_in[:, c0:c1])
    fb = jnp.zeros((1, LANES), F32).at[0, :n_heads].set(f_bias)
    seg = _block_diag(jnp.full((n_heads, HEAD_DIM, HEAD_DIM), 1.0 / HEAD_DIM, F32)).astype(BF16)
    q, k, v, c_tok, u_pool, u_ssm = _inproj(
        x, row(norm1_g), w_in[:, :c0].astype(BF16), wf,
        w_in[:, c1:c1 + pool_width].astype(BF16), w_in[:, c1 + pool_width:].astype(BF16), fb,
        row(jnp.tile(q_gain, n_heads) * HEAD_DIM ** -0.5), row(jnp.tile(k_gain, n_heads)),
        seg, ts=ts)
    att = _attention(q, k, v, c_tok, jnp.swapaxes(c_tok, 1, 2), blk=blk)
    pool = _pool(u_pool, _block_diag(pool_w).astype(BF16), row(pool_scale), ts=ts)
    are, aim, bblk, cblk = _s5_params(a_re, a_im, log_dt, b_re, b_im, c_re, c_im)
    ssm = _s5(u_ssm.reshape(s * b, ssm_w), are, aim, bblk, cblk, row(ssm_d),
              glu_w.astype(BF16), row(glu_b), nb=b, tc=tc)
    x1 = _merge(x, att, pool, ssm.reshape(s, b * ssm_w), row(out_norm_g),
                w_out.astype(BF16), ts=ts)

    n_experts = peer_u.shape[0]
    hk = peer_keys.shape[0] * PEER_TOPK
    kh = peer_keys.astype(BF16)
    h2, off, sh, gates = _route(x1.reshape(t * SUBLANES, LANES), row(norm2_g),
                                peer_wq.astype(BF16), kh, tm=tm, n_experts=n_experts)
    tok_major = lambda a: a.reshape(hk, t).T
    off, sh, gates = tok_major(off), tok_major(sh), tok_major(gates)
    tiles = lambda a: a.reshape(t, SUBLANES, LANES)
    coef = _peer_u(off, sh, tiles(h2), gates, _pack_table(peer_u), tt=tt)
    x2 = _peer_v(off, sh, coef, tiles(x1), _pack_table(peer_v), tt=tt)
    return x2.reshape(b, s * SUBLANES, LANES)


def kernel(x, norm1_g, w_in, f_bias, q_gain, k_gain, pool_w, pool_scale, ssm_a_re, ssm_a_im,
           ssm_log_dt, ssm_b_re, ssm_b_im, ssm_c_re, ssm_c_im, ssm_d, glu_w, glu_b, out_norm_g,
           w_out, norm2_g, peer_wq, peer_keys, peer_u, peer_v):
    per_layer = (norm1_g, w_in, f_bias, q_gain, k_gain, pool_w, pool_scale, ssm_a_re, ssm_a_im,
                 ssm_log_dt, ssm_b_re, ssm_b_im, ssm_c_re, ssm_c_im, ssm_d, glu_w, glu_b,
                 out_norm_g, w_out, norm2_g, peer_wq, peer_keys, peer_u, peer_v)
    b, s, d = x.shape
    assert d == SUBLANES * LANES
    x = x.reshape(b, s * SUBLANES, LANES)
    for l in range(norm1_g.shape[0]):
        x = _layer(x, *[p[l] for p in per_layer])
    return x.reshape(b, s, d)
```

```python
import functools
import math

import jax
import jax.numpy as jnp
from jax import lax
from jax.experimental import pallas as pl
from jax.experimental.pallas import tpu as pltpu

F32 = jnp.float32
BF16 = jnp.bfloat16
U32 = jnp.uint32
I32 = jnp.int32

EPS = 1e-6
HEAD_DIM = 64
POOL_WINDOWS = (2, 4, 8, 16)
POOL_HALO = 16
SSM_STATE = 64
SSM_GROUP_CH = 16
PEER_KEYS = 128
PEER_TOPK = 16
U_EDGES_PER_DOT = 32

LANES = 128
SUBLANES = 8
VMEM_LIMIT_BYTES = 56 * 1024 * 1024

_NT = (((1,), (1,)), ((), ()))


def _rmsnorm(x, g):
    return x * lax.rsqrt(jnp.mean(x * x, axis=-1, keepdims=True) + EPS) * g


def _gelu_tanh(x):
    c = math.sqrt(2.0 / math.pi)
    return x * (0.5 * (1.0 + jnp.tanh(c * (x + 0.044715 * (x * x * x)))))


def _params(sem, vmem=None):
    return pltpu.CompilerParams(dimension_semantics=sem, vmem_limit_bytes=vmem)


def _load_tiled(ref, n):
    return jnp.concatenate([ref[pl.ds(s, n, stride=SUBLANES), :] for s in range(SUBLANES)], axis=1)


def _store_tiled(ref, val, n):
    for s in range(SUBLANES):
        ref[pl.ds(s, n, stride=SUBLANES), :] = val[:, s * LANES:(s + 1) * LANES]


def _inproj_kernel(x_ref, g_ref, wqkv_ref, wf_ref, wpool_ref, wssm_ref, fb_ref,
                   qg_ref, kg_ref, seg_ref, pmix_ref, pscale_ref,
                   q_out, k_out, v_out, c_out, pool_out, ssm_out, carry_ref, halo_ref,
                   *, att_w, n_heads):
    i = pl.program_id(1)

    @pl.when(i == 0)
    def _():
        carry_ref[...] = jnp.zeros_like(carry_ref)
        halo_ref[...] = jnp.zeros_like(halo_ref)

    ts = x_ref.shape[0] // SUBLANES
    x = _load_tiled(x_ref, ts)
    h = _rmsnorm(x, g_ref[...])
    hb = h.astype(BF16)
    qkv = jnp.dot(hb, wqkv_ref[...], preferred_element_type=F32)
    seg = seg_ref[...]

    def headnorm(t, gain):
        ms = jnp.dot((t * t).astype(BF16), seg, preferred_element_type=F32)
        return t * lax.rsqrt(ms + EPS) * gain

    q_out[...] = headnorm(qkv[:, :att_w], qg_ref[...]).astype(BF16)
    k_out[...] = headnorm(qkv[:, att_w:2 * att_w], kg_ref[...]).astype(BF16)
    v_out[...] = qkv[:, 2 * att_w:].astype(BF16)
    u_pool = jnp.dot(hb, wpool_ref[...], preferred_element_type=F32)
    pool_out[...] = _pool_mix(u_pool, halo_ref[...], i * ts, pmix_ref[...], pscale_ref[...])
    halo_ref[...] = u_pool[ts - POOL_HALO:, :]
    ssm_out[...] = jnp.dot(hb, wssm_ref[...], preferred_element_type=F32)

    f = jnp.dot(h, wf_ref[...], precision=lax.Precision.HIGHEST,
                preferred_element_type=F32) + fb_ref[...]
    ls = jnp.minimum(f, 0.0) - jnp.log1p(jnp.exp(-jnp.abs(f)))
    r = lax.broadcasted_iota(I32, (ts, ts), 0)
    c = lax.broadcasted_iota(I32, (ts, ts), 1)
    tri = jnp.where(c <= r, 1.0, 0.0).astype(F32)
    cs = jnp.dot(tri, ls, precision=lax.Precision.HIGHEST,
                 preferred_element_type=F32) + carry_ref[0:1, :]
    carry_ref[...] = jnp.broadcast_to(cs[ts - 1:ts, :], carry_ref.shape)
    c_out[...] = cs[:, :n_heads]


def _tiled_spec(ts):
    return pl.BlockSpec((None, ts * SUBLANES, LANES), lambda bi, i: (bi, i, 0))


def _inproj(x, g, wqkv, wf, wpool, wssm, fb, qg, kg, seg, pmix, pscale, *, ts):
    b, s = x.shape[0], x.shape[1] // SUBLANES
    att_w = seg.shape[0]
    n_heads = att_w // HEAD_DIM
    pw = wpool.shape[1]
    sw = wssm.shape[1]
    full = lambda a: pl.BlockSpec(a.shape, lambda bi, i: (0,) * a.ndim)
    tok = lambda w: pl.BlockSpec((None, ts, w), lambda bi, i: (bi, i, 0))
    return pl.pallas_call(
        functools.partial(_inproj_kernel, att_w=att_w, n_heads=n_heads),
        grid=(b, s // ts),
        in_specs=[_tiled_spec(ts), full(g), full(wqkv), full(wf), full(wpool), full(wssm),
                  full(fb), full(qg), full(kg), full(seg), full(pmix), full(pscale)],
        out_specs=[tok(att_w), tok(att_w), tok(att_w), tok(n_heads), tok(pw),
                   pl.BlockSpec((ts, sw), lambda bi, i: (i, bi))],
        out_shape=[jax.ShapeDtypeStruct((b, s, att_w), BF16),
                   jax.ShapeDtypeStruct((b, s, att_w), BF16),
                   jax.ShapeDtypeStruct((b, s, att_w), BF16),
                   jax.ShapeDtypeStruct((b, s, n_heads), F32),
                   jax.ShapeDtypeStruct((b, s, pw), F32),
                   jax.ShapeDtypeStruct((s, b * sw), F32)],
        scratch_shapes=[pltpu.VMEM((SUBLANES, LANES), F32), pltpu.VMEM((POOL_HALO, pw), F32)],
        compiler_params=_params(("arbitrary", "arbitrary"), VMEM_LIMIT_BYTES),
        name="inproj",
    )(x, g, wqkv, wf, wpool, wssm, fb, qg, kg, seg, pmix, pscale)


def _attn_kernel(q_ref, k_ref, v_ref, ct_ref, cr_ref, o_ref, m_sc, l_sc, acc_sc, s_sc, *, blk):
    p = pl.program_id(1)
    i = pl.program_id(2)
    q = q_ref[...]
    lane = lax.broadcasted_iota(I32, (blk, LANES), 1)
    ct = ct_ref[...]
    hl = lax.broadcasted_iota(I32, ct.shape, 1)
    heads = (2 * p, 2 * p + 1)
    qm = (jnp.where(lane < HEAD_DIM, q, jnp.zeros_like(q)),
          jnp.where(lane >= HEAD_DIM, q, jnp.zeros_like(q)))
    cq = tuple(jnp.sum(jnp.where(hl == hd, ct, 0.0), axis=1, keepdims=True) for hd in heads)
    m_sc[...] = jnp.full(m_sc.shape, -jnp.inf, F32)
    l_sc[...] = jnp.zeros(l_sc.shape, F32)
    acc_sc[...] = jnp.zeros(acc_sc.shape, F32)

    def qk(kb):
        kblk = k_ref[pl.ds(pl.multiple_of(kb * blk, blk), blk), :]
        return [lax.dot_general(qm[hh], kblk, _NT, preferred_element_type=F32) for hh in range(2)]

    def step(kb, s_cur, causal):
        start = pl.multiple_of(kb * blk, blk)
        vblk = v_ref[pl.ds(start, blk), :]
        for hh in range(2):
            ck = cr_ref[pl.ds(heads[hh], 1), pl.ds(start, blk)]
            z = (s_cur[hh] + cq[hh]) - ck
            if causal:
                row = lax.broadcasted_iota(I32, (blk, blk), 0)
                col = lax.broadcasted_iota(I32, (blk, blk), 1)
                z = jnp.where(col <= row, z, -jnp.inf)
            m_old = m_sc[hh]
            m_new = jnp.maximum(m_old, jnp.max(z, axis=1, keepdims=True))
            alpha = jnp.exp(m_old - m_new)
            pe = jnp.exp(z - m_new)
            l_sc[hh] = alpha * l_sc[hh] + jnp.sum(pe, axis=1, keepdims=True)
            acc_sc[hh] = alpha * acc_sc[hh] + jnp.dot(pe.astype(BF16), vblk,
                                                      preferred_element_type=F32)
            m_sc[hh] = m_new

    s0 = qk(0)
    s_sc[0] = s0[0]
    s_sc[1] = s0[1]

    def body(kb, carry):
        s_cur = [s_sc[0], s_sc[1]]
        s_next = qk(kb + 1)
        step(kb, s_cur, False)
        s_sc[0] = s_next[0]
        s_sc[1] = s_next[1]
        return carry

    lax.fori_loop(0, i, body, 0)
    step(i, [s_sc[0], s_sc[1]], True)
    o_ref[...] = jnp.where(lane < HEAD_DIM, acc_sc[0] / l_sc[0], acc_sc[1] / l_sc[1])


def _attention(q, k, v, c_tok, c_row, *, blk):
    b, s, w = q.shape
    n_heads = c_tok.shape[2]
    return pl.pallas_call(
        functools.partial(_attn_kernel, blk=blk),
        grid=(b, w // LANES, s // blk),
        in_specs=[pl.BlockSpec((None, blk, LANES), lambda bi, p, i: (bi, i, p)),
                  pl.BlockSpec((None, s, LANES), lambda bi, p, i: (bi, 0, p)),
                  pl.BlockSpec((None, s, LANES), lambda bi, p, i: (bi, 0, p)),
                  pl.BlockSpec((None, blk, n_heads), lambda bi, p, i: (bi, i, 0)),
                  pl.BlockSpec((None, n_heads, s), lambda bi, p, i: (bi, 0, 0))],
        out_specs=pl.BlockSpec((None, blk, LANES), lambda bi, p, i: (bi, i, p)),
        out_shape=jax.ShapeDtypeStruct((b, s, w), F32),
        scratch_shapes=[pltpu.VMEM((2, blk, 1), F32), pltpu.VMEM((2, blk, 1), F32),
                        pltpu.VMEM((2, blk, LANES), F32), pltpu.VMEM((2, blk, blk), F32)],
        compiler_params=_params(("arbitrary", "arbitrary", "arbitrary"), VMEM_LIMIT_BYTES),
        name="fox_attention",
    )(q, k, v, c_tok, c_row)


def _pool_mix(cur, prev, first_pos, w, scale):
    ts, pw = cur.shape
    group_dim = pw // len(POOL_WINDOWS)
    ext = jnp.concatenate([prev, cur], axis=0)
    pos = first_pos + lax.broadcasted_iota(I32, (ts, 1), 0)
    grp = lax.broadcasted_iota(I32, (ts, pw), 1) // group_dim
    acc = cur
    pooled = jnp.zeros_like(cur)
    for j in range(1, max(POOL_WINDOWS)):
        acc = acc + ext[POOL_HALO - j:POOL_HALO - j + ts, :]
        win = j + 1
        if win in POOL_WINDOWS:
            cnt = jnp.minimum(pos + 1, win).astype(F32)
            pooled = jnp.where(grp == POOL_WINDOWS.index(win), acc / cnt, pooled)
    pooled = pooled - cur
    return jnp.dot(pooled.astype(BF16), w, preferred_element_type=F32) * scale


def _s5_kernel(u_ref, are_ref, aim_ref, bblk_ref, cblk_ref, d_ref, gw_ref, gb_ref,
               o_ref, st_ref, hre_ref, him_ref, *, nb, n):
    @pl.when(pl.program_id(0) == 0)
    def _():
        hre_ref[...] = jnp.zeros_like(hre_ref)
        him_ref[...] = jnp.zeros_like(him_ref)

    u = u_ref[...]
    tc = u.shape[0] // nb
    st_ref[...] = jnp.dot(u.astype(BF16), bblk_ref[...], preferred_element_type=F32)
    ar = jnp.broadcast_to(are_ref[...], (nb, n))
    ai = jnp.broadcast_to(aim_ref[...], (nb, n))

    def step(t, carry):
        hr, hi = carry
        r0 = pl.multiple_of(t * nb, nb)
        br = st_ref[pl.ds(r0, nb), 0:n]
        bi = st_ref[pl.ds(r0, nb), n:2 * n]
        nr = (ar * hr - ai * hi) + br
        ni = (ar * hi + ai * hr) + bi
        st_ref[pl.ds(r0, nb), 0:n] = nr
        st_ref[pl.ds(r0, nb), n:2 * n] = ni
        return nr, ni

    hr, hi = lax.fori_loop(0, tc, step, (hre_ref[...], him_ref[...]))
    hre_ref[...] = hr
    him_ref[...] = hi
    y = jnp.dot(st_ref[...].astype(BF16), cblk_ref[...], preferred_element_type=F32)
    z = _gelu_tanh(y + d_ref[...] * u)
    gate = jax.nn.sigmoid(jnp.dot(z.astype(BF16), gw_ref[...], preferred_element_type=F32)
                          + gb_ref[...])
    o_ref[...] = z * gate


def _s5(u_tm, a_re, a_im, bblk, cblk, dskip, gw, gb, *, nb, tc):
    rows, sw = u_tm.shape
    n = a_re.shape[1]
    rb = tc * nb
    full = lambda a: pl.BlockSpec(a.shape, lambda i: (0,) * a.ndim)
    return pl.pallas_call(
        functools.partial(_s5_kernel, nb=nb, n=n),
        grid=(rows // rb,),
        in_specs=[pl.BlockSpec((rb, sw), lambda i: (i, 0)), full(a_re), full(a_im),
                  full(bblk), full(cblk), full(dskip), full(gw), full(gb)],
        out_specs=pl.BlockSpec((rb, sw), lambda i: (i, 0)),
        out_shape=jax.ShapeDtypeStruct((rows, sw), F32),
        scratch_shapes=[pltpu.VMEM((rb, 2 * n), F32), pltpu.VMEM((nb, n), F32),
                        pltpu.VMEM((nb, n), F32)],
        compiler_params=_params(("arbitrary",), VMEM_LIMIT_BYTES),
        name="s5_scan",
    )(u_tm, a_re, a_im, bblk, cblk, dskip, gw, gb)


def _merge_kernel(x_ref, att_ref, pool_ref, ssm_ref, og_ref, w_ref, o_ref, *, att_w, pool_w):
    og = og_ref[...]
    p1 = att_w + pool_w
    merged = jnp.concatenate([
        _rmsnorm(att_ref[...], og[:, :att_w]),
        _rmsnorm(pool_ref[...], og[:, att_w:p1]),
        _rmsnorm(ssm_ref[...], og[:, p1:])], axis=1).astype(BF16)
    ts = merged.shape[0]
    out = _load_tiled(x_ref, ts) + jnp.dot(merged, w_ref[...], preferred_element_type=F32)
    _store_tiled(o_ref, out, ts)


def _merge(x, att, pool, ssm_tm, og, w, *, ts):
    b, s = x.shape[0], x.shape[1] // SUBLANES
    att_w, pool_w = att.shape[2], pool.shape[2]
    sw = ssm_tm.shape[1] // b
    tok = lambda wd: pl.BlockSpec((None, ts, wd), lambda bi, i: (bi, i, 0))
    return pl.pallas_call(
        functools.partial(_merge_kernel, att_w=att_w, pool_w=pool_w),
        grid=(b, s // ts),
        in_specs=[_tiled_spec(ts), tok(att_w), tok(pool_w),
                  pl.BlockSpec((ts, sw), lambda bi, i: (i, bi)),
                  pl.BlockSpec(og.shape, lambda bi, i: (0, 0)),
                  pl.BlockSpec(w.shape, lambda bi, i: (0, 0))],
        out_specs=_tiled_spec(ts),
        out_shape=jax.ShapeDtypeStruct(x.shape, F32),
        compiler_params=_params(("arbitrary", "arbitrary"), VMEM_LIMIT_BYTES),
        name="merge_outproj",
    )(x, att, pool, ssm_tm, og, w)


def _topk_rows(a, k, extra=None):
    nrow, nl = a.shape
    row = lax.broadcasted_iota(I32, (nrow, nl), 0).astype(F32)
    out_row = lax.broadcasted_iota(I32, (k, nl), 0)
    vals = jnp.zeros((k, nl), F32)
    sel = jnp.zeros((k, nl), F32)
    for j in range(k):
        m = jnp.max(a, axis=0, keepdims=True)
        ix = jnp.min(jnp.where(a == m, row, float(nrow)), axis=0, keepdims=True)
        hit = row == ix
        if extra is None:
            pick = ix
        else:
            pick = jnp.sum(jnp.where(hit, extra, 0.0), axis=0, keepdims=True)
        vals = jnp.where(out_row == j, m, vals)
        sel = jnp.where(out_row == j, pick, sel)
        a = jnp.where(hit, -jnp.inf, a)
    return vals, sel


def _route_kernel(x_ref, g_ref, wq_ref, keys_ref, h_out, off_out, sh_out, gate_out, q_sc, stage_sc,
                  *, n_half):
    hd = pl.program_id(1)
    qd = 2 * PEER_KEYS

    @pl.when(hd == 0)
    def _():
        tm = q_sc.shape[0]
        h = _rmsnorm(_load_tiled(x_ref, tm), g_ref[...])
        _store_tiled(h_out, h, tm)
        q_sc[...] = jnp.dot(h.astype(BF16), wq_ref[...], preferred_element_type=F32)

    q = q_sc[:, pl.ds(pl.multiple_of(hd * qd, qd), qd)].astype(BF16)
    tops = []
    for i in range(2):
        sc = lax.dot_general(keys_ref[i], q[:, i * PEER_KEYS:(i + 1) * PEER_KEYS], _NT,
                             preferred_element_type=F32)
        tops.append(_topk_rows(sc, PEER_TOPK))
    (s1, i1), (s2, i2) = tops
    k = PEER_TOPK
    sub8 = lax.broadcasted_iota(I32, (SUBLANES, s1.shape[1]), 0)
    cands, ecands = [], []
    a = 0
    while a < k and k // (a + 1) > 1:
        nb = k // (a + 1)
        for b0 in range(0, nb, SUBLANES):
            c = s1[a:a + 1, :] + s2[b0:b0 + SUBLANES, :]
            if b0 + SUBLANES > nb:
                c = jnp.where(sub8 < nb - b0, c, -jnp.inf)
            cands.append(c)
            ecands.append(i1[a:a + 1, :] * float(PEER_KEYS) + i2[b0:b0 + SUBLANES, :])
        a += 1
    cands.append(s1[a:, :] + s2[0:1, :])
    ecands.append(i1[a:, :] * float(PEER_KEYS) + i2[0:1, :])
    best, experts = _topk_rows(jnp.concatenate(cands, axis=0), k,
                               extra=jnp.concatenate(ecands, axis=0))
    w = jnp.exp(best - best[0:1, :])
    rows = pl.ds(pl.multiple_of(hd * k, k), k)
    stage_sc[2, rows, :] = w / jnp.sum(w, axis=0, keepdims=True)
    hi_half = experts < float(n_half)
    stage_sc[0, rows, :] = jnp.where(hi_half, experts, experts - float(n_half)) * float(SUBLANES)
    stage_sc[1, rows, :] = jnp.where(hi_half, 0.0, 16.0)

    @pl.when(hd == pl.num_programs(1) - 1)
    def _():
        off_out[...] = stage_sc[0].T.astype(I32)
        sh_out[...] = stage_sc[1].T
        gate_out[...] = stage_sc[2].T


def _route(x, g, wq, keys, *, tm, n_experts):
    t = x.shape[0] // SUBLANES
    n_heads = keys.shape[0]
    n_sel = n_heads * PEER_TOPK
    hk = pl.BlockSpec((tm, n_sel), lambda i, h: (i, 0))
    shp = lambda dt: jax.ShapeDtypeStruct((t, n_sel), dt)
    tiled = pl.BlockSpec((tm * SUBLANES, LANES), lambda i, h: (i, 0))
    return pl.pallas_call(
        functools.partial(_route_kernel, n_half=n_experts // 2),
        grid=(t // tm, n_heads),
        in_specs=[tiled,
                  pl.BlockSpec(g.shape, lambda i, h: (0, 0)),
                  pl.BlockSpec(wq.shape, lambda i, h: (0, 0)),
                  pl.BlockSpec((None, 2, PEER_KEYS, keys.shape[3]), lambda i, h: (h, 0, 0, 0))],
        out_specs=[tiled, hk, hk, hk],
        out_shape=[jax.ShapeDtypeStruct(x.shape, F32), shp(I32), shp(F32), shp(F32)],
        scratch_shapes=[pltpu.VMEM((tm, wq.shape[1]), F32), pltpu.VMEM((3, n_sel, tm), F32)],
        compiler_params=_params(("arbitrary", "arbitrary"), VMEM_LIMIT_BYTES),
        name="peer_route",
    )(x, g, wq, keys)


def _expert_row(tab_ref, off, sh):
    w = tab_ref[pl.ds(pl.multiple_of(off, SUBLANES), SUBLANES), :]
    return lax.bitcast_convert_type((w << sh) & jnp.uint32(0xFFFF0000), F32)


def _splat_rows(mxu_src, mxu_dst, xlu_src=None, xlu_dst=None, *, tt, hk, also=None):
    eye = (lax.broadcasted_iota(I32, (hk, hk), 0) == lax.broadcasted_iota(I32, (hk, hk), 1))
    ones = jnp.ones((hk, hk), BF16)

    def body(t, carry):
        v = jnp.broadcast_to(mxu_src[pl.ds(t, 1), :], (hk, hk))
        rep = jnp.dot(jnp.where(eye, v, 0.0).astype(BF16), ones, preferred_element_type=F32)
        mxu_dst(t, rep.astype(I32))
        if xlu_src is not None:
            xlu_dst(t, jnp.broadcast_to(xlu_src[pl.ds(t, 1), :], (hk, hk)).T)
        if also is not None:
            also(t)
        return carry

    lax.fori_loop(0, tt, body, 0, unroll=8)


def _splat(rep_ref, row):
    return jnp.broadcast_to(rep_ref[pl.ds(row, 1), :], (SUBLANES, LANES))


def _peer_u_kernel(*refs, tt, hk):
    n_off = tt * hk // SUBLANES
    off_refs = [refs[0].at[pl.ds(j * n_off, n_off)] for j in range(SUBLANES)]
    sh_ref, shn_ref, h_ref, gate_ref, tab_ref, coef_ref, part_ref, srep_ref, pre_ref = refs[1:]
    ngrp = hk // SUBLANES

    def store_srep(t, block):
        srep_ref[pl.ds(pl.multiple_of(t * hk, hk), hk), :] = block

    @pl.when(pl.program_id(0) == 0)
    def _():
        _splat_rows(sh_ref, store_srep, tt=tt, hk=hk)

    chunk = U_EDGES_PER_DOT
    sel = (lax.broadcasted_iota(I32, (chunk, chunk * SUBLANES), 1) // SUBLANES
           == lax.broadcasted_iota(I32, (chunk, chunk * SUBLANES), 0)).astype(BF16)

    def tok(t, carry):
        h = h_ref[t]
        base = pl.multiple_of(t * hk, hk)
        for c in range(hk // chunk):
            e0 = base + c * chunk
            ps = [_expert_row(tab_ref, off_refs[k % SUBLANES][t * ngrp + (c * chunk + k) // SUBLANES],
                              _splat(srep_ref, e0 + k).astype(U32)) * h
                  for k in range(chunk)]
            stacked = jnp.concatenate(ps, axis=0).astype(BF16)
            part_ref[pl.ds(e0, chunk), :] = jnp.dot(sel, stacked, preferred_element_type=F32)
        return carry

    lax.fori_loop(0, tt, tok, 0, unroll=8)

    def lane_sums(t):
        p = part_ref[pl.ds(pl.multiple_of(t * hk, hk), hk), :]
        pre_ref[pl.ds(t, 1), :] = jnp.sum(p.T, axis=0, keepdims=True)

    _splat_rows(shn_ref, store_srep, tt=tt, hk=hk, also=lane_sums)
    coef_ref[...] = gate_ref[...] * _gelu_tanh(pre_ref[...])


def _peer_v_kernel(*refs, tt, hk, n_acc):
    n_off = tt * hk // SUBLANES
    off_refs = [refs[0].at[pl.ds(j * n_off, n_off)] for j in range(SUBLANES)]
    sh_ref, coef_ref, x_ref, tab_ref, o_ref, rep_ref = refs[1:]
    ngrp = hk // SUBLANES

    def store_coef(t, block):
        rep_ref[pl.ds(2 * t * hk, hk, stride=2), :] = block

    def store_shift(t, block):
        rep_ref[pl.ds(2 * t * hk + 1, hk, stride=2), :] = lax.bitcast_convert_type(block, F32)

    _splat_rows(sh_ref, store_shift, coef_ref, store_coef, tt=tt, hk=hk)

    def tok(t, carry):
        accs = [None] * n_acc
        e0 = 2 * t * hk
        for k in range(hk):
            off = off_refs[k % SUBLANES][t * ngrp + k // SUBLANES]
            sv = lax.bitcast_convert_type(_splat(rep_ref, e0 + 2 * k + 1), U32)
            term = _splat(rep_ref, e0 + 2 * k) * _expert_row(tab_ref, off, sv)
            accs[k % n_acc] = term if accs[k % n_acc] is None else accs[k % n_acc] + term
        while len(accs) > 1:
            accs = [accs[j] + accs[j + len(accs) // 2] for j in range(len(accs) // 2)]
        o_ref[t] = x_ref[t] + accs[0]
        return carry

    lax.fori_loop(0, tt, tok, 0, unroll=2)


def _resident(a):
    return pl.BlockSpec(a.shape, lambda i: (0,) * a.ndim, pipeline_mode=pl.Buffered(1))


def _split_offsets(off, tt):
    t, hk = off.shape
    grouped = off.reshape(t // tt, tt, hk // SUBLANES, SUBLANES)
    return jnp.transpose(grouped, (0, 3, 1, 2)).reshape(-1)


def _peer_u(off, sh, h, gates, tab, *, tt):
    t, hk = off.shape
    assert hk == LANES and h.shape[1:] == (SUBLANES, LANES)
    smem = pl.BlockSpec((tt * hk,), lambda i: (i,), memory_space=pltpu.SMEM)
    compact = pl.BlockSpec((tt, hk), lambda i: (i, 0))
    last = t // tt - 1
    nxt = pl.BlockSpec((tt, hk), lambda i: (jnp.minimum(i + 1, last), 0))
    return pl.pallas_call(
        functools.partial(_peer_u_kernel, tt=tt, hk=hk),
        grid=(t // tt,),
        in_specs=[
            smem, compact, nxt, pl.BlockSpec((tt, SUBLANES, LANES), lambda i: (i, 0, 0)), compact,
            _resident(tab)],
        out_specs=compact,
        out_shape=jax.ShapeDtypeStruct((t, hk), F32),
        scratch_shapes=[pltpu.VMEM((tt * hk, LANES), F32), pltpu.VMEM((tt * hk, LANES), I32),
                        pltpu.VMEM((tt, hk), F32)],
        compiler_params=_params(("arbitrary",), VMEM_LIMIT_BYTES),
        name="peer_u",
    )(_split_offsets(off, tt), sh, sh, h, gates, tab)


def _peer_v(off, sh, coef, x, tab, *, tt):
    t, hk = off.shape
    assert hk == LANES and x.shape[1:] == (SUBLANES, LANES)
    smem = pl.BlockSpec((tt * hk,), lambda i: (i,), memory_space=pltpu.SMEM)
    tile = pl.BlockSpec((tt, SUBLANES, LANES), lambda i: (i, 0, 0))
    compact = pl.BlockSpec((tt, hk), lambda i: (i, 0))
    return pl.pallas_call(
        functools.partial(_peer_v_kernel, tt=tt, hk=hk, n_acc=4),
        grid=(t // tt,),
        in_specs=[smem, compact, compact, tile, _resident(tab)],
        out_specs=tile,
        out_shape=jax.ShapeDtypeStruct(x.shape, F32),
        scratch_shapes=[pltpu.VMEM((2 * tt * hk, LANES), F32)],
        compiler_params=_params(("arbitrary",), VMEM_LIMIT_BYTES),
        name="peer_v",
    )(_split_offsets(off, tt), sh, coef, x, tab)


def _pack_kernel(hi_ref, lo_ref, o_ref):
    def bf16_bits(x):
        bits = lax.bitcast_convert_type(x, U32)
        r = bits + (jnp.uint32(0x7FFF) + ((bits >> 16) & jnp.uint32(1)))
        return jnp.where(x != x, jnp.uint32(0x7FC00000), r)

    w = (bf16_bits(hi_ref[...]) & jnp.uint32(0xFFFF0000)) | (bf16_bits(lo_ref[...]) >> 16)
    _store_tiled(o_ref, w, w.shape[0])


def _pack_table(tab, rows=256):
    n, d = tab.shape
    assert d == SUBLANES * LANES and (n // 2) % rows == 0
    nblk = n // 2 // rows
    return pl.pallas_call(
        _pack_kernel,
        grid=(nblk,),
        in_specs=[pl.BlockSpec((rows, d), lambda i: (i, 0)),
                  pl.BlockSpec((rows, d), lambda i: (i + nblk, 0))],
        out_specs=pl.BlockSpec((rows * SUBLANES, LANES), lambda i: (i, 0)),
        out_shape=jax.ShapeDtypeStruct((n // 2 * SUBLANES, LANES), U32),
        compiler_params=_params(("arbitrary",)),
        name="pack_table",
    )(tab, tab)


def _block_diag(blocks):
    g, r, c = blocks.shape
    eye = jnp.eye(g, dtype=blocks.dtype)
    return (eye[:, None, :, None] * blocks[:, :, None, :]).reshape(g * r, g * c)


def _s5_params(a_re, a_im, log_dt, b_re, b_im, c_re, c_im):
    a = lax.complex(a_re, a_im)
    dt = jnp.exp(log_dt)[:, None]
    a_bar = jnp.exp(a * dt)
    b_bar = ((a_bar - 1.0) / a)[..., None] * lax.complex(b_re, b_im)
    bt = jnp.swapaxes(b_bar, 1, 2)
    bblk = jnp.concatenate([_block_diag(jnp.real(bt)), _block_diag(jnp.imag(bt))], axis=1)
    ct_re = jnp.swapaxes(c_re, 1, 2)
    ct_im = jnp.swapaxes(c_im, 1, 2)
    cblk = jnp.concatenate([_block_diag(ct_re), -_block_diag(ct_im)], axis=0)
    n = a_bar.size
    return (jnp.real(a_bar).reshape(1, n), jnp.imag(a_bar).reshape(1, n),
            bblk.astype(BF16), cblk.astype(BF16))


def _tiles(s, t):
    ts = min(512, s)
    blk = min(512, s)
    tc = min(64, s)
    tm = min(512, t)
    tt = min(128, t)
    return ts, blk, tc, tm, tt


def _layer(x, norm1_g, w_in, f_bias, q_gain, k_gain, pool_w, pool_scale,
           a_re, a_im, log_dt, b_re, b_im, c_re, c_im, ssm_d, glu_w, glu_b,
           out_norm_g, w_out, norm2_g, peer_wq, peer_keys, peer_u, peer_v):
    b, s, d = x.shape[0], x.shape[1] // SUBLANES, SUBLANES * LANES
    t = b * s
    n_heads = f_bias.shape[0]
    att_w = n_heads * HEAD_DIM
    pool_width = pool_scale.shape[0]
    ssm_w = ssm_d.shape[0]
    ts, blk, tc, tm, tt = _tiles(s, t)
    row = lambda v: v.reshape(1, -1)

    c0, c1 = 3 * att_w, 3 * att_w + n_heads
    wf = jnp.zeros((d, LANES), F32).at[:, :n_heads].set(w_in[:, c0:c1])
    fb = jnp.zeros((1, LANES), F32).at[0, :n_heads].set(f_bias)
    seg = _block_diag(jnp.full((n_heads, HEAD_DIM, HEAD_DIM), 1.0 / HEAD_DIM, F32)).astype(BF16)
    q, k, v, c_tok, pool, u_ssm = _inproj(
        x, row(norm1_g), w_in[:, :c0].astype(BF16), wf,
        w_in[:, c1:c1 + pool_width].astype(BF16), w_in[:, c1 + pool_width:].astype(BF16), fb,
        row(jnp.tile(q_gain, n_heads) * HEAD_DIM ** -0.5), row(jnp.tile(k_gain, n_heads)),
        seg, _block_diag(pool_w).astype(BF16), row(pool_scale), ts=ts)
    att = _attention(q, k, v, c_tok, jnp.swapaxes(c_tok, 1, 2), blk=blk)
    are, aim, bblk, cblk = _s5_params(a_re, a_im, log_dt, b_re, b_im, c_re, c_im)
    ssm = _s5(u_ssm.reshape(s * b, ssm_w), are, aim, bblk, cblk, row(ssm_d),
              glu_w.astype(BF16), row(glu_b), nb=b, tc=tc)
    x1 = _merge(x, att, pool, ssm.reshape(s, b * ssm_w), row(out_norm_g),
                w_out.astype(BF16), ts=ts)

    n_experts = peer_u.shape[0]
    kh = peer_keys.astype(BF16)
    h2, off, sh, gates = _route(x1.reshape(t * SUBLANES, LANES), row(norm2_g),
                                peer_wq.astype(BF16), kh, tm=tm, n_experts=n_experts)
    tiles = lambda a: a.reshape(t, SUBLANES, LANES)
    coef = _peer_u(off, sh, tiles(h2), gates, _pack_table(peer_u), tt=tt)
    x2 = _peer_v(off, sh, coef, tiles(x1), _pack_table(peer_v), tt=tt)
    return x2.reshape(b, s * SUBLANES, LANES)


def kernel(x, norm1_g, w_in, f_bias, q_gain, k_gain, pool_w, pool_scale, ssm_a_re, ssm_a_im,
           ssm_log_dt, ssm_b_re, ssm_b_im, ssm_c_re, ssm_c_im, ssm_d, glu_w, glu_b, out_norm_g,
           w_out, norm2_g, peer_wq, peer_keys, peer_u, peer_v):
    per_layer = (norm1_g, w_in, f_bias, q_gain, k_gain, pool_w, pool_scale, ssm_a_re, ssm_a_im,
                 ssm_log_dt, ssm_b_re, ssm_b_im, ssm_c_re, ssm_c_im, ssm_d, glu_w, glu_b,
                 out_norm_g, w_out, norm2_g, peer_wq, peer_keys, peer_u, peer_v)
    b, s, d = x.shape
    assert d == SUBLANES * LANES
    x = x.reshape(b, s * SUBLANES, LANES)
    for l in range(norm1_g.shape[0]):
        x = _layer(x, *[p[l] for p in per_layer])
    return x.reshape(b, s, d)
```

```python
import functools
import math

import jax
import jax.numpy as jnp
from jax import lax
from jax.experimental import pallas as pl
from jax.experimental.pallas import tpu as pltpu

F32 = jnp.float32
BF16 = jnp.bfloat16
U32 = jnp.uint32
I32 = jnp.int32

EPS = 1e-6
HEAD_DIM = 64
POOL_WINDOWS = (2, 4, 8, 16)
POOL_HALO = 16
SSM_STATE = 64
SSM_GROUP_CH = 16
PEER_KEYS = 128
PEER_TOPK = 16
U_EDGES_PER_DOT = 32

LANES = 128
SUBLANES = 8
VMEM_LIMIT_BYTES = 56 * 1024 * 1024

_NT = (((1,), (1,)), ((), ()))


def _rmsnorm(x, g):
    return x * lax.rsqrt(jnp.mean(x * x, axis=-1, keepdims=True) + EPS) * g


def _gelu_tanh(x):
    c = math.sqrt(2.0 / math.pi)
    return x * (0.5 * (1.0 + jnp.tanh(c * (x + 0.044715 * (x * x * x)))))


def _params(sem, vmem=None):
    return pltpu.CompilerParams(dimension_semantics=sem, vmem_limit_bytes=vmem)


def _load_tiled(ref, n):
    return jnp.concatenate([ref[pl.ds(s, n, stride=SUBLANES), :] for s in range(SUBLANES)], axis=1)


def _store_tiled(ref, val, n):
    for s in range(SUBLANES):
        ref[pl.ds(s, n, stride=SUBLANES), :] = val[:, s * LANES:(s + 1) * LANES]


def _inproj_kernel(x_ref, g_ref, wqkv_ref, wf_ref, wpool_ref, wssm_ref, fb_ref,
                   qg_ref, kg_ref, seg_ref, pmix_ref, pscale_ref,
                   q_out, k_out, v_out, c_out, pool_out, ssm_out, carry_ref, halo_ref,
                   *, att_w, n_heads):
    i = pl.program_id(1)

    @pl.when(i == 0)
    def _():
        carry_ref[...] = jnp.zeros_like(carry_ref)
        halo_ref[...] = jnp.zeros_like(halo_ref)

    ts = x_ref.shape[0] // SUBLANES
    x = _load_tiled(x_ref, ts)
    h = _rmsnorm(x, g_ref[...])
    hb = h.astype(BF16)
    qkv = jnp.dot(hb, wqkv_ref[...], preferred_element_type=F32)
    seg = seg_ref[...]

    def headnorm(t, gain):
        ms = jnp.dot((t * t).astype(BF16), seg, preferred_element_type=F32)
        return t * lax.rsqrt(ms + EPS) * gain

    q_out[...] = headnorm(qkv[:, :att_w], qg_ref[...]).astype(BF16)
    k_out[...] = headnorm(qkv[:, att_w:2 * att_w], kg_ref[...]).astype(BF16)
    v_out[...] = qkv[:, 2 * att_w:].astype(BF16)
    u_pool = jnp.dot(hb, wpool_ref[...], preferred_element_type=F32)
    pool_out[...] = _pool_mix(u_pool, halo_ref[...], i * ts, pmix_ref[...], pscale_ref[...])
    halo_ref[...] = u_pool[ts - POOL_HALO:, :]
    ssm_out[...] = jnp.dot(hb, wssm_ref[...], preferred_element_type=F32)

    f = jnp.dot(h, wf_ref[...], precision=lax.Precision.HIGHEST,
                preferred_element_type=F32) + fb_ref[...]
    ls = jnp.minimum(f, 0.0) - jnp.log1p(jnp.exp(-jnp.abs(f)))
    r = lax.broadcasted_iota(I32, (ts, ts), 0)
    c = lax.broadcasted_iota(I32, (ts, ts), 1)
    tri = jnp.where(c <= r, 1.0, 0.0).astype(F32)
    cs = jnp.dot(tri, ls, precision=lax.Precision.HIGHEST,
                 preferred_element_type=F32) + carry_ref[0:1, :]
    carry_ref[...] = jnp.broadcast_to(cs[ts - 1:ts, :], carry_ref.shape)
    c_out[...] = cs[:, :n_heads]


def _tiled_spec(ts):
    return pl.BlockSpec((None, ts * SUBLANES, LANES), lambda bi, i: (bi, i, 0))


def _inproj(x, g, wqkv, wf, wpool, wssm, fb, qg, kg, seg, pmix, pscale, *, ts):
    b, s = x.shape[0], x.shape[1] // SUBLANES
    att_w = seg.shape[0]
    n_heads = att_w // HEAD_DIM
    pw = wpool.shape[1]
    sw = wssm.shape[1]
    full = lambda a: pl.BlockSpec(a.shape, lambda bi, i: (0,) * a.ndim)
    tok = lambda w: pl.BlockSpec((None, ts, w), lambda bi, i: (bi, i, 0))
    return pl.pallas_call(
        functools.partial(_inproj_kernel, att_w=att_w, n_heads=n_heads),
        grid=(b, s // ts),
        in_specs=[_tiled_spec(ts), full(g), full(wqkv), full(wf), full(wpool), full(wssm),
                  full(fb), full(qg), full(kg), full(seg), full(pmix), full(pscale)],
        out_specs=[tok(att_w), tok(att_w), tok(att_w), tok(n_heads), tok(pw), tok(sw)],
        out_shape=[jax.ShapeDtypeStruct((b, s, att_w), BF16),
                   jax.ShapeDtypeStruct((b, s, att_w), BF16),
                   jax.ShapeDtypeStruct((b, s, att_w), BF16),
                   jax.ShapeDtypeStruct((b, s, n_heads), F32),
                   jax.ShapeDtypeStruct((b, s, pw), F32),
                   jax.ShapeDtypeStruct((b, s, sw), F32)],
        scratch_shapes=[pltpu.VMEM((SUBLANES, LANES), F32), pltpu.VMEM((POOL_HALO, pw), F32)],
        compiler_params=_params(("arbitrary", "arbitrary"), VMEM_LIMIT_BYTES),
        name="inproj",
    )(x, g, wqkv, wf, wpool, wssm, fb, qg, kg, seg, pmix, pscale)


def _attn_kernel(q_ref, k_ref, v_ref, ct_ref, cr_ref, o_ref, m_sc, l_sc, acc_sc, s_sc, *, blk):
    p = pl.program_id(1)
    i = pl.program_id(2)
    q = q_ref[...]
    lane = lax.broadcasted_iota(I32, (blk, LANES), 1)
    ct = ct_ref[...]
    hl = lax.broadcasted_iota(I32, ct.shape, 1)
    heads = (2 * p, 2 * p + 1)
    qm = (jnp.where(lane < HEAD_DIM, q, jnp.zeros_like(q)),
          jnp.where(lane >= HEAD_DIM, q, jnp.zeros_like(q)))
    cq = tuple(jnp.sum(jnp.where(hl == hd, ct, 0.0), axis=1, keepdims=True) for hd in heads)
    m_sc[...] = jnp.full(m_sc.shape, -jnp.inf, F32)
    l_sc[...] = jnp.zeros(l_sc.shape, F32)
    acc_sc[...] = jnp.zeros(acc_sc.shape, F32)

    def qk(kb):
        kblk = k_ref[pl.ds(pl.multiple_of(kb * blk, blk), blk), :]
        return [lax.dot_general(qm[hh], kblk, _NT, preferred_element_type=F32) for hh in range(2)]

    def step(kb, s_cur, causal):
        start = pl.multiple_of(kb * blk, blk)
        vblk = v_ref[pl.ds(start, blk), :]
        for hh in range(2):
            ck = cr_ref[pl.ds(heads[hh], 1), pl.ds(start, blk)]
            z = (s_cur[hh] + cq[hh]) - ck
            if causal:
                row = lax.broadcasted_iota(I32, (blk, blk), 0)
                col = lax.broadcasted_iota(I32, (blk, blk), 1)
                z = jnp.where(col <= row, z, -jnp.inf)
            m_old = m_sc[hh]
            m_new = jnp.maximum(m_old, jnp.max(z, axis=1, keepdims=True))
            alpha = jnp.exp(m_old - m_new)
            pe = jnp.exp(z - m_new)
            l_sc[hh] = alpha * l_sc[hh] + jnp.sum(pe, axis=1, keepdims=True)
            acc_sc[hh] = alpha * acc_sc[hh] + jnp.dot(pe.astype(BF16), vblk,
                                                      preferred_element_type=F32)
            m_sc[hh] = m_new

    s0 = qk(0)
    s_sc[0] = s0[0]
    s_sc[1] = s0[1]

    def body(kb, carry):
        s_cur = [s_sc[0], s_sc[1]]
        s_next = qk(kb + 1)
        step(kb, s_cur, False)
        s_sc[0] = s_next[0]
        s_sc[1] = s_next[1]
        return carry

    lax.fori_loop(0, i, body, 0)
    step(i, [s_sc[0], s_sc[1]], True)
    o_ref[...] = jnp.where(lane < HEAD_DIM, acc_sc[0] / l_sc[0], acc_sc[1] / l_sc[1])


def _attention(q, k, v, c_tok, c_row, *, blk):
    b, s, w = q.shape
    n_heads = c_tok.shape[2]
    return pl.pallas_call(
        functools.partial(_attn_kernel, blk=blk),
        grid=(b, w // LANES, s // blk),
        in_specs=[pl.BlockSpec((None, blk, LANES), lambda bi, p, i: (bi, i, p)),
                  pl.BlockSpec((None, s, LANES), lambda bi, p, i: (bi, 0, p)),
                  pl.BlockSpec((None, s, LANES), lambda bi, p, i: (bi, 0, p)),
                  pl.BlockSpec((None, blk, n_heads), lambda bi, p, i: (bi, i, 0)),
                  pl.BlockSpec((None, n_heads, s), lambda bi, p, i: (bi, 0, 0))],
        out_specs=pl.BlockSpec((None, blk, LANES), lambda bi, p, i: (bi, i, p)),
        out_shape=jax.ShapeDtypeStruct((b, s, w), F32),
        scratch_shapes=[pltpu.VMEM((2, blk, 1), F32), pltpu.VMEM((2, blk, 1), F32),
                        pltpu.VMEM((2, blk, LANES), F32), pltpu.VMEM((2, blk, blk), F32)],
        compiler_params=_params(("arbitrary", "arbitrary", "arbitrary"), VMEM_LIMIT_BYTES),
        name="fox_attention",
    )(q, k, v, c_tok, c_row)


def _pool_mix(cur, prev, first_pos, w, scale):
    ts, pw = cur.shape
    group_dim = pw // len(POOL_WINDOWS)
    ext = jnp.concatenate([prev, cur], axis=0)
    pos = first_pos + lax.broadcasted_iota(I32, (ts, 1), 0)
    grp = lax.broadcasted_iota(I32, (ts, pw), 1) // group_dim
    acc = cur
    pooled = jnp.zeros_like(cur)
    for j in range(1, max(POOL_WINDOWS)):
        acc = acc + ext[POOL_HALO - j:POOL_HALO - j + ts, :]
        win = j + 1
        if win in POOL_WINDOWS:
            cnt = jnp.minimum(pos + 1, win).astype(F32)
            pooled = jnp.where(grp == POOL_WINDOWS.index(win), acc / cnt, pooled)
    pooled = pooled - cur
    return jnp.dot(pooled.astype(BF16), w, preferred_element_type=F32) * scale


def _s5_kernel(u_ref, are_ref, aim_ref, bblk_ref, cblk_ref, d_ref, gw_ref, gb_ref,
               o_ref, st_ref, hre_ref, him_ref, tm_ref, *, nb, n):
    @pl.when(pl.program_id(0) == 0)
    def _():
        hre_ref[...] = jnp.zeros_like(hre_ref)
        him_ref[...] = jnp.zeros_like(him_ref)

    tc = u_ref.shape[1]
    nslab = tm_ref.shape[0]
    for bi in range(nb):
        for c in range(nslab):
            tm_ref[c, pl.ds(bi, tc, stride=nb), :] = u_ref[bi, :, c * LANES:(c + 1) * LANES]
    u = jnp.concatenate([tm_ref[c] for c in range(nslab)], axis=1)
    st_ref[...] = jnp.dot(u.astype(BF16), bblk_ref[...], preferred_element_type=F32)
    ar = jnp.broadcast_to(are_ref[...], (nb, n))
    ai = jnp.broadcast_to(aim_ref[...], (nb, n))

    def step(t, carry):
        hr, hi = carry
        r0 = pl.multiple_of(t * nb, nb)
        br = st_ref[pl.ds(r0, nb), 0:n]
        bi = st_ref[pl.ds(r0, nb), n:2 * n]
        nr = (ar * hr - ai * hi) + br
        ni = (ar * hi + ai * hr) + bi
        st_ref[pl.ds(r0, nb), 0:n] = nr
        st_ref[pl.ds(r0, nb), n:2 * n] = ni
        return nr, ni

    hr, hi = lax.fori_loop(0, tc, step, (hre_ref[...], him_ref[...]))
    hre_ref[...] = hr
    him_ref[...] = hi
    y = jnp.dot(st_ref[...].astype(BF16), cblk_ref[...], preferred_element_type=F32)
    z = _gelu_tanh(y + d_ref[...] * u)
    gate = jax.nn.sigmoid(jnp.dot(z.astype(BF16), gw_ref[...], preferred_element_type=F32)
                          + gb_ref[...])
    out = z * gate
    for c in range(nslab):
        tm_ref[c] = out[:, c * LANES:(c + 1) * LANES]
    for bi in range(nb):
        for c in range(nslab):
            o_ref[bi, :, c * LANES:(c + 1) * LANES] = tm_ref[c, pl.ds(bi, tc, stride=nb), :]


def _s5(u, a_re, a_im, bblk, cblk, dskip, gw, gb, *, tc):
    nb, s, sw = u.shape
    n = a_re.shape[1]
    rb = tc * nb
    full = lambda a: pl.BlockSpec(a.shape, lambda i: (0,) * a.ndim)
    blk = pl.BlockSpec((nb, tc, sw), lambda i: (0, i, 0))
    return pl.pallas_call(
        functools.partial(_s5_kernel, nb=nb, n=n),
        grid=(s // tc,),
        in_specs=[blk, full(a_re), full(a_im),
                  full(bblk), full(cblk), full(dskip), full(gw), full(gb)],
        out_specs=blk,
        out_shape=jax.ShapeDtypeStruct(u.shape, F32),
        scratch_shapes=[pltpu.VMEM((rb, 2 * n), F32), pltpu.VMEM((nb, n), F32),
                        pltpu.VMEM((nb, n), F32), pltpu.VMEM((sw // LANES, rb, LANES), F32)],
        compiler_params=_params(("arbitrary",), VMEM_LIMIT_BYTES),
        name="s5_scan",
    )(u, a_re, a_im, bblk, cblk, dskip, gw, gb)


def _merge_kernel(x_ref, att_ref, pool_ref, ssm_ref, og_ref, w_ref, o_ref, *, att_w, pool_w):
    og = og_ref[...]
    p1 = att_w + pool_w
    merged = jnp.concatenate([
        _rmsnorm(att_ref[...], og[:, :att_w]),
        _rmsnorm(pool_ref[...], og[:, att_w:p1]),
        _rmsnorm(ssm_ref[...], og[:, p1:])], axis=1).astype(BF16)
    ts = merged.shape[0]
    out = _load_tiled(x_ref, ts) + jnp.dot(merged, w_ref[...], preferred_element_type=F32)
    _store_tiled(o_ref, out, ts)


def _merge(x, att, pool, ssm, og, w, *, ts):
    b, s = x.shape[0], x.shape[1] // SUBLANES
    att_w, pool_w, sw = att.shape[2], pool.shape[2], ssm.shape[2]
    tok = lambda wd: pl.BlockSpec((None, ts, wd), lambda bi, i: (bi, i, 0))
    return pl.pallas_call(
        functools.partial(_merge_kernel, att_w=att_w, pool_w=pool_w),
        grid=(b, s // ts),
        in_specs=[_tiled_spec(ts), tok(att_w), tok(pool_w), tok(sw),
                  pl.BlockSpec(og.shape, lambda bi, i: (0, 0)),
                  pl.BlockSpec(w.shape, lambda bi, i: (0, 0))],
        out_specs=_tiled_spec(ts),
        out_shape=jax.ShapeDtypeStruct(x.shape, F32),
        compiler_params=_params(("arbitrary", "arbitrary"), VMEM_LIMIT_BYTES),
        name="merge_outproj",
    )(x, att, pool, ssm, og, w)


def _topk_rows(a, k, extra=None):
    nrow, nl = a.shape
    row = lax.broadcasted_iota(I32, (nrow, nl), 0).astype(F32)
    out_row = lax.broadcasted_iota(I32, (k, nl), 0)
    vals = jnp.zeros((k, nl), F32)
    sel = jnp.zeros((k, nl), F32)
    for j in range(k):
        m = jnp.max(a, axis=0, keepdims=True)
        ix = jnp.min(jnp.where(a == m, row, float(nrow)), axis=0, keepdims=True)
        hit = row == ix
        if extra is None:
            pick = ix
        else:
            pick = jnp.sum(jnp.where(hit, extra, 0.0), axis=0, keepdims=True)
        vals = jnp.where(out_row == j, m, vals)
        sel = jnp.where(out_row == j, pick, sel)
        a = jnp.where(hit, -jnp.inf, a)
    return vals, sel


def _route_kernel(x_ref, g_ref, wq_ref, keys_ref, h_out, off_out, sh_out, gate_out, q_sc, stage_sc,
                  *, n_half):
    hd = pl.program_id(1)
    qd = 2 * PEER_KEYS

    @pl.when(hd == 0)
    def _():
        tm = q_sc.shape[0]
        h = _rmsnorm(_load_tiled(x_ref, tm), g_ref[...])
        _store_tiled(h_out, h, tm)
        q_sc[...] = jnp.dot(h.astype(BF16), wq_ref[...], preferred_element_type=F32)

    q = q_sc[:, pl.ds(pl.multiple_of(hd * qd, qd), qd)].astype(BF16)
    tops = []
    for i in range(2):
        sc = lax.dot_general(keys_ref[i], q[:, i * PEER_KEYS:(i + 1) * PEER_KEYS], _NT,
                             preferred_element_type=F32)
        tops.append(_topk_rows(sc, PEER_TOPK))
    (s1, i1), (s2, i2) = tops
    k = PEER_TOPK
    sub8 = lax.broadcasted_iota(I32, (SUBLANES, s1.shape[1]), 0)
    cands, ecands = [], []
    a = 0
    while a < k and k // (a + 1) > 1:
        nb = k // (a + 1)
        for b0 in range(0, nb, SUBLANES):
            c = s1[a:a + 1, :] + s2[b0:b0 + SUBLANES, :]
            if b0 + SUBLANES > nb:
                c = jnp.where(sub8 < nb - b0, c, -jnp.inf)
            cands.append(c)
            ecands.append(i1[a:a + 1, :] * float(PEER_KEYS) + i2[b0:b0 + SUBLANES, :])
        a += 1
    cands.append(s1[a:, :] + s2[0:1, :])
    ecands.append(i1[a:, :] * float(PEER_KEYS) + i2[0:1, :])
    best, experts = _topk_rows(jnp.concatenate(cands, axis=0), k,
                               extra=jnp.concatenate(ecands, axis=0))
    w = jnp.exp(best - best[0:1, :])
    rows = pl.ds(pl.multiple_of(hd * k, k), k)
    stage_sc[2, rows, :] = w / jnp.sum(w, axis=0, keepdims=True)
    hi_half = experts < float(n_half)
    stage_sc[0, rows, :] = jnp.where(hi_half, experts, experts - float(n_half)) * float(SUBLANES)
    stage_sc[1, rows, :] = jnp.where(hi_half, 0.0, 16.0)

    @pl.when(hd == pl.num_programs(1) - 1)
    def _():
        off_out[...] = stage_sc[0].T.astype(I32)
        sh_out[...] = stage_sc[1].T
        gate_out[...] = stage_sc[2].T


def _route(x, g, wq, keys, *, tm, n_experts):
    t = x.shape[0] // SUBLANES
    n_heads = keys.shape[0]
    n_sel = n_heads * PEER_TOPK
    hk = pl.BlockSpec((tm, n_sel), lambda i, h: (i, 0))
    shp = lambda dt: jax.ShapeDtypeStruct((t, n_sel), dt)
    tiled = pl.BlockSpec((tm * SUBLANES, LANES), lambda i, h: (i, 0))
    return pl.pallas_call(
        functools.partial(_route_kernel, n_half=n_experts // 2),
        grid=(t // tm, n_heads),
        in_specs=[tiled,
                  pl.BlockSpec(g.shape, lambda i, h: (0, 0)),
                  pl.BlockSpec(wq.shape, lambda i, h: (0, 0)),
                  pl.BlockSpec((None, 2, PEER_KEYS, keys.shape[3]), lambda i, h: (h, 0, 0, 0))],
        out_specs=[tiled, hk, hk, hk],
        out_shape=[jax.ShapeDtypeStruct(x.shape, F32), shp(I32), shp(F32), shp(F32)],
        scratch_shapes=[pltpu.VMEM((tm, wq.shape[1]), F32), pltpu.VMEM((3, n_sel, tm), F32)],
        compiler_params=_params(("arbitrary", "arbitrary"), VMEM_LIMIT_BYTES),
        name="peer_route",
    )(x, g, wq, keys)


def _expert_row(tab_ref, off, sh):
    w = tab_ref[pl.ds(pl.multiple_of(off, SUBLANES), SUBLANES), :]
    return lax.bitcast_convert_type((w << sh) & jnp.uint32(0xFFFF0000), F32)


def _splat_rows(mxu_src, mxu_dst, xlu_src=None, xlu_dst=None, *, tt, hk, also=None):
    eye = (lax.broadcasted_iota(I32, (hk, hk), 0) == lax.broadcasted_iota(I32, (hk, hk), 1))
    ones = jnp.ones((hk, hk), BF16)

    def body(t, carry):
        v = jnp.broadcast_to(mxu_src[pl.ds(t, 1), :], (hk, hk))
        rep = jnp.dot(jnp.where(eye, v, 0.0).astype(BF16), ones, preferred_element_type=F32)
        mxu_dst(t, rep.astype(I32))
        if xlu_src is not None:
            xlu_dst(t, jnp.broadcast_to(xlu_src[pl.ds(t, 1), :], (hk, hk)).T)
        if also is not None:
            also(t)
        return carry

    lax.fori_loop(0, tt, body, 0, unroll=8)


def _splat(rep_ref, row):
    return jnp.broadcast_to(rep_ref[pl.ds(row, 1), :], (SUBLANES, LANES))


def _peer_u_kernel(*refs, tt, hk):
    n_off = tt * hk // SUBLANES
    off_refs = [refs[0].at[pl.ds(j * n_off, n_off)] for j in range(SUBLANES)]
    sh_ref, shn_ref, h_ref, gate_ref, tab_ref, coef_ref, part_ref, srep_ref, pre_ref = refs[1:]
    ngrp = hk // SUBLANES

    def store_srep(t, block):
        srep_ref[pl.ds(pl.multiple_of(t * hk, hk), hk), :] = block

    @pl.when(pl.program_id(0) == 0)
    def _():
        _splat_rows(sh_ref, store_srep, tt=tt, hk=hk)

    chunk = U_EDGES_PER_DOT
    sel = (lax.broadcasted_iota(I32, (chunk, chunk * SUBLANES), 1) // SUBLANES
           == lax.broadcasted_iota(I32, (chunk, chunk * SUBLANES), 0)).astype(BF16)

    def tok(t, carry):
        h = h_ref[t]
        base = pl.multiple_of(t * hk, hk)
        for c in range(hk // chunk):
            e0 = base + c * chunk
            ps = [_expert_row(tab_ref, off_refs[k % SUBLANES][t * ngrp + (c * chunk + k) // SUBLANES],
                              _splat(srep_ref, e0 + k).astype(U32)) * h
                  for k in range(chunk)]
            stacked = jnp.concatenate(ps, axis=0).astype(BF16)
            part_ref[pl.ds(e0, chunk), :] = jnp.dot(sel, stacked, preferred_element_type=F32)
        return carry

    lax.fori_loop(0, tt, tok, 0, unroll=8)

    def lane_sums(t):
        p = part_ref[pl.ds(pl.multiple_of(t * hk, hk), hk), :]
        pre_ref[pl.ds(t, 1), :] = jnp.sum(p.T, axis=0, keepdims=True)

    _splat_rows(shn_ref, store_srep, tt=tt, hk=hk, also=lane_sums)
    coef_ref[...] = gate_ref[...] * _gelu_tanh(pre_ref[...])


def _peer_v_kernel(*refs, tt, hk, n_acc):
    n_off = tt * hk // SUBLANES
    off_refs = [refs[0].at[pl.ds(j * n_off, n_off)] for j in range(SUBLANES)]
    sh_ref, coef_ref, x_ref, tab_ref, o_ref, rep_ref = refs[1:]
    ngrp = hk // SUBLANES

    def store_coef(t, block):
        rep_ref[pl.ds(2 * t * hk, hk, stride=2), :] = block

    def store_shift(t, block):
        rep_ref[pl.ds(2 * t * hk + 1, hk, stride=2), :] = lax.bitcast_convert_type(block, F32)

    _splat_rows(sh_ref, store_shift, coef_ref, store_coef, tt=tt, hk=hk)

    def tok(t, carry):
        accs = [None] * n_acc
        e0 = 2 * t * hk
        for k in range(hk):
            off = off_refs[k % SUBLANES][t * ngrp + k // SUBLANES]
            sv = lax.bitcast_convert_type(_splat(rep_ref, e0 + 2 * k + 1), U32)
            term = _splat(rep_ref, e0 + 2 * k) * _expert_row(tab_ref, off, sv)
            accs[k % n_acc] = term if accs[k % n_acc] is None else accs[k % n_acc] + term
        while len(accs) > 1:
            accs = [accs[j] + accs[j + len(accs) // 2] for j in range(len(accs) // 2)]
        o_ref[t] = x_ref[t] + accs[0]
        return carry

    lax.fori_loop(0, tt, tok, 0, unroll=2)


def _resident(a):
    return pl.BlockSpec(a.shape, lambda i: (0,) * a.ndim, pipeline_mode=pl.Buffered(1))


def _split_offsets(off, tt):
    t, hk = off.shape
    grouped = off.reshape(t // tt, tt, hk // SUBLANES, SUBLANES)
    return jnp.transpose(grouped, (0, 3, 1, 2)).reshape(-1)


def _peer_u(off, sh, h, gates, tab, *, tt):
    t, hk = off.shape
    assert hk == LANES and h.shape[1:] == (SUBLANES, LANES)
    smem = pl.BlockSpec((tt * hk,), lambda i: (i,), memory_space=pltpu.SMEM)
    compact = pl.BlockSpec((tt, hk), lambda i: (i, 0))
    last = t // tt - 1
    nxt = pl.BlockSpec((tt, hk), lambda i: (jnp.minimum(i + 1, last), 0))
    return pl.pallas_call(
        functools.partial(_peer_u_kernel, tt=tt, hk=hk),
        grid=(t // tt,),
        in_specs=[
            smem, compact, nxt, pl.BlockSpec((tt, SUBLANES, LANES), lambda i: (i, 0, 0)), compact,
            _resident(tab)],
        out_specs=compact,
        out_shape=jax.ShapeDtypeStruct((t, hk), F32),
        scratch_shapes=[pltpu.VMEM((tt * hk, LANES), F32), pltpu.VMEM((tt * hk, LANES), I32),
                        pltpu.VMEM((tt, hk), F32)],
        compiler_params=_params(("arbitrary",), VMEM_LIMIT_BYTES),
        name="peer_u",
    )(_split_offsets(off, tt), sh, sh, h, gates, tab)


def _peer_v(off, sh, coef, x, tab, *, tt):
    t, hk = off.shape
    assert hk == LANES and x.shape[1:] == (SUBLANES, LANES)
    smem = pl.BlockSpec((tt * hk,), lambda i: (i,), memory_space=pltpu.SMEM)
    tile = pl.BlockSpec((tt, SUBLANES, LANES), lambda i: (i, 0, 0))
    compact = pl.BlockSpec((tt, hk), lambda i: (i, 0))
    return pl.pallas_call(
        functools.partial(_peer_v_kernel, tt=tt, hk=hk, n_acc=4),
        grid=(t // tt,),
        in_specs=[smem, compact, compact, tile, _resident(tab)],
        out_specs=tile,
        out_shape=jax.ShapeDtypeStruct(x.shape, F32),
        scratch_shapes=[pltpu.VMEM((2 * tt * hk, LANES), F32)],
        compiler_params=_params(("arbitrary",), VMEM_LIMIT_BYTES),
        name="peer_v",
    )(_split_offsets(off, tt), sh, coef, x, tab)


def _pack_kernel(hi_ref, lo_ref, o_ref):
    def bf16_bits(x):
        bits = lax.bitcast_convert_type(x, U32)
        r = bits + (jnp.uint32(0x7FFF) + ((bits >> 16) & jnp.uint32(1)))
        return jnp.where(x != x, jnp.uint32(0x7FC00000), r)

    w = (bf16_bits(hi_ref[...]) & jnp.uint32(0xFFFF0000)) | (bf16_bits(lo_ref[...]) >> 16)
    _store_tiled(o_ref, w, w.shape[0])


def _pack_table(tab, rows=256):
    n, d = tab.shape
    assert d == SUBLANES * LANES and (n // 2) % rows == 0
    nblk = n // 2 // rows
    return pl.pallas_call(
        _pack_kernel,
        grid=(nblk,),
        in_specs=[pl.BlockSpec((rows, d), lambda i: (i, 0)),
                  pl.BlockSpec((rows, d), lambda i: (i + nblk, 0))],
        out_specs=pl.BlockSpec((rows * SUBLANES, LANES), lambda i: (i, 0)),
        out_shape=jax.ShapeDtypeStruct((n // 2 * SUBLANES, LANES), U32),
        compiler_params=_params(("arbitrary",)),
        name="pack_table",
    )(tab, tab)


def _block_diag(blocks):
    g, r, c = blocks.shape
    eye = jnp.eye(g, dtype=blocks.dtype)
    return (eye[:, None, :, None] * blocks[:, :, None, :]).reshape(g * r, g * c)


def _s5_params(a_re, a_im, log_dt, b_re, b_im, c_re, c_im):
    a = lax.complex(a_re, a_im)
    dt = jnp.exp(log_dt)[:, None]
    a_bar = jnp.exp(a * dt)
    b_bar = ((a_bar - 1.0) / a)[..., None] * lax.complex(b_re, b_im)
    bt = jnp.swapaxes(b_bar, 1, 2)
    bblk = jnp.concatenate([_block_diag(jnp.real(bt)), _block_diag(jnp.imag(bt))], axis=1)
    ct_re = jnp.swapaxes(c_re, 1, 2)
    ct_im = jnp.swapaxes(c_im, 1, 2)
    cblk = jnp.concatenate([_block_diag(ct_re), -_block_diag(ct_im)], axis=0)
    n = a_bar.size
    return (jnp.real(a_bar).reshape(1, n), jnp.imag(a_bar).reshape(1, n),
            bblk.astype(BF16), cblk.astype(BF16))


def _tiles(s, t):
    ts = min(512, s)
    blk = min(512, s)
    tc = min(64, s)
    tm = min(512, t)
    tt = min(128, t)
    return ts, blk, tc, tm, tt


def _layer(x, norm1_g, w_in, f_bias, q_gain, k_gain, pool_w, pool_scale,
           a_re, a_im, log_dt, b_re, b_im, c_re, c_im, ssm_d, glu_w, glu_b,
           out_norm_g, w_out, norm2_g, peer_wq, peer_keys, peer_u, peer_v):
    b, s, d = x.shape[0], x.shape[1] // SUBLANES, SUBLANES * LANES
    t = b * s
    n_heads = f_bias.shape[0]
    att_w = n_heads * HEAD_DIM
    pool_width = pool_scale.shape[0]
    ssm_w = ssm_d.shape[0]
    ts, blk, tc, tm, tt = _tiles(s, t)
    row = lambda v: v.reshape(1, -1)

    c0, c1 = 3 * att_w, 3 * att_w + n_heads
    wf = jnp.zeros((d, LANES), F32).at[:, :n_heads].set(w_in[:, c0:c1])
    fb = jnp.zeros((1, LANES), F32).at[0, :n_heads].set(f_bias)
    seg = _block_diag(jnp.full((n_heads, HEAD_DIM, HEAD_DIM), 1.0 / HEAD_DIM, F32)).astype(BF16)
    q, k, v, c_tok, pool, u_ssm = _inproj(
        x, row(norm1_g), w_in[:, :c0].astype(BF16), wf,
        w_in[:, c1:c1 + pool_width].astype(BF16), w_in[:, c1 + pool_width:].astype(BF16), fb,
        row(jnp.tile(q_gain, n_heads) * HEAD_DIM ** -0.5), row(jnp.tile(k_gain, n_heads)),
        seg, _block_diag(pool_w).astype(BF16), row(pool_scale), ts=ts)
    att = _attention(q, k, v, c_tok, jnp.swapaxes(c_tok, 1, 2), blk=blk)
    are, aim, bblk, cblk = _s5_params(a_re, a_im, log_dt, b_re, b_im, c_re, c_im)
    ssm = _s5(u_ssm, are, aim, bblk, cblk, row(ssm_d), glu_w.astype(BF16), row(glu_b), tc=tc)
    x1 = _merge(x, att, pool, ssm, row(out_norm_g), w_out.astype(BF16), ts=ts)

    n_experts = peer_u.shape[0]
    kh = peer_keys.astype(BF16)
    h2, off, sh, gates = _route(x1.reshape(t * SUBLANES, LANES), row(norm2_g),
                                peer_wq.astype(BF16), kh, tm=tm, n_experts=n_experts)
    tiles = lambda a: a.reshape(t, SUBLANES, LANES)
    coef = _peer_u(off, sh, tiles(h2), gates, _pack_table(peer_u), tt=tt)
    x2 = _peer_v(off, sh, coef, tiles(x1), _pack_table(peer_v), tt=tt)
    return x2.reshape(b, s * SUBLANES, LANES)


def kernel(x, norm1_g, w_in, f_bias, q_gain, k_gain, pool_w, pool_scale, ssm_a_re, ssm_a_im,
           ssm_log_dt, ssm_b_re, ssm_b_im, ssm_c_re, ssm_c_im, ssm_d, glu_w, glu_b, out_norm_g,
           w_out, norm2_g, peer_wq, peer_keys, peer_u, peer_v):
    per_layer = (norm1_g, w_in, f_bias, q_gain, k_gain, pool_w, pool_scale, ssm_a_re, ssm_a_im,
                 ssm_log_dt, ssm_b_re, ssm_b_im, ssm_c_re, ssm_c_im, ssm_d, glu_w, glu_b,
                 out_norm_g, w_out, norm2_g, peer_wq, peer_keys, peer_u, peer_v)
    b, s, d = x.shape
    assert d == SUBLANES * LANES
    x = x.reshape(b, s * SUBLANES, LANES)
    for l in range(norm1_g.shape[0]):
        x = _layer(x, *[p[l] for p in per_layer])
    return x.reshape(b, s, d)
```

```python
import functools
import math

import jax
import jax.numpy as jnp
from jax import lax
from jax.experimental import pallas as pl
from jax.experimental.pallas import tpu as pltpu

F32 = jnp.float32
BF16 = jnp.bfloat16
U32 = jnp.uint32
I32 = jnp.int32

EPS = 1e-6
HEAD_DIM = 64
POOL_WINDOWS = (2, 4, 8, 16)
POOL_HALO = 16
PEER_KEYS = 128
PEER_TOPK = 16
U_EDGES_PER_DOT = 32

LANES = 128
SUBLANES = 8
VMEM_LIMIT_BYTES = 56 * 1024 * 1024

_NT = (((1,), (1,)), ((), ()))


def _rmsnorm(x, g):
    return x * lax.rsqrt(jnp.mean(x * x, axis=-1, keepdims=True) + EPS) * g


def _gelu_tanh(x):
    c = math.sqrt(2.0 / math.pi)
    return x * (0.5 * (1.0 + jnp.tanh(c * (x + 0.044715 * (x * x * x)))))


def _params(sem, vmem=None):
    return pltpu.CompilerParams(dimension_semantics=sem, vmem_limit_bytes=vmem)


def _load_tiled(ref, n):
    return jnp.concatenate([ref[pl.ds(s, n, stride=SUBLANES), :] for s in range(SUBLANES)], axis=1)


def _store_tiled(ref, val, n):
    for s in range(SUBLANES):
        ref[pl.ds(s, n, stride=SUBLANES), :] = val[:, s * LANES:(s + 1) * LANES]


def _inproj_kernel(x_ref, g_ref, wqkv_ref, wf_ref, wpool_ref, wssm_ref, fb_ref,
                   qg_ref, kg_ref, seg_ref, pmix_ref, pscale_ref,
                   q_out, k_out, v_out, c_out, pool_out, ssm_out, carry_ref, halo_ref,
                   *, att_w, n_heads):
    i = pl.program_id(1)

    @pl.when(i == 0)
    def _():
        carry_ref[...] = jnp.zeros_like(carry_ref)
        halo_ref[...] = jnp.zeros_like(halo_ref)

    ts = x_ref.shape[0] // SUBLANES
    x = _load_tiled(x_ref, ts)
    h = _rmsnorm(x, g_ref[...])
    hb = h.astype(BF16)
    qkv = jnp.dot(hb, wqkv_ref[...], preferred_element_type=F32)
    seg = seg_ref[...]

    def headnorm(t, gain):
        ms = jnp.dot((t * t).astype(BF16), seg, preferred_element_type=F32)
        return t * lax.rsqrt(ms + EPS) * gain

    q_out[...] = headnorm(qkv[:, :att_w], qg_ref[...]).astype(BF16)
    k_out[...] = headnorm(qkv[:, att_w:2 * att_w], kg_ref[...]).astype(BF16)
    v_out[...] = qkv[:, 2 * att_w:].astype(BF16)
    u_pool = jnp.dot(hb, wpool_ref[...], preferred_element_type=F32)
    pool_out[...] = _pool_mix(u_pool, halo_ref[...], i * ts, pmix_ref[...], pscale_ref[...])
    halo_ref[...] = u_pool[ts - POOL_HALO:, :]
    ssm_out[...] = jnp.dot(hb, wssm_ref[...], preferred_element_type=F32)

    f = jnp.dot(h, wf_ref[...], precision=lax.Precision.HIGHEST,
                preferred_element_type=F32) + fb_ref[...]
    ls = jnp.minimum(f, 0.0) - jnp.log1p(jnp.exp(-jnp.abs(f)))
    r = lax.broadcasted_iota(I32, (ts, ts), 0)
    c = lax.broadcasted_iota(I32, (ts, ts), 1)
    tri = jnp.where(c <= r, 1.0, 0.0).astype(F32)
    cs = jnp.dot(tri, ls, precision=lax.Precision.HIGHEST,
                 preferred_element_type=F32) + carry_ref[0:1, :]
    carry_ref[...] = jnp.broadcast_to(cs[ts - 1:ts, :], carry_ref.shape)
    c_out[...] = cs[:, :n_heads]


def _tiled_spec(ts):
    return pl.BlockSpec((None, ts * SUBLANES, LANES), lambda bi, i: (bi, i, 0))


def _inproj(x, g, wqkv, wf, wpool, wssm, fb, qg, kg, seg, pmix, pscale, *, ts):
    b, s = x.shape[0], x.shape[1] // SUBLANES
    att_w = seg.shape[0]
    n_heads = att_w // HEAD_DIM
    pw = wpool.shape[1]
    sw = wssm.shape[1]
    full = lambda a: pl.BlockSpec(a.shape, lambda bi, i: (0,) * a.ndim)
    tok = lambda w: pl.BlockSpec((None, ts, w), lambda bi, i: (bi, i, 0))
    return pl.pallas_call(
        functools.partial(_inproj_kernel, att_w=att_w, n_heads=n_heads),
        grid=(b, s // ts),
        in_specs=[_tiled_spec(ts), full(g), full(wqkv), full(wf), full(wpool), full(wssm),
                  full(fb), full(qg), full(kg), full(seg), full(pmix), full(pscale)],
        out_specs=[tok(att_w), tok(att_w), tok(att_w), tok(n_heads), tok(pw), tok(sw)],
        out_shape=[jax.ShapeDtypeStruct((b, s, att_w), BF16),
                   jax.ShapeDtypeStruct((b, s, att_w), BF16),
                   jax.ShapeDtypeStruct((b, s, att_w), BF16),
                   jax.ShapeDtypeStruct((b, s, n_heads), F32),
                   jax.ShapeDtypeStruct((b, s, pw), F32),
                   jax.ShapeDtypeStruct((b, s, sw), F32)],
        scratch_shapes=[pltpu.VMEM((SUBLANES, LANES), F32), pltpu.VMEM((POOL_HALO, pw), F32)],
        compiler_params=_params(("arbitrary", "arbitrary"), VMEM_LIMIT_BYTES),
        name="inproj",
    )(x, g, wqkv, wf, wpool, wssm, fb, qg, kg, seg, pmix, pscale)


def _attn_kernel(q_ref, k_ref, v_ref, ct_ref, cr_ref, o_ref, m_sc, l_sc, acc_sc, s_sc, *, blk):
    p = pl.program_id(1)
    i = pl.program_id(2)
    q = q_ref[...]
    lane = lax.broadcasted_iota(I32, (blk, LANES), 1)
    ct = ct_ref[...]
    hl = lax.broadcasted_iota(I32, ct.shape, 1)
    heads = (2 * p, 2 * p + 1)
    qm = (jnp.where(lane < HEAD_DIM, q, jnp.zeros_like(q)),
          jnp.where(lane >= HEAD_DIM, q, jnp.zeros_like(q)))
    cq = tuple(jnp.sum(jnp.where(hl == hd, ct, 0.0), axis=1, keepdims=True) for hd in heads)
    m_sc[...] = jnp.full(m_sc.shape, -jnp.inf, F32)
    l_sc[...] = jnp.zeros(l_sc.shape, F32)
    acc_sc[...] = jnp.zeros(acc_sc.shape, F32)

    def qk(kb):
        kblk = k_ref[pl.ds(pl.multiple_of(kb * blk, blk), blk), :]
        return [lax.dot_general(qm[hh], kblk, _NT, preferred_element_type=F32) for hh in range(2)]

    def step(kb, s_cur, causal):
        start = pl.multiple_of(kb * blk, blk)
        vblk = v_ref[pl.ds(start, blk), :]
        for hh in range(2):
            ck = cr_ref[pl.ds(heads[hh], 1), pl.ds(start, blk)]
            z = (s_cur[hh] + cq[hh]) - ck
            if causal:
                row = lax.broadcasted_iota(I32, (blk, blk), 0)
                col = lax.broadcasted_iota(I32, (blk, blk), 1)
                z = jnp.where(col <= row, z, -jnp.inf)
            m_old = m_sc[hh]
            m_new = jnp.maximum(m_old, jnp.max(z, axis=1, keepdims=True))
            alpha = jnp.exp(m_old - m_new)
            pe = jnp.exp(z - m_new)
            l_sc[hh] = alpha * l_sc[hh] + jnp.sum(pe, axis=1, keepdims=True)
            acc_sc[hh] = alpha * acc_sc[hh] + jnp.dot(pe.astype(BF16), vblk,
                                                      preferred_element_type=F32)
            m_sc[hh] = m_new

    s0 = qk(0)
    s_sc[0] = s0[0]
    s_sc[1] = s0[1]

    def body(kb, carry):
        s_cur = [s_sc[0], s_sc[1]]
        s_next = qk(kb + 1)
        step(kb, s_cur, False)
        s_sc[0] = s_next[0]
        s_sc[1] = s_next[1]
        return carry

    lax.fori_loop(0, i, body, 0)
    step(i, [s_sc[0], s_sc[1]], True)
    o_ref[...] = jnp.where(lane < HEAD_DIM, acc_sc[0] / l_sc[0], acc_sc[1] / l_sc[1])


def _attention(q, k, v, c_tok, c_row, *, blk):
    b, s, w = q.shape
    n_heads = c_tok.shape[2]
    return pl.pallas_call(
        functools.partial(_attn_kernel, blk=blk),
        grid=(b, w // LANES, s // blk),
        in_specs=[pl.BlockSpec((None, blk, LANES), lambda bi, p, i: (bi, i, p)),
                  pl.BlockSpec((None, s, LANES), lambda bi, p, i: (bi, 0, p)),
                  pl.BlockSpec((None, s, LANES), lambda bi, p, i: (bi, 0, p)),
                  pl.BlockSpec((None, blk, n_heads), lambda bi, p, i: (bi, i, 0)),
                  pl.BlockSpec((None, n_heads, s), lambda bi, p, i: (bi, 0, 0))],
        out_specs=pl.BlockSpec((None, blk, LANES), lambda bi, p, i: (bi, i, p)),
        out_shape=jax.ShapeDtypeStruct((b, s, w), F32),
        scratch_shapes=[pltpu.VMEM((2, blk, 1), F32), pltpu.VMEM((2, blk, 1), F32),
                        pltpu.VMEM((2, blk, LANES), F32), pltpu.VMEM((2, blk, blk), F32)],
        compiler_params=_params(("arbitrary", "arbitrary", "arbitrary"), VMEM_LIMIT_BYTES),
        name="fox_attention",
    )(q, k, v, c_tok, c_row)


def _pool_mix(cur, prev, first_pos, w, scale):
    ts, pw = cur.shape
    group_dim = pw // len(POOL_WINDOWS)
    ext = jnp.concatenate([prev, cur], axis=0)
    pos = first_pos + lax.broadcasted_iota(I32, (ts, 1), 0)
    grp = lax.broadcasted_iota(I32, (ts, pw), 1) // group_dim
    acc = cur
    pooled = jnp.zeros_like(cur)
    for j in range(1, max(POOL_WINDOWS)):
        acc = acc + ext[POOL_HALO - j:POOL_HALO - j + ts, :]
        win = j + 1
        if win in POOL_WINDOWS:
            cnt = jnp.minimum(pos + 1, win).astype(F32)
            pooled = jnp.where(grp == POOL_WINDOWS.index(win), acc / cnt, pooled)
    pooled = pooled - cur
    return jnp.dot(pooled.astype(BF16), w, preferred_element_type=F32) * scale


def _s5_kernel(u_ref, are_ref, aim_ref, bblk_ref, cblk_ref, d_ref, gw_ref, gb_ref,
               o_ref, st_ref, hre_ref, him_ref, tm_ref, *, nb, n):
    @pl.when(pl.program_id(0) == 0)
    def _():
        hre_ref[...] = jnp.zeros_like(hre_ref)
        him_ref[...] = jnp.zeros_like(him_ref)

    tc = u_ref.shape[1]
    nslab = tm_ref.shape[0]
    for bi in range(nb):
        for c in range(nslab):
            tm_ref[c, pl.ds(bi, tc, stride=nb), :] = u_ref[bi, :, c * LANES:(c + 1) * LANES]
    u = jnp.concatenate([tm_ref[c] for c in range(nslab)], axis=1)
    st_ref[...] = jnp.dot(u.astype(BF16), bblk_ref[...], preferred_element_type=F32)
    ar = jnp.broadcast_to(are_ref[...], (nb, n))
    ai = jnp.broadcast_to(aim_ref[...], (nb, n))

    def step(t, carry):
        hr, hi = carry
        r0 = pl.multiple_of(t * nb, nb)
        br = st_ref[pl.ds(r0, nb), 0:n]
        bi = st_ref[pl.ds(r0, nb), n:2 * n]
        nr = (ar * hr - ai * hi) + br
        ni = (ar * hi + ai * hr) + bi
        st_ref[pl.ds(r0, nb), 0:n] = nr
        st_ref[pl.ds(r0, nb), n:2 * n] = ni
        return nr, ni

    hr, hi = lax.fori_loop(0, tc, step, (hre_ref[...], him_ref[...]))
    hre_ref[...] = hr
    him_ref[...] = hi
    y = jnp.dot(st_ref[...].astype(BF16), cblk_ref[...], preferred_element_type=F32)
    z = _gelu_tanh(y + d_ref[...] * u)
    gate = jax.nn.sigmoid(jnp.dot(z.astype(BF16), gw_ref[...], preferred_element_type=F32)
                          + gb_ref[...])
    out = z * gate
    for c in range(nslab):
        tm_ref[c] = out[:, c * LANES:(c + 1) * LANES]
    for bi in range(nb):
        for c in range(nslab):
            o_ref[bi, :, c * LANES:(c + 1) * LANES] = tm_ref[c, pl.ds(bi, tc, stride=nb), :]


def _s5(u, a_re, a_im, bblk, cblk, dskip, gw, gb, *, tc):
    nb, s, sw = u.shape
    n = a_re.shape[1]
    rb = tc * nb
    full = lambda a: pl.BlockSpec(a.shape, lambda i: (0,) * a.ndim)
    blk = pl.BlockSpec((nb, tc, sw), lambda i: (0, i, 0))
    return pl.pallas_call(
        functools.partial(_s5_kernel, nb=nb, n=n),
        grid=(s // tc,),
        in_specs=[blk, full(a_re), full(a_im),
                  full(bblk), full(cblk), full(dskip), full(gw), full(gb)],
        out_specs=blk,
        out_shape=jax.ShapeDtypeStruct(u.shape, F32),
        scratch_shapes=[pltpu.VMEM((rb, 2 * n), F32), pltpu.VMEM((nb, n), F32),
                        pltpu.VMEM((nb, n), F32), pltpu.VMEM((sw // LANES, rb, LANES), F32)],
        compiler_params=_params(("arbitrary",), VMEM_LIMIT_BYTES),
        name="s5_scan",
    )(u, a_re, a_im, bblk, cblk, dskip, gw, gb)


def _merge_kernel(x_ref, att_ref, pool_ref, ssm_ref, og_ref, w_ref, o_ref, *, att_w, pool_w):
    og = og_ref[...]
    p1 = att_w + pool_w
    merged = jnp.concatenate([
        _rmsnorm(att_ref[...], og[:, :att_w]),
        _rmsnorm(pool_ref[...], og[:, att_w:p1]),
        _rmsnorm(ssm_ref[...], og[:, p1:])], axis=1).astype(BF16)
    ts = merged.shape[0]
    out = _load_tiled(x_ref, ts) + jnp.dot(merged, w_ref[...], preferred_element_type=F32)
    _store_tiled(o_ref, out, ts)


def _merge(x, att, pool, ssm, og, w, *, ts):
    b, s = x.shape[0], x.shape[1] // SUBLANES
    att_w, pool_w, sw = att.shape[2], pool.shape[2], ssm.shape[2]
    tok = lambda wd: pl.BlockSpec((None, ts, wd), lambda bi, i: (bi, i, 0))
    return pl.pallas_call(
        functools.partial(_merge_kernel, att_w=att_w, pool_w=pool_w),
        grid=(b, s // ts),
        in_specs=[_tiled_spec(ts), tok(att_w), tok(pool_w), tok(sw),
                  pl.BlockSpec(og.shape, lambda bi, i: (0, 0)),
                  pl.BlockSpec(w.shape, lambda bi, i: (0, 0))],
        out_specs=_tiled_spec(ts),
        out_shape=jax.ShapeDtypeStruct(x.shape, F32),
        compiler_params=_params(("arbitrary", "arbitrary"), VMEM_LIMIT_BYTES),
        name="merge_outproj",
    )(x, att, pool, ssm, og, w)


def _topk_rows(a, k, extra=None):
    nrow, nl = a.shape
    row = lax.broadcasted_iota(I32, (nrow, nl), 0).astype(F32)
    out_row = lax.broadcasted_iota(I32, (k, nl), 0)
    vals = jnp.zeros((k, nl), F32)
    sel = jnp.zeros((k, nl), F32)
    for j in range(k):
        m = jnp.max(a, axis=0, keepdims=True)
        ix = jnp.min(jnp.where(a == m, row, float(nrow)), axis=0, keepdims=True)
        hit = row == ix
        if extra is None:
            pick = ix
        else:
            pick = jnp.sum(jnp.where(hit, extra, 0.0), axis=0, keepdims=True)
        vals = jnp.where(out_row == j, m, vals)
        sel = jnp.where(out_row == j, pick, sel)
        a = jnp.where(hit, -jnp.inf, a)
    return vals, sel


def _route_kernel(x_ref, g_ref, wq_ref, keys_ref, h_out, off_out, sh_out, gate_out, q_sc, stage_sc,
                  *, n_half):
    hd = pl.program_id(1)
    qd = 2 * PEER_KEYS

    @pl.when(hd == 0)
    def _():
        tm = q_sc.shape[0]
        h = _rmsnorm(_load_tiled(x_ref, tm), g_ref[...])
        _store_tiled(h_out, h, tm)
        q_sc[...] = jnp.dot(h.astype(BF16), wq_ref[...], preferred_element_type=F32)

    q = q_sc[:, pl.ds(pl.multiple_of(hd * qd, qd), qd)].astype(BF16)
    tops = []
    for i in range(2):
        sc = lax.dot_general(keys_ref[i], q[:, i * PEER_KEYS:(i + 1) * PEER_KEYS], _NT,
                             preferred_element_type=F32)
        tops.append(_topk_rows(sc, PEER_TOPK))
    (s1, i1), (s2, i2) = tops
    k = PEER_TOPK
    sub8 = lax.broadcasted_iota(I32, (SUBLANES, s1.shape[1]), 0)
    cands, ecands = [], []
    a = 0
    while a < k and k // (a + 1) > 1:
        nb = k // (a + 1)
        for b0 in range(0, nb, SUBLANES):
            c = s1[a:a + 1, :] + s2[b0:b0 + SUBLANES, :]
            if b0 + SUBLANES > nb:
                c = jnp.where(sub8 < nb - b0, c, -jnp.inf)
            cands.append(c)
            ecands.append(i1[a:a + 1, :] * float(PEER_KEYS) + i2[b0:b0 + SUBLANES, :])
        a += 1
    cands.append(s1[a:, :] + s2[0:1, :])
    ecands.append(i1[a:, :] * float(PEER_KEYS) + i2[0:1, :])
    best, experts = _topk_rows(jnp.concatenate(cands, axis=0), k,
                               extra=jnp.concatenate(ecands, axis=0))
    w = jnp.exp(best - best[0:1, :])
    rows = pl.ds(pl.multiple_of(hd * k, k), k)
    stage_sc[2, rows, :] = w / jnp.sum(w, axis=0, keepdims=True)
    hi_half = experts < float(n_half)
    stage_sc[0, rows, :] = jnp.where(hi_half, experts, experts - float(n_half)) * float(SUBLANES)
    stage_sc[1, rows, :] = jnp.where(hi_half, 0.0, 16.0)

    @pl.when(hd == pl.num_programs(1) - 1)
    def _():
        off_out[...] = stage_sc[0].T.astype(I32)
        sh_out[...] = stage_sc[1].T
        gate_out[...] = stage_sc[2].T


def _route(x, g, wq, keys, *, tm, n_experts):
    t = x.shape[0] // SUBLANES
    n_heads = keys.shape[0]
    n_sel = n_heads * PEER_TOPK
    hk = pl.BlockSpec((tm, n_sel), lambda i, h: (i, 0))
    shp = lambda dt: jax.ShapeDtypeStruct((t, n_sel), dt)
    tiled = pl.BlockSpec((tm * SUBLANES, LANES), lambda i, h: (i, 0))
    return pl.pallas_call(
        functools.partial(_route_kernel, n_half=n_experts // 2),
        grid=(t // tm, n_heads),
        in_specs=[tiled,
                  pl.BlockSpec(g.shape, lambda i, h: (0, 0)),
                  pl.BlockSpec(wq.shape, lambda i, h: (0, 0)),
                  pl.BlockSpec((None, 2, PEER_KEYS, keys.shape[3]), lambda i, h: (h, 0, 0, 0))],
        out_specs=[tiled, hk, hk, hk],
        out_shape=[jax.ShapeDtypeStruct(x.shape, F32), shp(I32), shp(F32), shp(F32)],
        scratch_shapes=[pltpu.VMEM((tm, wq.shape[1]), F32), pltpu.VMEM((3, n_sel, tm), F32)],
        compiler_params=_params(("arbitrary", "arbitrary"), VMEM_LIMIT_BYTES),
        name="peer_route",
    )(x, g, wq, keys)


def _expert_row(tab_ref, off, sh):
    w = tab_ref[pl.ds(pl.multiple_of(off, SUBLANES), SUBLANES), :]
    return lax.bitcast_convert_type((w << sh) & jnp.uint32(0xFFFF0000), F32)


def _splat_rows(mxu_src, mxu_dst, xlu_src=None, xlu_dst=None, *, tt, hk, also=None):
    eye = (lax.broadcasted_iota(I32, (hk, hk), 0) == lax.broadcasted_iota(I32, (hk, hk), 1))
    ones = jnp.ones((hk, hk), BF16)

    def body(t, carry):
        v = jnp.broadcast_to(mxu_src[pl.ds(t, 1), :], (hk, hk))
        rep = jnp.dot(jnp.where(eye, v, 0.0).astype(BF16), ones, preferred_element_type=F32)
        mxu_dst(t, rep.astype(I32))
        if xlu_src is not None:
            xlu_dst(t, jnp.broadcast_to(xlu_src[pl.ds(t, 1), :], (hk, hk)).T)
        if also is not None:
            also(t)
        return carry

    lax.fori_loop(0, tt, body, 0, unroll=8)


def _splat(rep_ref, row):
    return jnp.broadcast_to(rep_ref[pl.ds(row, 1), :], (SUBLANES, LANES))


def _peer_u_kernel(*refs, tt, hk):
    n_off = tt * hk // SUBLANES
    off_refs = [refs[0].at[pl.ds(j * n_off, n_off)] for j in range(SUBLANES)]
    sh_ref, shn_ref, h_ref, gate_ref, tab_ref, coef_ref, part_ref, srep_ref, pre_ref = refs[1:]
    ngrp = hk // SUBLANES

    def store_srep(t, block):
        srep_ref[pl.ds(pl.multiple_of(t * hk, hk), hk), :] = block

    @pl.when(pl.program_id(0) == 0)
    def _():
        _splat_rows(sh_ref, store_srep, tt=tt, hk=hk)

    chunk = U_EDGES_PER_DOT
    sel = (lax.broadcasted_iota(I32, (chunk, chunk * SUBLANES), 1) // SUBLANES
           == lax.broadcasted_iota(I32, (chunk, chunk * SUBLANES), 0)).astype(BF16)

    def tok(t, carry):
        h = h_ref[t]
        base = pl.multiple_of(t * hk, hk)
        for c in range(hk // chunk):
            e0 = base + c * chunk
            ps = []
            for k0 in range(0, chunk, SUBLANES):
                sblk = srep_ref[pl.ds(e0 + k0, SUBLANES), :].astype(U32)
                for j in range(SUBLANES):
                    k = k0 + j
                    sv = jnp.broadcast_to(sblk[j:j + 1, :], (SUBLANES, LANES))
                    ps.append(_expert_row(
                        tab_ref, off_refs[j][t * ngrp + (c * chunk + k) // SUBLANES], sv) * h)
            stacked = jnp.concatenate(ps, axis=0).astype(BF16)
            part_ref[pl.ds(e0, chunk), :] = jnp.dot(sel, stacked, preferred_element_type=F32)
        return carry

    lax.fori_loop(0, tt, tok, 0, unroll=8)

    def lane_sums(t):
        p = part_ref[pl.ds(pl.multiple_of(t * hk, hk), hk), :]
        pre_ref[pl.ds(t, 1), :] = jnp.sum(p.T, axis=0, keepdims=True)

    _splat_rows(shn_ref, store_srep, tt=tt, hk=hk, also=lane_sums)
    coef_ref[...] = gate_ref[...] * _gelu_tanh(pre_ref[...])


def _peer_v_kernel(*refs, tt, hk, n_acc):
    n_off = tt * hk // SUBLANES
    off_refs = [refs[0].at[pl.ds(j * n_off, n_off)] for j in range(SUBLANES)]
    sh_ref, coef_ref, x_ref, tab_ref, o_ref, rep_ref = refs[1:]
    ngrp = hk // SUBLANES

    def store_coef(t, block):
        rep_ref[pl.ds(2 * t * hk, hk, stride=2), :] = block

    def store_shift(t, block):
        rep_ref[pl.ds(2 * t * hk + 1, hk, stride=2), :] = lax.bitcast_convert_type(block, F32)

    _splat_rows(sh_ref, store_shift, coef_ref, store_coef, tt=tt, hk=hk)

    def tok(t, carry):
        accs = [None] * n_acc
        e0 = 2 * t * hk
        for k in range(hk):
            off = off_refs[k % SUBLANES][t * ngrp + k // SUBLANES]
            sv = lax.bitcast_convert_type(_splat(rep_ref, e0 + 2 * k + 1), U32)
            term = _splat(rep_ref, e0 + 2 * k) * _expert_row(tab_ref, off, sv)
            accs[k % n_acc] = term if accs[k % n_acc] is None else accs[k % n_acc] + term
        while len(accs) > 1:
            accs = [accs[j] + accs[j + len(accs) // 2] for j in range(len(accs) // 2)]
        o_ref[t] = x_ref[t] + accs[0]
        return carry

    lax.fori_loop(0, tt, tok, 0, unroll=2)


def _resident(a):
    return pl.BlockSpec(a.shape, lambda i: (0,) * a.ndim, pipeline_mode=pl.Buffered(1))


def _split_offsets(off, tt):
    t, hk = off.shape
    grouped = off.reshape(t // tt, tt, hk // SUBLANES, SUBLANES)
    return jnp.transpose(grouped, (0, 3, 1, 2)).reshape(-1)


def _peer_u(off, sh, h, gates, tab, *, tt):
    t, hk = off.shape
    assert hk == LANES and h.shape[1:] == (SUBLANES, LANES)
    smem = pl.BlockSpec((tt * hk,), lambda i: (i,), memory_space=pltpu.SMEM)
    compact = pl.BlockSpec((tt, hk), lambda i: (i, 0))
    last = t // tt - 1
    nxt = pl.BlockSpec((tt, hk), lambda i: (jnp.minimum(i + 1, last), 0))
    return pl.pallas_call(
        functools.partial(_peer_u_kernel, tt=tt, hk=hk),
        grid=(t // tt,),
        in_specs=[
            smem, compact, nxt, pl.BlockSpec((tt, SUBLANES, LANES), lambda i: (i, 0, 0)), compact,
            _resident(tab)],
        out_specs=compact,
        out_shape=jax.ShapeDtypeStruct((t, hk), F32),
        scratch_shapes=[pltpu.VMEM((tt * hk, LANES), F32), pltpu.VMEM((tt * hk, LANES), I32),
                        pltpu.VMEM((tt, hk), F32)],
        compiler_params=_params(("arbitrary",), VMEM_LIMIT_BYTES),
        name="peer_u",
    )(_split_offsets(off, tt), sh, sh, h, gates, tab)


def _peer_v(off, sh, coef, x, tab, *, tt):
    t, hk = off.shape
    assert hk == LANES and x.shape[1:] == (SUBLANES, LANES)
    smem = pl.BlockSpec((tt * hk,), lambda i: (i,), memory_space=pltpu.SMEM)
    tile = pl.BlockSpec((tt, SUBLANES, LANES), lambda i: (i, 0, 0))
    compact = pl.BlockSpec((tt, hk), lambda i: (i, 0))
    return pl.pallas_call(
        functools.partial(_peer_v_kernel, tt=tt, hk=hk, n_acc=4),
        grid=(t // tt,),
        in_specs=[smem, compact, compact, tile, _resident(tab)],
        out_specs=tile,
        out_shape=jax.ShapeDtypeStruct(x.shape, F32),
        scratch_shapes=[pltpu.VMEM((2 * tt * hk, LANES), F32)],
        compiler_params=_params(("arbitrary",), VMEM_LIMIT_BYTES),
        name="peer_v",
    )(_split_offsets(off, tt), sh, coef, x, tab)


def _pack_kernel(hi_ref, lo_ref, o_ref):
    def bf16_bits(x):
        bits = lax.bitcast_convert_type(x, U32)
        r = bits + (jnp.uint32(0x7FFF) + ((bits >> 16) & jnp.uint32(1)))
        return jnp.where(x != x, jnp.uint32(0x7FC00000), r)

    w = (bf16_bits(hi_ref[...]) & jnp.uint32(0xFFFF0000)) | (bf16_bits(lo_ref[...]) >> 16)
    _store_tiled(o_ref, w, w.shape[0])


def _pack_table(tab, rows=256):
    n, d = tab.shape
    assert d == SUBLANES * LANES and (n // 2) % rows == 0
    nblk = n // 2 // rows
    return pl.pallas_call(
        _pack_kernel,
        grid=(nblk,),
        in_specs=[pl.BlockSpec((rows, d), lambda i: (i, 0)),
                  pl.BlockSpec((rows, d), lambda i: (i + nblk, 0))],
        out_specs=pl.BlockSpec((rows * SUBLANES, LANES), lambda i: (i, 0)),
        out_shape=jax.ShapeDtypeStruct((n // 2 * SUBLANES, LANES), U32),
        compiler_params=_params(("arbitrary",)),
        name="pack_table",
    )(tab, tab)


def _block_diag(blocks):
    g, r, c = blocks.shape
    eye = jnp.eye(g, dtype=blocks.dtype)
    return (eye[:, None, :, None] * blocks[:, :, None, :]).reshape(g * r, g * c)


def _s5_params(a_re, a_im, log_dt, b_re, b_im, c_re, c_im):
    a = lax.complex(a_re, a_im)
    dt = jnp.exp(log_dt)[:, None]
    a_bar = jnp.exp(a * dt)
    b_bar = ((a_bar - 1.0) / a)[..., None] * lax.complex(b_re, b_im)
    bt = jnp.swapaxes(b_bar, 1, 2)
    bblk = jnp.concatenate([_block_diag(jnp.real(bt)), _block_diag(jnp.imag(bt))], axis=1)
    ct_re = jnp.swapaxes(c_re, 1, 2)
    ct_im = jnp.swapaxes(c_im, 1, 2)
    cblk = jnp.concatenate([_block_diag(ct_re), -_block_diag(ct_im)], axis=0)
    n = a_bar.size
    return (jnp.real(a_bar).reshape(1, n), jnp.imag(a_bar).reshape(1, n),
            bblk.astype(BF16), cblk.astype(BF16))


def _tiles(s, t):
    ts = min(512, s)
    blk = min(512, s)
    tc = min(64, s)
    tm = min(512, t)
    tt = min(128, t)
    return ts, blk, tc, tm, tt


def _layer(x, norm1_g, w_in, f_bias, q_gain, k_gain, pool_w, pool_scale,
           a_re, a_im, log_dt, b_re, b_im, c_re, c_im, ssm_d, glu_w, glu_b,
           out_norm_g, w_out, norm2_g, peer_wq, peer_keys, peer_u, peer_v):
    b, s, d = x.shape[0], x.shape[1] // SUBLANES, SUBLANES * LANES
    t = b * s
    n_heads = f_bias.shape[0]
    att_w = n_heads * HEAD_DIM
    pool_width = pool_scale.shape[0]
    ssm_w = ssm_d.shape[0]
    ts, blk, tc, tm, tt = _tiles(s, t)
    row = lambda v: v.reshape(1, -1)

    c0, c1 = 3 * att_w, 3 * att_w + n_heads
    wf = jnp.zeros((d, LANES), F32).at[:, :n_heads].set(w_in[:, c0:c1])
    fb = jnp.zeros((1, LANES), F32).at[0, :n_heads].set(f_bias)
    seg = _block_diag(jnp.full((n_heads, HEAD_DIM, HEAD_DIM), 1.0 / HEAD_DIM, F32)).astype(BF16)
    q, k, v, c_tok, pool, u_ssm = _inproj(
        x, row(norm1_g), w_in[:, :c0].astype(BF16), wf,
        w_in[:, c1:c1 + pool_width].astype(BF16), w_in[:, c1 + pool_width:].astype(BF16), fb,
        row(jnp.tile(q_gain, n_heads) * HEAD_DIM ** -0.5), row(jnp.tile(k_gain, n_heads)),
        seg, _block_diag(pool_w).astype(BF16), row(pool_scale), ts=ts)
    att = _attention(q, k, v, c_tok, jnp.swapaxes(c_tok, 1, 2), blk=blk)
    are, aim, bblk, cblk = _s5_params(a_re, a_im, log_dt, b_re, b_im, c_re, c_im)
    ssm = _s5(u_ssm, are, aim, bblk, cblk, row(ssm_d), glu_w.astype(BF16), row(glu_b), tc=tc)
    x1 = _merge(x, att, pool, ssm, row(out_norm_g), w_out.astype(BF16), ts=ts)

    n_experts = peer_u.shape[0]
    kh = peer_keys.astype(BF16)
    h2, off, sh, gates = _route(x1.reshape(t * SUBLANES, LANES), row(norm2_g),
                                peer_wq.astype(BF16), kh, tm=tm, n_experts=n_experts)
    tiles = lambda a: a.reshape(t, SUBLANES, LANES)
    coef = _peer_u(off, sh, tiles(h2), gates, _pack_table(peer_u), tt=tt)
    x2 = _peer_v(off, sh, coef, tiles(x1), _pack_table(peer_v), tt=tt)
    return x2.reshape(b, s * SUBLANES, LANES)


def kernel(x, norm1_g, w_in, f_bias, q_gain, k_gain, pool_w, pool_scale, ssm_a_re, ssm_a_im,
           ssm_log_dt, ssm_b_re, ssm_b_im, ssm_c_re, ssm_c_im, ssm_d, glu_w, glu_b, out_norm_g,
           w_out, norm2_g, peer_wq, peer_keys, peer_u, peer_v):
    per_layer = (norm1_g, w_in, f_bias, q_gain, k_gain, pool_w, pool_scale, ssm_a_re, ssm_a_im,
                 ssm_log_dt, ssm_b_re, ssm_b_im, ssm_c_re, ssm_c_im, ssm_d, glu_w, glu_b,
                 out_norm_g, w_out, norm2_g, peer_wq, peer_keys, peer_u, peer_v)
    b, s, d = x.shape
    assert d == SUBLANES * LANES
    x = x.reshape(b, s * SUBLANES, LANES)
    for l in range(norm1_g.shape[0]):
        x = _layer(x, *[p[l] for p in per_layer])
    return x.reshape(b, s, d)
```

```python
import functools
import math

import jax
import jax.numpy as jnp
from jax import lax
from jax.experimental import pallas as pl
from jax.experimental.pallas import tpu as pltpu

F32 = jnp.float32
BF16 = jnp.bfloat16
U32 = jnp.uint32
I32 = jnp.int32

EPS = 1e-6
HEAD_DIM = 64
POOL_WINDOWS = (2, 4, 8, 16)
POOL_HALO = 16
PEER_KEYS = 128
PEER_TOPK = 16
U_EDGES_PER_DOT = 32

LANES = 128
SUBLANES = 8
VMEM_LIMIT_BYTES = 56 * 1024 * 1024

_NT = (((1,), (1,)), ((), ()))


def _rmsnorm(x, g):
    return x * lax.rsqrt(jnp.mean(x * x, axis=-1, keepdims=True) + EPS) * g


def _gelu_tanh(x):
    c = math.sqrt(2.0 / math.pi)
    return x * (0.5 * (1.0 + jnp.tanh(c * (x + 0.044715 * (x * x * x)))))


def _params(sem, vmem=None):
    return pltpu.CompilerParams(dimension_semantics=sem, vmem_limit_bytes=vmem)


def _load_tiled(ref, n):
    return jnp.concatenate([ref[pl.ds(s, n, stride=SUBLANES), :] for s in range(SUBLANES)], axis=1)


def _store_tiled(ref, val, n):
    for s in range(SUBLANES):
        ref[pl.ds(s, n, stride=SUBLANES), :] = val[:, s * LANES:(s + 1) * LANES]


def _inproj_kernel(x_ref, g_ref, wqkv_ref, wf_ref, wpool_ref, wssm_ref, fb_ref,
                   qg_ref, kg_ref, seg_ref, pmix_ref, pscale_ref,
                   q_out, k_out, v_out, c_out, pool_out, ssm_out, carry_ref, halo_ref,
                   *, att_w, n_heads):
    i = pl.program_id(1)

    @pl.when(i == 0)
    def _():
        carry_ref[...] = jnp.zeros_like(carry_ref)
        halo_ref[...] = jnp.zeros_like(halo_ref)

    ts = x_ref.shape[0] // SUBLANES
    x = _load_tiled(x_ref, ts)
    h = _rmsnorm(x, g_ref[...])
    hb = h.astype(BF16)
    qkv = jnp.dot(hb, wqkv_ref[...], preferred_element_type=F32)
    seg = seg_ref[...]

    def headnorm(t, gain):
        ms = jnp.dot((t * t).astype(BF16), seg, preferred_element_type=F32)
        return t * lax.rsqrt(ms + EPS) * gain

    q_out[...] = headnorm(qkv[:, :att_w], qg_ref[...]).astype(BF16)
    k_out[...] = headnorm(qkv[:, att_w:2 * att_w], kg_ref[...]).astype(BF16)
    v_out[...] = qkv[:, 2 * att_w:].astype(BF16)
    u_pool = jnp.dot(hb, wpool_ref[...], preferred_element_type=F32)
    pool_out[...] = _pool_mix(u_pool, halo_ref[...], i * ts, pmix_ref[...], pscale_ref[...])
    halo_ref[...] = u_pool[ts - POOL_HALO:, :]
    ssm_out[...] = jnp.dot(hb, wssm_ref[...], preferred_element_type=F32)

    f = jnp.dot(h, wf_ref[...], precision=lax.Precision.HIGHEST,
                preferred_element_type=F32) + fb_ref[...]
    ls = jnp.minimum(f, 0.0) - jnp.log1p(jnp.exp(-jnp.abs(f)))
    r = lax.broadcasted_iota(I32, (ts, ts), 0)
    c = lax.broadcasted_iota(I32, (ts, ts), 1)
    tri = jnp.where(c <= r, 1.0, 0.0).astype(F32)
    cs = jnp.dot(tri, ls, precision=lax.Precision.HIGHEST,
                 preferred_element_type=F32) + carry_ref[0:1, :]
    carry_ref[...] = jnp.broadcast_to(cs[ts - 1:ts, :], carry_ref.shape)
    c_out[...] = cs[:, :n_heads]


def _tiled_spec(ts):
    return pl.BlockSpec((None, ts * SUBLANES, LANES), lambda bi, i: (bi, i, 0))


def _inproj(x, g, wqkv, wf, wpool, wssm, fb, qg, kg, seg, pmix, pscale, *, ts):
    b, s = x.shape[0], x.shape[1] // SUBLANES
    att_w = seg.shape[0]
    n_heads = att_w // HEAD_DIM
    pw = wpool.shape[1]
    sw = wssm.shape[1]
    full = lambda a: pl.BlockSpec(a.shape, lambda bi, i: (0,) * a.ndim)
    tok = lambda w: pl.BlockSpec((None, ts, w), lambda bi, i: (bi, i, 0))
    return pl.pallas_call(
        functools.partial(_inproj_kernel, att_w=att_w, n_heads=n_heads),
        grid=(b, s // ts),
        in_specs=[_tiled_spec(ts), full(g), full(wqkv), full(wf), full(wpool), full(wssm),
                  full(fb), full(qg), full(kg), full(seg), full(pmix), full(pscale)],
        out_specs=[tok(att_w), tok(att_w), tok(att_w), tok(n_heads), tok(pw), tok(sw)],
        out_shape=[jax.ShapeDtypeStruct((b, s, att_w), BF16),
                   jax.ShapeDtypeStruct((b, s, att_w), BF16),
                   jax.ShapeDtypeStruct((b, s, att_w), BF16),
                   jax.ShapeDtypeStruct((b, s, n_heads), F32),
                   jax.ShapeDtypeStruct((b, s, pw), F32),
                   jax.ShapeDtypeStruct((b, s, sw), F32)],
        scratch_shapes=[pltpu.VMEM((SUBLANES, LANES), F32), pltpu.VMEM((POOL_HALO, pw), F32)],
        compiler_params=_params(("arbitrary", "arbitrary"), VMEM_LIMIT_BYTES),
        name="inproj",
    )(x, g, wqkv, wf, wpool, wssm, fb, qg, kg, seg, pmix, pscale)


def _attn_kernel(q_ref, k_ref, v_ref, ct_ref, cr_ref, o_ref, m_sc, l_sc, acc_sc, s_sc, *, blk):
    p = pl.program_id(1)
    i = pl.program_id(2)
    q = q_ref[...]
    lane = lax.broadcasted_iota(I32, (blk, LANES), 1)
    ct = ct_ref[...]
    hl = lax.broadcasted_iota(I32, ct.shape, 1)
    heads = (2 * p, 2 * p + 1)
    qm = (jnp.where(lane < HEAD_DIM, q, jnp.zeros_like(q)),
          jnp.where(lane >= HEAD_DIM, q, jnp.zeros_like(q)))
    cq = tuple(jnp.sum(jnp.where(hl == hd, ct, 0.0), axis=1, keepdims=True) for hd in heads)
    m_sc[...] = jnp.full(m_sc.shape, -jnp.inf, F32)
    l_sc[...] = jnp.zeros(l_sc.shape, F32)
    acc_sc[...] = jnp.zeros(acc_sc.shape, F32)

    def qk(kb):
        kblk = k_ref[pl.ds(pl.multiple_of(kb * blk, blk), blk), :]
        return [lax.dot_general(qm[hh], kblk, _NT, preferred_element_type=F32) for hh in range(2)]

    def step(kb, s_cur, causal):
        start = pl.multiple_of(kb * blk, blk)
        vblk = v_ref[pl.ds(start, blk), :]
        for hh in range(2):
            ck = cr_ref[pl.ds(heads[hh], 1), pl.ds(start, blk)]
            z = (s_cur[hh] + cq[hh]) - ck
            if causal:
                row = lax.broadcasted_iota(I32, (blk, blk), 0)
                col = lax.broadcasted_iota(I32, (blk, blk), 1)
                z = jnp.where(col <= row, z, -jnp.inf)
            m_old = m_sc[hh]
            m_new = jnp.maximum(m_old, jnp.max(z, axis=1, keepdims=True))
            alpha = jnp.exp(m_old - m_new)
            pe = jnp.exp(z - m_new)
            l_sc[hh] = alpha * l_sc[hh] + jnp.sum(pe, axis=1, keepdims=True)
            acc_sc[hh] = alpha * acc_sc[hh] + jnp.dot(pe.astype(BF16), vblk,
                                                      preferred_element_type=F32)
            m_sc[hh] = m_new

    s0 = qk(0)
    s_sc[0] = s0[0]
    s_sc[1] = s0[1]

    def body(kb, carry):
        s_cur = [s_sc[0], s_sc[1]]
        s_next = qk(kb + 1)
        step(kb, s_cur, False)
        s_sc[0] = s_next[0]
        s_sc[1] = s_next[1]
        return carry

    lax.fori_loop(0, i, body, 0)
    step(i, [s_sc[0], s_sc[1]], True)
    o_ref[...] = jnp.where(lane < HEAD_DIM, acc_sc[0] / l_sc[0], acc_sc[1] / l_sc[1])


def _attention(q, k, v, c_tok, c_row, *, blk):
    b, s, w = q.shape
    n_heads = c_tok.shape[2]
    return pl.pallas_call(
        functools.partial(_attn_kernel, blk=blk),
        grid=(b, w // LANES, s // blk),
        in_specs=[pl.BlockSpec((None, blk, LANES), lambda bi, p, i: (bi, i, p)),
                  pl.BlockSpec((None, s, LANES), lambda bi, p, i: (bi, 0, p)),
                  pl.BlockSpec((None, s, LANES), lambda bi, p, i: (bi, 0, p)),
                  pl.BlockSpec((None, blk, n_heads), lambda bi, p, i: (bi, i, 0)),
                  pl.BlockSpec((None, n_heads, s), lambda bi, p, i: (bi, 0, 0))],
        out_specs=pl.BlockSpec((None, blk, LANES), lambda bi, p, i: (bi, i, p)),
        out_shape=jax.ShapeDtypeStruct((b, s, w), F32),
        scratch_shapes=[pltpu.VMEM((2, blk, 1), F32), pltpu.VMEM((2, blk, 1), F32),
                        pltpu.VMEM((2, blk, LANES), F32), pltpu.VMEM((2, blk, blk), F32)],
        compiler_params=_params(("arbitrary", "arbitrary", "arbitrary"), VMEM_LIMIT_BYTES),
        name="fox_attention",
    )(q, k, v, c_tok, c_row)


def _pool_mix(cur, prev, first_pos, w, scale):
    ts, pw = cur.shape
    group_dim = pw // len(POOL_WINDOWS)
    ext = jnp.concatenate([prev, cur], axis=0)
    pos = first_pos + lax.broadcasted_iota(I32, (ts, 1), 0)
    grp = lax.broadcasted_iota(I32, (ts, pw), 1) // group_dim
    acc = cur
    pooled = jnp.zeros_like(cur)
    for j in range(1, max(POOL_WINDOWS)):
        acc = acc + ext[POOL_HALO - j:POOL_HALO - j + ts, :]
        win = j + 1
        if win in POOL_WINDOWS:
            cnt = jnp.minimum(pos + 1, win).astype(F32)
            pooled = jnp.where(grp == POOL_WINDOWS.index(win), acc / cnt, pooled)
    pooled = pooled - cur
    return jnp.dot(pooled.astype(BF16), w, preferred_element_type=F32) * scale


def _s5_kernel(u_ref, are_ref, aim_ref, bblk_ref, cblk_ref, d_ref, gw_ref, gb_ref,
               o_ref, st_ref, hre_ref, him_ref, tm_ref, *, nb, n):
    @pl.when(pl.program_id(0) == 0)
    def _():
        hre_ref[...] = jnp.zeros_like(hre_ref)
        him_ref[...] = jnp.zeros_like(him_ref)

    tc = u_ref.shape[1]
    nslab = tm_ref.shape[0]
    for bi in range(nb):
        for c in range(nslab):
            tm_ref[c, pl.ds(bi, tc, stride=nb), :] = u_ref[bi, :, c * LANES:(c + 1) * LANES]
    u = jnp.concatenate([tm_ref[c] for c in range(nslab)], axis=1)
    st_ref[...] = jnp.dot(u.astype(BF16), bblk_ref[...], preferred_element_type=F32)
    ar = jnp.broadcast_to(are_ref[...], (nb, n))
    ai = jnp.broadcast_to(aim_ref[...], (nb, n))

    def step(t, carry):
        hr, hi = carry
        r0 = pl.multiple_of(t * nb, nb)
        br = st_ref[pl.ds(r0, nb), 0:n]
        bi = st_ref[pl.ds(r0, nb), n:2 * n]
        nr = (ar * hr - ai * hi) + br
        ni = (ar * hi + ai * hr) + bi
        st_ref[pl.ds(r0, nb), 0:n] = nr
        st_ref[pl.ds(r0, nb), n:2 * n] = ni
        return nr, ni

    hr, hi = lax.fori_loop(0, tc, step, (hre_ref[...], him_ref[...]))
    hre_ref[...] = hr
    him_ref[...] = hi
    y = jnp.dot(st_ref[...].astype(BF16), cblk_ref[...], preferred_element_type=F32)
    z = _gelu_tanh(y + d_ref[...] * u)
    gate = jax.nn.sigmoid(jnp.dot(z.astype(BF16), gw_ref[...], preferred_element_type=F32)
                          + gb_ref[...])
    out = z * gate
    for c in range(nslab):
        tm_ref[c] = out[:, c * LANES:(c + 1) * LANES]
    for bi in range(nb):
        for c in range(nslab):
            o_ref[bi, :, c * LANES:(c + 1) * LANES] = tm_ref[c, pl.ds(bi, tc, stride=nb), :]


def _s5(u, a_re, a_im, bblk, cblk, dskip, gw, gb, *, tc):
    nb, s, sw = u.shape
    n = a_re.shape[1]
    rb = tc * nb
    full = lambda a: pl.BlockSpec(a.shape, lambda i: (0,) * a.ndim)
    blk = pl.BlockSpec((nb, tc, sw), lambda i: (0, i, 0))
    return pl.pallas_call(
        functools.partial(_s5_kernel, nb=nb, n=n),
        grid=(s // tc,),
        in_specs=[blk, full(a_re), full(a_im),
                  full(bblk), full(cblk), full(dskip), full(gw), full(gb)],
        out_specs=blk,
        out_shape=jax.ShapeDtypeStruct(u.shape, F32),
        scratch_shapes=[pltpu.VMEM((rb, 2 * n), F32), pltpu.VMEM((nb, n), F32),
                        pltpu.VMEM((nb, n), F32), pltpu.VMEM((sw // LANES, rb, LANES), F32)],
        compiler_params=_params(("arbitrary",), VMEM_LIMIT_BYTES),
        name="s5_scan",
    )(u, a_re, a_im, bblk, cblk, dskip, gw, gb)


def _merge_kernel(x_ref, att_ref, pool_ref, ssm_ref, og_ref, w_ref, o_ref, *, att_w, pool_w):
    og = og_ref[...]
    p1 = att_w + pool_w
    merged = jnp.concatenate([
        _rmsnorm(att_ref[...], og[:, :att_w]),
        _rmsnorm(pool_ref[...], og[:, att_w:p1]),
        _rmsnorm(ssm_ref[...], og[:, p1:])], axis=1).astype(BF16)
    ts = merged.shape[0]
    out = _load_tiled(x_ref, ts) + jnp.dot(merged, w_ref[...], preferred_element_type=F32)
    _store_tiled(o_ref, out, ts)


def _merge(x, att, pool, ssm, og, w, *, ts):
    b, s = x.shape[0], x.shape[1] // SUBLANES
    att_w, pool_w, sw = att.shape[2], pool.shape[2], ssm.shape[2]
    tok = lambda wd: pl.BlockSpec((None, ts, wd), lambda bi, i: (bi, i, 0))
    return pl.pallas_call(
        functools.partial(_merge_kernel, att_w=att_w, pool_w=pool_w),
        grid=(b, s // ts),
        in_specs=[_tiled_spec(ts), tok(att_w), tok(pool_w), tok(sw),
                  pl.BlockSpec(og.shape, lambda bi, i: (0, 0)),
                  pl.BlockSpec(w.shape, lambda bi, i: (0, 0))],
        out_specs=_tiled_spec(ts),
        out_shape=jax.ShapeDtypeStruct(x.shape, F32),
        compiler_params=_params(("arbitrary", "arbitrary"), VMEM_LIMIT_BYTES),
        name="merge_outproj",
    )(x, att, pool, ssm, og, w)


def _topk_rows(a, k, extra=None):
    nrow, nl = a.shape
    row = lax.broadcasted_iota(I32, (nrow, nl), 0).astype(F32)
    out_row = lax.broadcasted_iota(I32, (k, nl), 0)
    vals = jnp.zeros((k, nl), F32)
    sel = jnp.zeros((k, nl), F32)
    for j in range(k):
        m = jnp.max(a, axis=0, keepdims=True)
        ix = jnp.min(jnp.where(a == m, row, float(nrow)), axis=0, keepdims=True)
        hit = row == ix
        if extra is None:
            pick = ix
        else:
            pick = jnp.sum(jnp.where(hit, extra, 0.0), axis=0, keepdims=True)
        vals = jnp.where(out_row == j, m, vals)
        sel = jnp.where(out_row == j, pick, sel)
        a = jnp.where(hit, -jnp.inf, a)
    return vals, sel


def _route_kernel(x_ref, g_ref, wq_ref, keys_ref, h_out, off_out, sh_out, gate_out, q_sc, stage_sc,
                  *, n_half):
    hd = pl.program_id(1)
    qd = 2 * PEER_KEYS

    @pl.when(hd == 0)
    def _():
        tm = q_sc.shape[0]
        h = _rmsnorm(_load_tiled(x_ref, tm), g_ref[...])
        _store_tiled(h_out, h, tm)
        q_sc[...] = jnp.dot(h.astype(BF16), wq_ref[...], preferred_element_type=F32)

    q = q_sc[:, pl.ds(pl.multiple_of(hd * qd, qd), qd)].astype(BF16)
    tops = []
    for i in range(2):
        sc = lax.dot_general(keys_ref[i], q[:, i * PEER_KEYS:(i + 1) * PEER_KEYS], _NT,
                             preferred_element_type=F32)
        tops.append(_topk_rows(sc, PEER_TOPK))
    (s1, i1), (s2, i2) = tops
    k = PEER_TOPK
    sub8 = lax.broadcasted_iota(I32, (SUBLANES, s1.shape[1]), 0)
    cands, ecands = [], []
    a = 0
    while a < k and k // (a + 1) > 1:
        nb = k // (a + 1)
        for b0 in range(0, nb, SUBLANES):
            c = s1[a:a + 1, :] + s2[b0:b0 + SUBLANES, :]
            if b0 + SUBLANES > nb:
                c = jnp.where(sub8 < nb - b0, c, -jnp.inf)
            cands.append(c)
            ecands.append(i1[a:a + 1, :] * float(PEER_KEYS) + i2[b0:b0 + SUBLANES, :])
        a += 1
    cands.append(s1[a:, :] + s2[0:1, :])
    ecands.append(i1[a:, :] * float(PEER_KEYS) + i2[0:1, :])
    best, experts = _topk_rows(jnp.concatenate(cands, axis=0), k,
                               extra=jnp.concatenate(ecands, axis=0))
    w = jnp.exp(best - best[0:1, :])
    rows = pl.ds(pl.multiple_of(hd * k, k), k)
    stage_sc[2, rows, :] = w / jnp.sum(w, axis=0, keepdims=True)
    hi_half = experts < float(n_half)
    stage_sc[0, rows, :] = jnp.where(hi_half, experts, experts - float(n_half)) * float(SUBLANES)
    stage_sc[1, rows, :] = jnp.where(hi_half, 0.0, 16.0)

    @pl.when(hd == pl.num_programs(1) - 1)
    def _():
        off_out[...] = stage_sc[0].T.astype(I32)
        sh_out[...] = stage_sc[1].T
        gate_out[...] = stage_sc[2].T


def _route(x, g, wq, keys, *, tm, n_experts):
    t = x.shape[0] // SUBLANES
    n_heads = keys.shape[0]
    n_sel = n_heads * PEER_TOPK
    hk = pl.BlockSpec((tm, n_sel), lambda i, h: (i, 0))
    shp = lambda dt: jax.ShapeDtypeStruct((t, n_sel), dt)
    tiled = pl.BlockSpec((tm * SUBLANES, LANES), lambda i, h: (i, 0))
    return pl.pallas_call(
        functools.partial(_route_kernel, n_half=n_experts // 2),
        grid=(t // tm, n_heads),
        in_specs=[tiled,
                  pl.BlockSpec(g.shape, lambda i, h: (0, 0)),
                  pl.BlockSpec(wq.shape, lambda i, h: (0, 0)),
                  pl.BlockSpec((None, 2, PEER_KEYS, keys.shape[3]), lambda i, h: (h, 0, 0, 0))],
        out_specs=[tiled, hk, hk, hk],
        out_shape=[jax.ShapeDtypeStruct(x.shape, F32), shp(I32), shp(F32), shp(F32)],
        scratch_shapes=[pltpu.VMEM((tm, wq.shape[1]), F32), pltpu.VMEM((3, n_sel, tm), F32)],
        compiler_params=_params(("arbitrary", "arbitrary"), VMEM_LIMIT_BYTES),
        name="peer_route",
    )(x, g, wq, keys)


def _expert_row(tab_ref, off, sh):
    w = tab_ref[pl.ds(pl.multiple_of(off, SUBLANES), SUBLANES), :]
    return lax.bitcast_convert_type((w << sh) & jnp.uint32(0xFFFF0000), F32)


def _splat_rows(mxu_src, mxu_dst, xlu_src=None, xlu_dst=None, *, tt, hk, also=None):
    eye = (lax.broadcasted_iota(I32, (hk, hk), 0) == lax.broadcasted_iota(I32, (hk, hk), 1))
    ones = jnp.ones((hk, hk), BF16)

    def body(t, carry):
        v = jnp.broadcast_to(mxu_src[pl.ds(t, 1), :], (hk, hk))
        rep = jnp.dot(jnp.where(eye, v, 0.0).astype(BF16), ones, preferred_element_type=F32)
        mxu_dst(t, rep.astype(I32))
        if xlu_src is not None:
            xlu_dst(t, jnp.broadcast_to(xlu_src[pl.ds(t, 1), :], (hk, hk)).T)
        if also is not None:
            also(t)
        return carry

    lax.fori_loop(0, tt, body, 0, unroll=8)


def _splat(rep_ref, row):
    return jnp.broadcast_to(rep_ref[pl.ds(row, 1), :], (SUBLANES, LANES))


def _peer_u_kernel(*refs, tt, hk):
    n_off = tt * hk // SUBLANES
    off_refs = [refs[0].at[pl.ds(j * n_off, n_off)] for j in range(SUBLANES)]
    sh_ref, shn_ref, h_ref, gate_ref, tab_ref, coef_ref, part_ref, srep_ref, pre_ref = refs[1:]
    ngrp = hk // SUBLANES

    def store_srep(t, block):
        srep_ref[pl.ds(pl.multiple_of(t * hk, hk), hk), :] = block

    @pl.when(pl.program_id(0) == 0)
    def _():
        _splat_rows(sh_ref, store_srep, tt=tt, hk=hk)

    chunk = U_EDGES_PER_DOT
    sel = (lax.broadcasted_iota(I32, (chunk, chunk * SUBLANES), 1) // SUBLANES
           == lax.broadcasted_iota(I32, (chunk, chunk * SUBLANES), 0)).astype(BF16)

    def tok(t, carry):
        h = h_ref[t]
        base = pl.multiple_of(t * hk, hk)
        for c in range(hk // chunk):
            e0 = base + c * chunk
            ps = []
            for k0 in range(0, chunk, SUBLANES):
                sblk = srep_ref[pl.ds(e0 + k0, SUBLANES), :].astype(U32)
                for j in range(SUBLANES):
                    k = k0 + j
                    sv = jnp.broadcast_to(sblk[j:j + 1, :], (SUBLANES, LANES))
                    ps.append(_expert_row(
                        tab_ref, off_refs[j][t * ngrp + (c * chunk + k) // SUBLANES], sv) * h)
            stacked = jnp.concatenate(ps, axis=0).astype(BF16)
            part_ref[pl.ds(e0, chunk), :] = jnp.dot(sel, stacked, preferred_element_type=F32)
        return carry

    lax.fori_loop(0, tt, tok, 0, unroll=8)

    def lane_sums(t):
        p = part_ref[pl.ds(pl.multiple_of(t * hk, hk), hk), :]
        pre_ref[pl.ds(t, 1), :] = jnp.sum(p.T, axis=0, keepdims=True)

    _splat_rows(shn_ref, store_srep, tt=tt, hk=hk, also=lane_sums)
    coef_ref[...] = gate_ref[...] * _gelu_tanh(pre_ref[...])


def _peer_v_kernel(*refs, tt, hk, n_acc):
    n_off = tt * hk // SUBLANES
    off_refs = [refs[0].at[pl.ds(j * n_off, n_off)] for j in range(SUBLANES)]
    sh_ref, coef_ref, x_ref, tab_ref, o_ref, rep_ref = refs[1:]
    ngrp = hk // SUBLANES

    def store_coef(t, block):
        rep_ref[pl.ds(2 * t * hk, hk, stride=2), :] = block

    def store_shift(t, block):
        rep_ref[pl.ds(2 * t * hk + 1, hk, stride=2), :] = lax.bitcast_convert_type(block, F32)

    _splat_rows(sh_ref, store_shift, coef_ref, store_coef, tt=tt, hk=hk)

    def tok(t, carry):
        accs = [None] * n_acc
        e0 = 2 * t * hk
        per_load = SUBLANES // 2
        for k0 in range(0, hk, per_load):
            blk = rep_ref[pl.ds(pl.multiple_of(e0 + 2 * k0, SUBLANES), SUBLANES), :]
            for j in range(per_load):
                k = k0 + j
                bcast = lambda r: jnp.broadcast_to(blk[r:r + 1, :], (SUBLANES, LANES))
                off = off_refs[k % SUBLANES][t * ngrp + k // SUBLANES]
                sv = lax.bitcast_convert_type(bcast(2 * j + 1), U32)
                term = bcast(2 * j) * _expert_row(tab_ref, off, sv)
                accs[k % n_acc] = term if accs[k % n_acc] is None else accs[k % n_acc] + term
        while len(accs) > 1:
            accs = [accs[j] + accs[j + len(accs) // 2] for j in range(len(accs) // 2)]
        o_ref[t] = x_ref[t] + accs[0]
        return carry

    lax.fori_loop(0, tt, tok, 0, unroll=2)


def _resident(a):
    return pl.BlockSpec(a.shape, lambda i: (0,) * a.ndim, pipeline_mode=pl.Buffered(1))


def _split_offsets(off, tt):
    t, hk = off.shape
    grouped = off.reshape(t // tt, tt, hk // SUBLANES, SUBLANES)
    return jnp.transpose(grouped, (0, 3, 1, 2)).reshape(-1)


def _peer_u(off, sh, h, gates, tab, *, tt):
    t, hk = off.shape
    assert hk == LANES and h.shape[1:] == (SUBLANES, LANES)
    smem = pl.BlockSpec((tt * hk,), lambda i: (i,), memory_space=pltpu.SMEM)
    compact = pl.BlockSpec((tt, hk), lambda i: (i, 0))
    last = t // tt - 1
    nxt = pl.BlockSpec((tt, hk), lambda i: (jnp.minimum(i + 1, last), 0))
    return pl.pallas_call(
        functools.partial(_peer_u_kernel, tt=tt, hk=hk),
        grid=(t // tt,),
        in_specs=[
            smem, compact, nxt, pl.BlockSpec((tt, SUBLANES, LANES), lambda i: (i, 0, 0)), compact,
            _resident(tab)],
        out_specs=compact,
        out_shape=jax.ShapeDtypeStruct((t, hk), F32),
        scratch_shapes=[pltpu.VMEM((tt * hk, LANES), F32), pltpu.VMEM((tt * hk, LANES), I32),
                        pltpu.VMEM((tt, hk), F32)],
        compiler_params=_params(("arbitrary",), VMEM_LIMIT_BYTES),
        name="peer_u",
    )(_split_offsets(off, tt), sh, sh, h, gates, tab)


def _peer_v(off, sh, coef, x, tab, *, tt):
    t, hk = off.shape
    assert hk == LANES and x.shape[1:] == (SUBLANES, LANES)
    smem = pl.BlockSpec((tt * hk,), lambda i: (i,), memory_space=pltpu.SMEM)
    tile = pl.BlockSpec((tt, SUBLANES, LANES), lambda i: (i, 0, 0))
    compact = pl.BlockSpec((tt, hk), lambda i: (i, 0))
    return pl.pallas_call(
        functools.partial(_peer_v_kernel, tt=tt, hk=hk, n_acc=4),
        grid=(t // tt,),
        in_specs=[smem, compact, compact, tile, _resident(tab)],
        out_specs=tile,
        out_shape=jax.ShapeDtypeStruct(x.shape, F32),
        scratch_shapes=[pltpu.VMEM((2 * tt * hk, LANES), F32)],
        compiler_params=_params(("arbitrary",), VMEM_LIMIT_BYTES),
        name="peer_v",
    )(_split_offsets(off, tt), sh, coef, x, tab)


def _pack_kernel(hi_ref, lo_ref, o_ref):
    def bf16_bits(x):
        bits = lax.bitcast_convert_type(x, U32)
        r = bits + (jnp.uint32(0x7FFF) + ((bits >> 16) & jnp.uint32(1)))
        return jnp.where(x != x, jnp.uint32(0x7FC00000), r)

    w = (bf16_bits(hi_ref[...]) & jnp.uint32(0xFFFF0000)) | (bf16_bits(lo_ref[...]) >> 16)
    _store_tiled(o_ref, w, w.shape[0])


def _pack_table(tab, rows=256):
    n, d = tab.shape
    assert d == SUBLANES * LANES and (n // 2) % rows == 0
    nblk = n // 2 // rows
    return pl.pallas_call(
        _pack_kernel,
        grid=(nblk,),
        in_specs=[pl.BlockSpec((rows, d), lambda i: (i, 0)),
                  pl.BlockSpec((rows, d), lambda i: (i + nblk, 0))],
        out_specs=pl.BlockSpec((rows * SUBLANES, LANES), lambda i: (i, 0)),
        out_shape=jax.ShapeDtypeStruct((n // 2 * SUBLANES, LANES), U32),
        compiler_params=_params(("arbitrary",)),
        name="pack_table",
    )(tab, tab)


def _block_diag(blocks):
    g, r, c = blocks.shape
    eye = jnp.eye(g, dtype=blocks.dtype)
    return (eye[:, None, :, None] * blocks[:, :, None, :]).reshape(g * r, g * c)


def _s5_params(a_re, a_im, log_dt, b_re, b_im, c_re, c_im):
    a = lax.complex(a_re, a_im)
    dt = jnp.exp(log_dt)[:, None]
    a_bar = jnp.exp(a * dt)
    b_bar = ((a_bar - 1.0) / a)[..., None] * lax.complex(b_re, b_im)
    bt = jnp.swapaxes(b_bar, 1, 2)
    bblk = jnp.concatenate([_block_diag(jnp.real(bt)), _block_diag(jnp.imag(bt))], axis=1)
    ct_re = jnp.swapaxes(c_re, 1, 2)
    ct_im = jnp.swapaxes(c_im, 1, 2)
    cblk = jnp.concatenate([_block_diag(ct_re), -_block_diag(ct_im)], axis=0)
    n = a_bar.size
    return (jnp.real(a_bar).reshape(1, n), jnp.imag(a_bar).reshape(1, n),
            bblk.astype(BF16), cblk.astype(BF16))


def _tiles(s, t):
    ts = min(512, s)
    blk = min(512, s)
    tc = min(64, s)
    tm = min(512, t)
    tt = min(128, t)
    return ts, blk, tc, tm, tt


def _layer(x, norm1_g, w_in, f_bias, q_gain, k_gain, pool_w, pool_scale,
           a_re, a_im, log_dt, b_re, b_im, c_re, c_im, ssm_d, glu_w, glu_b,
           out_norm_g, w_out, norm2_g, peer_wq, peer_keys, peer_u, peer_v):
    b, s, d = x.shape[0], x.shape[1] // SUBLANES, SUBLANES * LANES
    t = b * s
    n_heads = f_bias.shape[0]
    att_w = n_heads * HEAD_DIM
    pool_width = pool_scale.shape[0]
    ssm_w = ssm_d.shape[0]
    ts, blk, tc, tm, tt = _tiles(s, t)
    row = lambda v: v.reshape(1, -1)

    c0, c1 = 3 * att_w, 3 * att_w + n_heads
    wf = jnp.zeros((d, LANES), F32).at[:, :n_heads].set(w_in[:, c0:c1])
    fb = jnp.zeros((1, LANES), F32).at[0, :n_heads].set(f_bias)
    seg = _block_diag(jnp.full((n_heads, HEAD_DIM, HEAD_DIM), 1.0 / HEAD_DIM, F32)).astype(BF16)
    q, k, v, c_tok, pool, u_ssm = _inproj(
        x, row(norm1_g), w_in[:, :c0].astype(BF16), wf,
        w_in[:, c1:c1 + pool_width].astype(BF16), w_in[:, c1 + pool_width:].astype(BF16), fb,
        row(jnp.tile(q_gain, n_heads) * HEAD_DIM ** -0.5), row(jnp.tile(k_gain, n_heads)),
        seg, _block_diag(pool_w).astype(BF16), row(pool_scale), ts=ts)
    att = _attention(q, k, v, c_tok, jnp.swapaxes(c_tok, 1, 2), blk=blk)
    are, aim, bblk, cblk = _s5_params(a_re, a_im, log_dt, b_re, b_im, c_re, c_im)
    ssm = _s5(u_ssm, are, aim, bblk, cblk, row(ssm_d), glu_w.astype(BF16), row(glu_b), tc=tc)
    x1 = _merge(x, att, pool, ssm, row(out_norm_g), w_out.astype(BF16), ts=ts)

    n_experts = peer_u.shape[0]
    kh = peer_keys.astype(BF16)
    h2, off, sh, gates = _route(x1.reshape(t * SUBLANES, LANES), row(norm2_g),
                                peer_wq.astype(BF16), kh, tm=tm, n_experts=n_experts)
    tiles = lambda a: a.reshape(t, SUBLANES, LANES)
    coef = _peer_u(off, sh, tiles(h2), gates, _pack_table(peer_u), tt=tt)
    x2 = _peer_v(off, sh, coef, tiles(x1), _pack_table(peer_v), tt=tt)
    return x2.reshape(b, s * SUBLANES, LANES)


def kernel(x, norm1_g, w_in, f_bias, q_gain, k_gain, pool_w, pool_scale, ssm_a_re, ssm_a_im,
           ssm_log_dt, ssm_b_re, ssm_b_im, ssm_c_re, ssm_c_im, ssm_d, glu_w, glu_b, out_norm_g,
           w_out, norm2_g, peer_wq, peer_keys, peer_u, peer_v):
    per_layer = (norm1_g, w_in, f_bias, q_gain, k_gain, pool_w, pool_scale, ssm_a_re, ssm_a_im,
                 ssm_log_dt, ssm_b_re, ssm_b_im, ssm_c_re, ssm_c_im, ssm_d, glu_w, glu_b,
                 out_norm_g, w_out, norm2_g, peer_wq, peer_keys, peer_u, peer_v)
    b, s, d = x.shape
    assert d == SUBLANES * LANES
    x = x.reshape(b, s * SUBLANES, LANES)
    for l in range(norm1_g.shape[0]):
        x = _layer(x, *[p[l] for p in per_layer])
    return x.reshape(b, s, d)
```

```python
import functools
import math

import jax
import jax.numpy as jnp
from jax import lax
from jax.experimental import pallas as pl
from jax.experimental.pallas import tpu as pltpu

F32 = jnp.float32
BF16 = jnp.bfloat16
U32 = jnp.uint32
I32 = jnp.int32

EPS = 1e-6
HEAD_DIM = 64
POOL_WINDOWS = (2, 4, 8, 16)
POOL_HALO = 16
PEER_KEYS = 128
PEER_TOPK = 16
U_EDGES_PER_DOT = 32

LANES = 128
SUBLANES = 8
VMEM_LIMIT_BYTES = 56 * 1024 * 1024

_NT = (((1,), (1,)), ((), ()))


def _rmsnorm(x, g):
    return x * lax.rsqrt(jnp.mean(x * x, axis=-1, keepdims=True) + EPS) * g


def _gelu_tanh(x):
    c = math.sqrt(2.0 / math.pi)
    return x * (0.5 * (1.0 + jnp.tanh(c * (x + 0.044715 * (x * x * x)))))


def _params(sem, vmem=None):
    return pltpu.CompilerParams(dimension_semantics=sem, vmem_limit_bytes=vmem)


def _load_tiled(ref, n):
    return jnp.concatenate([ref[pl.ds(s, n, stride=SUBLANES), :] for s in range(SUBLANES)], axis=1)


def _store_tiled(ref, val, n):
    for s in range(SUBLANES):
        ref[pl.ds(s, n, stride=SUBLANES), :] = val[:, s * LANES:(s + 1) * LANES]


def _inproj_kernel(x_ref, g_ref, wqkv_ref, wf_ref, wpool_ref, wssm_ref, fb_ref,
                   qg_ref, kg_ref, seg_ref, pmix_ref, pscale_ref,
                   q_out, k_out, v_out, c_out, pool_out, ssm_out, carry_ref, halo_ref,
                   *, att_w, n_heads):
    i = pl.program_id(1)

    @pl.when(i == 0)
    def _():
        carry_ref[...] = jnp.zeros_like(carry_ref)
        halo_ref[...] = jnp.zeros_like(halo_ref)

    ts = x_ref.shape[0] // SUBLANES
    x = _load_tiled(x_ref, ts)
    h = _rmsnorm(x, g_ref[...])
    hb = h.astype(BF16)
    qkv = jnp.dot(hb, wqkv_ref[...], preferred_element_type=F32)
    seg = seg_ref[...]

    def headnorm(t, gain):
        ms = jnp.dot((t * t).astype(BF16), seg, preferred_element_type=F32)
        return t * lax.rsqrt(ms + EPS) * gain

    q_out[...] = headnorm(qkv[:, :att_w], qg_ref[...]).astype(BF16)
    k_out[...] = headnorm(qkv[:, att_w:2 * att_w], kg_ref[...]).astype(BF16)
    v_out[...] = qkv[:, 2 * att_w:].astype(BF16)
    u_pool = jnp.dot(hb, wpool_ref[...], preferred_element_type=F32)
    pool_out[...] = _pool_mix(u_pool, halo_ref[...], i * ts, pmix_ref[...], pscale_ref[...])
    halo_ref[...] = u_pool[ts - POOL_HALO:, :]
    ssm_out[...] = jnp.dot(hb, wssm_ref[...], preferred_element_type=F32)

    f = jnp.dot(h, wf_ref[...], precision=lax.Precision.HIGHEST,
                preferred_element_type=F32) + fb_ref[...]
    ls = jnp.minimum(f, 0.0) - jnp.log1p(jnp.exp(-jnp.abs(f)))
    r = lax.broadcasted_iota(I32, (ts, ts), 0)
    c = lax.broadcasted_iota(I32, (ts, ts), 1)
    tri = jnp.where(c <= r, 1.0, 0.0).astype(F32)
    cs = jnp.dot(tri, ls, precision=lax.Precision.HIGHEST,
                 preferred_element_type=F32) + carry_ref[0:1, :]
    carry_ref[...] = jnp.broadcast_to(cs[ts - 1:ts, :], carry_ref.shape)
    c_out[...] = cs[:, :n_heads]


def _tiled_spec(ts):
    return pl.BlockSpec((None, ts * SUBLANES, LANES), lambda bi, i: (bi, i, 0))


def _inproj(x, g, wqkv, wf, wpool, wssm, fb, qg, kg, seg, pmix, pscale, *, ts):
    b, s = x.shape[0], x.shape[1] // SUBLANES
    att_w = seg.shape[0]
    n_heads = att_w // HEAD_DIM
    pw = wpool.shape[1]
    sw = wssm.shape[1]
    full = lambda a: pl.BlockSpec(a.shape, lambda bi, i: (0,) * a.ndim)
    tok = lambda w: pl.BlockSpec((None, ts, w), lambda bi, i: (bi, i, 0))
    return pl.pallas_call(
        functools.partial(_inproj_kernel, att_w=att_w, n_heads=n_heads),
        grid=(b, s // ts),
        in_specs=[_tiled_spec(ts), full(g), full(wqkv), full(wf), full(wpool), full(wssm),
                  full(fb), full(qg), full(kg), full(seg), full(pmix), full(pscale)],
        out_specs=[tok(att_w), tok(att_w), tok(att_w), tok(n_heads), tok(pw), tok(sw)],
        out_shape=[jax.ShapeDtypeStruct((b, s, att_w), BF16),
                   jax.ShapeDtypeStruct((b, s, att_w), BF16),
                   jax.ShapeDtypeStruct((b, s, att_w), BF16),
                   jax.ShapeDtypeStruct((b, s, n_heads), F32),
                   jax.ShapeDtypeStruct((b, s, pw), F32),
                   jax.ShapeDtypeStruct((b, s, sw), F32)],
        scratch_shapes=[pltpu.VMEM((SUBLANES, LANES), F32), pltpu.VMEM((POOL_HALO, pw), F32)],
        compiler_params=_params(("arbitrary", "arbitrary"), VMEM_LIMIT_BYTES),
        name="inproj",
    )(x, g, wqkv, wf, wpool, wssm, fb, qg, kg, seg, pmix, pscale)


def _attn_kernel(q_ref, k_ref, v_ref, ct_ref, cr_ref, o_ref, m_sc, l_sc, acc_sc, s_sc, *, blk):
    p = pl.program_id(1)
    i = pl.program_id(2)
    q = q_ref[...]
    lane = lax.broadcasted_iota(I32, (blk, LANES), 1)
    ct = ct_ref[...]
    hl = lax.broadcasted_iota(I32, ct.shape, 1)
    heads = (2 * p, 2 * p + 1)
    qm = (jnp.where(lane < HEAD_DIM, q, jnp.zeros_like(q)),
          jnp.where(lane >= HEAD_DIM, q, jnp.zeros_like(q)))
    cq = tuple(jnp.sum(jnp.where(hl == hd, ct, 0.0), axis=1, keepdims=True) for hd in heads)
    m_sc[...] = jnp.full(m_sc.shape, -jnp.inf, F32)
    l_sc[...] = jnp.zeros(l_sc.shape, F32)
    acc_sc[...] = jnp.zeros(acc_sc.shape, F32)

    def qk(kb):
        kblk = k_ref[pl.ds(pl.multiple_of(kb * blk, blk), blk), :]
        return [lax.dot_general(qm[hh], kblk, _NT, preferred_element_type=F32) for hh in range(2)]

    def step(kb, s_cur, causal):
        start = pl.multiple_of(kb * blk, blk)
        vblk = v_ref[pl.ds(start, blk), :]
        for hh in range(2):
            ck = cr_ref[pl.ds(heads[hh], 1), pl.ds(start, blk)]
            z = (s_cur[hh] + cq[hh]) - ck
            if causal:
                row = lax.broadcasted_iota(I32, (blk, blk), 0)
                col = lax.broadcasted_iota(I32, (blk, blk), 1)
                z = jnp.where(col <= row, z, -jnp.inf)
            m_old = m_sc[hh]
            m_new = jnp.maximum(m_old, jnp.max(z, axis=1, keepdims=True))
            alpha = jnp.exp(m_old - m_new)
            pe = jnp.exp(z - m_new)
            l_sc[hh] = alpha * l_sc[hh] + jnp.sum(pe, axis=1, keepdims=True)
            acc_sc[hh] = alpha * acc_sc[hh] + jnp.dot(pe.astype(BF16), vblk,
                                                      preferred_element_type=F32)
            m_sc[hh] = m_new

    s0 = qk(0)
    s_sc[0] = s0[0]
    s_sc[1] = s0[1]

    def body(kb, carry):
        s_cur = [s_sc[0], s_sc[1]]
        s_next = qk(kb + 1)
        step(kb, s_cur, False)
        s_sc[0] = s_next[0]
        s_sc[1] = s_next[1]
        return carry

    lax.fori_loop(0, i, body, 0)
    step(i, [s_sc[0], s_sc[1]], True)
    o_ref[...] = jnp.where(lane < HEAD_DIM, acc_sc[0] / l_sc[0], acc_sc[1] / l_sc[1])


def _attention(q, k, v, c_tok, c_row, *, blk):
    b, s, w = q.shape
    n_heads = c_tok.shape[2]
    return pl.pallas_call(
        functools.partial(_attn_kernel, blk=blk),
        grid=(b, w // LANES, s // blk),
        in_specs=[pl.BlockSpec((None, blk, LANES), lambda bi, p, i: (bi, i, p)),
                  pl.BlockSpec((None, s, LANES), lambda bi, p, i: (bi, 0, p)),
                  pl.BlockSpec((None, s, LANES), lambda bi, p, i: (bi, 0, p)),
                  pl.BlockSpec((None, blk, n_heads), lambda bi, p, i: (bi, i, 0)),
                  pl.BlockSpec((None, n_heads, s), lambda bi, p, i: (bi, 0, 0))],
        out_specs=pl.BlockSpec((None, blk, LANES), lambda bi, p, i: (bi, i, p)),
        out_shape=jax.ShapeDtypeStruct((b, s, w), F32),
        scratch_shapes=[pltpu.VMEM((2, blk, 1), F32), pltpu.VMEM((2, blk, 1), F32),
                        pltpu.VMEM((2, blk, LANES), F32), pltpu.VMEM((2, blk, blk), F32)],
        compiler_params=_params(("arbitrary", "arbitrary", "arbitrary"), VMEM_LIMIT_BYTES),
        name="fox_attention",
    )(q, k, v, c_tok, c_row)


def _pool_mix(cur, prev, first_pos, w, scale):
    ts, pw = cur.shape
    group_dim = pw // len(POOL_WINDOWS)
    ext = jnp.concatenate([prev, cur], axis=0)
    pos = first_pos + lax.broadcasted_iota(I32, (ts, 1), 0)
    grp = lax.broadcasted_iota(I32, (ts, pw), 1) // group_dim
    acc = cur
    pooled = jnp.zeros_like(cur)
    for j in range(1, max(POOL_WINDOWS)):
        acc = acc + ext[POOL_HALO - j:POOL_HALO - j + ts, :]
        win = j + 1
        if win in POOL_WINDOWS:
            cnt = jnp.minimum(pos + 1, win).astype(F32)
            pooled = jnp.where(grp == POOL_WINDOWS.index(win), acc / cnt, pooled)
    pooled = pooled - cur
    return jnp.dot(pooled.astype(BF16), w, preferred_element_type=F32) * scale


def _s5_kernel(u_ref, are_ref, aim_ref, bblk_ref, cblk_ref, d_ref, gw_ref, gb_ref,
               o_ref, st_ref, hre_ref, him_ref, tm_ref, *, nb, n):
    @pl.when(pl.program_id(0) == 0)
    def _():
        hre_ref[...] = jnp.zeros_like(hre_ref)
        him_ref[...] = jnp.zeros_like(him_ref)

    tc = u_ref.shape[1]
    nslab = tm_ref.shape[0]
    for bi in range(nb):
        for c in range(nslab):
            tm_ref[c, pl.ds(bi, tc, stride=nb), :] = u_ref[bi, :, c * LANES:(c + 1) * LANES]
    u = jnp.concatenate([tm_ref[c] for c in range(nslab)], axis=1)
    st_ref[...] = jnp.dot(u.astype(BF16), bblk_ref[...], preferred_element_type=F32)
    ar = jnp.broadcast_to(are_ref[...], (nb, n))
    ai = jnp.broadcast_to(aim_ref[...], (nb, n))

    def step(t, carry):
        hr, hi = carry
        r0 = pl.multiple_of(t * nb, nb)
        br = st_ref[pl.ds(r0, nb), 0:n]
        bi = st_ref[pl.ds(r0, nb), n:2 * n]
        nr = (ar * hr - ai * hi) + br
        ni = (ar * hi + ai * hr) + bi
        st_ref[pl.ds(r0, nb), 0:n] = nr
        st_ref[pl.ds(r0, nb), n:2 * n] = ni
        return nr, ni

    hr, hi = lax.fori_loop(0, tc, step, (hre_ref[...], him_ref[...]))
    hre_ref[...] = hr
    him_ref[...] = hi
    y = jnp.dot(st_ref[...].astype(BF16), cblk_ref[...], preferred_element_type=F32)
    z = _gelu_tanh(y + d_ref[...] * u)
    gate = jax.nn.sigmoid(jnp.dot(z.astype(BF16), gw_ref[...], preferred_element_type=F32)
                          + gb_ref[...])
    out = z * gate
    for c in range(nslab):
        tm_ref[c] = out[:, c * LANES:(c + 1) * LANES]
    for bi in range(nb):
        for c in range(nslab):
            o_ref[bi, :, c * LANES:(c + 1) * LANES] = tm_ref[c, pl.ds(bi, tc, stride=nb), :]


def _s5(u, a_re, a_im, bblk, cblk, dskip, gw, gb, *, tc):
    nb, s, sw = u.shape
    n = a_re.shape[1]
    rb = tc * nb
    full = lambda a: pl.BlockSpec(a.shape, lambda i: (0,) * a.ndim)
    blk = pl.BlockSpec((nb, tc, sw), lambda i: (0, i, 0))
    return pl.pallas_call(
        functools.partial(_s5_kernel, nb=nb, n=n),
        grid=(s // tc,),
        in_specs=[blk, full(a_re), full(a_im),
                  full(bblk), full(cblk), full(dskip), full(gw), full(gb)],
        out_specs=blk,
        out_shape=jax.ShapeDtypeStruct(u.shape, F32),
        scratch_shapes=[pltpu.VMEM((rb, 2 * n), F32), pltpu.VMEM((nb, n), F32),
                        pltpu.VMEM((nb, n), F32), pltpu.VMEM((sw // LANES, rb, LANES), F32)],
        compiler_params=_params(("arbitrary",), VMEM_LIMIT_BYTES),
        name="s5_scan",
    )(u, a_re, a_im, bblk, cblk, dskip, gw, gb)


def _merge_kernel(x_ref, att_ref, pool_ref, ssm_ref, og_ref, w_ref, o_ref, *, att_w, pool_w):
    og = og_ref[...]
    p1 = att_w + pool_w
    merged = jnp.concatenate([
        _rmsnorm(att_ref[...], og[:, :att_w]),
        _rmsnorm(pool_ref[...], og[:, att_w:p1]),
        _rmsnorm(ssm_ref[...], og[:, p1:])], axis=1).astype(BF16)
    ts = merged.shape[0]
    out = _load_tiled(x_ref, ts) + jnp.dot(merged, w_ref[...], preferred_element_type=F32)
    _store_tiled(o_ref, out, ts)


def _merge(x, att, pool, ssm, og, w, *, ts):
    b, s = x.shape[0], x.shape[1] // SUBLANES
    att_w, pool_w, sw = att.shape[2], pool.shape[2], ssm.shape[2]
    tok = lambda wd: pl.BlockSpec((None, ts, wd), lambda bi, i: (bi, i, 0))
    return pl.pallas_call(
        functools.partial(_merge_kernel, att_w=att_w, pool_w=pool_w),
        grid=(b, s // ts),
        in_specs=[_tiled_spec(ts), tok(att_w), tok(pool_w), tok(sw),
                  pl.BlockSpec(og.shape, lambda bi, i: (0, 0)),
                  pl.BlockSpec(w.shape, lambda bi, i: (0, 0))],
        out_specs=_tiled_spec(ts),
        out_shape=jax.ShapeDtypeStruct(x.shape, F32),
        compiler_params=_params(("arbitrary", "arbitrary"), VMEM_LIMIT_BYTES),
        name="merge_outproj",
    )(x, att, pool, ssm, og, w)


def _topk_rows(a, k, extra=None):
    nrow, nl = a.shape
    row = lax.broadcasted_iota(I32, (nrow, nl), 0).astype(F32)
    out_row = lax.broadcasted_iota(I32, (k, nl), 0)
    vals = jnp.zeros((k, nl), F32)
    sel = jnp.zeros((k, nl), F32)
    for j in range(k):
        m = jnp.max(a, axis=0, keepdims=True)
        ix = jnp.min(jnp.where(a == m, row, float(nrow)), axis=0, keepdims=True)
        hit = row == ix
        if extra is None:
            pick = ix
        else:
            pick = jnp.sum(jnp.where(hit, extra, 0.0), axis=0, keepdims=True)
        vals = jnp.where(out_row == j, m, vals)
        sel = jnp.where(out_row == j, pick, sel)
        a = jnp.where(hit, -jnp.inf, a)
    return vals, sel


def _route_kernel(x_ref, g_ref, wq_ref, keys_ref, h_out, off_out, sh_out, gate_out, q_sc, stage_sc,
                  *, n_half):
    hd = pl.program_id(1)
    qd = 2 * PEER_KEYS

    @pl.when(hd == 0)
    def _():
        tm = q_sc.shape[0]
        h = _rmsnorm(_load_tiled(x_ref, tm), g_ref[...])
        _store_tiled(h_out, h, tm)
        q_sc[...] = jnp.dot(h.astype(BF16), wq_ref[...], preferred_element_type=F32)

    q = q_sc[:, pl.ds(pl.multiple_of(hd * qd, qd), qd)].astype(BF16)
    tops = []
    for i in range(2):
        sc = lax.dot_general(keys_ref[i], q[:, i * PEER_KEYS:(i + 1) * PEER_KEYS], _NT,
                             preferred_element_type=F32)
        tops.append(_topk_rows(sc, PEER_TOPK))
    (s1, i1), (s2, i2) = tops
    k = PEER_TOPK
    sub8 = lax.broadcasted_iota(I32, (SUBLANES, s1.shape[1]), 0)
    cands, ecands = [], []
    a = 0
    while a < k and k // (a + 1) > 1:
        nb = k // (a + 1)
        for b0 in range(0, nb, SUBLANES):
            c = s1[a:a + 1, :] + s2[b0:b0 + SUBLANES, :]
            if b0 + SUBLANES > nb:
                c = jnp.where(sub8 < nb - b0, c, -jnp.inf)
            cands.append(c)
            ecands.append(i1[a:a + 1, :] * float(PEER_KEYS) + i2[b0:b0 + SUBLANES, :])
        a += 1
    cands.append(s1[a:, :] + s2[0:1, :])
    ecands.append(i1[a:, :] * float(PEER_KEYS) + i2[0:1, :])
    best, experts = _topk_rows(jnp.concatenate(cands, axis=0), k,
                               extra=jnp.concatenate(ecands, axis=0))
    w = jnp.exp(best - best[0:1, :])
    rows = pl.ds(pl.multiple_of(hd * k, k), k)
    stage_sc[2, rows, :] = w / jnp.sum(w, axis=0, keepdims=True)
    hi_half = experts < float(n_half)
    stage_sc[0, rows, :] = jnp.where(hi_half, experts, experts - float(n_half)) * float(SUBLANES)
    stage_sc[1, rows, :] = jnp.where(hi_half, 0.0, 16.0)

    @pl.when(hd == pl.num_programs(1) - 1)
    def _():
        off_out[...] = stage_sc[0].T.astype(I32)
        sh_out[...] = stage_sc[1].T
        gate_out[...] = stage_sc[2].T


def _route(x, g, wq, keys, *, tm, n_experts):
    t = x.shape[0] // SUBLANES
    n_heads = keys.shape[0]
    n_sel = n_heads * PEER_TOPK
    hk = pl.BlockSpec((tm, n_sel), lambda i, h: (i, 0))
    shp = lambda dt: jax.ShapeDtypeStruct((t, n_sel), dt)
    tiled = pl.BlockSpec((tm * SUBLANES, LANES), lambda i, h: (i, 0))
    return pl.pallas_call(
        functools.partial(_route_kernel, n_half=n_experts // 2),
        grid=(t // tm, n_heads),
        in_specs=[tiled,
                  pl.BlockSpec(g.shape, lambda i, h: (0, 0)),
                  pl.BlockSpec(wq.shape, lambda i, h: (0, 0)),
                  pl.BlockSpec((None, 2, PEER_KEYS, keys.shape[3]), lambda i, h: (h, 0, 0, 0))],
        out_specs=[tiled, hk, hk, hk],
        out_shape=[jax.ShapeDtypeStruct(x.shape, F32), shp(I32), shp(F32), shp(F32)],
        scratch_shapes=[pltpu.VMEM((tm, wq.shape[1]), F32), pltpu.VMEM((3, n_sel, tm), F32)],
        compiler_params=_params(("arbitrary", "arbitrary"), VMEM_LIMIT_BYTES),
        name="peer_route",
    )(x, g, wq, keys)


def _expert_row(tab_ref, off, sh):
    w = tab_ref[pl.ds(pl.multiple_of(off, SUBLANES), SUBLANES), :]
    return lax.bitcast_convert_type((w << sh) & jnp.uint32(0xFFFF0000), F32)


def _splat_rows(mxu_src, mxu_dst, xlu_src=None, xlu_dst=None, *, tt, hk, also=None):
    eye = (lax.broadcasted_iota(I32, (hk, hk), 0) == lax.broadcasted_iota(I32, (hk, hk), 1))
    ones = jnp.ones((hk, hk), BF16)

    def body(t, carry):
        v = jnp.broadcast_to(mxu_src[pl.ds(t, 1), :], (hk, hk))
        rep = jnp.dot(jnp.where(eye, v, 0.0).astype(BF16), ones, preferred_element_type=F32)
        mxu_dst(t, rep.astype(I32))
        if xlu_src is not None:
            xlu_dst(t, jnp.broadcast_to(xlu_src[pl.ds(t, 1), :], (hk, hk)).T)
        if also is not None:
            also(t)
        return carry

    lax.fori_loop(0, tt, body, 0, unroll=8)


def _splat(rep_ref, row):
    return jnp.broadcast_to(rep_ref[pl.ds(row, 1), :], (SUBLANES, LANES))


def _peer_u_kernel(*refs, tt, hk):
    n_off = tt * hk // SUBLANES
    off_refs = [refs[0].at[pl.ds(j * n_off, n_off)] for j in range(SUBLANES)]
    sh_ref, shn_ref, h_ref, gate_ref, tab_ref, coef_ref, part_ref, srep_ref, pre_ref = refs[1:]
    ngrp = hk // SUBLANES

    def store_srep(t, block):
        srep_ref[pl.ds(pl.multiple_of(t * hk, hk), hk), :] = block

    @pl.when(pl.program_id(0) == 0)
    def _():
        _splat_rows(sh_ref, store_srep, tt=tt, hk=hk)

    chunk = U_EDGES_PER_DOT
    sel = (lax.broadcasted_iota(I32, (chunk, chunk * SUBLANES), 1) // SUBLANES
           == lax.broadcasted_iota(I32, (chunk, chunk * SUBLANES), 0)).astype(BF16)

    def tok(t, carry):
        h = h_ref[t]
        base = pl.multiple_of(t * hk, hk)
        for c in range(hk // chunk):
            e0 = base + c * chunk
            ps = []
            for k0 in range(0, chunk, SUBLANES):
                sblk = srep_ref[pl.ds(e0 + k0, SUBLANES), :].astype(U32)
                for j in range(SUBLANES):
                    k = k0 + j
                    sv = jnp.broadcast_to(sblk[j:j + 1, :], (SUBLANES, LANES))
                    ps.append(_expert_row(
                        tab_ref, off_refs[j][t * ngrp + (c * chunk + k) // SUBLANES], sv) * h)
            stacked = jnp.concatenate(ps, axis=0).astype(BF16)
            part_ref[pl.ds(e0, chunk), :] = jnp.dot(sel, stacked, preferred_element_type=F32)
        return carry

    lax.fori_loop(0, tt, tok, 0, unroll=8)

    def lane_sums(t):
        p = part_ref[pl.ds(pl.multiple_of(t * hk, hk), hk), :]
        pre_ref[pl.ds(t, 1), :] = jnp.sum(p.T, axis=0, keepdims=True)

    _splat_rows(shn_ref, store_srep, tt=tt, hk=hk, also=lane_sums)
    coef_ref[...] = gate_ref[...] * _gelu_tanh(pre_ref[...])


def _peer_v_kernel(*refs, tt, hk, n_acc):
    n_off = tt * hk // SUBLANES
    off_refs = [refs[0].at[pl.ds(j * n_off, n_off)] for j in range(SUBLANES)]
    sh_ref, coef_ref, x_ref, tab_ref, o_ref, rep_ref = refs[1:]
    ngrp = hk // SUBLANES

    def store_coef(t, block):
        rep_ref[pl.ds(2 * t * hk, hk, stride=2), :] = block

    def store_shift(t, block):
        rep_ref[pl.ds(2 * t * hk + 1, hk, stride=2), :] = lax.bitcast_convert_type(block, F32)

    _splat_rows(sh_ref, store_shift, coef_ref, store_coef, tt=tt, hk=hk)

    def tok(t, carry):
        accs = [None] * n_acc
        e0 = 2 * t * hk
        per_load = SUBLANES // 2
        for k0 in range(0, hk, per_load):
            blk = rep_ref[pl.ds(pl.multiple_of(e0 + 2 * k0, SUBLANES), SUBLANES), :]
            for j in range(per_load):
                k = k0 + j
                bcast = lambda r: jnp.broadcast_to(blk[r:r + 1, :], (SUBLANES, LANES))
                off = off_refs[k % SUBLANES][t * ngrp + k // SUBLANES]
                sv = lax.bitcast_convert_type(bcast(2 * j + 1), U32)
                term = _splat(rep_ref, e0 + 2 * k) * _expert_row(tab_ref, off, sv)
                accs[k % n_acc] = term if accs[k % n_acc] is None else accs[k % n_acc] + term
        while len(accs) > 1:
            accs = [accs[j] + accs[j + len(accs) // 2] for j in range(len(accs) // 2)]
        o_ref[t] = x_ref[t] + accs[0]
        return carry

    lax.fori_loop(0, tt, tok, 0, unroll=2)


def _resident(a):
    return pl.BlockSpec(a.shape, lambda i: (0,) * a.ndim, pipeline_mode=pl.Buffered(1))


def _split_offsets(off, tt):
    t, hk = off.shape
    grouped = off.reshape(t // tt, tt, hk // SUBLANES, SUBLANES)
    return jnp.transpose(grouped, (0, 3, 1, 2)).reshape(-1)


def _peer_u(off, sh, h, gates, tab, *, tt):
    t, hk = off.shape
    assert hk == LANES and h.shape[1:] == (SUBLANES, LANES)
    smem = pl.BlockSpec((tt * hk,), lambda i: (i,), memory_space=pltpu.SMEM)
    compact = pl.BlockSpec((tt, hk), lambda i: (i, 0))
    last = t // tt - 1
    nxt = pl.BlockSpec((tt, hk), lambda i: (jnp.minimum(i + 1, last), 0))
    return pl.pallas_call(
        functools.partial(_peer_u_kernel, tt=tt, hk=hk),
        grid=(t // tt,),
        in_specs=[
            smem, compact, nxt, pl.BlockSpec((tt, SUBLANES, LANES), lambda i: (i, 0, 0)), compact,
            _resident(tab)],
        out_specs=compact,
        out_shape=jax.ShapeDtypeStruct((t, hk), F32),
        scratch_shapes=[pltpu.VMEM((tt * hk, LANES), F32), pltpu.VMEM((tt * hk, LANES), I32),
                        pltpu.VMEM((tt, hk), F32)],
        compiler_params=_params(("arbitrary",), VMEM_LIMIT_BYTES),
        name="peer_u",
    )(_split_offsets(off, tt), sh, sh, h, gates, tab)


def _peer_v(off, sh, coef, x, tab, *, tt):
    t, hk = off.shape
    assert hk == LANES and x.shape[1:] == (SUBLANES, LANES)
    smem = pl.BlockSpec((tt * hk,), lambda i: (i,), memory_space=pltpu.SMEM)
    tile = pl.BlockSpec((tt, SUBLANES, LANES), lambda i: (i, 0, 0))
    compact = pl.BlockSpec((tt, hk), lambda i: (i, 0))
    return pl.pallas_call(
        functools.partial(_peer_v_kernel, tt=tt, hk=hk, n_acc=4),
        grid=(t // tt,),
        in_specs=[smem, compact, compact, tile, _resident(tab)],
        out_specs=tile,
        out_shape=jax.ShapeDtypeStruct(x.shape, F32),
        scratch_shapes=[pltpu.VMEM((2 * tt * hk, LANES), F32)],
        compiler_params=_params(("arbitrary",), VMEM_LIMIT_BYTES),
        name="peer_v",
    )(_split_offsets(off, tt), sh, coef, x, tab)


def _pack_kernel(hi_ref, lo_ref, o_ref):
    def bf16_bits(x):
        bits = lax.bitcast_convert_type(x, U32)
        r = bits + (jnp.uint32(0x7FFF) + ((bits >> 16) & jnp.uint32(1)))
        return jnp.where(x != x, jnp.uint32(0x7FC00000), r)

    w = (bf16_bits(hi_ref[...]) & jnp.uint32(0xFFFF0000)) | (bf16_bits(lo_ref[...]) >> 16)
    _store_tiled(o_ref, w, w.shape[0])


def _pack_table(tab, rows=256):
    n, d = tab.shape
    assert d == SUBLANES * LANES and (n // 2) % rows == 0
    nblk = n // 2 // rows
    return pl.pallas_call(
        _pack_kernel,
        grid=(nblk,),
        in_specs=[pl.BlockSpec((rows, d), lambda i: (i, 0)),
                  pl.BlockSpec((rows, d), lambda i: (i + nblk, 0))],
        out_specs=pl.BlockSpec((rows * SUBLANES, LANES), lambda i: (i, 0)),
        out_shape=jax.ShapeDtypeStruct((n // 2 * SUBLANES, LANES), U32),
        compiler_params=_params(("arbitrary",)),
        name="pack_table",
    )(tab, tab)


def _block_diag(blocks):
    g, r, c = blocks.shape
    eye = jnp.eye(g, dtype=blocks.dtype)
    return (eye[:, None, :, None] * blocks[:, :, None, :]).reshape(g * r, g * c)


def _s5_params(a_re, a_im, log_dt, b_re, b_im, c_re, c_im):
    a = lax.complex(a_re, a_im)
    dt = jnp.exp(log_dt)[:, None]
    a_bar = jnp.exp(a * dt)
    b_bar = ((a_bar - 1.0) / a)[..., None] * lax.complex(b_re, b_im)
    bt = jnp.swapaxes(b_bar, 1, 2)
    bblk = jnp.concatenate([_block_diag(jnp.real(bt)), _block_diag(jnp.imag(bt))], axis=1)
    ct_re = jnp.swapaxes(c_re, 1, 2)
    ct_im = jnp.swapaxes(c_im, 1, 2)
    cblk = jnp.concatenate([_block_diag(ct_re), -_block_diag(ct_im)], axis=0)
    n = a_bar.size
    return (jnp.real(a_bar).reshape(1, n), jnp.imag(a_bar).reshape(1, n),
            bblk.astype(BF16), cblk.astype(BF16))


def _tiles(s, t):
    ts = min(512, s)
    blk = min(512, s)
    tc = min(64, s)
    tm = min(512, t)
    tt = min(128, t)
    return ts, blk, tc, tm, tt


def _layer(x, norm1_g, w_in, f_bias, q_gain, k_gain, pool_w, pool_scale,
           a_re, a_im, log_dt, b_re, b_im, c_re, c_im, ssm_d, glu_w, glu_b,
           out_norm_g, w_out, norm2_g, peer_wq, peer_keys, peer_u, peer_v):
    b, s, d = x.shape[0], x.shape[1] // SUBLANES, SUBLANES * LANES
    t = b * s
    n_heads = f_bias.shape[0]
    att_w = n_heads * HEAD_DIM
    pool_width = pool_scale.shape[0]
    ssm_w = ssm_d.shape[0]
    ts, blk, tc, tm, tt = _tiles(s, t)
    row = lambda v: v.reshape(1, -1)

    c0, c1 = 3 * att_w, 3 * att_w + n_heads
    wf = jnp.zeros((d, LANES), F32).at[:, :n_heads].set(w_in[:, c0:c1])
    fb = jnp.zeros((1, LANES), F32).at[0, :n_heads].set(f_bias)
    seg = _block_diag(jnp.full((n_heads, HEAD_DIM, HEAD_DIM), 1.0 / HEAD_DIM, F32)).astype(BF16)
    q, k, v, c_tok, pool, u_ssm = _inproj(
        x, row(norm1_g), w_in[:, :c0].astype(BF16), wf,
        w_in[:, c1:c1 + pool_width].astype(BF16), w_in[:, c1 + pool_width:].astype(BF16), fb,
        row(jnp.tile(q_gain, n_heads) * HEAD_DIM ** -0.5), row(jnp.tile(k_gain, n_heads)),
        seg, _block_diag(pool_w).astype(BF16), row(pool_scale), ts=ts)
    att = _attention(q, k, v, c_tok, jnp.swapaxes(c_tok, 1, 2), blk=blk)
    are, aim, bblk, cblk = _s5_params(a_re, a_im, log_dt, b_re, b_im, c_re, c_im)
    ssm = _s5(u_ssm, are, aim, bblk, cblk, row(ssm_d), glu_w.astype(BF16), row(glu_b), tc=tc)
    x1 = _merge(x, att, pool, ssm, row(out_norm_g), w_out.astype(BF16), ts=ts)

    n_experts = peer_u.shape[0]
    kh = peer_keys.astype(BF16)
    h2, off, sh, gates = _route(x1.reshape(t * SUBLANES, LANES), row(norm2_g),
                                peer_wq.astype(BF16), kh, tm=tm, n_experts=n_experts)
    tiles = lambda a: a.reshape(t, SUBLANES, LANES)
    coef = _peer_u(off, sh, tiles(h2), gates, _pack_table(peer_u), tt=tt)
    x2 = _peer_v(off, sh, coef, tiles(x1), _pack_table(peer_v), tt=tt)
    return x2.reshape(b, s * SUBLANES, LANES)


def kernel(x, norm1_g, w_in, f_bias, q_gain, k_gain, pool_w, pool_scale, ssm_a_re, ssm_a_im,
           ssm_log_dt, ssm_b_re, ssm_b_im, ssm_c_re, ssm_c_im, ssm_d, glu_w, glu_b, out_norm_g,
           w_out, norm2_g, peer_wq, peer_keys, peer_u, peer_v):
    per_layer = (norm1_g, w_in, f_bias, q_gain, k_gain, pool_w, pool_scale, ssm_a_re, ssm_a_im,
                 ssm_log_dt, ssm_b_re, ssm_b_im, ssm_c_re, ssm_c_im, ssm_d, glu_w, glu_b,
                 out_norm_g, w_out, norm2_g, peer_wq, peer_keys, peer_u, peer_v)
    b, s, d = x.shape
    assert d == SUBLANES * LANES
    x = x.reshape(b, s * SUBLANES, LANES)
    for l in range(norm1_g.shape[0]):
        x = _layer(x, *[p[l] for p in per_layer])
    return x.reshape(b, s, d)
```

```python
import functools
import math

import jax
import jax.numpy as jnp
from jax import lax
from jax.experimental import pallas as pl
from jax.experimental.pallas import tpu as pltpu

F32 = jnp.float32
BF16 = jnp.bfloat16
U32 = jnp.uint32
I32 = jnp.int32

EPS = 1e-6
HEAD_DIM = 64
POOL_WINDOWS = (2, 4, 8, 16)
POOL_HALO = 16
PEER_KEYS = 128
PEER_TOPK = 16
U_EDGES_PER_DOT = 32

LANES = 128
SUBLANES = 8
VMEM_LIMIT_BYTES = 56 * 1024 * 1024

_NT = (((1,), (1,)), ((), ()))


def _rmsnorm(x, g):
    return x * lax.rsqrt(jnp.mean(x * x, axis=-1, keepdims=True) + EPS) * g


def _gelu_tanh(x):
    c = math.sqrt(2.0 / math.pi)
    return x * (0.5 * (1.0 + jnp.tanh(c * (x + 0.044715 * (x * x * x)))))


def _params(sem, vmem=None):
    return pltpu.CompilerParams(dimension_semantics=sem, vmem_limit_bytes=vmem)


def _load_tiled(ref, n):
    return jnp.concatenate([ref[pl.ds(s, n, stride=SUBLANES), :] for s in range(SUBLANES)], axis=1)


def _store_tiled(ref, val, n):
    for s in range(SUBLANES):
        ref[pl.ds(s, n, stride=SUBLANES), :] = val[:, s * LANES:(s + 1) * LANES]


def _inproj_kernel(x_ref, g_ref, wqkv_ref, wf_ref, wpool_ref, wssm_ref, fb_ref,
                   qg_ref, kg_ref, seg_ref, pmix_ref, pscale_ref,
                   q_out, k_out, v_out, c_out, pool_out, ssm_out, carry_ref, halo_ref,
                   *, att_w, n_heads):
    i = pl.program_id(1)

    @pl.when(i == 0)
    def _():
        carry_ref[...] = jnp.zeros_like(carry_ref)
        halo_ref[...] = jnp.zeros_like(halo_ref)

    ts = x_ref.shape[0] // SUBLANES
    x = _load_tiled(x_ref, ts)
    h = _rmsnorm(x, g_ref[...])
    hb = h.astype(BF16)
    qkv = jnp.dot(hb, wqkv_ref[...], preferred_element_type=F32)
    seg = seg_ref[...]

    def headnorm(t, gain):
        ms = jnp.dot((t * t).astype(BF16), seg, preferred_element_type=F32)
        return t * lax.rsqrt(ms + EPS) * gain

    q_out[...] = headnorm(qkv[:, :att_w], qg_ref[...]).astype(BF16)
    k_out[...] = headnorm(qkv[:, att_w:2 * att_w], kg_ref[...]).astype(BF16)
    v_out[...] = qkv[:, 2 * att_w:].astype(BF16)
    u_pool = jnp.dot(hb, wpool_ref[...], preferred_element_type=F32)
    pool_out[...] = _pool_mix(u_pool, halo_ref[...], i * ts, pmix_ref[...], pscale_ref[...])
    halo_ref[...] = u_pool[ts - POOL_HALO:, :]
    ssm_out[...] = jnp.dot(hb, wssm_ref[...], preferred_element_type=F32)

    f = jnp.dot(h, wf_ref[...], precision=lax.Precision.HIGHEST,
                preferred_element_type=F32) + fb_ref[...]
    ls = jnp.minimum(f, 0.0) - jnp.log1p(jnp.exp(-jnp.abs(f)))
    r = lax.broadcasted_iota(I32, (ts, ts), 0)
    c = lax.broadcasted_iota(I32, (ts, ts), 1)
    tri = jnp.where(c <= r, 1.0, 0.0).astype(F32)
    cs = jnp.dot(tri, ls, precision=lax.Precision.HIGHEST,
                 preferred_element_type=F32) + carry_ref[0:1, :]
    carry_ref[...] = jnp.broadcast_to(cs[ts - 1:ts, :], carry_ref.shape)
    c_out[...] = cs[:, :n_heads]


def _tiled_spec(ts):
    return pl.BlockSpec((None, ts * SUBLANES, LANES), lambda bi, i: (bi, i, 0))


def _inproj(x, g, wqkv, wf, wpool, wssm, fb, qg, kg, seg, pmix, pscale, *, ts):
    b, s = x.shape[0], x.shape[1] // SUBLANES
    att_w = seg.shape[0]
    n_heads = att_w // HEAD_DIM
    pw = wpool.shape[1]
    sw = wssm.shape[1]
    full = lambda a: pl.BlockSpec(a.shape, lambda bi, i: (0,) * a.ndim)
    tok = lambda w: pl.BlockSpec((None, ts, w), lambda bi, i: (bi, i, 0))
    return pl.pallas_call(
        functools.partial(_inproj_kernel, att_w=att_w, n_heads=n_heads),
        grid=(b, s // ts),
        in_specs=[_tiled_spec(ts), full(g), full(wqkv), full(wf), full(wpool), full(wssm),
                  full(fb), full(qg), full(kg), full(seg), full(pmix), full(pscale)],
        out_specs=[tok(att_w), tok(att_w), tok(att_w), tok(n_heads), tok(pw), tok(sw)],
        out_shape=[jax.ShapeDtypeStruct((b, s, att_w), BF16),
                   jax.ShapeDtypeStruct((b, s, att_w), BF16),
                   jax.ShapeDtypeStruct((b, s, att_w), BF16),
                   jax.ShapeDtypeStruct((b, s, n_heads), F32),
                   jax.ShapeDtypeStruct((b, s, pw), F32),
                   jax.ShapeDtypeStruct((b, s, sw), F32)],
        scratch_shapes=[pltpu.VMEM((SUBLANES, LANES), F32), pltpu.VMEM((POOL_HALO, pw), F32)],
        compiler_params=_params(("arbitrary", "arbitrary"), VMEM_LIMIT_BYTES),
        name="inproj",
    )(x, g, wqkv, wf, wpool, wssm, fb, qg, kg, seg, pmix, pscale)


def _attn_kernel(q_ref, k_ref, v_ref, ct_ref, cr_ref, o_ref, m_sc, l_sc, acc_sc, s_sc, *, blk):
    p = pl.program_id(1)
    i = pl.program_id(2)
    q = q_ref[...]
    lane = lax.broadcasted_iota(I32, (blk, LANES), 1)
    ct = ct_ref[...]
    hl = lax.broadcasted_iota(I32, ct.shape, 1)
    heads = (2 * p, 2 * p + 1)
    qm = (jnp.where(lane < HEAD_DIM, q, jnp.zeros_like(q)),
          jnp.where(lane >= HEAD_DIM, q, jnp.zeros_like(q)))
    cq = tuple(jnp.sum(jnp.where(hl == hd, ct, 0.0), axis=1, keepdims=True) for hd in heads)
    m_sc[...] = jnp.full(m_sc.shape, -jnp.inf, F32)
    l_sc[...] = jnp.zeros(l_sc.shape, F32)
    acc_sc[...] = jnp.zeros(acc_sc.shape, F32)

    def qk(kb):
        kblk = k_ref[pl.ds(pl.multiple_of(kb * blk, blk), blk), :]
        return [lax.dot_general(qm[hh], kblk, _NT, preferred_element_type=F32) for hh in range(2)]

    def step(kb, s_cur, causal):
        start = pl.multiple_of(kb * blk, blk)
        vblk = v_ref[pl.ds(start, blk), :]
        for hh in range(2):
            ck = cr_ref[pl.ds(heads[hh], 1), pl.ds(start, blk)]
            z = (s_cur[hh] + cq[hh]) - ck
            if causal:
                row = lax.broadcasted_iota(I32, (blk, blk), 0)
                col = lax.broadcasted_iota(I32, (blk, blk), 1)
                z = jnp.where(col <= row, z, -jnp.inf)
            m_old = m_sc[hh]
            m_new = jnp.maximum(m_old, jnp.max(z, axis=1, keepdims=True))
            alpha = jnp.exp(m_old - m_new)
            pe = jnp.exp(z - m_new)
            l_sc[hh] = alpha * l_sc[hh] + jnp.sum(pe, axis=1, keepdims=True)
            acc_sc[hh] = alpha * acc_sc[hh] + jnp.dot(pe.astype(BF16), vblk,
                                                      preferred_element_type=F32)
            m_sc[hh] = m_new

    s0 = qk(0)
    s_sc[0] = s0[0]
    s_sc[1] = s0[1]

    def body(kb, carry):
        s_cur = [s_sc[0], s_sc[1]]
        s_next = qk(kb + 1)
        step(kb, s_cur, False)
        s_sc[0] = s_next[0]
        s_sc[1] = s_next[1]
        return carry

    lax.fori_loop(0, i, body, 0)
    step(i, [s_sc[0], s_sc[1]], True)
    o_ref[...] = jnp.where(lane < HEAD_DIM, acc_sc[0] / l_sc[0], acc_sc[1] / l_sc[1])


def _attention(q, k, v, c_tok, c_row, *, blk):
    b, s, w = q.shape
    n_heads = c_tok.shape[2]
    return pl.pallas_call(
        functools.partial(_attn_kernel, blk=blk),
        grid=(b, w // LANES, s // blk),
        in_specs=[pl.BlockSpec((None, blk, LANES), lambda bi, p, i: (bi, i, p)),
                  pl.BlockSpec((None, s, LANES), lambda bi, p, i: (bi, 0, p)),
                  pl.BlockSpec((None, s, LANES), lambda bi, p, i: (bi, 0, p)),
                  pl.BlockSpec((None, blk, n_heads), lambda bi, p, i: (bi, i, 0)),
                  pl.BlockSpec((None, n_heads, s), lambda bi, p, i: (bi, 0, 0))],
        out_specs=pl.BlockSpec((None, blk, LANES), lambda bi, p, i: (bi, i, p)),
        out_shape=jax.ShapeDtypeStruct((b, s, w), F32),
        scratch_shapes=[pltpu.VMEM((2, blk, 1), F32), pltpu.VMEM((2, blk, 1), F32),
                        pltpu.VMEM((2, blk, LANES), F32), pltpu.VMEM((2, blk, blk), F32)],
        compiler_params=_params(("arbitrary", "arbitrary", "arbitrary"), VMEM_LIMIT_BYTES),
        name="fox_attention",
    )(q, k, v, c_tok, c_row)


def _pool_mix(cur, prev, first_pos, w, scale):
    ts, pw = cur.shape
    group_dim = pw // len(POOL_WINDOWS)
    ext = jnp.concatenate([prev, cur], axis=0)
    pos = first_pos + lax.broadcasted_iota(I32, (ts, 1), 0)
    grp = lax.broadcasted_iota(I32, (ts, pw), 1) // group_dim
    acc = cur
    pooled = jnp.zeros_like(cur)
    for j in range(1, max(POOL_WINDOWS)):
        acc = acc + ext[POOL_HALO - j:POOL_HALO - j + ts, :]
        win = j + 1
        if win in POOL_WINDOWS:
            cnt = jnp.minimum(pos + 1, win).astype(F32)
            pooled = jnp.where(grp == POOL_WINDOWS.index(win), acc / cnt, pooled)
    pooled = pooled - cur
    return jnp.dot(pooled.astype(BF16), w, preferred_element_type=F32) * scale


def _s5_kernel(u_ref, are_ref, aim_ref, bblk_ref, cblk_ref, d_ref, gw_ref, gb_ref,
               o_ref, st_ref, hre_ref, him_ref, tm_ref, *, nb, n):
    @pl.when(pl.program_id(0) == 0)
    def _():
        hre_ref[...] = jnp.zeros_like(hre_ref)
        him_ref[...] = jnp.zeros_like(him_ref)

    tc = u_ref.shape[1]
    nslab = tm_ref.shape[0]
    for bi in range(nb):
        for c in range(nslab):
            tm_ref[c, pl.ds(bi, tc, stride=nb), :] = u_ref[bi, :, c * LANES:(c + 1) * LANES]
    u = jnp.concatenate([tm_ref[c] for c in range(nslab)], axis=1)
    st_ref[...] = jnp.dot(u.astype(BF16), bblk_ref[...], preferred_element_type=F32)
    ar = jnp.broadcast_to(are_ref[...], (nb, n))
    ai = jnp.broadcast_to(aim_ref[...], (nb, n))

    def step(t, carry):
        hr, hi = carry
        r0 = pl.multiple_of(t * nb, nb)
        br = st_ref[pl.ds(r0, nb), 0:n]
        bi = st_ref[pl.ds(r0, nb), n:2 * n]
        nr = (ar * hr - ai * hi) + br
        ni = (ar * hi + ai * hr) + bi
        st_ref[pl.ds(r0, nb), 0:n] = nr
        st_ref[pl.ds(r0, nb), n:2 * n] = ni
        return nr, ni

    hr, hi = lax.fori_loop(0, tc, step, (hre_ref[...], him_ref[...]))
    hre_ref[...] = hr
    him_ref[...] = hi
    y = jnp.dot(st_ref[...].astype(BF16), cblk_ref[...], preferred_element_type=F32)
    z = _gelu_tanh(y + d_ref[...] * u)
    gate = jax.nn.sigmoid(jnp.dot(z.astype(BF16), gw_ref[...], preferred_element_type=F32)
                          + gb_ref[...])
    out = z * gate
    for c in range(nslab):
        tm_ref[c] = out[:, c * LANES:(c + 1) * LANES]
    for bi in range(nb):
        for c in range(nslab):
            o_ref[bi, :, c * LANES:(c + 1) * LANES] = tm_ref[c, pl.ds(bi, tc, stride=nb), :]


def _s5(u, a_re, a_im, bblk, cblk, dskip, gw, gb, *, tc):
    nb, s, sw = u.shape
    n = a_re.shape[1]
    rb = tc * nb
    full = lambda a: pl.BlockSpec(a.shape, lambda i: (0,) * a.ndim)
    blk = pl.BlockSpec((nb, tc, sw), lambda i: (0, i, 0))
    return pl.pallas_call(
        functools.partial(_s5_kernel, nb=nb, n=n),
        grid=(s // tc,),
        in_specs=[blk, full(a_re), full(a_im),
                  full(bblk), full(cblk), full(dskip), full(gw), full(gb)],
        out_specs=blk,
        out_shape=jax.ShapeDtypeStruct(u.shape, F32),
        scratch_shapes=[pltpu.VMEM((rb, 2 * n), F32), pltpu.VMEM((nb, n), F32),
                        pltpu.VMEM((nb, n), F32), pltpu.VMEM((sw // LANES, rb, LANES), F32)],
        compiler_params=_params(("arbitrary",), VMEM_LIMIT_BYTES),
        name="s5_scan",
    )(u, a_re, a_im, bblk, cblk, dskip, gw, gb)


def _merge_kernel(x_ref, att_ref, pool_ref, ssm_ref, og_ref, w_ref, o_ref, *, att_w, pool_w):
    og = og_ref[...]
    p1 = att_w + pool_w
    merged = jnp.concatenate([
        _rmsnorm(att_ref[...], og[:, :att_w]),
        _rmsnorm(pool_ref[...], og[:, att_w:p1]),
        _rmsnorm(ssm_ref[...], og[:, p1:])], axis=1).astype(BF16)
    ts = merged.shape[0]
    out = _load_tiled(x_ref, ts) + jnp.dot(merged, w_ref[...], preferred_element_type=F32)
    _store_tiled(o_ref, out, ts)


def _merge(x, att, pool, ssm, og, w, *, ts):
    b, s = x.shape[0], x.shape[1] // SUBLANES
    att_w, pool_w, sw = att.shape[2], pool.shape[2], ssm.shape[2]
    tok = lambda wd: pl.BlockSpec((None, ts, wd), lambda bi, i: (bi, i, 0))
    return pl.pallas_call(
        functools.partial(_merge_kernel, att_w=att_w, pool_w=pool_w),
        grid=(b, s // ts),
        in_specs=[_tiled_spec(ts), tok(att_w), tok(pool_w), tok(sw),
                  pl.BlockSpec(og.shape, lambda bi, i: (0, 0)),
                  pl.BlockSpec(w.shape, lambda bi, i: (0, 0))],
        out_specs=_tiled_spec(ts),
        out_shape=jax.ShapeDtypeStruct(x.shape, F32),
        compiler_params=_params(("arbitrary", "arbitrary"), VMEM_LIMIT_BYTES),
        name="merge_outproj",
    )(x, att, pool, ssm, og, w)


def _topk_rows(a, k, extra=None):
    nrow, nl = a.shape
    row = lax.broadcasted_iota(I32, (nrow, nl), 0).astype(F32)
    out_row = lax.broadcasted_iota(I32, (k, nl), 0)
    vals = jnp.zeros((k, nl), F32)
    sel = jnp.zeros((k, nl), F32)
    for j in range(k):
        m = jnp.max(a, axis=0, keepdims=True)
        ix = jnp.min(jnp.where(a == m, row, float(nrow)), axis=0, keepdims=True)
        hit = row == ix
        if extra is None:
            pick = ix
        else:
            pick = jnp.sum(jnp.where(hit, extra, 0.0), axis=0, keepdims=True)
        vals = jnp.where(out_row == j, m, vals)
        sel = jnp.where(out_row == j, pick, sel)
        a = jnp.where(hit, -jnp.inf, a)
    return vals, sel


def _route_kernel(x_ref, g_ref, wq_ref, keys_ref, h_out, off_out, sh_out, gate_out, q_sc, stage_sc,
                  *, n_half):
    hd = pl.program_id(1)
    qd = 2 * PEER_KEYS

    @pl.when(hd == 0)
    def _():
        tm = q_sc.shape[0]
        h = _rmsnorm(_load_tiled(x_ref, tm), g_ref[...])
        _store_tiled(h_out, h, tm)
        q_sc[...] = jnp.dot(h.astype(BF16), wq_ref[...], preferred_element_type=F32)

    q = q_sc[:, pl.ds(pl.multiple_of(hd * qd, qd), qd)].astype(BF16)
    tops = []
    for i in range(2):
        sc = lax.dot_general(keys_ref[i], q[:, i * PEER_KEYS:(i + 1) * PEER_KEYS], _NT,
                             preferred_element_type=F32)
        tops.append(_topk_rows(sc, PEER_TOPK))
    (s1, i1), (s2, i2) = tops
    k = PEER_TOPK
    sub8 = lax.broadcasted_iota(I32, (SUBLANES, s1.shape[1]), 0)
    cands, ecands = [], []
    a = 0
    while a < k and k // (a + 1) > 1:
        nb = k // (a + 1)
        for b0 in range(0, nb, SUBLANES):
            c = s1[a:a + 1, :] + s2[b0:b0 + SUBLANES, :]
            if b0 + SUBLANES > nb:
                c = jnp.where(sub8 < nb - b0, c, -jnp.inf)
            cands.append(c)
            ecands.append(i1[a:a + 1, :] * float(PEER_KEYS) + i2[b0:b0 + SUBLANES, :])
        a += 1
    cands.append(s1[a:, :] + s2[0:1, :])
    ecands.append(i1[a:, :] * float(PEER_KEYS) + i2[0:1, :])
    best, experts = _topk_rows(jnp.concatenate(cands, axis=0), k,
                               extra=jnp.concatenate(ecands, axis=0))
    w = jnp.exp(best - best[0:1, :])
    rows = pl.ds(pl.multiple_of(hd * k, k), k)
    stage_sc[2, rows, :] = w / jnp.sum(w, axis=0, keepdims=True)
    hi_half = experts < float(n_half)
    stage_sc[0, rows, :] = jnp.where(hi_half, experts, experts - float(n_half)) * float(SUBLANES)
    stage_sc[1, rows, :] = jnp.where(hi_half, 0.0, 16.0)

    @pl.when(hd == pl.num_programs(1) - 1)
    def _():
        off_out[...] = stage_sc[0].T.astype(I32)
        sh_out[...] = stage_sc[1].T
        gate_out[...] = stage_sc[2].T


def _route(x, g, wq, keys, *, tm, n_experts):
    t = x.shape[0] // SUBLANES
    n_heads = keys.shape[0]
    n_sel = n_heads * PEER_TOPK
    hk = pl.BlockSpec((tm, n_sel), lambda i, h: (i, 0))
    shp = lambda dt: jax.ShapeDtypeStruct((t, n_sel), dt)
    tiled = pl.BlockSpec((tm * SUBLANES, LANES), lambda i, h: (i, 0))
    return pl.pallas_call(
        functools.partial(_route_kernel, n_half=n_experts // 2),
        grid=(t // tm, n_heads),
        in_specs=[tiled,
                  pl.BlockSpec(g.shape, lambda i, h: (0, 0)),
                  pl.BlockSpec(wq.shape, lambda i, h: (0, 0)),
                  pl.BlockSpec((None, 2, PEER_KEYS, keys.shape[3]), lambda i, h: (h, 0, 0, 0))],
        out_specs=[tiled, hk, hk, hk],
        out_shape=[jax.ShapeDtypeStruct(x.shape, F32), shp(I32), shp(F32), shp(F32)],
        scratch_shapes=[pltpu.VMEM((tm, wq.shape[1]), F32), pltpu.VMEM((3, n_sel, tm), F32)],
        compiler_params=_params(("arbitrary", "arbitrary"), VMEM_LIMIT_BYTES),
        name="peer_route",
    )(x, g, wq, keys)


def _expert_row(tab_ref, off, sh):
    w = tab_ref[pl.ds(pl.multiple_of(off, SUBLANES), SUBLANES), :]
    return lax.bitcast_convert_type((w << sh) & jnp.uint32(0xFFFF0000), F32)


def _splat_rows(mxu_src, mxu_dst, xlu_src=None, xlu_dst=None, *, tt, hk, also=None):
    eye = (lax.broadcasted_iota(I32, (hk, hk), 0) == lax.broadcasted_iota(I32, (hk, hk), 1))
    ones = jnp.ones((hk, hk), BF16)

    def body(t, carry):
        v = jnp.broadcast_to(mxu_src[pl.ds(t, 1), :], (hk, hk))
        rep = jnp.dot(jnp.where(eye, v, 0.0).astype(BF16), ones, preferred_element_type=F32)
        mxu_dst(t, rep.astype(I32))
        if xlu_src is not None:
            xlu_dst(t, jnp.broadcast_to(xlu_src[pl.ds(t, 1), :], (hk, hk)).T)
        if also is not None:
            also(t)
        return carry

    lax.fori_loop(0, tt, body, 0, unroll=8)


def _splat(rep_ref, row):
    return jnp.broadcast_to(rep_ref[pl.ds(row, 1), :], (SUBLANES, LANES))


def _peer_u_kernel(*refs, tt, hk):
    n_off = tt * hk // SUBLANES
    off_refs = [refs[0].at[pl.ds(j * n_off, n_off)] for j in range(SUBLANES)]
    sh_ref, shn_ref, h_ref, gate_ref, tab_ref, coef_ref, part_ref, srep_ref, pre_ref = refs[1:]
    ngrp = hk // SUBLANES

    def store_srep(t, block):
        srep_ref[pl.ds(pl.multiple_of(t * hk, hk), hk), :] = block

    @pl.when(pl.program_id(0) == 0)
    def _():
        _splat_rows(sh_ref, store_srep, tt=tt, hk=hk)

    chunk = U_EDGES_PER_DOT
    sel = (lax.broadcasted_iota(I32, (chunk, chunk * SUBLANES), 1) // SUBLANES
           == lax.broadcasted_iota(I32, (chunk, chunk * SUBLANES), 0)).astype(BF16)

    def tok(t, carry):
        h = h_ref[t]
        base = pl.multiple_of(t * hk, hk)
        for c in range(hk // chunk):
            e0 = base + c * chunk
            ps = []
            for k0 in range(0, chunk, SUBLANES):
                sblk = srep_ref[pl.ds(e0 + k0, SUBLANES), :].astype(U32)
                for j in range(SUBLANES):
                    k = k0 + j
                    sv = jnp.broadcast_to(sblk[j:j + 1, :], (SUBLANES, LANES))
                    ps.append(_expert_row(
                        tab_ref, off_refs[j][t * ngrp + (c * chunk + k) // SUBLANES], sv) * h)
            stacked = jnp.concatenate(ps, axis=0).astype(BF16)
            part_ref[pl.ds(e0, chunk), :] = jnp.dot(sel, stacked, preferred_element_type=F32)
        return carry

    lax.fori_loop(0, tt, tok, 0, unroll=8)

    def lane_sums(t):
        p = part_ref[pl.ds(pl.multiple_of(t * hk, hk), hk), :]
        pre_ref[pl.ds(t, 1), :] = jnp.sum(p.T, axis=0, keepdims=True)

    _splat_rows(shn_ref, store_srep, tt=tt, hk=hk, also=lane_sums)
    coef_ref[...] = gate_ref[...] * _gelu_tanh(pre_ref[...])


def _peer_v_kernel(*refs, tt, hk, n_acc):
    n_off = tt * hk // SUBLANES
    off_refs = [refs[0].at[pl.ds(j * n_off, n_off)] for j in range(SUBLANES)]
    sh_ref, coef_ref, x_ref, tab_ref, o_ref, rep_ref = refs[1:]
    ngrp = hk // SUBLANES

    def store_coef(t, block):
        rep_ref[pl.ds(2 * t * hk, hk, stride=2), :] = block

    def store_shift(t, block):
        rep_ref[pl.ds(2 * t * hk + 1, hk, stride=2), :] = lax.bitcast_convert_type(block, F32)

    _splat_rows(sh_ref, store_shift, coef_ref, store_coef, tt=tt, hk=hk)

    def tok(t, carry):
        accs = [None] * n_acc
        e0 = 2 * t * hk
        per_load = SUBLANES // 2
        for k0 in range(0, hk, per_load):
            blk = rep_ref[pl.ds(pl.multiple_of(e0 + 2 * k0, SUBLANES), SUBLANES), :]
            for j in range(per_load):
                k = k0 + j
                bcast = lambda r: jnp.broadcast_to(blk[r:r + 1, :], (SUBLANES, LANES))
                off = off_refs[k % SUBLANES][t * ngrp + k // SUBLANES]
                sv = lax.bitcast_convert_type(bcast(2 * j + 1), U32)
                term = _splat(rep_ref, e0 + 2 * k) * _expert_row(tab_ref, off, sv)
                accs[k % n_acc] = term if accs[k % n_acc] is None else accs[k % n_acc] + term
        while len(accs) > 1:
            accs = [accs[j] + accs[j + len(accs) // 2] for j in range(len(accs) // 2)]
        o_ref[t] = x_ref[t] + accs[0]
        return carry

    lax.fori_loop(0, tt, tok, 0, unroll=4)


def _resident(a):
    return pl.BlockSpec(a.shape, lambda i: (0,) * a.ndim, pipeline_mode=pl.Buffered(1))


def _split_offsets(off, tt):
    t, hk = off.shape
    grouped = off.reshape(t // tt, tt, hk // SUBLANES, SUBLANES)
    return jnp.transpose(grouped, (0, 3, 1, 2)).reshape(-1)


def _peer_u(off, sh, h, gates, tab, *, tt):
    t, hk = off.shape
    assert hk == LANES and h.shape[1:] == (SUBLANES, LANES)
    smem = pl.BlockSpec((tt * hk,), lambda i: (i,), memory_space=pltpu.SMEM)
    compact = pl.BlockSpec((tt, hk), lambda i: (i, 0))
    last = t // tt - 1
    nxt = pl.BlockSpec((tt, hk), lambda i: (jnp.minimum(i + 1, last), 0))
    return pl.pallas_call(
        functools.partial(_peer_u_kernel, tt=tt, hk=hk),
        grid=(t // tt,),
        in_specs=[
            smem, compact, nxt, pl.BlockSpec((tt, SUBLANES, LANES), lambda i: (i, 0, 0)), compact,
            _resident(tab)],
        out_specs=compact,
        out_shape=jax.ShapeDtypeStruct((t, hk), F32),
        scratch_shapes=[pltpu.VMEM((tt * hk, LANES), F32), pltpu.VMEM((tt * hk, LANES), I32),
                        pltpu.VMEM((tt, hk), F32)],
        compiler_params=_params(("arbitrary",), VMEM_LIMIT_BYTES),
        name="peer_u",
    )(_split_offsets(off, tt), sh, sh, h, gates, tab)


def _peer_v(off, sh, coef, x, tab, *, tt):
    t, hk = off.shape
    assert hk == LANES and x.shape[1:] == (SUBLANES, LANES)
    smem = pl.BlockSpec((tt * hk,), lambda i: (i,), memory_space=pltpu.SMEM)
    tile = pl.BlockSpec((tt, SUBLANES, LANES), lambda i: (i, 0, 0))
    compact = pl.BlockSpec((tt, hk), lambda i: (i, 0))
    return pl.pallas_call(
        functools.partial(_peer_v_kernel, tt=tt, hk=hk, n_acc=4),
        grid=(t // tt,),
        in_specs=[smem, compact, compact, tile, _resident(tab)],
        out_specs=tile,
        out_shape=jax.ShapeDtypeStruct(x.shape, F32),
        scratch_shapes=[pltpu.VMEM((2 * tt * hk, LANES), F32)],
        compiler_params=_params(("arbitrary",), VMEM_LIMIT_BYTES),
        name="peer_v",
    )(_split_offsets(off, tt), sh, coef, x, tab)


def _pack_kernel(hi_ref, lo_ref, o_ref):
    def bf16_bits(x):
        bits = lax.bitcast_convert_type(x, U32)
        r = bits + (jnp.uint32(0x7FFF) + ((bits >> 16) & jnp.uint32(1)))
        return jnp.where(x != x, jnp.uint32(0x7FC00000), r)

    w = (bf16_bits(hi_ref[...]) & jnp.uint32(0xFFFF0000)) | (bf16_bits(lo_ref[...]) >> 16)
    _store_tiled(o_ref, w, w.shape[0])


def _pack_table(tab, rows=256):
    n, d = tab.shape
    assert d == SUBLANES * LANES and (n // 2) % rows == 0
    nblk = n // 2 // rows
    return pl.pallas_call(
        _pack_kernel,
        grid=(nblk,),
        in_specs=[pl.BlockSpec((rows, d), lambda i: (i, 0)),
                  pl.BlockSpec((rows, d), lambda i: (i + nblk, 0))],
        out_specs=pl.BlockSpec((rows * SUBLANES, LANES), lambda i: (i, 0)),
        out_shape=jax.ShapeDtypeStruct((n // 2 * SUBLANES, LANES), U32),
        compiler_params=_params(("arbitrary",)),
        name="pack_table",
    )(tab, tab)


def _block_diag(blocks):
    g, r, c = blocks.shape
    eye = jnp.eye(g, dtype=blocks.dtype)
    return (eye[:, None, :, None] * blocks[:, :, None, :]).reshape(g * r, g * c)


def _s5_params(a_re, a_im, log_dt, b_re, b_im, c_re, c_im):
    a = lax.complex(a_re, a_im)
    dt = jnp.exp(log_dt)[:, None]
    a_bar = jnp.exp(a * dt)
    b_bar = ((a_bar - 1.0) / a)[..., None] * lax.complex(b_re, b_im)
    bt = jnp.swapaxes(b_bar, 1, 2)
    bblk = jnp.concatenate([_block_diag(jnp.real(bt)), _block_diag(jnp.imag(bt))], axis=1)
    ct_re = jnp.swapaxes(c_re, 1, 2)
    ct_im = jnp.swapaxes(c_im, 1, 2)
    cblk = jnp.concatenate([_block_diag(ct_re), -_block_diag(ct_im)], axis=0)
    n = a_bar.size
    return (jnp.real(a_bar).reshape(1, n), jnp.imag(a_bar).reshape(1, n),
            bblk.astype(BF16), cblk.astype(BF16))


def _tiles(s, t):
    ts = min(512, s)
    blk = min(512, s)
    tc = min(64, s)
    tm = min(512, t)
    tt = min(128, t)
    return ts, blk, tc, tm, tt


def _layer(x, norm1_g, w_in, f_bias, q_gain, k_gain, pool_w, pool_scale,
           a_re, a_im, log_dt, b_re, b_im, c_re, c_im, ssm_d, glu_w, glu_b,
           out_norm_g, w_out, norm2_g, peer_wq, peer_keys, peer_u, peer_v):
    b, s, d = x.shape[0], x.shape[1] // SUBLANES, SUBLANES * LANES
    t = b * s
    n_heads = f_bias.shape[0]
    att_w = n_heads * HEAD_DIM
    pool_width = pool_scale.shape[0]
    ssm_w = ssm_d.shape[0]
    ts, blk, tc, tm, tt = _tiles(s, t)
    row = lambda v: v.reshape(1, -1)

    c0, c1 = 3 * att_w, 3 * att_w + n_heads
    wf = jnp.zeros((d, LANES), F32).at[:, :n_heads].set(w_in[:, c0:c1])
    fb = jnp.zeros((1, LANES), F32).at[0, :n_heads].set(f_bias)
    seg = _block_diag(jnp.full((n_heads, HEAD_DIM, HEAD_DIM), 1.0 / HEAD_DIM, F32)).astype(BF16)
    q, k, v, c_tok, pool, u_ssm = _inproj(
        x, row(norm1_g), w_in[:, :c0].astype(BF16), wf,
        w_in[:, c1:c1 + pool_width].astype(BF16), w_in[:, c1 + pool_width:].astype(BF16), fb,
        row(jnp.tile(q_gain, n_heads) * HEAD_DIM ** -0.5), row(jnp.tile(k_gain, n_heads)),
        seg, _block_diag(pool_w).astype(BF16), row(pool_scale), ts=ts)
    att = _attention(q, k, v, c_tok, jnp.swapaxes(c_tok, 1, 2), blk=blk)
    are, aim, bblk, cblk = _s5_params(a_re, a_im, log_dt, b_re, b_im, c_re, c_im)
    ssm = _s5(u_ssm, are, aim, bblk, cblk, row(ssm_d), glu_w.astype(BF16), row(glu_b), tc=tc)
    x1 = _merge(x, att, pool, ssm, row(out_norm_g), w_out.astype(BF16), ts=ts)

    n_experts = peer_u.shape[0]
    kh = peer_keys.astype(BF16)
    h2, off, sh, gates = _route(x1.reshape(t * SUBLANES, LANES), row(norm2_g),
                                peer_wq.astype(BF16), kh, tm=tm, n_experts=n_experts)
    tiles = lambda a: a.reshape(t, SUBLANES, LANES)
    coef = _peer_u(off, sh, tiles(h2), gates, _pack_table(peer_u), tt=tt)
    x2 = _peer_v(off, sh, coef, tiles(x1), _pack_table(peer_v), tt=tt)
    return x2.reshape(b, s * SUBLANES, LANES)


def kernel(x, norm1_g, w_in, f_bias, q_gain, k_gain, pool_w, pool_scale, ssm_a_re, ssm_a_im,
           ssm_log_dt, ssm_b_re, ssm_b_im, ssm_c_re, ssm_c_im, ssm_d, glu_w, glu_b, out_norm_g,
           w_out, norm2_g, peer_wq, peer_keys, peer_u, peer_v):
    per_layer = (norm1_g, w_in, f_bias, q_gain, k_gain, pool_w, pool_scale, ssm_a_re, ssm_a_im,
                 ssm_log_dt, ssm_b_re, ssm_b_im, ssm_c_re, ssm_c_im, ssm_d, glu_w, glu_b,
                 out_norm_g, w_out, norm2_g, peer_wq, peer_keys, peer_u, peer_v)
    b, s, d = x.shape
    assert d == SUBLANES * LANES
    x = x.reshape(b, s * SUBLANES, LANES)
    for l in range(norm1_g.shape[0]):
        x = _layer(x, *[p[l] for p in per_layer])
    return x.reshape(b, s, d)
```

```python
import functools
import math

import jax
import jax.numpy as jnp
from jax import lax
from jax.experimental import pallas as pl
from jax.experimental.pallas import tpu as pltpu

F32 = jnp.float32
BF16 = jnp.bfloat16
U32 = jnp.uint32
I32 = jnp.int32

EPS = 1e-6
HEAD_DIM = 64
POOL_WINDOWS = (2, 4, 8, 16)
POOL_HALO = 16
PEER_KEYS = 128
PEER_TOPK = 16
U_EDGES_PER_DOT = 16

LANES = 128
SUBLANES = 8
VMEM_LIMIT_BYTES = 56 * 1024 * 1024

_NT = (((1,), (1,)), ((), ()))


def _rmsnorm(x, g):
    return x * lax.rsqrt(jnp.mean(x * x, axis=-1, keepdims=True) + EPS) * g


def _gelu_tanh(x):
    c = math.sqrt(2.0 / math.pi)
    return x * (0.5 * (1.0 + jnp.tanh(c * (x + 0.044715 * (x * x * x)))))


def _params(sem, vmem=None):
    return pltpu.CompilerParams(dimension_semantics=sem, vmem_limit_bytes=vmem)


def _load_tiled(ref, n):
    return jnp.concatenate([ref[pl.ds(s, n, stride=SUBLANES), :] for s in range(SUBLANES)], axis=1)


def _store_tiled(ref, val, n):
    for s in range(SUBLANES):
        ref[pl.ds(s, n, stride=SUBLANES), :] = val[:, s * LANES:(s + 1) * LANES]


def _inproj_kernel(x_ref, g_ref, wqkv_ref, wf_ref, wpool_ref, wssm_ref, fb_ref,
                   qg_ref, kg_ref, seg_ref, pmix_ref, pscale_ref,
                   q_out, k_out, v_out, c_out, pool_out, ssm_out, carry_ref, halo_ref,
                   *, att_w, n_heads):
    i = pl.program_id(1)

    @pl.when(i == 0)
    def _():
        carry_ref[...] = jnp.zeros_like(carry_ref)
        halo_ref[...] = jnp.zeros_like(halo_ref)

    ts = x_ref.shape[0] // SUBLANES
    x = _load_tiled(x_ref, ts)
    h = _rmsnorm(x, g_ref[...])
    hb = h.astype(BF16)
    qkv = jnp.dot(hb, wqkv_ref[...], preferred_element_type=F32)
    seg = seg_ref[...]

    def headnorm(t, gain):
        ms = jnp.dot((t * t).astype(BF16), seg, preferred_element_type=F32)
        return t * lax.rsqrt(ms + EPS) * gain

    q_out[...] = headnorm(qkv[:, :att_w], qg_ref[...]).astype(BF16)
    k_out[...] = headnorm(qkv[:, att_w:2 * att_w], kg_ref[...]).astype(BF16)
    v_out[...] = qkv[:, 2 * att_w:].astype(BF16)
    u_pool = jnp.dot(hb, wpool_ref[...], preferred_element_type=F32)
    pool_out[...] = _pool_mix(u_pool, halo_ref[...], i * ts, pmix_ref[...], pscale_ref[...])
    halo_ref[...] = u_pool[ts - POOL_HALO:, :]
    ssm_out[...] = jnp.dot(hb, wssm_ref[...], preferred_element_type=F32)

    f = jnp.dot(h, wf_ref[...], precision=lax.Precision.HIGHEST,
                preferred_element_type=F32) + fb_ref[...]
    ls = jnp.minimum(f, 0.0) - jnp.log1p(jnp.exp(-jnp.abs(f)))
    r = lax.broadcasted_iota(I32, (ts, ts), 0)
    c = lax.broadcasted_iota(I32, (ts, ts), 1)
    tri = jnp.where(c <= r, 1.0, 0.0).astype(F32)
    cs = jnp.dot(tri, ls, precision=lax.Precision.HIGHEST,
                 preferred_element_type=F32) + carry_ref[0:1, :]
    carry_ref[...] = jnp.broadcast_to(cs[ts - 1:ts, :], carry_ref.shape)
    c_out[...] = cs[:, :n_heads]


def _tiled_spec(ts):
    return pl.BlockSpec((None, ts * SUBLANES, LANES), lambda bi, i: (bi, i, 0))


def _inproj(x, g, wqkv, wf, wpool, wssm, fb, qg, kg, seg, pmix, pscale, *, ts):
    b, s = x.shape[0], x.shape[1] // SUBLANES
    att_w = seg.shape[0]
    n_heads = att_w // HEAD_DIM
    pw = wpool.shape[1]
    sw = wssm.shape[1]
    full = lambda a: pl.BlockSpec(a.shape, lambda bi, i: (0,) * a.ndim)
    tok = lambda w: pl.BlockSpec((None, ts, w), lambda bi, i: (bi, i, 0))
    return pl.pallas_call(
        functools.partial(_inproj_kernel, att_w=att_w, n_heads=n_heads),
        grid=(b, s // ts),
        in_specs=[_tiled_spec(ts), full(g), full(wqkv), full(wf), full(wpool), full(wssm),
                  full(fb), full(qg), full(kg), full(seg), full(pmix), full(pscale)],
        out_specs=[tok(att_w), tok(att_w), tok(att_w), tok(n_heads), tok(pw), tok(sw)],
        out_shape=[jax.ShapeDtypeStruct((b, s, att_w), BF16),
                   jax.ShapeDtypeStruct((b, s, att_w), BF16),
                   jax.ShapeDtypeStruct((b, s, att_w), BF16),
                   jax.ShapeDtypeStruct((b, s, n_heads), F32),
                   jax.ShapeDtypeStruct((b, s, pw), F32),
                   jax.ShapeDtypeStruct((b, s, sw), F32)],
        scratch_shapes=[pltpu.VMEM((SUBLANES, LANES), F32), pltpu.VMEM((POOL_HALO, pw), F32)],
        compiler_params=_params(("arbitrary", "arbitrary"), VMEM_LIMIT_BYTES),
        name="inproj",
    )(x, g, wqkv, wf, wpool, wssm, fb, qg, kg, seg, pmix, pscale)


def _attn_kernel(q_ref, k_ref, v_ref, ct_ref, cr_ref, o_ref, m_sc, l_sc, acc_sc, s_sc, *, blk):
    p = pl.program_id(1)
    i = pl.program_id(2)
    q = q_ref[...]
    lane = lax.broadcasted_iota(I32, (blk, LANES), 1)
    ct = ct_ref[...]
    hl = lax.broadcasted_iota(I32, ct.shape, 1)
    heads = (2 * p, 2 * p + 1)
    qm = (jnp.where(lane < HEAD_DIM, q, jnp.zeros_like(q)),
          jnp.where(lane >= HEAD_DIM, q, jnp.zeros_like(q)))
    cq = tuple(jnp.sum(jnp.where(hl == hd, ct, 0.0), axis=1, keepdims=True) for hd in heads)
    m_sc[...] = jnp.full(m_sc.shape, -jnp.inf, F32)
    l_sc[...] = jnp.zeros(l_sc.shape, F32)
    acc_sc[...] = jnp.zeros(acc_sc.shape, F32)

    def qk(kb):
        kblk = k_ref[pl.ds(pl.multiple_of(kb * blk, blk), blk), :]
        return [lax.dot_general(qm[hh], kblk, _NT, preferred_element_type=F32) for hh in range(2)]

    def step(kb, s_cur, causal):
        start = pl.multiple_of(kb * blk, blk)
        vblk = v_ref[pl.ds(start, blk), :]
        for hh in range(2):
            ck = cr_ref[pl.ds(heads[hh], 1), pl.ds(start, blk)]
            z = (s_cur[hh] + cq[hh]) - ck
            if causal:
                row = lax.broadcasted_iota(I32, (blk, blk), 0)
                col = lax.broadcasted_iota(I32, (blk, blk), 1)
                z = jnp.where(col <= row, z, -jnp.inf)
            m_old = m_sc[hh]
            m_new = jnp.maximum(m_old, jnp.max(z, axis=1, keepdims=True))
            alpha = jnp.exp(m_old - m_new)
            pe = jnp.exp(z - m_new)
            l_sc[hh] = alpha * l_sc[hh] + jnp.sum(pe, axis=1, keepdims=True)
            acc_sc[hh] = alpha * acc_sc[hh] + jnp.dot(pe.astype(BF16), vblk,
                                                      preferred_element_type=F32)
            m_sc[hh] = m_new

    s0 = qk(0)
    s_sc[0] = s0[0]
    s_sc[1] = s0[1]

    def body(kb, carry):
        s_cur = [s_sc[0], s_sc[1]]
        s_next = qk(kb + 1)
        step(kb, s_cur, False)
        s_sc[0] = s_next[0]
        s_sc[1] = s_next[1]
        return carry

    lax.fori_loop(0, i, body, 0)
    step(i, [s_sc[0], s_sc[1]], True)
    o_ref[...] = jnp.where(lane < HEAD_DIM, acc_sc[0] / l_sc[0], acc_sc[1] / l_sc[1])


def _attention(q, k, v, c_tok, c_row, *, blk):
    b, s, w = q.shape
    n_heads = c_tok.shape[2]
    return pl.pallas_call(
        functools.partial(_attn_kernel, blk=blk),
        grid=(b, w // LANES, s // blk),
        in_specs=[pl.BlockSpec((None, blk, LANES), lambda bi, p, i: (bi, i, p)),
                  pl.BlockSpec((None, s, LANES), lambda bi, p, i: (bi, 0, p)),
                  pl.BlockSpec((None, s, LANES), lambda bi, p, i: (bi, 0, p)),
                  pl.BlockSpec((None, blk, n_heads), lambda bi, p, i: (bi, i, 0)),
                  pl.BlockSpec((None, n_heads, s), lambda bi, p, i: (bi, 0, 0))],
        out_specs=pl.BlockSpec((None, blk, LANES), lambda bi, p, i: (bi, i, p)),
        out_shape=jax.ShapeDtypeStruct((b, s, w), F32),
        scratch_shapes=[pltpu.VMEM((2, blk, 1), F32), pltpu.VMEM((2, blk, 1), F32),
                        pltpu.VMEM((2, blk, LANES), F32), pltpu.VMEM((2, blk, blk), F32)],
        compiler_params=_params(("arbitrary", "arbitrary", "arbitrary"), VMEM_LIMIT_BYTES),
        name="fox_attention",
    )(q, k, v, c_tok, c_row)


def _pool_mix(cur, prev, first_pos, w, scale):
    ts, pw = cur.shape
    group_dim = pw // len(POOL_WINDOWS)
    ext = jnp.concatenate([prev, cur], axis=0)
    pos = first_pos + lax.broadcasted_iota(I32, (ts, 1), 0)
    grp = lax.broadcasted_iota(I32, (ts, pw), 1) // group_dim
    acc = cur
    pooled = jnp.zeros_like(cur)
    for j in range(1, max(POOL_WINDOWS)):
        acc = acc + ext[POOL_HALO - j:POOL_HALO - j + ts, :]
        win = j + 1
        if win in POOL_WINDOWS:
            cnt = jnp.minimum(pos + 1, win).astype(F32)
            pooled = jnp.where(grp == POOL_WINDOWS.index(win), acc / cnt, pooled)
    pooled = pooled - cur
    return jnp.dot(pooled.astype(BF16), w, preferred_element_type=F32) * scale


def _s5_kernel(u_ref, are_ref, aim_ref, bblk_ref, cblk_ref, d_ref, gw_ref, gb_ref,
               o_ref, st_ref, hre_ref, him_ref, tm_ref, *, nb, n):
    @pl.when(pl.program_id(0) == 0)
    def _():
        hre_ref[...] = jnp.zeros_like(hre_ref)
        him_ref[...] = jnp.zeros_like(him_ref)

    tc = u_ref.shape[1]
    nslab = tm_ref.shape[0]
    for bi in range(nb):
        for c in range(nslab):
            tm_ref[c, pl.ds(bi, tc, stride=nb), :] = u_ref[bi, :, c * LANES:(c + 1) * LANES]
    u = jnp.concatenate([tm_ref[c] for c in range(nslab)], axis=1)
    st_ref[...] = jnp.dot(u.astype(BF16), bblk_ref[...], preferred_element_type=F32)
    ar = jnp.broadcast_to(are_ref[...], (nb, n))
    ai = jnp.broadcast_to(aim_ref[...], (nb, n))

    def step(t, carry):
        hr, hi = carry
        r0 = pl.multiple_of(t * nb, nb)
        br = st_ref[pl.ds(r0, nb), 0:n]
        bi = st_ref[pl.ds(r0, nb), n:2 * n]
        nr = (ar * hr - ai * hi) + br
        ni = (ar * hi + ai * hr) + bi
        st_ref[pl.ds(r0, nb), 0:n] = nr
        st_ref[pl.ds(r0, nb), n:2 * n] = ni
        return nr, ni

    hr, hi = lax.fori_loop(0, tc, step, (hre_ref[...], him_ref[...]))
    hre_ref[...] = hr
    him_ref[...] = hi
    y = jnp.dot(st_ref[...].astype(BF16), cblk_ref[...], preferred_element_type=F32)
    z = _gelu_tanh(y + d_ref[...] * u)
    gate = jax.nn.sigmoid(jnp.dot(z.astype(BF16), gw_ref[...], preferred_element_type=F32)
                          + gb_ref[...])
    out = z * gate
    for c in range(nslab):
        tm_ref[c] = out[:, c * LANES:(c + 1) * LANES]
    for bi in range(nb):
        for c in range(nslab):
            o_ref[bi, :, c * LANES:(c + 1) * LANES] = tm_ref[c, pl.ds(bi, tc, stride=nb), :]


def _s5(u, a_re, a_im, bblk, cblk, dskip, gw, gb, *, tc):
    nb, s, sw = u.shape
    n = a_re.shape[1]
    rb = tc * nb
    full = lambda a: pl.BlockSpec(a.shape, lambda i: (0,) * a.ndim)
    blk = pl.BlockSpec((nb, tc, sw), lambda i: (0, i, 0))
    return pl.pallas_call(
        functools.partial(_s5_kernel, nb=nb, n=n),
        grid=(s // tc,),
        in_specs=[blk, full(a_re), full(a_im),
                  full(bblk), full(cblk), full(dskip), full(gw), full(gb)],
        out_specs=blk,
        out_shape=jax.ShapeDtypeStruct(u.shape, F32),
        scratch_shapes=[pltpu.VMEM((rb, 2 * n), F32), pltpu.VMEM((nb, n), F32),
                        pltpu.VMEM((nb, n), F32), pltpu.VMEM((sw // LANES, rb, LANES), F32)],
        compiler_params=_params(("arbitrary",), VMEM_LIMIT_BYTES),
        name="s5_scan",
    )(u, a_re, a_im, bblk, cblk, dskip, gw, gb)


def _merge_kernel(x_ref, att_ref, pool_ref, ssm_ref, og_ref, w_ref, o_ref, *, att_w, pool_w):
    og = og_ref[...]
    p1 = att_w + pool_w
    merged = jnp.concatenate([
        _rmsnorm(att_ref[...], og[:, :att_w]),
        _rmsnorm(pool_ref[...], og[:, att_w:p1]),
        _rmsnorm(ssm_ref[...], og[:, p1:])], axis=1).astype(BF16)
    ts = merged.shape[0]
    out = _load_tiled(x_ref, ts) + jnp.dot(merged, w_ref[...], preferred_element_type=F32)
    _store_tiled(o_ref, out, ts)


def _merge(x, att, pool, ssm, og, w, *, ts):
    b, s = x.shape[0], x.shape[1] // SUBLANES
    att_w, pool_w, sw = att.shape[2], pool.shape[2], ssm.shape[2]
    tok = lambda wd: pl.BlockSpec((None, ts, wd), lambda bi, i: (bi, i, 0))
    return pl.pallas_call(
        functools.partial(_merge_kernel, att_w=att_w, pool_w=pool_w),
        grid=(b, s // ts),
        in_specs=[_tiled_spec(ts), tok(att_w), tok(pool_w), tok(sw),
                  pl.BlockSpec(og.shape, lambda bi, i: (0, 0)),
                  pl.BlockSpec(w.shape, lambda bi, i: (0, 0))],
        out_specs=_tiled_spec(ts),
        out_shape=jax.ShapeDtypeStruct(x.shape, F32),
        compiler_params=_params(("arbitrary", "arbitrary"), VMEM_LIMIT_BYTES),
        name="merge_outproj",
    )(x, att, pool, ssm, og, w)


def _topk_rows(a, k, extra=None):
    nrow, nl = a.shape
    row = lax.broadcasted_iota(I32, (nrow, nl), 0).astype(F32)
    out_row = lax.broadcasted_iota(I32, (k, nl), 0)
    vals = jnp.zeros((k, nl), F32)
    sel = jnp.zeros((k, nl), F32)
    for j in range(k):
        m = jnp.max(a, axis=0, keepdims=True)
        ix = jnp.min(jnp.where(a == m, row, float(nrow)), axis=0, keepdims=True)
        hit = row == ix
        if extra is None:
            pick = ix
        else:
            pick = jnp.sum(jnp.where(hit, extra, 0.0), axis=0, keepdims=True)
        vals = jnp.where(out_row == j, m, vals)
        sel = jnp.where(out_row == j, pick, sel)
        a = jnp.where(hit, -jnp.inf, a)
    return vals, sel


def _route_kernel(x_ref, g_ref, wq_ref, keys_ref, h_out, off_out, sh_out, gate_out, q_sc, stage_sc,
                  *, n_half):
    hd = pl.program_id(1)
    qd = 2 * PEER_KEYS

    @pl.when(hd == 0)
    def _():
        tm = q_sc.shape[0]
        h = _rmsnorm(_load_tiled(x_ref, tm), g_ref[...])
        _store_tiled(h_out, h, tm)
        q_sc[...] = jnp.dot(h.astype(BF16), wq_ref[...], preferred_element_type=F32)

    q = q_sc[:, pl.ds(pl.multiple_of(hd * qd, qd), qd)].astype(BF16)
    tops = []
    for i in range(2):
        sc = lax.dot_general(keys_ref[i], q[:, i * PEER_KEYS:(i + 1) * PEER_KEYS], _NT,
                             preferred_element_type=F32)
        tops.append(_topk_rows(sc, PEER_TOPK))
    (s1, i1), (s2, i2) = tops
    k = PEER_TOPK
    sub8 = lax.broadcasted_iota(I32, (SUBLANES, s1.shape[1]), 0)
    cands, ecands = [], []
    a = 0
    while a < k and k // (a + 1) > 1:
        nb = k // (a + 1)
        for b0 in range(0, nb, SUBLANES):
            c = s1[a:a + 1, :] + s2[b0:b0 + SUBLANES, :]
            if b0 + SUBLANES > nb:
                c = jnp.where(sub8 < nb - b0, c, -jnp.inf)
            cands.append(c)
            ecands.append(i1[a:a + 1, :] * float(PEER_KEYS) + i2[b0:b0 + SUBLANES, :])
        a += 1
    cands.append(s1[a:, :] + s2[0:1, :])
    ecands.append(i1[a:, :] * float(PEER_KEYS) + i2[0:1, :])
    best, experts = _topk_rows(jnp.concatenate(cands, axis=0), k,
                               extra=jnp.concatenate(ecands, axis=0))
    w = jnp.exp(best - best[0:1, :])
    rows = pl.ds(pl.multiple_of(hd * k, k), k)
    stage_sc[2, rows, :] = w / jnp.sum(w, axis=0, keepdims=True)
    hi_half = experts < float(n_half)
    stage_sc[0, rows, :] = jnp.where(hi_half, experts, experts - float(n_half)) * float(SUBLANES)
    stage_sc[1, rows, :] = jnp.where(hi_half, 0.0, 16.0)

    @pl.when(hd == pl.num_programs(1) - 1)
    def _():
        off_out[...] = stage_sc[0].T.astype(I32)
        sh_out[...] = stage_sc[1].T
        gate_out[...] = stage_sc[2].T


def _route(x, g, wq, keys, *, tm, n_experts):
    t = x.shape[0] // SUBLANES
    n_heads = keys.shape[0]
    n_sel = n_heads * PEER_TOPK
    hk = pl.BlockSpec((tm, n_sel), lambda i, h: (i, 0))
    shp = lambda dt: jax.ShapeDtypeStruct((t, n_sel), dt)
    tiled = pl.BlockSpec((tm * SUBLANES, LANES), lambda i, h: (i, 0))
    return pl.pallas_call(
        functools.partial(_route_kernel, n_half=n_experts // 2),
        grid=(t // tm, n_heads),
        in_specs=[tiled,
                  pl.BlockSpec(g.shape, lambda i, h: (0, 0)),
                  pl.BlockSpec(wq.shape, lambda i, h: (0, 0)),
                  pl.BlockSpec((None, 2, PEER_KEYS, keys.shape[3]), lambda i, h: (h, 0, 0, 0))],
        out_specs=[tiled, hk, hk, hk],
        out_shape=[jax.ShapeDtypeStruct(x.shape, F32), shp(I32), shp(F32), shp(F32)],
        scratch_shapes=[pltpu.VMEM((tm, wq.shape[1]), F32), pltpu.VMEM((3, n_sel, tm), F32)],
        compiler_params=_params(("arbitrary", "arbitrary"), VMEM_LIMIT_BYTES),
        name="peer_route",
    )(x, g, wq, keys)


def _expert_row(tab_ref, off, sh):
    w = tab_ref[pl.ds(pl.multiple_of(off, SUBLANES), SUBLANES), :]
    return lax.bitcast_convert_type((w << sh) & jnp.uint32(0xFFFF0000), F32)


def _splat_rows(mxu_src, mxu_dst, xlu_src=None, xlu_dst=None, *, tt, hk, also=None):
    eye = (lax.broadcasted_iota(I32, (hk, hk), 0) == lax.broadcasted_iota(I32, (hk, hk), 1))
    ones = jnp.ones((hk, hk), BF16)

    def body(t, carry):
        v = jnp.broadcast_to(mxu_src[pl.ds(t, 1), :], (hk, hk))
        rep = jnp.dot(jnp.where(eye, v, 0.0).astype(BF16), ones, preferred_element_type=F32)
        mxu_dst(t, rep.astype(I32))
        if xlu_src is not None:
            xlu_dst(t, jnp.broadcast_to(xlu_src[pl.ds(t, 1), :], (hk, hk)).T)
        if also is not None:
            also(t)
        return carry

    lax.fori_loop(0, tt, body, 0, unroll=8)


def _splat(rep_ref, row):
    return jnp.broadcast_to(rep_ref[pl.ds(row, 1), :], (SUBLANES, LANES))


def _peer_u_kernel(*refs, tt, hk):
    n_off = tt * hk // SUBLANES
    off_refs = [refs[0].at[pl.ds(j * n_off, n_off)] for j in range(SUBLANES)]
    sh_ref, shn_ref, h_ref, gate_ref, tab_ref, coef_ref, part_ref, srep_ref, pre_ref = refs[1:]
    ngrp = hk // SUBLANES

    def store_srep(t, block):
        srep_ref[pl.ds(pl.multiple_of(t * hk, hk), hk), :] = block

    @pl.when(pl.program_id(0) == 0)
    def _():
        _splat_rows(sh_ref, store_srep, tt=tt, hk=hk)

    chunk = U_EDGES_PER_DOT
    sel = (lax.broadcasted_iota(I32, (chunk, chunk * SUBLANES), 1) // SUBLANES
           == lax.broadcasted_iota(I32, (chunk, chunk * SUBLANES), 0)).astype(BF16)

    def tok(t, carry):
        h = h_ref[t]
        base = pl.multiple_of(t * hk, hk)
        for c in range(hk // chunk):
            e0 = base + c * chunk
            ps = []
            for k0 in range(0, chunk, SUBLANES):
                sblk = srep_ref[pl.ds(e0 + k0, SUBLANES), :].astype(U32)
                for j in range(SUBLANES):
                    k = k0 + j
                    sv = jnp.broadcast_to(sblk[j:j + 1, :], (SUBLANES, LANES))
                    ps.append(_expert_row(
                        tab_ref, off_refs[j][t * ngrp + (c * chunk + k) // SUBLANES], sv) * h)
            stacked = jnp.concatenate(ps, axis=0).astype(BF16)
            part_ref[pl.ds(e0, chunk), :] = jnp.dot(sel, stacked, preferred_element_type=F32)
        return carry

    lax.fori_loop(0, tt, tok, 0, unroll=8)

    def lane_sums(t):
        p = part_ref[pl.ds(pl.multiple_of(t * hk, hk), hk), :]
        pre_ref[pl.ds(t, 1), :] = jnp.sum(p.T, axis=0, keepdims=True)

    _splat_rows(shn_ref, store_srep, tt=tt, hk=hk, also=lane_sums)
    coef_ref[...] = gate_ref[...] * _gelu_tanh(pre_ref[...])


def _peer_v_kernel(*refs, tt, hk, n_acc):
    n_off = tt * hk // SUBLANES
    off_refs = [refs[0].at[pl.ds(j * n_off, n_off)] for j in range(SUBLANES)]
    sh_ref, coef_ref, x_ref, tab_ref, o_ref, rep_ref = refs[1:]
    ngrp = hk // SUBLANES

    def store_coef(t, block):
        rep_ref[pl.ds(2 * t * hk, hk, stride=2), :] = block

    def store_shift(t, block):
        rep_ref[pl.ds(2 * t * hk + 1, hk, stride=2), :] = lax.bitcast_convert_type(block, F32)

    _splat_rows(sh_ref, store_shift, coef_ref, store_coef, tt=tt, hk=hk)

    def tok(t, carry):
        accs = [None] * n_acc
        e0 = 2 * t * hk
        per_load = SUBLANES // 2
        for k0 in range(0, hk, per_load):
            blk = rep_ref[pl.ds(pl.multiple_of(e0 + 2 * k0, SUBLANES), SUBLANES), :]
            for j in range(per_load):
                k = k0 + j
                bcast = lambda r: jnp.broadcast_to(blk[r:r + 1, :], (SUBLANES, LANES))
                off = off_refs[k % SUBLANES][t * ngrp + k // SUBLANES]
                sv = lax.bitcast_convert_type(bcast(2 * j + 1), U32)
                term = _splat(rep_ref, e0 + 2 * k) * _expert_row(tab_ref, off, sv)
                accs[k % n_acc] = term if accs[k % n_acc] is None else accs[k % n_acc] + term
        while len(accs) > 1:
            accs = [accs[j] + accs[j + len(accs) // 2] for j in range(len(accs) // 2)]
        o_ref[t] = x_ref[t] + accs[0]
        return carry

    lax.fori_loop(0, tt, tok, 0, unroll=8)


def _resident(a):
    return pl.BlockSpec(a.shape, lambda i: (0,) * a.ndim, pipeline_mode=pl.Buffered(1))


def _split_offsets(off, tt):
    t, hk = off.shape
    grouped = off.reshape(t // tt, tt, hk // SUBLANES, SUBLANES)
    return jnp.transpose(grouped, (0, 3, 1, 2)).reshape(-1)


def _peer_u(off, sh, h, gates, tab, *, tt):
    t, hk = off.shape
    assert hk == LANES and h.shape[1:] == (SUBLANES, LANES)
    smem = pl.BlockSpec((tt * hk,), lambda i: (i,), memory_space=pltpu.SMEM)
    compact = pl.BlockSpec((tt, hk), lambda i: (i, 0))
    last = t // tt - 1
    nxt = pl.BlockSpec((tt, hk), lambda i: (jnp.minimum(i + 1, last), 0))
    return pl.pallas_call(
        functools.partial(_peer_u_kernel, tt=tt, hk=hk),
        grid=(t // tt,),
        in_specs=[
            smem, compact, nxt, pl.BlockSpec((tt, SUBLANES, LANES), lambda i: (i, 0, 0)), compact,
            _resident(tab)],
        out_specs=compact,
        out_shape=jax.ShapeDtypeStruct((t, hk), F32),
        scratch_shapes=[pltpu.VMEM((tt * hk, LANES), F32), pltpu.VMEM((tt * hk, LANES), I32),
                        pltpu.VMEM((tt, hk), F32)],
        compiler_params=_params(("arbitrary",), VMEM_LIMIT_BYTES),
        name="peer_u",
    )(_split_offsets(off, tt), sh, sh, h, gates, tab)


def _peer_v(off, sh, coef, x, tab, *, tt):
    t, hk = off.shape
    assert hk == LANES and x.shape[1:] == (SUBLANES, LANES)
    smem = pl.BlockSpec((tt * hk,), lambda i: (i,), memory_space=pltpu.SMEM)
    tile = pl.BlockSpec((tt, SUBLANES, LANES), lambda i: (i, 0, 0))
    compact = pl.BlockSpec((tt, hk), lambda i: (i, 0))
    return pl.pallas_call(
        functools.partial(_peer_v_kernel, tt=tt, hk=hk, n_acc=4),
        grid=(t // tt,),
        in_specs=[smem, compact, compact, tile, _resident(tab)],
        out_specs=tile,
        out_shape=jax.ShapeDtypeStruct(x.shape, F32),
        scratch_shapes=[pltpu.VMEM((2 * tt * hk, LANES), F32)],
        compiler_params=_params(("arbitrary",), VMEM_LIMIT_BYTES),
        name="peer_v",
    )(_split_offsets(off, tt), sh, coef, x, tab)


def _pack_kernel(hi_ref, lo_ref, o_ref):
    def bf16_bits(x):
        bits = lax.bitcast_convert_type(x, U32)
        r = bits + (jnp.uint32(0x7FFF) + ((bits >> 16) & jnp.uint32(1)))
        return jnp.where(x != x, jnp.uint32(0x7FC00000), r)

    w = (bf16_bits(hi_ref[...]) & jnp.uint32(0xFFFF0000)) | (bf16_bits(lo_ref[...]) >> 16)
    _store_tiled(o_ref, w, w.shape[0])


def _pack_table(tab, rows=256):
    n, d = tab.shape
    assert d == SUBLANES * LANES and (n // 2) % rows == 0
    nblk = n // 2 // rows
    return pl.pallas_call(
        _pack_kernel,
        grid=(nblk,),
        in_specs=[pl.BlockSpec((rows, d), lambda i: (i, 0)),
                  pl.BlockSpec((rows, d), lambda i: (i + nblk, 0))],
        out_specs=pl.BlockSpec((rows * SUBLANES, LANES), lambda i: (i, 0)),
        out_shape=jax.ShapeDtypeStruct((n // 2 * SUBLANES, LANES), U32),
        compiler_params=_params(("arbitrary",)),
        name="pack_table",
    )(tab, tab)


def _block_diag(blocks):
    g, r, c = blocks.shape
    eye = jnp.eye(g, dtype=blocks.dtype)
    return (eye[:, None, :, None] * blocks[:, :, None, :]).reshape(g * r, g * c)


def _s5_params(a_re, a_im, log_dt, b_re, b_im, c_re, c_im):
    a = lax.complex(a_re, a_im)
    dt = jnp.exp(log_dt)[:, None]
    a_bar = jnp.exp(a * dt)
    b_bar = ((a_bar - 1.0) / a)[..., None] * lax.complex(b_re, b_im)
    bt = jnp.swapaxes(b_bar, 1, 2)
    bblk = jnp.concatenate([_block_diag(jnp.real(bt)), _block_diag(jnp.imag(bt))], axis=1)
    ct_re = jnp.swapaxes(c_re, 1, 2)
    ct_im = jnp.swapaxes(c_im, 1, 2)
    cblk = jnp.concatenate([_block_diag(ct_re), -_block_diag(ct_im)], axis=0)
    n = a_bar.size
    return (jnp.real(a_bar).reshape(1, n), jnp.imag(a_bar).reshape(1, n),
            bblk.astype(BF16), cblk.astype(BF16))


def _tiles(s, t):
    ts = min(512, s)
    blk = min(512, s)
    tc = min(64, s)
    tm = min(512, t)
    tt = min(128, t)
    return ts, blk, tc, tm, tt


def _layer(x, norm1_g, w_in, f_bias, q_gain, k_gain, pool_w, pool_scale,
           a_re, a_im, log_dt, b_re, b_im, c_re, c_im, ssm_d, glu_w, glu_b,
           out_norm_g, w_out, norm2_g, peer_wq, peer_keys, peer_u, peer_v):
    b, s, d = x.shape[0], x.shape[1] // SUBLANES, SUBLANES * LANES
    t = b * s
    n_heads = f_bias.shape[0]
    att_w = n_heads * HEAD_DIM
    pool_width = pool_scale.shape[0]
    ssm_w = ssm_d.shape[0]
    ts, blk, tc, tm, tt = _tiles(s, t)
    row = lambda v: v.reshape(1, -1)

    c0, c1 = 3 * att_w, 3 * att_w + n_heads
    wf = jnp.zeros((d, LANES), F32).at[:, :n_heads].set(w_in[:, c0:c1])
    fb = jnp.zeros((1, LANES), F32).at[0, :n_heads].set(f_bias)
    seg = _block_diag(jnp.full((n_heads, HEAD_DIM, HEAD_DIM), 1.0 / HEAD_DIM, F32)).astype(BF16)
    q, k, v, c_tok, pool, u_ssm = _inproj(
        x, row(norm1_g), w_in[:, :c0].astype(BF16), wf,
        w_in[:, c1:c1 + pool_width].astype(BF16), w_in[:, c1 + pool_width:].astype(BF16), fb,
        row(jnp.tile(q_gain, n_heads) * HEAD_DIM ** -0.5), row(jnp.tile(k_gain, n_heads)),
        seg, _block_diag(pool_w).astype(BF16), row(pool_scale), ts=ts)
    att = _attention(q, k, v, c_tok, jnp.swapaxes(c_tok, 1, 2), blk=blk)
    are, aim, bblk, cblk = _s5_params(a_re, a_im, log_dt, b_re, b_im, c_re, c_im)
    ssm = _s5(u_ssm, are, aim, bblk, cblk, row(ssm_d), glu_w.astype(BF16), row(glu_b), tc=tc)
    x1 = _merge(x, att, pool, ssm, row(out_norm_g), w_out.astype(BF16), ts=ts)

    n_experts = peer_u.shape[0]
    kh = peer_keys.astype(BF16)
    h2, off, sh, gates = _route(x1.reshape(t * SUBLANES, LANES), row(norm2_g),
                                peer_wq.astype(BF16), kh, tm=tm, n_experts=n_experts)
    tiles = lambda a: a.reshape(t, SUBLANES, LANES)
    coef = _peer_u(off, sh, tiles(h2), gates, _pack_table(peer_u), tt=tt)
    x2 = _peer_v(off, sh, coef, tiles(x1), _pack_table(peer_v), tt=tt)
    return x2.reshape(b, s * SUBLANES, LANES)


def kernel(x, norm1_g, w_in, f_bias, q_gain, k_gain, pool_w, pool_scale, ssm_a_re, ssm_a_im,
           ssm_log_dt, ssm_b_re, ssm_b_im, ssm_c_re, ssm_c_im, ssm_d, glu_w, glu_b, out_norm_g,
           w_out, norm2_g, peer_wq, peer_keys, peer_u, peer_v):
    per_layer = (norm1_g, w_in, f_bias, q_gain, k_gain, pool_w, pool_scale, ssm_a_re, ssm_a_im,
                 ssm_log_dt, ssm_b_re, ssm_b_im, ssm_c_re, ssm_c_im, ssm_d, glu_w, glu_b,
                 out_norm_g, w_out, norm2_g, peer_wq, peer_keys, peer_u, peer_v)
    b, s, d = x.shape
    assert d == SUBLANES * LANES
    x = x.reshape(b, s * SUBLANES, LANES)
    for l in range(norm1_g.shape[0]):
        x = _layer(x, *[p[l] for p in per_layer])
    return x.reshape(b, s, d)
```

```python
import functools
import math

import jax
import jax.numpy as jnp
from jax import lax
from jax.experimental import pallas as pl
from jax.experimental.pallas import tpu as pltpu

F32 = jnp.float32
BF16 = jnp.bfloat16
U32 = jnp.uint32
I32 = jnp.int32

EPS = 1e-6
HEAD_DIM = 64
POOL_WINDOWS = (2, 4, 8, 16)
POOL_HALO = 16
PEER_KEYS = 128
PEER_TOPK = 16
U_EDGES_PER_DOT = 8

LANES = 128
SUBLANES = 8
VMEM_LIMIT_BYTES = 56 * 1024 * 1024

_NT = (((1,), (1,)), ((), ()))


def _rmsnorm(x, g):
    return x * lax.rsqrt(jnp.mean(x * x, axis=-1, keepdims=True) + EPS) * g


def _gelu_tanh(x):
    c = math.sqrt(2.0 / math.pi)
    return x * (0.5 * (1.0 + jnp.tanh(c * (x + 0.044715 * (x * x * x)))))


def _params(sem, vmem=None):
    return pltpu.CompilerParams(dimension_semantics=sem, vmem_limit_bytes=vmem)


def _load_tiled(ref, n):
    return jnp.concatenate([ref[pl.ds(s, n, stride=SUBLANES), :] for s in range(SUBLANES)], axis=1)


def _store_tiled(ref, val, n):
    for s in range(SUBLANES):
        ref[pl.ds(s, n, stride=SUBLANES), :] = val[:, s * LANES:(s + 1) * LANES]


def _inproj_kernel(x_ref, g_ref, wqkv_ref, wf_ref, wpool_ref, wssm_ref, fb_ref,
                   qg_ref, kg_ref, seg_ref, pmix_ref, pscale_ref,
                   q_out, k_out, v_out, c_out, pool_out, ssm_out, carry_ref, halo_ref,
                   *, att_w, n_heads):
    i = pl.program_id(1)

    @pl.when(i == 0)
    def _():
        carry_ref[...] = jnp.zeros_like(carry_ref)
        halo_ref[...] = jnp.zeros_like(halo_ref)

    ts = x_ref.shape[0] // SUBLANES
    x = _load_tiled(x_ref, ts)
    h = _rmsnorm(x, g_ref[...])
    hb = h.astype(BF16)
    qkv = jnp.dot(hb, wqkv_ref[...], preferred_element_type=F32)
    seg = seg_ref[...]

    def headnorm(t, gain):
        ms = jnp.dot((t * t).astype(BF16), seg, preferred_element_type=F32)
        return t * lax.rsqrt(ms + EPS) * gain

    q_out[...] = headnorm(qkv[:, :att_w], qg_ref[...]).astype(BF16)
    k_out[...] = headnorm(qkv[:, att_w:2 * att_w], kg_ref[...]).astype(BF16)
    v_out[...] = qkv[:, 2 * att_w:].astype(BF16)
    u_pool = jnp.dot(hb, wpool_ref[...], preferred_element_type=F32)
    pool_out[...] = _pool_mix(u_pool, halo_ref[...], i * ts, pmix_ref[...], pscale_ref[...])
    halo_ref[...] = u_pool[ts - POOL_HALO:, :]
    ssm_out[...] = jnp.dot(hb, wssm_ref[...], preferred_element_type=F32)

    f = jnp.dot(h, wf_ref[...], precision=lax.Precision.HIGHEST,
                preferred_element_type=F32) + fb_ref[...]
    ls = jnp.minimum(f, 0.0) - jnp.log1p(jnp.exp(-jnp.abs(f)))
    r = lax.broadcasted_iota(I32, (ts, ts), 0)
    c = lax.broadcasted_iota(I32, (ts, ts), 1)
    tri = jnp.where(c <= r, 1.0, 0.0).astype(F32)
    cs = jnp.dot(tri, ls, precision=lax.Precision.HIGHEST,
                 preferred_element_type=F32) + carry_ref[0:1, :]
    carry_ref[...] = jnp.broadcast_to(cs[ts - 1:ts, :], carry_ref.shape)
    c_out[...] = cs[:, :n_heads]


def _tiled_spec(ts):
    return pl.BlockSpec((None, ts * SUBLANES, LANES), lambda bi, i: (bi, i, 0))


def _inproj(x, g, wqkv, wf, wpool, wssm, fb, qg, kg, seg, pmix, pscale, *, ts):
    b, s = x.shape[0], x.shape[1] // SUBLANES
    att_w = seg.shape[0]
    n_heads = att_w // HEAD_DIM
    pw = wpool.shape[1]
    sw = wssm.shape[1]
    full = lambda a: pl.BlockSpec(a.shape, lambda bi, i: (0,) * a.ndim)
    tok = lambda w: pl.BlockSpec((None, ts, w), lambda bi, i: (bi, i, 0))
    return pl.pallas_call(
        functools.partial(_inproj_kernel, att_w=att_w, n_heads=n_heads),
        grid=(b, s // ts),
        in_specs=[_tiled_spec(ts), full(g), full(wqkv), full(wf), full(wpool), full(wssm),
                  full(fb), full(qg), full(kg), full(seg), full(pmix), full(pscale)],
        out_specs=[tok(att_w), tok(att_w), tok(att_w), tok(n_heads), tok(pw), tok(sw)],
        out_shape=[jax.ShapeDtypeStruct((b, s, att_w), BF16),
                   jax.ShapeDtypeStruct((b, s, att_w), BF16),
                   jax.ShapeDtypeStruct((b, s, att_w), BF16),
                   jax.ShapeDtypeStruct((b, s, n_heads), F32),
                   jax.ShapeDtypeStruct((b, s, pw), F32),
                   jax.ShapeDtypeStruct((b, s, sw), F32)],
        scratch_shapes=[pltpu.VMEM((SUBLANES, LANES), F32), pltpu.VMEM((POOL_HALO, pw), F32)],
        compiler_params=_params(("arbitrary", "arbitrary"), VMEM_LIMIT_BYTES),
        name="inproj",
    )(x, g, wqkv, wf, wpool, wssm, fb, qg, kg, seg, pmix, pscale)


def _attn_kernel(q_ref, k_ref, v_ref, ct_ref, cr_ref, o_ref, m_sc, l_sc, acc_sc, s_sc, *, blk):
    p = pl.program_id(1)
    i = pl.program_id(2)
    q = q_ref[...]
    lane = lax.broadcasted_iota(I32, (blk, LANES), 1)
    ct = ct_ref[...]
    hl = lax.broadcasted_iota(I32, ct.shape, 1)
    heads = (2 * p, 2 * p + 1)
    qm = (jnp.where(lane < HEAD_DIM, q, jnp.zeros_like(q)),
          jnp.where(lane >= HEAD_DIM, q, jnp.zeros_like(q)))
    cq = tuple(jnp.sum(jnp.where(hl == hd, ct, 0.0), axis=1, keepdims=True) for hd in heads)
    m_sc[...] = jnp.full(m_sc.shape, -jnp.inf, F32)
    l_sc[...] = jnp.zeros(l_sc.shape, F32)
    acc_sc[...] = jnp.zeros(acc_sc.shape, F32)

    def qk(kb):
        kblk = k_ref[pl.ds(pl.multiple_of(kb * blk, blk), blk), :]
        return [lax.dot_general(qm[hh], kblk, _NT, preferred_element_type=F32) for hh in range(2)]

    def step(kb, s_cur, causal):
        start = pl.multiple_of(kb * blk, blk)
        vblk = v_ref[pl.ds(start, blk), :]
        for hh in range(2):
            ck = cr_ref[pl.ds(heads[hh], 1), pl.ds(start, blk)]
            z = (s_cur[hh] + cq[hh]) - ck
            if causal:
                row = lax.broadcasted_iota(I32, (blk, blk), 0)
                col = lax.broadcasted_iota(I32, (blk, blk), 1)
                z = jnp.where(col <= row, z, -jnp.inf)
            m_old = m_sc[hh]
            m_new = jnp.maximum(m_old, jnp.max(z, axis=1, keepdims=True))
            alpha = jnp.exp(m_old - m_new)
            pe = jnp.exp(z - m_new)
            l_sc[hh] = alpha * l_sc[hh] + jnp.sum(pe, axis=1, keepdims=True)
            acc_sc[hh] = alpha * acc_sc[hh] + jnp.dot(pe.astype(BF16), vblk,
                                                      preferred_element_type=F32)
            m_sc[hh] = m_new

    s0 = qk(0)
    s_sc[0] = s0[0]
    s_sc[1] = s0[1]

    def body(kb, carry):
        s_cur = [s_sc[0], s_sc[1]]
        s_next = qk(kb + 1)
        step(kb, s_cur, False)
        s_sc[0] = s_next[0]
        s_sc[1] = s_next[1]
        return carry

    lax.fori_loop(0, i, body, 0)
    step(i, [s_sc[0], s_sc[1]], True)
    o_ref[...] = jnp.where(lane < HEAD_DIM, acc_sc[0] / l_sc[0], acc_sc[1] / l_sc[1])


def _attention(q, k, v, c_tok, c_row, *, blk):
    b, s, w = q.shape
    n_heads = c_tok.shape[2]
    return pl.pallas_call(
        functools.partial(_attn_kernel, blk=blk),
        grid=(b, w // LANES, s // blk),
        in_specs=[pl.BlockSpec((None, blk, LANES), lambda bi, p, i: (bi, i, p)),
                  pl.BlockSpec((None, s, LANES), lambda bi, p, i: (bi, 0, p)),
                  pl.BlockSpec((None, s, LANES), lambda bi, p, i: (bi, 0, p)),
                  pl.BlockSpec((None, blk, n_heads), lambda bi, p, i: (bi, i, 0)),
                  pl.BlockSpec((None, n_heads, s), lambda bi, p, i: (bi, 0, 0))],
        out_specs=pl.BlockSpec((None, blk, LANES), lambda bi, p, i: (bi, i, p)),
        out_shape=jax.ShapeDtypeStruct((b, s, w), F32),
        scratch_shapes=[pltpu.VMEM((2, blk, 1), F32), pltpu.VMEM((2, blk, 1), F32),
                        pltpu.VMEM((2, blk, LANES), F32), pltpu.VMEM((2, blk, blk), F32)],
        compiler_params=_params(("arbitrary", "arbitrary", "arbitrary"), VMEM_LIMIT_BYTES),
        name="fox_attention",
    )(q, k, v, c_tok, c_row)


def _pool_mix(cur, prev, first_pos, w, scale):
    ts, pw = cur.shape
    group_dim = pw // len(POOL_WINDOWS)
    ext = jnp.concatenate([prev, cur], axis=0)
    pos = first_pos + lax.broadcasted_iota(I32, (ts, 1), 0)
    grp = lax.broadcasted_iota(I32, (ts, pw), 1) // group_dim
    acc = cur
    pooled = jnp.zeros_like(cur)
    for j in range(1, max(POOL_WINDOWS)):
        acc = acc + ext[POOL_HALO - j:POOL_HALO - j + ts, :]
        win = j + 1
        if win in POOL_WINDOWS:
            cnt = jnp.minimum(pos + 1, win).astype(F32)
            pooled = jnp.where(grp == POOL_WINDOWS.index(win), acc / cnt, pooled)
    pooled = pooled - cur
    return jnp.dot(pooled.astype(BF16), w, preferred_element_type=F32) * scale


def _s5_kernel(u_ref, are_ref, aim_ref, bblk_ref, cblk_ref, d_ref, gw_ref, gb_ref,
               o_ref, st_ref, hre_ref, him_ref, tm_ref, *, nb, n):
    @pl.when(pl.program_id(0) == 0)
    def _():
        hre_ref[...] = jnp.zeros_like(hre_ref)
        him_ref[...] = jnp.zeros_like(him_ref)

    tc = u_ref.shape[1]
    nslab = tm_ref.shape[0]
    for bi in range(nb):
        for c in range(nslab):
            tm_ref[c, pl.ds(bi, tc, stride=nb), :] = u_ref[bi, :, c * LANES:(c + 1) * LANES]
    u = jnp.concatenate([tm_ref[c] for c in range(nslab)], axis=1)
    st_ref[...] = jnp.dot(u.astype(BF16), bblk_ref[...], preferred_element_type=F32)
    ar = jnp.broadcast_to(are_ref[...], (nb, n))
    ai = jnp.broadcast_to(aim_ref[...], (nb, n))

    def step(t, carry):
        hr, hi = carry
        r0 = pl.multiple_of(t * nb, nb)
        br = st_ref[pl.ds(r0, nb), 0:n]
        bi = st_ref[pl.ds(r0, nb), n:2 * n]
        nr = (ar * hr - ai * hi) + br
        ni = (ar * hi + ai * hr) + bi
        st_ref[pl.ds(r0, nb), 0:n] = nr
        st_ref[pl.ds(r0, nb), n:2 * n] = ni
        return nr, ni

    hr, hi = lax.fori_loop(0, tc, step, (hre_ref[...], him_ref[...]))
    hre_ref[...] = hr
    him_ref[...] = hi
    y = jnp.dot(st_ref[...].astype(BF16), cblk_ref[...], preferred_element_type=F32)
    z = _gelu_tanh(y + d_ref[...] * u)
    gate = jax.nn.sigmoid(jnp.dot(z.astype(BF16), gw_ref[...], preferred_element_type=F32)
                          + gb_ref[...])
    out = z * gate
    for c in range(nslab):
        tm_ref[c] = out[:, c * LANES:(c + 1) * LANES]
    for bi in range(nb):
        for c in range(nslab):
            o_ref[bi, :, c * LANES:(c + 1) * LANES] = tm_ref[c, pl.ds(bi, tc, stride=nb), :]


def _s5(u, a_re, a_im, bblk, cblk, dskip, gw, gb, *, tc):
    nb, s, sw = u.shape
    n = a_re.shape[1]
    rb = tc * nb
    full = lambda a: pl.BlockSpec(a.shape, lambda i: (0,) * a.ndim)
    blk = pl.BlockSpec((nb, tc, sw), lambda i: (0, i, 0))
    return pl.pallas_call(
        functools.partial(_s5_kernel, nb=nb, n=n),
        grid=(s // tc,),
        in_specs=[blk, full(a_re), full(a_im),
                  full(bblk), full(cblk), full(dskip), full(gw), full(gb)],
        out_specs=blk,
        out_shape=jax.ShapeDtypeStruct(u.shape, F32),
        scratch_shapes=[pltpu.VMEM((rb, 2 * n), F32), pltpu.VMEM((nb, n), F32),
                        pltpu.VMEM((nb, n), F32), pltpu.VMEM((sw // LANES, rb, LANES), F32)],
        compiler_params=_params(("arbitrary",), VMEM_LIMIT_BYTES),
        name="s5_scan",
    )(u, a_re, a_im, bblk, cblk, dskip, gw, gb)


def _merge_kernel(x_ref, att_ref, pool_ref, ssm_ref, og_ref, w_ref, o_ref, *, att_w, pool_w):
    og = og_ref[...]
    p1 = att_w + pool_w
    merged = jnp.concatenate([
        _rmsnorm(att_ref[...], og[:, :att_w]),
        _rmsnorm(pool_ref[...], og[:, att_w:p1]),
        _rmsnorm(ssm_ref[...], og[:, p1:])], axis=1).astype(BF16)
    ts = merged.shape[0]
    out = _load_tiled(x_ref, ts) + jnp.dot(merged, w_ref[...], preferred_element_type=F32)
    _store_tiled(o_ref, out, ts)


def _merge(x, att, pool, ssm, og, w, *, ts):
    b, s = x.shape[0], x.shape[1] // SUBLANES
    att_w, pool_w, sw = att.shape[2], pool.shape[2], ssm.shape[2]
    tok = lambda wd: pl.BlockSpec((None, ts, wd), lambda bi, i: (bi, i, 0))
    return pl.pallas_call(
        functools.partial(_merge_kernel, att_w=att_w, pool_w=pool_w),
        grid=(b, s // ts),
        in_specs=[_tiled_spec(ts), tok(att_w), tok(pool_w), tok(sw),
                  pl.BlockSpec(og.shape, lambda bi, i: (0, 0)),
                  pl.BlockSpec(w.shape, lambda bi, i: (0, 0))],
        out_specs=_tiled_spec(ts),
        out_shape=jax.ShapeDtypeStruct(x.shape, F32),
        compiler_params=_params(("arbitrary", "arbitrary"), VMEM_LIMIT_BYTES),
        name="merge_outproj",
    )(x, att, pool, ssm, og, w)


def _topk_rows(a, k, extra=None):
    nrow, nl = a.shape
    row = lax.broadcasted_iota(I32, (nrow, nl), 0).astype(F32)
    out_row = lax.broadcasted_iota(I32, (k, nl), 0)
    vals = jnp.zeros((k, nl), F32)
    sel = jnp.zeros((k, nl), F32)
    for j in range(k):
        m = jnp.max(a, axis=0, keepdims=True)
        ix = jnp.min(jnp.where(a == m, row, float(nrow)), axis=0, keepdims=True)
        hit = row == ix
        if extra is None:
            pick = ix
        else:
            pick = jnp.sum(jnp.where(hit, extra, 0.0), axis=0, keepdims=True)
        vals = jnp.where(out_row == j, m, vals)
        sel = jnp.where(out_row == j, pick, sel)
        a = jnp.where(hit, -jnp.inf, a)
    return vals, sel


def _route_kernel(x_ref, g_ref, wq_ref, keys_ref, h_out, off_out, sh_out, gate_out, q_sc, stage_sc,
                  *, n_half):
    hd = pl.program_id(1)
    qd = 2 * PEER_KEYS

    @pl.when(hd == 0)
    def _():
        tm = q_sc.shape[0]
        h = _rmsnorm(_load_tiled(x_ref, tm), g_ref[...])
        _store_tiled(h_out, h, tm)
        q_sc[...] = jnp.dot(h.astype(BF16), wq_ref[...], preferred_element_type=F32)

    q = q_sc[:, pl.ds(pl.multiple_of(hd * qd, qd), qd)].astype(BF16)
    tops = []
    for i in range(2):
        sc = lax.dot_general(keys_ref[i], q[:, i * PEER_KEYS:(i + 1) * PEER_KEYS], _NT,
                             preferred_element_type=F32)
        tops.append(_topk_rows(sc, PEER_TOPK))
    (s1, i1), (s2, i2) = tops
    k = PEER_TOPK
    sub8 = lax.broadcasted_iota(I32, (SUBLANES, s1.shape[1]), 0)
    cands, ecands = [], []
    a = 0
    while a < k and k // (a + 1) > 1:
        nb = k // (a + 1)
        for b0 in range(0, nb, SUBLANES):
            c = s1[a:a + 1, :] + s2[b0:b0 + SUBLANES, :]
            if b0 + SUBLANES > nb:
                c = jnp.where(sub8 < nb - b0, c, -jnp.inf)
            cands.append(c)
            ecands.append(i1[a:a + 1, :] * float(PEER_KEYS) + i2[b0:b0 + SUBLANES, :])
        a += 1
    cands.append(s1[a:, :] + s2[0:1, :])
    ecands.append(i1[a:, :] * float(PEER_KEYS) + i2[0:1, :])
    best, experts = _topk_rows(jnp.concatenate(cands, axis=0), k,
                               extra=jnp.concatenate(ecands, axis=0))
    w = jnp.exp(best - best[0:1, :])
    rows = pl.ds(pl.multiple_of(hd * k, k), k)
    stage_sc[2, rows, :] = w / jnp.sum(w, axis=0, keepdims=True)
    hi_half = experts < float(n_half)
    stage_sc[0, rows, :] = jnp.where(hi_half, experts, experts - float(n_half)) * float(SUBLANES)
    stage_sc[1, rows, :] = jnp.where(hi_half, 0.0, 16.0)

    @pl.when(hd == pl.num_programs(1) - 1)
    def _():
        off_out[...] = stage_sc[0].T.astype(I32)
        sh_out[...] = stage_sc[1].T
        gate_out[...] = stage_sc[2].T


def _route(x, g, wq, keys, *, tm, n_experts):
    t = x.shape[0] // SUBLANES
    n_heads = keys.shape[0]
    n_sel = n_heads * PEER_TOPK
    hk = pl.BlockSpec((tm, n_sel), lambda i, h: (i, 0))
    shp = lambda dt: jax.ShapeDtypeStruct((t, n_sel), dt)
    tiled = pl.BlockSpec((tm * SUBLANES, LANES), lambda i, h: (i, 0))
    return pl.pallas_call(
        functools.partial(_route_kernel, n_half=n_experts // 2),
        grid=(t // tm, n_heads),
        in_specs=[tiled,
                  pl.BlockSpec(g.shape, lambda i, h: (0, 0)),
                  pl.BlockSpec(wq.shape, lambda i, h: (0, 0)),
                  pl.BlockSpec((None, 2, PEER_KEYS, keys.shape[3]), lambda i, h: (h, 0, 0, 0))],
        out_specs=[tiled, hk, hk, hk],
        out_shape=[jax.ShapeDtypeStruct(x.shape, F32), shp(I32), shp(F32), shp(F32)],
        scratch_shapes=[pltpu.VMEM((tm, wq.shape[1]), F32), pltpu.VMEM((3, n_sel, tm), F32)],
        compiler_params=_params(("arbitrary", "arbitrary"), VMEM_LIMIT_BYTES),
        name="peer_route",
    )(x, g, wq, keys)


def _expert_row(tab_ref, off, sh):
    w = tab_ref[pl.ds(pl.multiple_of(off, SUBLANES), SUBLANES), :]
    return lax.bitcast_convert_type((w << sh) & jnp.uint32(0xFFFF0000), F32)


def _splat_rows(mxu_src, mxu_dst, xlu_src=None, xlu_dst=None, *, tt, hk, also=None):
    eye = (lax.broadcasted_iota(I32, (hk, hk), 0) == lax.broadcasted_iota(I32, (hk, hk), 1))
    ones = jnp.ones((hk, hk), BF16)

    def body(t, carry):
        v = jnp.broadcast_to(mxu_src[pl.ds(t, 1), :], (hk, hk))
        rep = jnp.dot(jnp.where(eye, v, 0.0).astype(BF16), ones, preferred_element_type=F32)
        mxu_dst(t, rep.astype(I32))
        if xlu_src is not None:
            xlu_dst(t, jnp.broadcast_to(xlu_src[pl.ds(t, 1), :], (hk, hk)).T)
        if also is not None:
            also(t)
        return carry

    lax.fori_loop(0, tt, body, 0, unroll=8)


def _splat(rep_ref, row):
    return jnp.broadcast_to(rep_ref[pl.ds(row, 1), :], (SUBLANES, LANES))


def _peer_u_kernel(*refs, tt, hk):
    n_off = tt * hk // SUBLANES
    off_refs = [refs[0].at[pl.ds(j * n_off, n_off)] for j in range(SUBLANES)]
    sh_ref, shn_ref, h_ref, gate_ref, tab_ref, coef_ref, part_ref, srep_ref, pre_ref = refs[1:]
    ngrp = hk // SUBLANES

    def store_srep(t, block):
        srep_ref[pl.ds(pl.multiple_of(t * hk, hk), hk), :] = block

    @pl.when(pl.program_id(0) == 0)
    def _():
        _splat_rows(sh_ref, store_srep, tt=tt, hk=hk)

    chunk = U_EDGES_PER_DOT
    sel = (lax.broadcasted_iota(I32, (chunk, chunk * SUBLANES), 1) // SUBLANES
           == lax.broadcasted_iota(I32, (chunk, chunk * SUBLANES), 0)).astype(BF16)

    def tok(t, carry):
        h = h_ref[t]
        base = pl.multiple_of(t * hk, hk)
        for c in range(hk // chunk):
            e0 = base + c * chunk
            ps = []
            for k0 in range(0, chunk, SUBLANES):
                sblk = srep_ref[pl.ds(e0 + k0, SUBLANES), :].astype(U32)
                for j in range(SUBLANES):
                    k = k0 + j
                    sv = jnp.broadcast_to(sblk[j:j + 1, :], (SUBLANES, LANES))
                    ps.append(_expert_row(
                        tab_ref, off_refs[j][t * ngrp + (c * chunk + k) // SUBLANES], sv) * h)
            stacked = jnp.concatenate(ps, axis=0).astype(BF16)
            part_ref[pl.ds(e0, chunk), :] = jnp.dot(sel, stacked, preferred_element_type=F32)
        return carry

    lax.fori_loop(0, tt, tok, 0, unroll=8)

    def lane_sums(t):
        p = part_ref[pl.ds(pl.multiple_of(t * hk, hk), hk), :]
        pre_ref[pl.ds(t, 1), :] = jnp.sum(p.T, axis=0, keepdims=True)

    _splat_rows(shn_ref, store_srep, tt=tt, hk=hk, also=lane_sums)
    coef_ref[...] = gate_ref[...] * _gelu_tanh(pre_ref[...])


def _peer_v_kernel(*refs, tt, hk, n_acc):
    n_off = tt * hk // SUBLANES
    off_refs = [refs[0].at[pl.ds(j * n_off, n_off)] for j in range(SUBLANES)]
    sh_ref, coef_ref, x_ref, tab_ref, o_ref, rep_ref = refs[1:]
    ngrp = hk // SUBLANES

    def store_coef(t, block):
        rep_ref[pl.ds(2 * t * hk, hk, stride=2), :] = block

    def store_shift(t, block):
        rep_ref[pl.ds(2 * t * hk + 1, hk, stride=2), :] = lax.bitcast_convert_type(block, F32)

    _splat_rows(sh_ref, store_shift, coef_ref, store_coef, tt=tt, hk=hk)

    def tok(t, carry):
        accs = [None] * n_acc
        e0 = 2 * t * hk
        per_load = SUBLANES // 2
        for k0 in range(0, hk, per_load):
            blk = rep_ref[pl.ds(pl.multiple_of(e0 + 2 * k0, SUBLANES), SUBLANES), :]
            for j in range(per_load):
                k = k0 + j
                bcast = lambda r: jnp.broadcast_to(blk[r:r + 1, :], (SUBLANES, LANES))
                off = off_refs[k % SUBLANES][t * ngrp + k // SUBLANES]
                sv = lax.bitcast_convert_type(bcast(2 * j + 1), U32)
                term = _splat(rep_ref, e0 + 2 * k) * _expert_row(tab_ref, off, sv)
                accs[k % n_acc] = term if accs[k % n_acc] is None else accs[k % n_acc] + term
        while len(accs) > 1:
            accs = [accs[j] + accs[j + len(accs) // 2] for j in range(len(accs) // 2)]
        o_ref[t] = x_ref[t] + accs[0]
        return carry

    lax.fori_loop(0, tt, tok, 0, unroll=8)


def _resident(a):
    return pl.BlockSpec(a.shape, lambda i: (0,) * a.ndim, pipeline_mode=pl.Buffered(1))


def _split_offsets(off, tt):
    t, hk = off.shape
    grouped = off.reshape(t // tt, tt, hk // SUBLANES, SUBLANES)
    return jnp.transpose(grouped, (0, 3, 1, 2)).reshape(-1)


def _peer_u(off, sh, h, gates, tab, *, tt):
    t, hk = off.shape
    assert hk == LANES and h.shape[1:] == (SUBLANES, LANES)
    smem = pl.BlockSpec((tt * hk,), lambda i: (i,), memory_space=pltpu.SMEM)
    compact = pl.BlockSpec((tt, hk), lambda i: (i, 0))
    last = t // tt - 1
    nxt = pl.BlockSpec((tt, hk), lambda i: (jnp.minimum(i + 1, last), 0))
    return pl.pallas_call(
        functools.partial(_peer_u_kernel, tt=tt, hk=hk),
        grid=(t // tt,),
        in_specs=[
            smem, compact, nxt, pl.BlockSpec((tt, SUBLANES, LANES), lambda i: (i, 0, 0)), compact,
            _resident(tab)],
        out_specs=compact,
        out_shape=jax.ShapeDtypeStruct((t, hk), F32),
        scratch_shapes=[pltpu.VMEM((tt * hk, LANES), F32), pltpu.VMEM((tt * hk, LANES), I32),
                        pltpu.VMEM((tt, hk), F32)],
        compiler_params=_params(("arbitrary",), VMEM_LIMIT_BYTES),
        name="peer_u",
    )(_split_offsets(off, tt), sh, sh, h, gates, tab)


def _peer_v(off, sh, coef, x, tab, *, tt):
    t, hk = off.shape
    assert hk == LANES and x.shape[1:] == (SUBLANES, LANES)
    smem = pl.BlockSpec((tt * hk,), lambda i: (i,), memory_space=pltpu.SMEM)
    tile = pl.BlockSpec((tt, SUBLANES, LANES), lambda i: (i, 0, 0))
    compact = pl.BlockSpec((tt, hk), lambda i: (i, 0))
    return pl.pallas_call(
        functools.partial(_peer_v_kernel, tt=tt, hk=hk, n_acc=4),
        grid=(t // tt,),
        in_specs=[smem, compact, compact, tile, _resident(tab)],
        out_specs=tile,
        out_shape=jax.ShapeDtypeStruct(x.shape, F32),
        scratch_shapes=[pltpu.VMEM((2 * tt * hk, LANES), F32)],
        compiler_params=_params(("arbitrary",), VMEM_LIMIT_BYTES),
        name="peer_v",
    )(_split_offsets(off, tt), sh, coef, x, tab)


def _pack_kernel(hi_ref, lo_ref, o_ref):
    def bf16_bits(x):
        bits = lax.bitcast_convert_type(x, U32)
        r = bits + (jnp.uint32(0x7FFF) + ((bits >> 16) & jnp.uint32(1)))
        return jnp.where(x != x, jnp.uint32(0x7FC00000), r)

    w = (bf16_bits(hi_ref[...]) & jnp.uint32(0xFFFF0000)) | (bf16_bits(lo_ref[...]) >> 16)
    _store_tiled(o_ref, w, w.shape[0])


def _pack_table(tab, rows=256):
    n, d = tab.shape
    assert d == SUBLANES * LANES and (n // 2) % rows == 0
    nblk = n // 2 // rows
    return pl.pallas_call(
        _pack_kernel,
        grid=(nblk,),
        in_specs=[pl.BlockSpec((rows, d), lambda i: (i, 0)),
                  pl.BlockSpec((rows, d), lambda i: (i + nblk, 0))],
        out_specs=pl.BlockSpec((rows * SUBLANES, LANES), lambda i: (i, 0)),
        out_shape=jax.ShapeDtypeStruct((n // 2 * SUBLANES, LANES), U32),
        compiler_params=_params(("arbitrary",)),
        name="pack_table",
    )(tab, tab)


def _block_diag(blocks):
    g, r, c = blocks.shape
    eye = jnp.eye(g, dtype=blocks.dtype)
    return (eye[:, None, :, None] * blocks[:, :, None, :]).reshape(g * r, g * c)


def _s5_params(a_re, a_im, log_dt, b_re, b_im, c_re, c_im):
    a = lax.complex(a_re, a_im)
    dt = jnp.exp(log_dt)[:, None]
    a_bar = jnp.exp(a * dt)
    b_bar = ((a_bar - 1.0) / a)[..., None] * lax.complex(b_re, b_im)
    bt = jnp.swapaxes(b_bar, 1, 2)
    bblk = jnp.concatenate([_block_diag(jnp.real(bt)), _block_diag(jnp.imag(bt))], axis=1)
    ct_re = jnp.swapaxes(c_re, 1, 2)
    ct_im = jnp.swapaxes(c_im, 1, 2)
    cblk = jnp.concatenate([_block_diag(ct_re), -_block_diag(ct_im)], axis=0)
    n = a_bar.size
    return (jnp.real(a_bar).reshape(1, n), jnp.imag(a_bar).reshape(1, n),
            bblk.astype(BF16), cblk.astype(BF16))


def _tiles(s, t):
    ts = min(512, s)
    blk = min(512, s)
    tc = min(64, s)
    tm = min(512, t)
    tt = min(128, t)
    return ts, blk, tc, tm, tt


def _layer(x, norm1_g, w_in, f_bias, q_gain, k_gain, pool_w, pool_scale,
           a_re, a_im, log_dt, b_re, b_im, c_re, c_im, ssm_d, glu_w, glu_b,
           out_norm_g, w_out, norm2_g, peer_wq, peer_keys, peer_u, peer_v):
    b, s, d = x.shape[0], x.shape[1] // SUBLANES, SUBLANES * LANES
    t = b * s
    n_heads = f_bias.shape[0]
    att_w = n_heads * HEAD_DIM
    pool_width = pool_scale.shape[0]
    ssm_w = ssm_d.shape[0]
    ts, blk, tc, tm, tt = _tiles(s, t)
    row = lambda v: v.reshape(1, -1)

    c0, c1 = 3 * att_w, 3 * att_w + n_heads
    wf = jnp.zeros((d, LANES), F32).at[:, :n_heads].set(w_in[:, c0:c1])
    fb = jnp.zeros((1, LANES), F32).at[0, :n_heads].set(f_bias)
    seg = _block_diag(jnp.full((n_heads, HEAD_DIM, HEAD_DIM), 1.0 / HEAD_DIM, F32)).astype(BF16)
    q, k, v, c_tok, pool, u_ssm = _inproj(
        x, row(norm1_g), w_in[:, :c0].astype(BF16), wf,
        w_in[:, c1:c1 + pool_width].astype(BF16), w_in[:, c1 + pool_width:].astype(BF16), fb,
        row(jnp.tile(q_gain, n_heads) * HEAD_DIM ** -0.5), row(jnp.tile(k_gain, n_heads)),
        seg, _block_diag(pool_w).astype(BF16), row(pool_scale), ts=ts)
    att = _attention(q, k, v, c_tok, jnp.swapaxes(c_tok, 1, 2), blk=blk)
    are, aim, bblk, cblk = _s5_params(a_re, a_im, log_dt, b_re, b_im, c_re, c_im)
    ssm = _s5(u_ssm, are, aim, bblk, cblk, row(ssm_d), glu_w.astype(BF16), row(glu_b), tc=tc)
    x1 = _merge(x, att, pool, ssm, row(out_norm_g), w_out.astype(BF16), ts=ts)

    n_experts = peer_u.shape[0]
    kh = peer_keys.astype(BF16)
    h2, off, sh, gates = _route(x1.reshape(t * SUBLANES, LANES), row(norm2_g),
                                peer_wq.astype(BF16), kh, tm=tm, n_experts=n_experts)
    tiles = lambda a: a.reshape(t, SUBLANES, LANES)
    coef = _peer_u(off, sh, tiles(h2), gates, _pack_table(peer_u), tt=tt)
    x2 = _peer_v(off, sh, coef, tiles(x1), _pack_table(peer_v), tt=tt)
    return x2.reshape(b, s * SUBLANES, LANES)


def kernel(x, norm1_g, w_in, f_bias, q_gain, k_gain, pool_w, pool_scale, ssm_a_re, ssm_a_im,
           ssm_log_dt, ssm_b_re, ssm_b_im, ssm_c_re, ssm_c_im, ssm_d, glu_w, glu_b, out_norm_g,
           w_out, norm2_g, peer_wq, peer_keys, peer_u, peer_v):
    per_layer = (norm1_g, w_in, f_bias, q_gain, k_gain, pool_w, pool_scale, ssm_a_re, ssm_a_im,
                 ssm_log_dt, ssm_b_re, ssm_b_im, ssm_c_re, ssm_c_im, ssm_d, glu_w, glu_b,
                 out_norm_g, w_out, norm2_g, peer_wq, peer_keys, peer_u, peer_v)
    b, s, d = x.shape
    assert d == SUBLANES * LANES
    x = x.reshape(b, s * SUBLANES, LANES)
    for l in range(norm1_g.shape[0]):
        x = _layer(x, *[p[l] for p in per_layer])
    return x.reshape(b, s, d)
```

```python
import functools
import math

import jax
import jax.numpy as jnp
from jax import lax
from jax.experimental import pallas as pl
from jax.experimental.pallas import tpu as pltpu

F32 = jnp.float32
BF16 = jnp.bfloat16
U32 = jnp.uint32
I32 = jnp.int32

EPS = 1e-6
HEAD_DIM = 64
POOL_WINDOWS = (2, 4, 8, 16)
POOL_HALO = 16
PEER_KEYS = 128
PEER_TOPK = 16
U_EDGES_PER_DOT = 16

LANES = 128
SUBLANES = 8
VMEM_LIMIT_BYTES = 56 * 1024 * 1024

_NT = (((1,), (1,)), ((), ()))


def _rmsnorm(x, g):
    return x * lax.rsqrt(jnp.mean(x * x, axis=-1, keepdims=True) + EPS) * g


def _gelu_tanh(x):
    c = math.sqrt(2.0 / math.pi)
    return x * (0.5 * (1.0 + jnp.tanh(c * (x + 0.044715 * (x * x * x)))))


def _params(sem, vmem=None):
    return pltpu.CompilerParams(dimension_semantics=sem, vmem_limit_bytes=vmem)


def _load_tiled(ref, n):
    return jnp.concatenate([ref[pl.ds(s, n, stride=SUBLANES), :] for s in range(SUBLANES)], axis=1)


def _store_tiled(ref, val, n):
    for s in range(SUBLANES):
        ref[pl.ds(s, n, stride=SUBLANES), :] = val[:, s * LANES:(s + 1) * LANES]


def _inproj_kernel(x_ref, g_ref, wqkv_ref, wf_ref, wpool_ref, wssm_ref, fb_ref,
                   qg_ref, kg_ref, seg_ref, pmix_ref, pscale_ref,
                   q_out, k_out, v_out, c_out, pool_out, ssm_out, carry_ref, halo_ref,
                   *, att_w, n_heads):
    i = pl.program_id(1)

    @pl.when(i == 0)
    def _():
        carry_ref[...] = jnp.zeros_like(carry_ref)
        halo_ref[...] = jnp.zeros_like(halo_ref)

    ts = x_ref.shape[0] // SUBLANES
    x = _load_tiled(x_ref, ts)
    h = _rmsnorm(x, g_ref[...])
    hb = h.astype(BF16)
    qkv = jnp.dot(hb, wqkv_ref[...], preferred_element_type=F32)
    seg = seg_ref[...]

    def headnorm(t, gain):
        ms = jnp.dot((t * t).astype(BF16), seg, preferred_element_type=F32)
        return t * lax.rsqrt(ms + EPS) * gain

    q_out[...] = headnorm(qkv[:, :att_w], qg_ref[...]).astype(BF16)
    k_out[...] = headnorm(qkv[:, att_w:2 * att_w], kg_ref[...]).astype(BF16)
    v_out[...] = qkv[:, 2 * att_w:].astype(BF16)
    u_pool = jnp.dot(hb, wpool_ref[...], preferred_element_type=F32)
    pool_out[...] = _pool_mix(u_pool, halo_ref[...], i * ts, pmix_ref[...], pscale_ref[...])
    halo_ref[...] = u_pool[ts - POOL_HALO:, :]
    ssm_out[...] = jnp.dot(hb, wssm_ref[...], preferred_element_type=F32)

    f = jnp.dot(h, wf_ref[...], precision=lax.Precision.HIGHEST,
                preferred_element_type=F32) + fb_ref[...]
    ls = jnp.minimum(f, 0.0) - jnp.log1p(jnp.exp(-jnp.abs(f)))
    r = lax.broadcasted_iota(I32, (ts, ts), 0)
    c = lax.broadcasted_iota(I32, (ts, ts), 1)
    tri = jnp.where(c <= r, 1.0, 0.0).astype(F32)
    cs = jnp.dot(tri, ls, precision=lax.Precision.HIGHEST,
                 preferred_element_type=F32) + carry_ref[0:1, :]
    carry_ref[...] = jnp.broadcast_to(cs[ts - 1:ts, :], carry_ref.shape)
    c_out[...] = cs[:, :n_heads]


def _tiled_spec(ts):
    return pl.BlockSpec((None, ts * SUBLANES, LANES), lambda bi, i: (bi, i, 0))


def _inproj(x, g, wqkv, wf, wpool, wssm, fb, qg, kg, seg, pmix, pscale, *, ts):
    b, s = x.shape[0], x.shape[1] // SUBLANES
    att_w = seg.shape[0]
    n_heads = att_w // HEAD_DIM
    pw = wpool.shape[1]
    sw = wssm.shape[1]
    full = lambda a: pl.BlockSpec(a.shape, lambda bi, i: (0,) * a.ndim)
    tok = lambda w: pl.BlockSpec((None, ts, w), lambda bi, i: (bi, i, 0))
    return pl.pallas_call(
        functools.partial(_inproj_kernel, att_w=att_w, n_heads=n_heads),
        grid=(b, s // ts),
        in_specs=[_tiled_spec(ts), full(g), full(wqkv), full(wf), full(wpool), full(wssm),
                  full(fb), full(qg), full(kg), full(seg), full(pmix), full(pscale)],
        out_specs=[tok(att_w), tok(att_w), tok(att_w), tok(n_heads), tok(pw), tok(sw)],
        out_shape=[jax.ShapeDtypeStruct((b, s, att_w), BF16),
                   jax.ShapeDtypeStruct((b, s, att_w), BF16),
                   jax.ShapeDtypeStruct((b, s, att_w), BF16),
                   jax.ShapeDtypeStruct((b, s, n_heads), F32),
                   jax.ShapeDtypeStruct((b, s, pw), F32),
                   jax.ShapeDtypeStruct((b, s, sw), F32)],
        scratch_shapes=[pltpu.VMEM((SUBLANES, LANES), F32), pltpu.VMEM((POOL_HALO, pw), F32)],
        compiler_params=_params(("arbitrary", "arbitrary"), VMEM_LIMIT_BYTES),
        name="inproj",
    )(x, g, wqkv, wf, wpool, wssm, fb, qg, kg, seg, pmix, pscale)


def _attn_kernel(q_ref, k_ref, v_ref, ct_ref, cr_ref, o_ref, m_sc, l_sc, acc_sc, s_sc, *, blk):
    p = pl.program_id(1)
    i = pl.program_id(2)
    q = q_ref[...]
    lane = lax.broadcasted_iota(I32, (blk, LANES), 1)
    ct = ct_ref[...]
    hl = lax.broadcasted_iota(I32, ct.shape, 1)
    heads = (2 * p, 2 * p + 1)
    qm = (jnp.where(lane < HEAD_DIM, q, jnp.zeros_like(q)),
          jnp.where(lane >= HEAD_DIM, q, jnp.zeros_like(q)))
    cq = tuple(jnp.sum(jnp.where(hl == hd, ct, 0.0), axis=1, keepdims=True) for hd in heads)
    m_sc[...] = jnp.full(m_sc.shape, -jnp.inf, F32)
    l_sc[...] = jnp.zeros(l_sc.shape, F32)
    acc_sc[...] = jnp.zeros(acc_sc.shape, F32)

    def qk(kb):
        kblk = k_ref[pl.ds(pl.multiple_of(kb * blk, blk), blk), :]
        return [lax.dot_general(qm[hh], kblk, _NT, preferred_element_type=F32) for hh in range(2)]

    def step(kb, s_cur, causal):
        start = pl.multiple_of(kb * blk, blk)
        vblk = v_ref[pl.ds(start, blk), :]
        for hh in range(2):
            ck = cr_ref[pl.ds(heads[hh], 1), pl.ds(start, blk)]
            z = (s_cur[hh] + cq[hh]) - ck
            if causal:
                row = lax.broadcasted_iota(I32, (blk, blk), 0)
                col = lax.broadcasted_iota(I32, (blk, blk), 1)
                z = jnp.where(col <= row, z, -jnp.inf)
            m_old = m_sc[hh]
            m_new = jnp.maximum(m_old, jnp.max(z, axis=1, keepdims=True))
            alpha = jnp.exp(m_old - m_new)
            pe = jnp.exp(z - m_new)
            l_sc[hh] = alpha * l_sc[hh] + jnp.sum(pe, axis=1, keepdims=True)
            acc_sc[hh] = alpha * acc_sc[hh] + jnp.dot(pe.astype(BF16), vblk,
                                                      preferred_element_type=F32)
            m_sc[hh] = m_new

    s0 = qk(0)
    s_sc[0] = s0[0]
    s_sc[1] = s0[1]

    def body(kb, carry):
        s_cur = [s_sc[0], s_sc[1]]
        s_next = qk(kb + 1)
        step(kb, s_cur, False)
        s_sc[0] = s_next[0]
        s_sc[1] = s_next[1]
        return carry

    lax.fori_loop(0, i, body, 0)
    step(i, [s_sc[0], s_sc[1]], True)
    o_ref[...] = jnp.where(lane < HEAD_DIM, acc_sc[0] / l_sc[0], acc_sc[1] / l_sc[1])


def _attention(q, k, v, c_tok, c_row, *, blk):
    b, s, w = q.shape
    n_heads = c_tok.shape[2]
    return pl.pallas_call(
        functools.partial(_attn_kernel, blk=blk),
        grid=(b, w // LANES, s // blk),
        in_specs=[pl.BlockSpec((None, blk, LANES), lambda bi, p, i: (bi, i, p)),
                  pl.BlockSpec((None, s, LANES), lambda bi, p, i: (bi, 0, p)),
                  pl.BlockSpec((None, s, LANES), lambda bi, p, i: (bi, 0, p)),
                  pl.BlockSpec((None, blk, n_heads), lambda bi, p, i: (bi, i, 0)),
                  pl.BlockSpec((None, n_heads, s), lambda bi, p, i: (bi, 0, 0))],
        out_specs=pl.BlockSpec((None, blk, LANES), lambda bi, p, i: (bi, i, p)),
        out_shape=jax.ShapeDtypeStruct((b, s, w), F32),
        scratch_shapes=[pltpu.VMEM((2, blk, 1), F32), pltpu.VMEM((2, blk, 1), F32),
                        pltpu.VMEM((2, blk, LANES), F32), pltpu.VMEM((2, blk, blk), F32)],
        compiler_params=_params(("arbitrary", "arbitrary", "arbitrary"), VMEM_LIMIT_BYTES),
        name="fox_attention",
    )(q, k, v, c_tok, c_row)


def _pool_mix(cur, prev, first_pos, w, scale):
    ts, pw = cur.shape
    group_dim = pw // len(POOL_WINDOWS)
    ext = jnp.concatenate([prev, cur], axis=0)
    pos = first_pos + lax.broadcasted_iota(I32, (ts, 1), 0)
    grp = lax.broadcasted_iota(I32, (ts, pw), 1) // group_dim
    acc = cur
    pooled = jnp.zeros_like(cur)
    for j in range(1, max(POOL_WINDOWS)):
        acc = acc + ext[POOL_HALO - j:POOL_HALO - j + ts, :]
        win = j + 1
        if win in POOL_WINDOWS:
            cnt = jnp.minimum(pos + 1, win).astype(F32)
            pooled = jnp.where(grp == POOL_WINDOWS.index(win), acc / cnt, pooled)
    pooled = pooled - cur
    return jnp.dot(pooled.astype(BF16), w, preferred_element_type=F32) * scale


def _s5_kernel(u_ref, are_ref, aim_ref, bblk_ref, cblk_ref, d_ref, gw_ref, gb_ref,
               o_ref, st_ref, hre_ref, him_ref, tm_ref, *, nb, n):
    @pl.when(pl.program_id(0) == 0)
    def _():
        hre_ref[...] = jnp.zeros_like(hre_ref)
        him_ref[...] = jnp.zeros_like(him_ref)

    tc = u_ref.shape[1]
    nslab = tm_ref.shape[0]
    for bi in range(nb):
        for c in range(nslab):
            tm_ref[c, pl.ds(bi, tc, stride=nb), :] = u_ref[bi, :, c * LANES:(c + 1) * LANES]
    u = jnp.concatenate([tm_ref[c] for c in range(nslab)], axis=1)
    st_ref[...] = jnp.dot(u.astype(BF16), bblk_ref[...], preferred_element_type=F32)
    ar = jnp.broadcast_to(are_ref[...], (nb, n))
    ai = jnp.broadcast_to(aim_ref[...], (nb, n))

    def step(t, carry):
        hr, hi = carry
        r0 = pl.multiple_of(t * nb, nb)
        br = st_ref[pl.ds(r0, nb), 0:n]
        bi = st_ref[pl.ds(r0, nb), n:2 * n]
        nr = (ar * hr - ai * hi) + br
        ni = (ar * hi + ai * hr) + bi
        st_ref[pl.ds(r0, nb), 0:n] = nr
        st_ref[pl.ds(r0, nb), n:2 * n] = ni
        return nr, ni

    hr, hi = lax.fori_loop(0, tc, step, (hre_ref[...], him_ref[...]))
    hre_ref[...] = hr
    him_ref[...] = hi
    y = jnp.dot(st_ref[...].astype(BF16), cblk_ref[...], preferred_element_type=F32)
    z = _gelu_tanh(y + d_ref[...] * u)
    gate = jax.nn.sigmoid(jnp.dot(z.astype(BF16), gw_ref[...], preferred_element_type=F32)
                          + gb_ref[...])
    out = z * gate
    for c in range(nslab):
        tm_ref[c] = out[:, c * LANES:(c + 1) * LANES]
    for bi in range(nb):
        for c in range(nslab):
            o_ref[bi, :, c * LANES:(c + 1) * LANES] = tm_ref[c, pl.ds(bi, tc, stride=nb), :]


def _s5(u, a_re, a_im, bblk, cblk, dskip, gw, gb, *, tc):
    nb, s, sw = u.shape
    n = a_re.shape[1]
    rb = tc * nb
    full = lambda a: pl.BlockSpec(a.shape, lambda i: (0,) * a.ndim)
    blk = pl.BlockSpec((nb, tc, sw), lambda i: (0, i, 0))
    return pl.pallas_call(
        functools.partial(_s5_kernel, nb=nb, n=n),
        grid=(s // tc,),
        in_specs=[blk, full(a_re), full(a_im),
                  full(bblk), full(cblk), full(dskip), full(gw), full(gb)],
        out_specs=blk,
        out_shape=jax.ShapeDtypeStruct(u.shape, F32),
        scratch_shapes=[pltpu.VMEM((rb, 2 * n), F32), pltpu.VMEM((nb, n), F32),
                        pltpu.VMEM((nb, n), F32), pltpu.VMEM((sw // LANES, rb, LANES), F32)],
        compiler_params=_params(("arbitrary",), VMEM_LIMIT_BYTES),
        name="s5_scan",
    )(u, a_re, a_im, bblk, cblk, dskip, gw, gb)


def _merge_kernel(x_ref, att_ref, pool_ref, ssm_ref, og_ref, w_ref, o_ref, *, att_w, pool_w):
    og = og_ref[...]
    p1 = att_w + pool_w
    merged = jnp.concatenate([
        _rmsnorm(att_ref[...], og[:, :att_w]),
        _rmsnorm(pool_ref[...], og[:, att_w:p1]),
        _rmsnorm(ssm_ref[...], og[:, p1:])], axis=1).astype(BF16)
    ts = merged.shape[0]
    out = _load_tiled(x_ref, ts) + jnp.dot(merged, w_ref[...], preferred_element_type=F32)
    _store_tiled(o_ref, out, ts)


def _merge(x, att, pool, ssm, og, w, *, ts):
    b, s = x.shape[0], x.shape[1] // SUBLANES
    att_w, pool_w, sw = att.shape[2], pool.shape[2], ssm.shape[2]
    tok = lambda wd: pl.BlockSpec((None, ts, wd), lambda bi, i: (bi, i, 0))
    return pl.pallas_call(
        functools.partial(_merge_kernel, att_w=att_w, pool_w=pool_w),
        grid=(b, s // ts),
        in_specs=[_tiled_spec(ts), tok(att_w), tok(pool_w), tok(sw),
                  pl.BlockSpec(og.shape, lambda bi, i: (0, 0)),
                  pl.BlockSpec(w.shape, lambda bi, i: (0, 0))],
        out_specs=_tiled_spec(ts),
        out_shape=jax.ShapeDtypeStruct(x.shape, F32),
        compiler_params=_params(("arbitrary", "arbitrary"), VMEM_LIMIT_BYTES),
        name="merge_outproj",
    )(x, att, pool, ssm, og, w)


def _topk_rows(a, k, extra=None):
    nrow, nl = a.shape
    row = lax.broadcasted_iota(I32, (nrow, nl), 0).astype(F32)
    out_row = lax.broadcasted_iota(I32, (k, nl), 0)
    vals = jnp.zeros((k, nl), F32)
    sel = jnp.zeros((k, nl), F32)
    for j in range(k):
        m = jnp.max(a, axis=0, keepdims=True)
        ix = jnp.min(jnp.where(a == m, row, float(nrow)), axis=0, keepdims=True)
        hit = row == ix
        if extra is None:
            pick = ix
        else:
            pick = jnp.sum(jnp.where(hit, extra, 0.0), axis=0, keepdims=True)
        vals = jnp.where(out_row == j, m, vals)
        sel = jnp.where(out_row == j, pick, sel)
        a = jnp.where(hit, -jnp.inf, a)
    return vals, sel


def _route_kernel(x_ref, g_ref, wq_ref, keys_ref, h_out, off_out, sh_out, gate_out, q_sc, stage_sc,
                  *, n_half):
    hd = pl.program_id(1)
    qd = 2 * PEER_KEYS

    @pl.when(hd == 0)
    def _():
        tm = q_sc.shape[0]
        h = _rmsnorm(_load_tiled(x_ref, tm), g_ref[...])
        _store_tiled(h_out, h, tm)
        q_sc[...] = jnp.dot(h.astype(BF16), wq_ref[...], preferred_element_type=F32)

    q = q_sc[:, pl.ds(pl.multiple_of(hd * qd, qd), qd)].astype(BF16)
    tops = []
    for i in range(2):
        sc = lax.dot_general(keys_ref[i], q[:, i * PEER_KEYS:(i + 1) * PEER_KEYS], _NT,
                             preferred_element_type=F32)
        tops.append(_topk_rows(sc, PEER_TOPK))
    (s1, i1), (s2, i2) = tops
    k = PEER_TOPK
    sub8 = lax.broadcasted_iota(I32, (SUBLANES, s1.shape[1]), 0)
    cands, ecands = [], []
    a = 0
    while a < k and k // (a + 1) > 1:
        nb = k // (a + 1)
        for b0 in range(0, nb, SUBLANES):
            c = s1[a:a + 1, :] + s2[b0:b0 + SUBLANES, :]
            if b0 + SUBLANES > nb:
                c = jnp.where(sub8 < nb - b0, c, -jnp.inf)
            cands.append(c)
            ecands.append(i1[a:a + 1, :] * float(PEER_KEYS) + i2[b0:b0 + SUBLANES, :])
        a += 1
    cands.append(s1[a:, :] + s2[0:1, :])
    ecands.append(i1[a:, :] * float(PEER_KEYS) + i2[0:1, :])
    best, experts = _topk_rows(jnp.concatenate(cands, axis=0), k,
                               extra=jnp.concatenate(ecands, axis=0))
    w = jnp.exp(best - best[0:1, :])
    rows = pl.ds(pl.multiple_of(hd * k, k), k)
    stage_sc[2, rows, :] = w / jnp.sum(w, axis=0, keepdims=True)
    hi_half = experts < float(n_half)
    stage_sc[0, rows, :] = jnp.where(hi_half, experts, experts - float(n_half)) * float(SUBLANES)
    stage_sc[1, rows, :] = jnp.where(hi_half, 0.0, 16.0)

    @pl.when(hd == pl.num_programs(1) - 1)
    def _():
        off_out[...] = stage_sc[0].T.astype(I32)
        sh_out[...] = stage_sc[1].T
        gate_out[...] = stage_sc[2].T


def _route(x, g, wq, keys, *, tm, n_experts):
    t = x.shape[0] // SUBLANES
    n_heads = keys.shape[0]
    n_sel = n_heads * PEER_TOPK
    hk = pl.BlockSpec((tm, n_sel), lambda i, h: (i, 0))
    shp = lambda dt: jax.ShapeDtypeStruct((t, n_sel), dt)
    tiled = pl.BlockSpec((tm * SUBLANES, LANES), lambda i, h: (i, 0))
    return pl.pallas_call(
        functools.partial(_route_kernel, n_half=n_experts // 2),
        grid=(t // tm, n_heads),
        in_specs=[tiled,
                  pl.BlockSpec(g.shape, lambda i, h: (0, 0)),
                  pl.BlockSpec(wq.shape, lambda i, h: (0, 0)),
                  pl.BlockSpec((None, 2, PEER_KEYS, keys.shape[3]), lambda i, h: (h, 0, 0, 0))],
        out_specs=[tiled, hk, hk, hk],
        out_shape=[jax.ShapeDtypeStruct(x.shape, F32), shp(I32), shp(F32), shp(F32)],
        scratch_shapes=[pltpu.VMEM((tm, wq.shape[1]), F32), pltpu.VMEM((3, n_sel, tm), F32)],
        compiler_params=_params(("arbitrary", "arbitrary"), VMEM_LIMIT_BYTES),
        name="peer_route",
    )(x, g, wq, keys)


def _expert_row(tab_ref, off, sh):
    w = tab_ref[pl.ds(pl.multiple_of(off, SUBLANES), SUBLANES), :]
    return lax.bitcast_convert_type((w << sh) & jnp.uint32(0xFFFF0000), F32)


def _splat_rows(mxu_src, mxu_dst, xlu_src=None, xlu_dst=None, *, tt, hk, also=None):
    eye = (lax.broadcasted_iota(I32, (hk, hk), 0) == lax.broadcasted_iota(I32, (hk, hk), 1))
    ones = jnp.ones((hk, hk), BF16)

    def body(t, carry):
        v = jnp.broadcast_to(mxu_src[pl.ds(t, 1), :], (hk, hk))
        rep = jnp.dot(jnp.where(eye, v, 0.0).astype(BF16), ones, preferred_element_type=F32)
        mxu_dst(t, rep.astype(I32))
        if xlu_src is not None:
            xlu_dst(t, jnp.broadcast_to(xlu_src[pl.ds(t, 1), :], (hk, hk)).T)
        if also is not None:
            also(t)
        return carry

    lax.fori_loop(0, tt, body, 0, unroll=8)


def _splat(rep_ref, row):
    return jnp.broadcast_to(rep_ref[pl.ds(row, 1), :], (SUBLANES, LANES))


def _peer_u_kernel(*refs, tt, hk):
    n_off = tt * hk // SUBLANES
    off_refs = [refs[0].at[pl.ds(j * n_off, n_off)] for j in range(SUBLANES)]
    sh_ref, shn_ref, h_ref, gate_ref, tab_ref, coef_ref, part_ref, srep_ref, pre_ref = refs[1:]
    ngrp = hk // SUBLANES

    def store_srep(t, block):
        srep_ref[pl.ds(pl.multiple_of(t * hk, hk), hk), :] = block

    @pl.when(pl.program_id(0) == 0)
    def _():
        _splat_rows(sh_ref, store_srep, tt=tt, hk=hk)

    chunk = U_EDGES_PER_DOT
    sel = (lax.broadcasted_iota(I32, (chunk, chunk * SUBLANES), 1) // SUBLANES
           == lax.broadcasted_iota(I32, (chunk, chunk * SUBLANES), 0)).astype(BF16)

    def tok(t, carry):
        h = h_ref[t]
        base = pl.multiple_of(t * hk, hk)
        for c in range(hk // chunk):
            e0 = base + c * chunk
            ps = []
            for k0 in range(0, chunk, SUBLANES):
                sblk = srep_ref[pl.ds(e0 + k0, SUBLANES), :].astype(U32)
                for j in range(SUBLANES):
                    k = k0 + j
                    sv = jnp.broadcast_to(sblk[j:j + 1, :], (SUBLANES, LANES))
                    ps.append(_expert_row(
                        tab_ref, off_refs[j][t * ngrp + (c * chunk + k) // SUBLANES], sv) * h)
            stacked = jnp.concatenate(ps, axis=0).astype(BF16)
            part_ref[pl.ds(e0, chunk), :] = jnp.dot(sel, stacked, preferred_element_type=F32)
        return carry

    lax.fori_loop(0, tt, tok, 0, unroll=16)

    def lane_sums(t):
        p = part_ref[pl.ds(pl.multiple_of(t * hk, hk), hk), :]
        pre_ref[pl.ds(t, 1), :] = jnp.sum(p.T, axis=0, keepdims=True)

    _splat_rows(shn_ref, store_srep, tt=tt, hk=hk, also=lane_sums)
    coef_ref[...] = gate_ref[...] * _gelu_tanh(pre_ref[...])


def _peer_v_kernel(*refs, tt, hk, n_acc):
    n_off = tt * hk // SUBLANES
    off_refs = [refs[0].at[pl.ds(j * n_off, n_off)] for j in range(SUBLANES)]
    sh_ref, coef_ref, x_ref, tab_ref, o_ref, rep_ref = refs[1:]
    ngrp = hk // SUBLANES

    def store_coef(t, block):
        rep_ref[pl.ds(2 * t * hk, hk, stride=2), :] = block

    def store_shift(t, block):
        rep_ref[pl.ds(2 * t * hk + 1, hk, stride=2), :] = lax.bitcast_convert_type(block, F32)

    _splat_rows(sh_ref, store_shift, coef_ref, store_coef, tt=tt, hk=hk)

    def tok(t, carry):
        accs = [None] * n_acc
        e0 = 2 * t * hk
        per_load = SUBLANES // 2
        for k0 in range(0, hk, per_load):
            blk = rep_ref[pl.ds(pl.multiple_of(e0 + 2 * k0, SUBLANES), SUBLANES), :]
            for j in range(per_load):
                k = k0 + j
                bcast = lambda r: jnp.broadcast_to(blk[r:r + 1, :], (SUBLANES, LANES))
                off = off_refs[k % SUBLANES][t * ngrp + k // SUBLANES]
                sv = lax.bitcast_convert_type(bcast(2 * j + 1), U32)
                term = _splat(rep_ref, e0 + 2 * k) * _expert_row(tab_ref, off, sv)
                accs[k % n_acc] = term if accs[k % n_acc] is None else accs[k % n_acc] + term
        while len(accs) > 1:
            accs = [accs[j] + accs[j + len(accs) // 2] for j in range(len(accs) // 2)]
        o_ref[t] = x_ref[t] + accs[0]
        return carry

    lax.fori_loop(0, tt, tok, 0, unroll=8)


def _resident(a):
    return pl.BlockSpec(a.shape, lambda i: (0,) * a.ndim, pipeline_mode=pl.Buffered(1))


def _split_offsets(off, tt):
    t, hk = off.shape
    grouped = off.reshape(t // tt, tt, hk // SUBLANES, SUBLANES)
    return jnp.transpose(grouped, (0, 3, 1, 2)).reshape(-1)


def _peer_u(off, sh, h, gates, tab, *, tt):
    t, hk = off.shape
    assert hk == LANES and h.shape[1:] == (SUBLANES, LANES)
    smem = pl.BlockSpec((tt * hk,), lambda i: (i,), memory_space=pltpu.SMEM)
    compact = pl.BlockSpec((tt, hk), lambda i: (i, 0))
    last = t // tt - 1
    nxt = pl.BlockSpec((tt, hk), lambda i: (jnp.minimum(i + 1, last), 0))
    return pl.pallas_call(
        functools.partial(_peer_u_kernel, tt=tt, hk=hk),
        grid=(t // tt,),
        in_specs=[
            smem, compact, nxt, pl.BlockSpec((tt, SUBLANES, LANES), lambda i: (i, 0, 0)), compact,
            _resident(tab)],
        out_specs=compact,
        out_shape=jax.ShapeDtypeStruct((t, hk), F32),
        scratch_shapes=[pltpu.VMEM((tt * hk, LANES), F32), pltpu.VMEM((tt * hk, LANES), I32),
                        pltpu.VMEM((tt, hk), F32)],
        compiler_params=_params(("arbitrary",), VMEM_LIMIT_BYTES),
        name="peer_u",
    )(_split_offsets(off, tt), sh, sh, h, gates, tab)


def _peer_v(off, sh, coef, x, tab, *, tt):
    t, hk = off.shape
    assert hk == LANES and x.shape[1:] == (SUBLANES, LANES)
    smem = pl.BlockSpec((tt * hk,), lambda i: (i,), memory_space=pltpu.SMEM)
    tile = pl.BlockSpec((tt, SUBLANES, LANES), lambda i: (i, 0, 0))
    compact = pl.BlockSpec((tt, hk), lambda i: (i, 0))
    return pl.pallas_call(
        functools.partial(_peer_v_kernel, tt=tt, hk=hk, n_acc=4),
        grid=(t // tt,),
        in_specs=[smem, compact, compact, tile, _resident(tab)],
        out_specs=tile,
        out_shape=jax.ShapeDtypeStruct(x.shape, F32),
        scratch_shapes=[pltpu.VMEM((2 * tt * hk, LANES), F32)],
        compiler_params=_params(("arbitrary",), VMEM_LIMIT_BYTES),
        name="peer_v",
    )(_split_offsets(off, tt), sh, coef, x, tab)


def _pack_kernel(hi_ref, lo_ref, o_ref):
    def bf16_bits(x):
        bits = lax.bitcast_convert_type(x, U32)
        r = bits + (jnp.uint32(0x7FFF) + ((bits >> 16) & jnp.uint32(1)))
        return jnp.where(x != x, jnp.uint32(0x7FC00000), r)

    w = (bf16_bits(hi_ref[...]) & jnp.uint32(0xFFFF0000)) | (bf16_bits(lo_ref[...]) >> 16)
    _store_tiled(o_ref, w, w.shape[0])


def _pack_table(tab, rows=256):
    n, d = tab.shape
    assert d == SUBLANES * LANES and (n // 2) % rows == 0
    nblk = n // 2 // rows
    return pl.pallas_call(
        _pack_kernel,
        grid=(nblk,),
        in_specs=[pl.BlockSpec((rows, d), lambda i: (i, 0)),
                  pl.BlockSpec((rows, d), lambda i: (i + nblk, 0))],
        out_specs=pl.BlockSpec((rows * SUBLANES, LANES), lambda i: (i, 0)),
        out_shape=jax.ShapeDtypeStruct((n // 2 * SUBLANES, LANES), U32),
        compiler_params=_params(("arbitrary",)),
        name="pack_table",
    )(tab, tab)


def _block_diag(blocks):
    g, r, c = blocks.shape
    eye = jnp.eye(g, dtype=blocks.dtype)
    return (eye[:, None, :, None] * blocks[:, :, None, :]).reshape(g * r, g * c)


def _s5_params(a_re, a_im, log_dt, b_re, b_im, c_re, c_im):
    a = lax.complex(a_re, a_im)
    dt = jnp.exp(log_dt)[:, None]
    a_bar = jnp.exp(a * dt)
    b_bar = ((a_bar - 1.0) / a)[..., None] * lax.complex(b_re, b_im)
    bt = jnp.swapaxes(b_bar, 1, 2)
    bblk = jnp.concatenate([_block_diag(jnp.real(bt)), _block_diag(jnp.imag(bt))], axis=1)
    ct_re = jnp.swapaxes(c_re, 1, 2)
    ct_im = jnp.swapaxes(c_im, 1, 2)
    cblk = jnp.concatenate([_block_diag(ct_re), -_block_diag(ct_im)], axis=0)
    n = a_bar.size
    return (jnp.real(a_bar).reshape(1, n), jnp.imag(a_bar).reshape(1, n),
            bblk.astype(BF16), cblk.astype(BF16))


def _tiles(s, t):
    ts = min(512, s)
    blk = min(512, s)
    tc = min(64, s)
    tm = min(512, t)
    tt = min(128, t)
    return ts, blk, tc, tm, tt


def _layer(x, norm1_g, w_in, f_bias, q_gain, k_gain, pool_w, pool_scale,
           a_re, a_im, log_dt, b_re, b_im, c_re, c_im, ssm_d, glu_w, glu_b,
           out_norm_g, w_out, norm2_g, peer_wq, peer_keys, peer_u, peer_v):
    b, s, d = x.shape[0], x.shape[1] // SUBLANES, SUBLANES * LANES
    t = b * s
    n_heads = f_bias.shape[0]
    att_w = n_heads * HEAD_DIM
    pool_width = pool_scale.shape[0]
    ssm_w = ssm_d.shape[0]
    ts, blk, tc, tm, tt = _tiles(s, t)
    row = lambda v: v.reshape(1, -1)

    c0, c1 = 3 * att_w, 3 * att_w + n_heads
    wf = jnp.zeros((d, LANES), F32).at[:, :n_heads].set(w_in[:, c0:c1])
    fb = jnp.zeros((1, LANES), F32).at[0, :n_heads].set(f_bias)
    seg = _block_diag(jnp.full((n_heads, HEAD_DIM, HEAD_DIM), 1.0 / HEAD_DIM, F32)).astype(BF16)
    q, k, v, c_tok, pool, u_ssm = _inproj(
        x, row(norm1_g), w_in[:, :c0].astype(BF16), wf,
        w_in[:, c1:c1 + pool_width].astype(BF16), w_in[:, c1 + pool_width:].astype(BF16), fb,
        row(jnp.tile(q_gain, n_heads) * HEAD_DIM ** -0.5), row(jnp.tile(k_gain, n_heads)),
        seg, _block_diag(pool_w).astype(BF16), row(pool_scale), ts=ts)
    att = _attention(q, k, v, c_tok, jnp.swapaxes(c_tok, 1, 2), blk=blk)
    are, aim, bblk, cblk = _s5_params(a_re, a_im, log_dt, b_re, b_im, c_re, c_im)
    ssm = _s5(u_ssm, are, aim, bblk, cblk, row(ssm_d), glu_w.astype(BF16), row(glu_b), tc=tc)
    x1 = _merge(x, att, pool, ssm, row(out_norm_g), w_out.astype(BF16), ts=ts)

    n_experts = peer_u.shape[0]
    kh = peer_keys.astype(BF16)
    h2, off, sh, gates = _route(x1.reshape(t * SUBLANES, LANES), row(norm2_g),
                                peer_wq.astype(BF16), kh, tm=tm, n_experts=n_experts)
    tiles = lambda a: a.reshape(t, SUBLANES, LANES)
    coef = _peer_u(off, sh, tiles(h2), gates, _pack_table(peer_u), tt=tt)
    x2 = _peer_v(off, sh, coef, tiles(x1), _pack_table(peer_v), tt=tt)
    return x2.reshape(b, s * SUBLANES, LANES)


def kernel(x, norm1_g, w_in, f_bias, q_gain, k_gain, pool_w, pool_scale, ssm_a_re, ssm_a_im,
           ssm_log_dt, ssm_b_re, ssm_b_im, ssm_c_re, ssm_c_im, ssm_d, glu_w, glu_b, out_norm_g,
           w_out, norm2_g, peer_wq, peer_keys, peer_u, peer_v):
    per_layer = (norm1_g, w_in, f_bias, q_gain, k_gain, pool_w, pool_scale, ssm_a_re, ssm_a_im,
                 ssm_log_dt, ssm_b_re, ssm_b_im, ssm_c_re, ssm_c_im, ssm_d, glu_w, glu_b,
                 out_norm_g, w_out, norm2_g, peer_wq, peer_keys, peer_u, peer_v)
    b, s, d = x.shape
    assert d == SUBLANES * LANES
    x = x.reshape(b, s * SUBLANES, LANES)
    for l in range(norm1_g.shape[0]):
        x = _layer(x, *[p[l] for p in per_layer])
    return x.reshape(b, s, d)
```

```python
import functools
import math

import jax
import jax.numpy as jnp
from jax import lax
from jax.experimental import pallas as pl
from jax.experimental.pallas import tpu as pltpu

F32 = jnp.float32
BF16 = jnp.bfloat16
U32 = jnp.uint32
I32 = jnp.int32

EPS = 1e-6
HEAD_DIM = 64
POOL_WINDOWS = (2, 4, 8, 16)
POOL_HALO = 16
PEER_KEYS = 128
PEER_TOPK = 16
U_EDGES_PER_DOT = 16

LANES = 128
SUBLANES = 8
VMEM_LIMIT_BYTES = 56 * 1024 * 1024

_NT = (((1,), (1,)), ((), ()))


def _rmsnorm(x, g):
    return x * lax.rsqrt(jnp.mean(x * x, axis=-1, keepdims=True) + EPS) * g


def _gelu_tanh(x):
    c = math.sqrt(2.0 / math.pi)
    return x * (0.5 * (1.0 + jnp.tanh(c * (x + 0.044715 * (x * x * x)))))


def _params(sem, vmem=None):
    return pltpu.CompilerParams(dimension_semantics=sem, vmem_limit_bytes=vmem)


def _load_tiled(ref, n):
    return jnp.concatenate([ref[pl.ds(s, n, stride=SUBLANES), :] for s in range(SUBLANES)], axis=1)


def _store_tiled(ref, val, n):
    for s in range(SUBLANES):
        ref[pl.ds(s, n, stride=SUBLANES), :] = val[:, s * LANES:(s + 1) * LANES]


def _inproj_kernel(x_ref, g_ref, wqkv_ref, wf_ref, wpool_ref, wssm_ref, fb_ref,
                   qg_ref, kg_ref, seg_ref, pmix_ref, pscale_ref,
                   q_out, k_out, v_out, c_out, pool_out, ssm_out, carry_ref, halo_ref,
                   *, att_w, n_heads):
    i = pl.program_id(1)

    @pl.when(i == 0)
    def _():
        carry_ref[...] = jnp.zeros_like(carry_ref)
        halo_ref[...] = jnp.zeros_like(halo_ref)

    ts = x_ref.shape[0] // SUBLANES
    x = _load_tiled(x_ref, ts)
    h = _rmsnorm(x, g_ref[...])
    hb = h.astype(BF16)
    qkv = jnp.dot(hb, wqkv_ref[...], preferred_element_type=F32)
    seg = seg_ref[...]

    def headnorm(t, gain):
        ms = jnp.dot((t * t).astype(BF16), seg, preferred_element_type=F32)
        return t * lax.rsqrt(ms + EPS) * gain

    q_out[...] = headnorm(qkv[:, :att_w], qg_ref[...]).astype(BF16)
    k_out[...] = headnorm(qkv[:, att_w:2 * att_w], kg_ref[...]).astype(BF16)
    v_out[...] = qkv[:, 2 * att_w:].astype(BF16)
    u_pool = jnp.dot(hb, wpool_ref[...], preferred_element_type=F32)
    pool_out[...] = _pool_mix(u_pool, halo_ref[...], i * ts, pmix_ref[...], pscale_ref[...])
    halo_ref[...] = u_pool[ts - POOL_HALO:, :]
    ssm_out[...] = jnp.dot(hb, wssm_ref[...], preferred_element_type=F32)

    f = jnp.dot(h, wf_ref[...], precision=lax.Precision.HIGHEST,
                preferred_element_type=F32) + fb_ref[...]
    ls = jnp.minimum(f, 0.0) - jnp.log1p(jnp.exp(-jnp.abs(f)))
    r = lax.broadcasted_iota(I32, (ts, ts), 0)
    c = lax.broadcasted_iota(I32, (ts, ts), 1)
    tri = jnp.where(c <= r, 1.0, 0.0).astype(F32)
    cs = jnp.dot(tri, ls, precision=lax.Precision.HIGHEST,
                 preferred_element_type=F32) + carry_ref[0:1, :]
    carry_ref[...] = jnp.broadcast_to(cs[ts - 1:ts, :], carry_ref.shape)
    c_out[...] = cs[:, :n_heads]


def _tiled_spec(ts):
    return pl.BlockSpec((None, ts * SUBLANES, LANES), lambda bi, i: (bi, i, 0))


def _inproj(x, g, wqkv, wf, wpool, wssm, fb, qg, kg, seg, pmix, pscale, *, ts):
    b, s = x.shape[0], x.shape[1] // SUBLANES
    att_w = seg.shape[0]
    n_heads = att_w // HEAD_DIM
    pw = wpool.shape[1]
    sw = wssm.shape[1]
    full = lambda a: pl.BlockSpec(a.shape, lambda bi, i: (0,) * a.ndim)
    tok = lambda w: pl.BlockSpec((None, ts, w), lambda bi, i: (bi, i, 0))
    return pl.pallas_call(
        functools.partial(_inproj_kernel, att_w=att_w, n_heads=n_heads),
        grid=(b, s // ts),
        in_specs=[_tiled_spec(ts), full(g), full(wqkv), full(wf), full(wpool), full(wssm),
                  full(fb), full(qg), full(kg), full(seg), full(pmix), full(pscale)],
        out_specs=[tok(att_w), tok(att_w), tok(att_w), tok(n_heads), tok(pw), tok(sw)],
        out_shape=[jax.ShapeDtypeStruct((b, s, att_w), BF16),
                   jax.ShapeDtypeStruct((b, s, att_w), BF16),
                   jax.ShapeDtypeStruct((b, s, att_w), BF16),
                   jax.ShapeDtypeStruct((b, s, n_heads), F32),
                   jax.ShapeDtypeStruct((b, s, pw), F32),
                   jax.ShapeDtypeStruct((b, s, sw), F32)],
        scratch_shapes=[pltpu.VMEM((SUBLANES, LANES), F32), pltpu.VMEM((POOL_HALO, pw), F32)],
        compiler_params=_params(("arbitrary", "arbitrary"), VMEM_LIMIT_BYTES),
        name="inproj",
    )(x, g, wqkv, wf, wpool, wssm, fb, qg, kg, seg, pmix, pscale)


def _attn_kernel(q_ref, k_ref, v_ref, ct_ref, cr_ref, o_ref, m_sc, l_sc, acc_sc, s_sc, *, blk):
    p = pl.program_id(1)
    i = pl.program_id(2)
    q = q_ref[...]
    lane = lax.broadcasted_iota(I32, (blk, LANES), 1)
    ct = ct_ref[...]
    hl = lax.broadcasted_iota(I32, ct.shape, 1)
    heads = (2 * p, 2 * p + 1)
    qm = (jnp.where(lane < HEAD_DIM, q, jnp.zeros_like(q)),
          jnp.where(lane >= HEAD_DIM, q, jnp.zeros_like(q)))
    cq = tuple(jnp.sum(jnp.where(hl == hd, ct, 0.0), axis=1, keepdims=True) for hd in heads)
    m_sc[...] = jnp.full(m_sc.shape, -jnp.inf, F32)
    l_sc[...] = jnp.zeros(l_sc.shape, F32)
    acc_sc[...] = jnp.zeros(acc_sc.shape, F32)

    def qk(kb):
        kblk = k_ref[pl.ds(pl.multiple_of(kb * blk, blk), blk), :]
        return [lax.dot_general(qm[hh], kblk, _NT, preferred_element_type=F32) for hh in range(2)]

    def step(kb, s_cur, causal):
        start = pl.multiple_of(kb * blk, blk)
        vblk = v_ref[pl.ds(start, blk), :]
        for hh in range(2):
            ck = cr_ref[pl.ds(heads[hh], 1), pl.ds(start, blk)]
            z = (s_cur[hh] + cq[hh]) - ck
            if causal:
                row = lax.broadcasted_iota(I32, (blk, blk), 0)
                col = lax.broadcasted_iota(I32, (blk, blk), 1)
                z = jnp.where(col <= row, z, -jnp.inf)
            m_old = m_sc[hh]
            m_new = jnp.maximum(m_old, jnp.max(z, axis=1, keepdims=True))
            alpha = jnp.exp(m_old - m_new)
            pe = jnp.exp(z - m_new)
            l_sc[hh] = alpha * l_sc[hh] + jnp.sum(pe, axis=1, keepdims=True)
            acc_sc[hh] = alpha * acc_sc[hh] + jnp.dot(pe.astype(BF16), vblk,
                                                      preferred_element_type=F32)
            m_sc[hh] = m_new

    s0 = qk(0)
    s_sc[0] = s0[0]
    s_sc[1] = s0[1]

    def body(kb, carry):
        s_cur = [s_sc[0], s_sc[1]]
        s_next = qk(kb + 1)
        step(kb, s_cur, False)
        s_sc[0] = s_next[0]
        s_sc[1] = s_next[1]
        return carry

    lax.fori_loop(0, i, body, 0)
    step(i, [s_sc[0], s_sc[1]], True)
    o_ref[...] = jnp.where(lane < HEAD_DIM, acc_sc[0] / l_sc[0], acc_sc[1] / l_sc[1])


def _attention(q, k, v, c_tok, c_row, *, blk):
    b, s, w = q.shape
    n_heads = c_tok.shape[2]
    return pl.pallas_call(
        functools.partial(_attn_kernel, blk=blk),
        grid=(b, w // LANES, s // blk),
        in_specs=[pl.BlockSpec((None, blk, LANES), lambda bi, p, i: (bi, i, p)),
                  pl.BlockSpec((None, s, LANES), lambda bi, p, i: (bi, 0, p)),
                  pl.BlockSpec((None, s, LANES), lambda bi, p, i: (bi, 0, p)),
                  pl.BlockSpec((None, blk, n_heads), lambda bi, p, i: (bi, i, 0)),
                  pl.BlockSpec((None, n_heads, s), lambda bi, p, i: (bi, 0, 0))],
        out_specs=pl.BlockSpec((None, blk, LANES), lambda bi, p, i: (bi, i, p)),
        out_shape=jax.ShapeDtypeStruct((b, s, w), F32),
        scratch_shapes=[pltpu.VMEM((2, blk, 1), F32), pltpu.VMEM((2, blk, 1), F32),
                        pltpu.VMEM((2, blk, LANES), F32), pltpu.VMEM((2, blk, blk), F32)],
        compiler_params=_params(("arbitrary", "arbitrary", "arbitrary"), VMEM_LIMIT_BYTES),
        name="fox_attention",
    )(q, k, v, c_tok, c_row)


def _pool_mix(cur, prev, first_pos, w, scale):
    ts, pw = cur.shape
    group_dim = pw // len(POOL_WINDOWS)
    ext = jnp.concatenate([prev, cur], axis=0)
    pos = first_pos + lax.broadcasted_iota(I32, (ts, 1), 0)
    grp = lax.broadcasted_iota(I32, (ts, pw), 1) // group_dim
    acc = cur
    pooled = jnp.zeros_like(cur)
    for j in range(1, max(POOL_WINDOWS)):
        acc = acc + ext[POOL_HALO - j:POOL_HALO - j + ts, :]
        win = j + 1
        if win in POOL_WINDOWS:
            cnt = jnp.minimum(pos + 1, win).astype(F32)
            pooled = jnp.where(grp == POOL_WINDOWS.index(win), acc / cnt, pooled)
    pooled = pooled - cur
    return jnp.dot(pooled.astype(BF16), w, preferred_element_type=F32) * scale


def _s5_kernel(u_ref, are_ref, aim_ref, bblk_ref, cblk_ref, d_ref, gw_ref, gb_ref,
               o_ref, st_ref, hre_ref, him_ref, tm_ref, *, nb, n):
    @pl.when(pl.program_id(0) == 0)
    def _():
        hre_ref[...] = jnp.zeros_like(hre_ref)
        him_ref[...] = jnp.zeros_like(him_ref)

    tc = u_ref.shape[1]
    nslab = tm_ref.shape[0]
    for bi in range(nb):
        for c in range(nslab):
            tm_ref[c, pl.ds(bi, tc, stride=nb), :] = u_ref[bi, :, c * LANES:(c + 1) * LANES]
    u = jnp.concatenate([tm_ref[c] for c in range(nslab)], axis=1)
    st_ref[...] = jnp.dot(u.astype(BF16), bblk_ref[...], preferred_element_type=F32)
    ar = jnp.broadcast_to(are_ref[...], (nb, n))
    ai = jnp.broadcast_to(aim_ref[...], (nb, n))

    def step(t, carry):
        hr, hi = carry
        r0 = pl.multiple_of(t * nb, nb)
        br = st_ref[pl.ds(r0, nb), 0:n]
        bi = st_ref[pl.ds(r0, nb), n:2 * n]
        nr = (ar * hr - ai * hi) + br
        ni = (ar * hi + ai * hr) + bi
        st_ref[pl.ds(r0, nb), 0:n] = nr
        st_ref[pl.ds(r0, nb), n:2 * n] = ni
        return nr, ni

    hr, hi = lax.fori_loop(0, tc, step, (hre_ref[...], him_ref[...]))
    hre_ref[...] = hr
    him_ref[...] = hi
    y = jnp.dot(st_ref[...].astype(BF16), cblk_ref[...], preferred_element_type=F32)
    z = _gelu_tanh(y + d_ref[...] * u)
    gate = jax.nn.sigmoid(jnp.dot(z.astype(BF16), gw_ref[...], preferred_element_type=F32)
                          + gb_ref[...])
    out = z * gate
    for c in range(nslab):
        tm_ref[c] = out[:, c * LANES:(c + 1) * LANES]
    for bi in range(nb):
        for c in range(nslab):
            o_ref[bi, :, c * LANES:(c + 1) * LANES] = tm_ref[c, pl.ds(bi, tc, stride=nb), :]


def _s5(u, a_re, a_im, bblk, cblk, dskip, gw, gb, *, tc):
    nb, s, sw = u.shape
    n = a_re.shape[1]
    rb = tc * nb
    full = lambda a: pl.BlockSpec(a.shape, lambda i: (0,) * a.ndim)
    blk = pl.BlockSpec((nb, tc, sw), lambda i: (0, i, 0))
    return pl.pallas_call(
        functools.partial(_s5_kernel, nb=nb, n=n),
        grid=(s // tc,),
        in_specs=[blk, full(a_re), full(a_im),
                  full(bblk), full(cblk), full(dskip), full(gw), full(gb)],
        out_specs=blk,
        out_shape=jax.ShapeDtypeStruct(u.shape, F32),
        scratch_shapes=[pltpu.VMEM((rb, 2 * n), F32), pltpu.VMEM((nb, n), F32),
                        pltpu.VMEM((nb, n), F32), pltpu.VMEM((sw // LANES, rb, LANES), F32)],
        compiler_params=_params(("arbitrary",), VMEM_LIMIT_BYTES),
        name="s5_scan",
    )(u, a_re, a_im, bblk, cblk, dskip, gw, gb)


def _merge_kernel(x_ref, att_ref, pool_ref, ssm_ref, og_ref, w_ref, o_ref, *, att_w, pool_w):
    og = og_ref[...]
    p1 = att_w + pool_w
    merged = jnp.concatenate([
        _rmsnorm(att_ref[...], og[:, :att_w]),
        _rmsnorm(pool_ref[...], og[:, att_w:p1]),
        _rmsnorm(ssm_ref[...], og[:, p1:])], axis=1).astype(BF16)
    ts = merged.shape[0]
    out = _load_tiled(x_ref, ts) + jnp.dot(merged, w_ref[...], preferred_element_type=F32)
    _store_tiled(o_ref, out, ts)


def _merge(x, att, pool, ssm, og, w, *, ts):
    b, s = x.shape[0], x.shape[1] // SUBLANES
    att_w, pool_w, sw = att.shape[2], pool.shape[2], ssm.shape[2]
    tok = lambda wd: pl.BlockSpec((None, ts, wd), lambda bi, i: (bi, i, 0))
    return pl.pallas_call(
        functools.partial(_merge_kernel, att_w=att_w, pool_w=pool_w),
        grid=(b, s // ts),
        in_specs=[_tiled_spec(ts), tok(att_w), tok(pool_w), tok(sw),
                  pl.BlockSpec(og.shape, lambda bi, i: (0, 0)),
                  pl.BlockSpec(w.shape, lambda bi, i: (0, 0))],
        out_specs=_tiled_spec(ts),
        out_shape=jax.ShapeDtypeStruct(x.shape, F32),
        compiler_params=_params(("arbitrary", "arbitrary"), VMEM_LIMIT_BYTES),
        name="merge_outproj",
    )(x, att, pool, ssm, og, w)


def _topk_rows(a, k, extra=None):
    nrow, nl = a.shape
    row = lax.broadcasted_iota(I32, (nrow, nl), 0).astype(F32)
    out_row = lax.broadcasted_iota(I32, (k, nl), 0)
    vals = jnp.zeros((k, nl), F32)
    sel = jnp.zeros((k, nl), F32)
    for j in range(k):
        m = jnp.max(a, axis=0, keepdims=True)
        ix = jnp.min(jnp.where(a == m, row, float(nrow)), axis=0, keepdims=True)
        hit = row == ix
        if extra is None:
            pick = ix
        else:
            pick = jnp.sum(jnp.where(hit, extra, 0.0), axis=0, keepdims=True)
        vals = jnp.where(out_row == j, m, vals)
        sel = jnp.where(out_row == j, pick, sel)
        a = jnp.where(hit, -jnp.inf, a)
    return vals, sel


def _route_kernel(x_ref, g_ref, wq_ref, keys_ref, h_out, off_out, sh_out, gate_out, q_sc, stage_sc,
                  *, n_half):
    hd = pl.program_id(1)
    qd = 2 * PEER_KEYS

    @pl.when(hd == 0)
    def _():
        tm = q_sc.shape[0]
        h = _rmsnorm(_load_tiled(x_ref, tm), g_ref[...])
        _store_tiled(h_out, h, tm)
        q_sc[...] = jnp.dot(h.astype(BF16), wq_ref[...], preferred_element_type=F32)

    q = q_sc[:, pl.ds(pl.multiple_of(hd * qd, qd), qd)].astype(BF16)
    tops = []
    for i in range(2):
        sc = lax.dot_general(keys_ref[i], q[:, i * PEER_KEYS:(i + 1) * PEER_KEYS], _NT,
                             preferred_element_type=F32)
        tops.append(_topk_rows(sc, PEER_TOPK))
    (s1, i1), (s2, i2) = tops
    k = PEER_TOPK
    sub8 = lax.broadcasted_iota(I32, (SUBLANES, s1.shape[1]), 0)
    cands, ecands = [], []
    a = 0
    while a < k and k // (a + 1) > 1:
        nb = k // (a + 1)
        for b0 in range(0, nb, SUBLANES):
            c = s1[a:a + 1, :] + s2[b0:b0 + SUBLANES, :]
            if b0 + SUBLANES > nb:
                c = jnp.where(sub8 < nb - b0, c, -jnp.inf)
            cands.append(c)
            ecands.append(i1[a:a + 1, :] * float(PEER_KEYS) + i2[b0:b0 + SUBLANES, :])
        a += 1
    cands.append(s1[a:, :] + s2[0:1, :])
    ecands.append(i1[a:, :] * float(PEER_KEYS) + i2[0:1, :])
    best, experts = _topk_rows(jnp.concatenate(cands, axis=0), k,
                               extra=jnp.concatenate(ecands, axis=0))
    w = jnp.exp(best - best[0:1, :])
    rows = pl.ds(pl.multiple_of(hd * k, k), k)
    stage_sc[2, rows, :] = w / jnp.sum(w, axis=0, keepdims=True)
    hi_half = experts < float(n_half)
    stage_sc[0, rows, :] = jnp.where(hi_half, experts, experts - float(n_half)) * float(SUBLANES)
    stage_sc[1, rows, :] = jnp.where(hi_half, 0.0, 16.0)

    @pl.when(hd == pl.num_programs(1) - 1)
    def _():
        off_out[...] = stage_sc[0].T.astype(I32)
        sh_out[...] = stage_sc[1].T
        gate_out[...] = stage_sc[2].T


def _route(x, g, wq, keys, *, tm, n_experts):
    t = x.shape[0] // SUBLANES
    n_heads = keys.shape[0]
    n_sel = n_heads * PEER_TOPK
    hk = pl.BlockSpec((tm, n_sel), lambda i, h: (i, 0))
    shp = lambda dt: jax.ShapeDtypeStruct((t, n_sel), dt)
    tiled = pl.BlockSpec((tm * SUBLANES, LANES), lambda i, h: (i, 0))
    return pl.pallas_call(
        functools.partial(_route_kernel, n_half=n_experts // 2),
        grid=(t // tm, n_heads),
        in_specs=[tiled,
                  pl.BlockSpec(g.shape, lambda i, h: (0, 0)),
                  pl.BlockSpec(wq.shape, lambda i, h: (0, 0)),
                  pl.BlockSpec((None, 2, PEER_KEYS, keys.shape[3]), lambda i, h: (h, 0, 0, 0))],
        out_specs=[tiled, hk, hk, hk],
        out_shape=[jax.ShapeDtypeStruct(x.shape, F32), shp(I32), shp(F32), shp(F32)],
        scratch_shapes=[pltpu.VMEM((tm, wq.shape[1]), F32), pltpu.VMEM((3, n_sel, tm), F32)],
        compiler_params=_params(("arbitrary", "arbitrary"), VMEM_LIMIT_BYTES),
        name="peer_route",
    )(x, g, wq, keys)


def _expert_row(tab_ref, off, sh):
    w = tab_ref[pl.ds(pl.multiple_of(off, SUBLANES), SUBLANES), :]
    return lax.bitcast_convert_type((w << sh) & jnp.uint32(0xFFFF0000), F32)


def _splat_rows(mxu_src, mxu_dst, xlu_src=None, xlu_dst=None, *, tt, hk, also=None):
    eye = (lax.broadcasted_iota(I32, (hk, hk), 0) == lax.broadcasted_iota(I32, (hk, hk), 1))
    ones = jnp.ones((hk, hk), BF16)

    def body(t, carry):
        v = jnp.broadcast_to(mxu_src[pl.ds(t, 1), :], (hk, hk))
        rep = jnp.dot(jnp.where(eye, v, 0.0).astype(BF16), ones, preferred_element_type=F32)
        mxu_dst(t, rep.astype(I32))
        if xlu_src is not None:
            xlu_dst(t, jnp.broadcast_to(xlu_src[pl.ds(t, 1), :], (hk, hk)).T)
        if also is not None:
            also(t)
        return carry

    lax.fori_loop(0, tt, body, 0, unroll=8)


def _splat(rep_ref, row):
    return jnp.broadcast_to(rep_ref[pl.ds(row, 1), :], (SUBLANES, LANES))


def _peer_u_kernel(*refs, tt, hk):
    n_off = tt * hk // SUBLANES
    off_refs = [refs[0].at[pl.ds(j * n_off, n_off)] for j in range(SUBLANES)]
    sh_ref, shn_ref, h_ref, gate_ref, tab_ref, coef_ref, part_ref, srep_ref, pre_ref = refs[1:]
    ngrp = hk // SUBLANES

    def store_srep(t, block):
        srep_ref[pl.ds(pl.multiple_of(t * hk, hk), hk), :] = block

    @pl.when(pl.program_id(0) == 0)
    def _():
        _splat_rows(sh_ref, store_srep, tt=tt, hk=hk)

    chunk = U_EDGES_PER_DOT
    sel = (lax.broadcasted_iota(I32, (chunk, chunk * SUBLANES), 1) // SUBLANES
           == lax.broadcasted_iota(I32, (chunk, chunk * SUBLANES), 0)).astype(BF16)

    def tok(t, carry):
        h = h_ref[t]
        base = pl.multiple_of(t * hk, hk)
        for c in range(hk // chunk):
            e0 = base + c * chunk
            ps = []
            for k0 in range(0, chunk, SUBLANES):
                sblk = srep_ref[pl.ds(e0 + k0, SUBLANES), :].astype(U32)
                for j in range(SUBLANES):
                    k = k0 + j
                    sv = jnp.broadcast_to(sblk[j:j + 1, :], (SUBLANES, LANES))
                    ps.append(_expert_row(
                        tab_ref, off_refs[j][t * ngrp + (c * chunk + k) // SUBLANES], sv) * h)
            stacked = jnp.concatenate(ps, axis=0).astype(BF16)
            part_ref[pl.ds(e0, chunk), :] = jnp.dot(sel, stacked, preferred_element_type=F32)
        return carry

    lax.fori_loop(0, tt, tok, 0, unroll=32)

    def lane_sums(t):
        p = part_ref[pl.ds(pl.multiple_of(t * hk, hk), hk), :]
        pre_ref[pl.ds(t, 1), :] = jnp.sum(p.T, axis=0, keepdims=True)

    _splat_rows(shn_ref, store_srep, tt=tt, hk=hk, also=lane_sums)
    coef_ref[...] = gate_ref[...] * _gelu_tanh(pre_ref[...])


def _peer_v_kernel(*refs, tt, hk, n_acc):
    n_off = tt * hk // SUBLANES
    off_refs = [refs[0].at[pl.ds(j * n_off, n_off)] for j in range(SUBLANES)]
    sh_ref, coef_ref, x_ref, tab_ref, o_ref, rep_ref = refs[1:]
    ngrp = hk // SUBLANES

    def store_coef(t, block):
        rep_ref[pl.ds(2 * t * hk, hk, stride=2), :] = block

    def store_shift(t, block):
        rep_ref[pl.ds(2 * t * hk + 1, hk, stride=2), :] = lax.bitcast_convert_type(block, F32)

    _splat_rows(sh_ref, store_shift, coef_ref, store_coef, tt=tt, hk=hk)

    def tok(t, carry):
        accs = [None] * n_acc
        e0 = 2 * t * hk
        per_load = SUBLANES // 2
        for k0 in range(0, hk, per_load):
            blk = rep_ref[pl.ds(pl.multiple_of(e0 + 2 * k0, SUBLANES), SUBLANES), :]
            for j in range(per_load):
                k = k0 + j
                bcast = lambda r: jnp.broadcast_to(blk[r:r + 1, :], (SUBLANES, LANES))
                off = off_refs[k % SUBLANES][t * ngrp + k // SUBLANES]
                sv = lax.bitcast_convert_type(bcast(2 * j + 1), U32)
                term = _splat(rep_ref, e0 + 2 * k) * _expert_row(tab_ref, off, sv)
                accs[k % n_acc] = term if accs[k % n_acc] is None else accs[k % n_acc] + term
        while len(accs) > 1:
            accs = [accs[j] + accs[j + len(accs) // 2] for j in range(len(accs) // 2)]
        o_ref[t] = x_ref[t] + accs[0]
        return carry

    lax.fori_loop(0, tt, tok, 0, unroll=8)


def _resident(a):
    return pl.BlockSpec(a.shape, lambda i: (0,) * a.ndim, pipeline_mode=pl.Buffered(1))


def _split_offsets(off, tt):
    t, hk = off.shape
    grouped = off.reshape(t // tt, tt, hk // SUBLANES, SUBLANES)
    return jnp.transpose(grouped, (0, 3, 1, 2)).reshape(-1)


def _peer_u(off, sh, h, gates, tab, *, tt):
    t, hk = off.shape
    assert hk == LANES and h.shape[1:] == (SUBLANES, LANES)
    smem = pl.BlockSpec((tt * hk,), lambda i: (i,), memory_space=pltpu.SMEM)
    compact = pl.BlockSpec((tt, hk), lambda i: (i, 0))
    last = t // tt - 1
    nxt = pl.BlockSpec((tt, hk), lambda i: (jnp.minimum(i + 1, last), 0))
    return pl.pallas_call(
        functools.partial(_peer_u_kernel, tt=tt, hk=hk),
        grid=(t // tt,),
        in_specs=[
            smem, compact, nxt, pl.BlockSpec((tt, SUBLANES, LANES), lambda i: (i, 0, 0)), compact,
            _resident(tab)],
        out_specs=compact,
        out_shape=jax.ShapeDtypeStruct((t, hk), F32),
        scratch_shapes=[pltpu.VMEM((tt * hk, LANES), F32), pltpu.VMEM((tt * hk, LANES), I32),
                        pltpu.VMEM((tt, hk), F32)],
        compiler_params=_params(("arbitrary",), VMEM_LIMIT_BYTES),
        name="peer_u",
    )(_split_offsets(off, tt), sh, sh, h, gates, tab)


def _peer_v(off, sh, coef, x, tab, *, tt):
    t, hk = off.shape
    assert hk == LANES and x.shape[1:] == (SUBLANES, LANES)
    smem = pl.BlockSpec((tt * hk,), lambda i: (i,), memory_space=pltpu.SMEM)
    tile = pl.BlockSpec((tt, SUBLANES, LANES), lambda i: (i, 0, 0))
    compact = pl.BlockSpec((tt, hk), lambda i: (i, 0))
    return pl.pallas_call(
        functools.partial(_peer_v_kernel, tt=tt, hk=hk, n_acc=4),
        grid=(t // tt,),
        in_specs=[smem, compact, compact, tile, _resident(tab)],
        out_specs=tile,
        out_shape=jax.ShapeDtypeStruct(x.shape, F32),
        scratch_shapes=[pltpu.VMEM((2 * tt * hk, LANES), F32)],
        compiler_params=_params(("arbitrary",), VMEM_LIMIT_BYTES),
        name="peer_v",
    )(_split_offsets(off, tt), sh, coef, x, tab)


def _pack_kernel(hi_ref, lo_ref, o_ref):
    def bf16_bits(x):
        bits = lax.bitcast_convert_type(x, U32)
        r = bits + (jnp.uint32(0x7FFF) + ((bits >> 16) & jnp.uint32(1)))
        return jnp.where(x != x, jnp.uint32(0x7FC00000), r)

    w = (bf16_bits(hi_ref[...]) & jnp.uint32(0xFFFF0000)) | (bf16_bits(lo_ref[...]) >> 16)
    _store_tiled(o_ref, w, w.shape[0])


def _pack_table(tab, rows=256):
    n, d = tab.shape
    assert d == SUBLANES * LANES and (n // 2) % rows == 0
    nblk = n // 2 // rows
    return pl.pallas_call(
        _pack_kernel,
        grid=(nblk,),
        in_specs=[pl.BlockSpec((rows, d), lambda i: (i, 0)),
                  pl.BlockSpec((rows, d), lambda i: (i + nblk, 0))],
        out_specs=pl.BlockSpec((rows * SUBLANES, LANES), lambda i: (i, 0)),
        out_shape=jax.ShapeDtypeStruct((n // 2 * SUBLANES, LANES), U32),
        compiler_params=_params(("arbitrary",)),
        name="pack_table",
    )(tab, tab)


def _block_diag(blocks):
    g, r, c = blocks.shape
    eye = jnp.eye(g, dtype=blocks.dtype)
    return (eye[:, None, :, None] * blocks[:, :, None, :]).reshape(g * r, g * c)


def _s5_params(a_re, a_im, log_dt, b_re, b_im, c_re, c_im):
    a = lax.complex(a_re, a_im)
    dt = jnp.exp(log_dt)[:, None]
    a_bar = jnp.exp(a * dt)
    b_bar = ((a_bar - 1.0) / a)[..., None] * lax.complex(b_re, b_im)
    bt = jnp.swapaxes(b_bar, 1, 2)
    bblk = jnp.concatenate([_block_diag(jnp.real(bt)), _block_diag(jnp.imag(bt))], axis=1)
    ct_re = jnp.swapaxes(c_re, 1, 2)
    ct_im = jnp.swapaxes(c_im, 1, 2)
    cblk = jnp.concatenate([_block_diag(ct_re), -_block_diag(ct_im)], axis=0)
    n = a_bar.size
    return (jnp.real(a_bar).reshape(1, n), jnp.imag(a_bar).reshape(1, n),
            bblk.astype(BF16), cblk.astype(BF16))


def _tiles(s, t):
    ts = min(512, s)
    blk = min(512, s)
    tc = min(64, s)
    tm = min(512, t)
    tt = min(128, t)
    return ts, blk, tc, tm, tt


def _layer(x, norm1_g, w_in, f_bias, q_gain, k_gain, pool_w, pool_scale,
           a_re, a_im, log_dt, b_re, b_im, c_re, c_im, ssm_d, glu_w, glu_b,
           out_norm_g, w_out, norm2_g, peer_wq, peer_keys, peer_u, peer_v):
    b, s, d = x.shape[0], x.shape[1] // SUBLANES, SUBLANES * LANES
    t = b * s
    n_heads = f_bias.shape[0]
    att_w = n_heads * HEAD_DIM
    pool_width = pool_scale.shape[0]
    ssm_w = ssm_d.shape[0]
    ts, blk, tc, tm, tt = _tiles(s, t)
    row = lambda v: v.reshape(1, -1)

    c0, c1 = 3 * att_w, 3 * att_w + n_heads
    wf = jnp.zeros((d, LANES), F32).at[:, :n_heads].set(w_in[:, c0:c1])
    fb = jnp.zeros((1, LANES), F32).at[0, :n_heads].set(f_bias)
    seg = _block_diag(jnp.full((n_heads, HEAD_DIM, HEAD_DIM), 1.0 / HEAD_DIM, F32)).astype(BF16)
    q, k, v, c_tok, pool, u_ssm = _inproj(
        x, row(norm1_g), w_in[:, :c0].astype(BF16), wf,
        w_in[:, c1:c1 + pool_width].astype(BF16), w_in[:, c1 + pool_width:].astype(BF16), fb,
        row(jnp.tile(q_gain, n_heads) * HEAD_DIM ** -0.5), row(jnp.tile(k_gain, n_heads)),
        seg, _block_diag(pool_w).astype(BF16), row(pool_scale), ts=ts)
    att = _attention(q, k, v, c_tok, jnp.swapaxes(c_tok, 1, 2), blk=blk)
    are, aim, bblk, cblk = _s5_params(a_re, a_im, log_dt, b_re, b_im, c_re, c_im)
    ssm = _s5(u_ssm, are, aim, bblk, cblk, row(ssm_d), glu_w.astype(BF16), row(glu_b), tc=tc)
    x1 = _merge(x, att, pool, ssm, row(out_norm_g), w_out.astype(BF16), ts=ts)

    n_experts = peer_u.shape[0]
    kh = peer_keys.astype(BF16)
    h2, off, sh, gates = _route(x1.reshape(t * SUBLANES, LANES), row(norm2_g),
                                peer_wq.astype(BF16), kh, tm=tm, n_experts=n_experts)
    tiles = lambda a: a.reshape(t, SUBLANES, LANES)
    coef = _peer_u(off, sh, tiles(h2), gates, _pack_table(peer_u), tt=tt)
    x2 = _peer_v(off, sh, coef, tiles(x1), _pack_table(peer_v), tt=tt)
    return x2.reshape(b, s * SUBLANES, LANES)


def kernel(x, norm1_g, w_in, f_bias, q_gain, k_gain, pool_w, pool_scale, ssm_a_re, ssm_a_im,
           ssm_log_dt, ssm_b_re, ssm_b_im, ssm_c_re, ssm_c_im, ssm_d, glu_w, glu_b, out_norm_g,
           w_out, norm2_g, peer_wq, peer_keys, peer_u, peer_v):
    per_layer = (norm1_g, w_in, f_bias, q_gain, k_gain, pool_w, pool_scale, ssm_a_re, ssm_a_im,
                 ssm_log_dt, ssm_b_re, ssm_b_im, ssm_c_re, ssm_c_im, ssm_d, glu_w, glu_b,
                 out_norm_g, w_out, norm2_g, peer_wq, peer_keys, peer_u, peer_v)
    b, s, d = x.shape
    assert d == SUBLANES * LANES
    x = x.reshape(b, s * SUBLANES, LANES)
    for l in range(norm1_g.shape[0]):
        x = _layer(x, *[p[l] for p in per_layer])
    return x.reshape(b, s, d)
```

```python
import functools
import math

import jax
import jax.numpy as jnp
from jax import lax
from jax.experimental import pallas as pl
from jax.experimental.pallas import tpu as pltpu

F32 = jnp.float32
BF16 = jnp.bfloat16
U32 = jnp.uint32
I32 = jnp.int32

EPS = 1e-6
HEAD_DIM = 64
POOL_WINDOWS = (2, 4, 8, 16)
POOL_HALO = 16
PEER_KEYS = 128
PEER_TOPK = 16
U_EDGES_PER_DOT = 16

LANES = 128
SUBLANES = 8
VMEM_LIMIT_BYTES = 56 * 1024 * 1024

_NT = (((1,), (1,)), ((), ()))


def _rmsnorm(x, g):
    return x * lax.rsqrt(jnp.mean(x * x, axis=-1, keepdims=True) + EPS) * g


def _gelu_tanh(x):
    c = math.sqrt(2.0 / math.pi)
    return x * (0.5 * (1.0 + jnp.tanh(c * (x + 0.044715 * (x * x * x)))))


def _params(sem, vmem=None):
    return pltpu.CompilerParams(dimension_semantics=sem, vmem_limit_bytes=vmem)


def _load_tiled(ref, n):
    return jnp.concatenate([ref[pl.ds(s, n, stride=SUBLANES), :] for s in range(SUBLANES)], axis=1)


def _store_tiled(ref, val, n):
    for s in range(SUBLANES):
        ref[pl.ds(s, n, stride=SUBLANES), :] = val[:, s * LANES:(s + 1) * LANES]


def _inproj_kernel(x_ref, g_ref, wqkv_ref, wf_ref, wpool_ref, wssm_ref, fb_ref,
                   qg_ref, kg_ref, seg_ref, pmix_ref, pscale_ref,
                   q_out, k_out, v_out, c_out, pool_out, ssm_out, carry_ref, halo_ref,
                   *, att_w, n_heads):
    i = pl.program_id(1)

    @pl.when(i == 0)
    def _():
        carry_ref[...] = jnp.zeros_like(carry_ref)
        halo_ref[...] = jnp.zeros_like(halo_ref)

    ts = x_ref.shape[0] // SUBLANES
    x = _load_tiled(x_ref, ts)
    h = _rmsnorm(x, g_ref[...])
    hb = h.astype(BF16)
    qkv = jnp.dot(hb, wqkv_ref[...], preferred_element_type=F32)
    seg = seg_ref[...]

    def headnorm(t, gain):
        ms = jnp.dot((t * t).astype(BF16), seg, preferred_element_type=F32)
        return t * lax.rsqrt(ms + EPS) * gain

    q_out[...] = headnorm(qkv[:, :att_w], qg_ref[...]).astype(BF16)
    k_out[...] = headnorm(qkv[:, att_w:2 * att_w], kg_ref[...]).astype(BF16)
    v_out[...] = qkv[:, 2 * att_w:].astype(BF16)
    u_pool = jnp.dot(hb, wpool_ref[...], preferred_element_type=F32)
    pool_out[...] = _pool_mix(u_pool, halo_ref[...], i * ts, pmix_ref[...], pscale_ref[...])
    halo_ref[...] = u_pool[ts - POOL_HALO:, :]
    ssm_out[...] = jnp.dot(hb, wssm_ref[...], preferred_element_type=F32)

    f = jnp.dot(h, wf_ref[...], precision=lax.Precision.HIGHEST,
                preferred_element_type=F32) + fb_ref[...]
    ls = jnp.minimum(f, 0.0) - jnp.log1p(jnp.exp(-jnp.abs(f)))
    r = lax.broadcasted_iota(I32, (ts, ts), 0)
    c = lax.broadcasted_iota(I32, (ts, ts), 1)
    tri = jnp.where(c <= r, 1.0, 0.0).astype(F32)
    cs = jnp.dot(tri, ls, precision=lax.Precision.HIGHEST,
                 preferred_element_type=F32) + carry_ref[0:1, :]
    carry_ref[...] = jnp.broadcast_to(cs[ts - 1:ts, :], carry_ref.shape)
    c_out[...] = cs[:, :n_heads]


def _tiled_spec(ts):
    return pl.BlockSpec((None, ts * SUBLANES, LANES), lambda bi, i: (bi, i, 0))


def _inproj(x, g, wqkv, wf, wpool, wssm, fb, qg, kg, seg, pmix, pscale, *, ts):
    b, s = x.shape[0], x.shape[1] // SUBLANES
    att_w = seg.shape[0]
    n_heads = att_w // HEAD_DIM
    pw = wpool.shape[1]
    sw = wssm.shape[1]
    full = lambda a: pl.BlockSpec(a.shape, lambda bi, i: (0,) * a.ndim)
    tok = lambda w: pl.BlockSpec((None, ts, w), lambda bi, i: (bi, i, 0))
    return pl.pallas_call(
        functools.partial(_inproj_kernel, att_w=att_w, n_heads=n_heads),
        grid=(b, s // ts),
        in_specs=[_tiled_spec(ts), full(g), full(wqkv), full(wf), full(wpool), full(wssm),
                  full(fb), full(qg), full(kg), full(seg), full(pmix), full(pscale)],
        out_specs=[tok(att_w), tok(att_w), tok(att_w), tok(n_heads), tok(pw), tok(sw)],
        out_shape=[jax.ShapeDtypeStruct((b, s, att_w), BF16),
                   jax.ShapeDtypeStruct((b, s, att_w), BF16),
                   jax.ShapeDtypeStruct((b, s, att_w), BF16),
                   jax.ShapeDtypeStruct((b, s, n_heads), F32),
                   jax.ShapeDtypeStruct((b, s, pw), F32),
                   jax.ShapeDtypeStruct((b, s, sw), F32)],
        scratch_shapes=[pltpu.VMEM((SUBLANES, LANES), F32), pltpu.VMEM((POOL_HALO, pw), F32)],
        compiler_params=_params(("arbitrary", "arbitrary"), VMEM_LIMIT_BYTES),
        name="inproj",
    )(x, g, wqkv, wf, wpool, wssm, fb, qg, kg, seg, pmix, pscale)


def _attn_kernel(q_ref, k_ref, v_ref, ct_ref, cr_ref, o_ref, m_sc, l_sc, acc_sc, s_sc, *, blk):
    p = pl.program_id(1)
    i = pl.program_id(2)
    q = q_ref[...]
    lane = lax.broadcasted_iota(I32, (blk, LANES), 1)
    ct = ct_ref[...]
    hl = lax.broadcasted_iota(I32, ct.shape, 1)
    heads = (2 * p, 2 * p + 1)
    qm = (jnp.where(lane < HEAD_DIM, q, jnp.zeros_like(q)),
          jnp.where(lane >= HEAD_DIM, q, jnp.zeros_like(q)))
    cq = tuple(jnp.sum(jnp.where(hl == hd, ct, 0.0), axis=1, keepdims=True) for hd in heads)
    m_sc[...] = jnp.full(m_sc.shape, -jnp.inf, F32)
    l_sc[...] = jnp.zeros(l_sc.shape, F32)
    acc_sc[...] = jnp.zeros(acc_sc.shape, F32)

    def qk(kb):
        kblk = k_ref[pl.ds(pl.multiple_of(kb * blk, blk), blk), :]
        return [lax.dot_general(qm[hh], kblk, _NT, preferred_element_type=F32) for hh in range(2)]

    def step(kb, s_cur, causal):
        start = pl.multiple_of(kb * blk, blk)
        vblk = v_ref[pl.ds(start, blk), :]
        for hh in range(2):
            ck = cr_ref[pl.ds(heads[hh], 1), pl.ds(start, blk)]
            z = (s_cur[hh] + cq[hh]) - ck
            if causal:
                row = lax.broadcasted_iota(I32, (blk, blk), 0)
                col = lax.broadcasted_iota(I32, (blk, blk), 1)
                z = jnp.where(col <= row, z, -jnp.inf)
            m_old = m_sc[hh]
            m_new = jnp.maximum(m_old, jnp.max(z, axis=1, keepdims=True))
            alpha = jnp.exp(m_old - m_new)
            pe = jnp.exp(z - m_new)
            l_sc[hh] = alpha * l_sc[hh] + jnp.sum(pe, axis=1, keepdims=True)
            acc_sc[hh] = alpha * acc_sc[hh] + jnp.dot(pe.astype(BF16), vblk,
                                                      preferred_element_type=F32)
            m_sc[hh] = m_new

    s0 = qk(0)
    s_sc[0] = s0[0]
    s_sc[1] = s0[1]

    def body(kb, carry):
        s_cur = [s_sc[0], s_sc[1]]
        s_next = qk(kb + 1)
        step(kb, s_cur, False)
        s_sc[0] = s_next[0]
        s_sc[1] = s_next[1]
        return carry

    lax.fori_loop(0, i, body, 0)
    step(i, [s_sc[0], s_sc[1]], True)
    o_ref[...] = jnp.where(lane < HEAD_DIM, acc_sc[0] / l_sc[0], acc_sc[1] / l_sc[1])


def _attention(q, k, v, c_tok, c_row, *, blk):
    b, s, w = q.shape
    n_heads = c_tok.shape[2]
    return pl.pallas_call(
        functools.partial(_attn_kernel, blk=blk),
        grid=(b, w // LANES, s // blk),
        in_specs=[pl.BlockSpec((None, blk, LANES), lambda bi, p, i: (bi, i, p)),
                  pl.BlockSpec((None, s, LANES), lambda bi, p, i: (bi, 0, p)),
                  pl.BlockSpec((None, s, LANES), lambda bi, p, i: (bi, 0, p)),
                  pl.BlockSpec((None, blk, n_heads), lambda bi, p, i: (bi, i, 0)),
                  pl.BlockSpec((None, n_heads, s), lambda bi, p, i: (bi, 0, 0))],
        out_specs=pl.BlockSpec((None, blk, LANES), lambda bi, p, i: (bi, i, p)),
        out_shape=jax.ShapeDtypeStruct((b, s, w), F32),
        scratch_shapes=[pltpu.VMEM((2, blk, 1), F32), pltpu.VMEM((2, blk, 1), F32),
                        pltpu.VMEM((2, blk, LANES), F32), pltpu.VMEM((2, blk, blk), F32)],
        compiler_params=_params(("arbitrary", "arbitrary", "arbitrary"), VMEM_LIMIT_BYTES),
        name="fox_attention",
    )(q, k, v, c_tok, c_row)


def _pool_mix(cur, prev, first_pos, w, scale):
    ts, pw = cur.shape
    group_dim = pw // len(POOL_WINDOWS)
    ext = jnp.concatenate([prev, cur], axis=0)
    pos = first_pos + lax.broadcasted_iota(I32, (ts, 1), 0)
    grp = lax.broadcasted_iota(I32, (ts, pw), 1) // group_dim
    acc = cur
    pooled = jnp.zeros_like(cur)
    for j in range(1, max(POOL_WINDOWS)):
        acc = acc + ext[POOL_HALO - j:POOL_HALO - j + ts, :]
        win = j + 1
        if win in POOL_WINDOWS:
            cnt = jnp.minimum(pos + 1, win).astype(F32)
            pooled = jnp.where(grp == POOL_WINDOWS.index(win), acc / cnt, pooled)
    pooled = pooled - cur
    return jnp.dot(pooled.astype(BF16), w, preferred_element_type=F32) * scale


def _s5_kernel(u_ref, are_ref, aim_ref, bblk_ref, cblk_ref, d_ref, gw_ref, gb_ref,
               o_ref, st_ref, hre_ref, him_ref, tm_ref, *, nb, n):
    @pl.when(pl.program_id(0) == 0)
    def _():
        hre_ref[...] = jnp.zeros_like(hre_ref)
        him_ref[...] = jnp.zeros_like(him_ref)

    tc = u_ref.shape[1]
    nslab = tm_ref.shape[0]
    for bi in range(nb):
        for c in range(nslab):
            tm_ref[c, pl.ds(bi, tc, stride=nb), :] = u_ref[bi, :, c * LANES:(c + 1) * LANES]
    u = jnp.concatenate([tm_ref[c] for c in range(nslab)], axis=1)
    st_ref[...] = jnp.dot(u.astype(BF16), bblk_ref[...], preferred_element_type=F32)
    ar = jnp.broadcast_to(are_ref[...], (nb, n))
    ai = jnp.broadcast_to(aim_ref[...], (nb, n))

    def step(t, carry):
        hr, hi = carry
        r0 = pl.multiple_of(t * nb, nb)
        br = st_ref[pl.ds(r0, nb), 0:n]
        bi = st_ref[pl.ds(r0, nb), n:2 * n]
        nr = (ar * hr - ai * hi) + br
        ni = (ar * hi + ai * hr) + bi
        st_ref[pl.ds(r0, nb), 0:n] = nr
        st_ref[pl.ds(r0, nb), n:2 * n] = ni
        return nr, ni

    hr, hi = lax.fori_loop(0, tc, step, (hre_ref[...], him_ref[...]))
    hre_ref[...] = hr
    him_ref[...] = hi
    y = jnp.dot(st_ref[...].astype(BF16), cblk_ref[...], preferred_element_type=F32)
    z = _gelu_tanh(y + d_ref[...] * u)
    gate = jax.nn.sigmoid(jnp.dot(z.astype(BF16), gw_ref[...], preferred_element_type=F32)
                          + gb_ref[...])
    out = z * gate
    for c in range(nslab):
        tm_ref[c] = out[:, c * LANES:(c + 1) * LANES]
    for bi in range(nb):
        for c in range(nslab):
            o_ref[bi, :, c * LANES:(c + 1) * LANES] = tm_ref[c, pl.ds(bi, tc, stride=nb), :]


def _s5(u, a_re, a_im, bblk, cblk, dskip, gw, gb, *, tc):
    nb, s, sw = u.shape
    n = a_re.shape[1]
    rb = tc * nb
    full = lambda a: pl.BlockSpec(a.shape, lambda i: (0,) * a.ndim)
    blk = pl.BlockSpec((nb, tc, sw), lambda i: (0, i, 0))
    return pl.pallas_call(
        functools.partial(_s5_kernel, nb=nb, n=n),
        grid=(s // tc,),
        in_specs=[blk, full(a_re), full(a_im),
                  full(bblk), full(cblk), full(dskip), full(gw), full(gb)],
        out_specs=blk,
        out_shape=jax.ShapeDtypeStruct(u.shape, F32),
        scratch_shapes=[pltpu.VMEM((rb, 2 * n), F32), pltpu.VMEM((nb, n), F32),
                        pltpu.VMEM((nb, n), F32), pltpu.VMEM((sw // LANES, rb, LANES), F32)],
        compiler_params=_params(("arbitrary",), VMEM_LIMIT_BYTES),
        name="s5_scan",
    )(u, a_re, a_im, bblk, cblk, dskip, gw, gb)


def _merge_kernel(x_ref, att_ref, pool_ref, ssm_ref, og_ref, w_ref, o_ref, *, att_w, pool_w):
    og = og_ref[...]
    p1 = att_w + pool_w
    merged = jnp.concatenate([
        _rmsnorm(att_ref[...], og[:, :att_w]),
        _rmsnorm(pool_ref[...], og[:, att_w:p1]),
        _rmsnorm(ssm_ref[...], og[:, p1:])], axis=1).astype(BF16)
    ts = merged.shape[0]
    out = _load_tiled(x_ref, ts) + jnp.dot(merged, w_ref[...], preferred_element_type=F32)
    _store_tiled(o_ref, out, ts)


def _merge(x, att, pool, ssm, og, w, *, ts):
    b, s = x.shape[0], x.shape[1] // SUBLANES
    att_w, pool_w, sw = att.shape[2], pool.shape[2], ssm.shape[2]
    tok = lambda wd: pl.BlockSpec((None, ts, wd), lambda bi, i: (bi, i, 0))
    return pl.pallas_call(
        functools.partial(_merge_kernel, att_w=att_w, pool_w=pool_w),
        grid=(b, s // ts),
        in_specs=[_tiled_spec(ts), tok(att_w), tok(pool_w), tok(sw),
                  pl.BlockSpec(og.shape, lambda bi, i: (0, 0)),
                  pl.BlockSpec(w.shape, lambda bi, i: (0, 0))],
        out_specs=_tiled_spec(ts),
        out_shape=jax.ShapeDtypeStruct(x.shape, F32),
        compiler_params=_params(("arbitrary", "arbitrary"), VMEM_LIMIT_BYTES),
        name="merge_outproj",
    )(x, att, pool, ssm, og, w)


def _topk_rows(a, k, extra=None):
    nrow, nl = a.shape
    row = lax.broadcasted_iota(I32, (nrow, nl), 0).astype(F32)
    out_row = lax.broadcasted_iota(I32, (k, nl), 0)
    vals = jnp.zeros((k, nl), F32)
    sel = jnp.zeros((k, nl), F32)
    for j in range(k):
        m = jnp.max(a, axis=0, keepdims=True)
        ix = jnp.min(jnp.where(a == m, row, float(nrow)), axis=0, keepdims=True)
        hit = row == ix
        if extra is None:
            pick = ix
        else:
            pick = jnp.sum(jnp.where(hit, extra, 0.0), axis=0, keepdims=True)
        vals = jnp.where(out_row == j, m, vals)
        sel = jnp.where(out_row == j, pick, sel)
        a = jnp.where(hit, -jnp.inf, a)
    return vals, sel


def _route_kernel(x_ref, g_ref, wq_ref, keys_ref, h_out, off_out, sh_out, gate_out, q_sc, stage_sc,
                  *, n_half):
    hd = pl.program_id(1)
    qd = 2 * PEER_KEYS

    @pl.when(hd == 0)
    def _():
        tm = q_sc.shape[0]
        h = _rmsnorm(_load_tiled(x_ref, tm), g_ref[...])
        _store_tiled(h_out, h, tm)
        q_sc[...] = jnp.dot(h.astype(BF16), wq_ref[...], preferred_element_type=F32)

    q = q_sc[:, pl.ds(pl.multiple_of(hd * qd, qd), qd)].astype(BF16)
    tops = []
    for i in range(2):
        sc = lax.dot_general(keys_ref[i], q[:, i * PEER_KEYS:(i + 1) * PEER_KEYS], _NT,
                             preferred_element_type=F32)
        tops.append(_topk_rows(sc, PEER_TOPK))
    (s1, i1), (s2, i2) = tops
    k = PEER_TOPK
    sub8 = lax.broadcasted_iota(I32, (SUBLANES, s1.shape[1]), 0)
    cands, ecands = [], []
    a = 0
    while a < k and k // (a + 1) > 1:
        nb = k // (a + 1)
        for b0 in range(0, nb, SUBLANES):
            c = s1[a:a + 1, :] + s2[b0:b0 + SUBLANES, :]
            if b0 + SUBLANES > nb:
                c = jnp.where(sub8 < nb - b0, c, -jnp.inf)
            cands.append(c)
            ecands.append(i1[a:a + 1, :] * float(PEER_KEYS) + i2[b0:b0 + SUBLANES, :])
        a += 1
    cands.append(s1[a:, :] + s2[0:1, :])
    ecands.append(i1[a:, :] * float(PEER_KEYS) + i2[0:1, :])
    best, experts = _topk_rows(jnp.concatenate(cands, axis=0), k,
                               extra=jnp.concatenate(ecands, axis=0))
    w = jnp.exp(best - best[0:1, :])
    rows = pl.ds(pl.multiple_of(hd * k, k), k)
    stage_sc[2, rows, :] = w / jnp.sum(w, axis=0, keepdims=True)
    hi_half = experts < float(n_half)
    stage_sc[0, rows, :] = jnp.where(hi_half, experts, experts - float(n_half)) * float(SUBLANES)
    stage_sc[1, rows, :] = jnp.where(hi_half, 0.0, 16.0)

    @pl.when(hd == pl.num_programs(1) - 1)
    def _():
        off_out[...] = stage_sc[0].T.astype(I32)
        sh_out[...] = stage_sc[1].T
        gate_out[...] = stage_sc[2].T


def _route(x, g, wq, keys, *, tm, n_experts):
    t = x.shape[0] // SUBLANES
    n_heads = keys.shape[0]
    n_sel = n_heads * PEER_TOPK
    hk = pl.BlockSpec((tm, n_sel), lambda i, h: (i, 0))
    shp = lambda dt: jax.ShapeDtypeStruct((t, n_sel), dt)
    tiled = pl.BlockSpec((tm * SUBLANES, LANES), lambda i, h: (i, 0))
    return pl.pallas_call(
        functools.partial(_route_kernel, n_half=n_experts // 2),
        grid=(t // tm, n_heads),
        in_specs=[tiled,
                  pl.BlockSpec(g.shape, lambda i, h: (0, 0)),
                  pl.BlockSpec(wq.shape, lambda i, h: (0, 0)),
                  pl.BlockSpec((None, 2, PEER_KEYS, keys.shape[3]), lambda i, h: (h, 0, 0, 0))],
        out_specs=[tiled, hk, hk, hk],
        out_shape=[jax.ShapeDtypeStruct(x.shape, F32), shp(I32), shp(F32), shp(F32)],
        scratch_shapes=[pltpu.VMEM((tm, wq.shape[1]), F32), pltpu.VMEM((3, n_sel, tm), F32)],
        compiler_params=_params(("arbitrary", "arbitrary"), VMEM_LIMIT_BYTES),
        name="peer_route",
    )(x, g, wq, keys)


def _expert_row(tab_ref, off, sh):
    w = tab_ref[pl.ds(pl.multiple_of(off, SUBLANES), SUBLANES), :]
    return lax.bitcast_convert_type((w << sh) & jnp.uint32(0xFFFF0000), F32)


def _splat_rows(mxu_src, mxu_dst, xlu_src=None, xlu_dst=None, *, tt, hk, also=None):
    eye = (lax.broadcasted_iota(I32, (hk, hk), 0) == lax.broadcasted_iota(I32, (hk, hk), 1))
    ones = jnp.ones((hk, hk), BF16)

    def body(t, carry):
        v = jnp.broadcast_to(mxu_src[pl.ds(t, 1), :], (hk, hk))
        rep = jnp.dot(jnp.where(eye, v, 0.0).astype(BF16), ones, preferred_element_type=F32)
        mxu_dst(t, rep.astype(I32))
        if xlu_src is not None:
            xlu_dst(t, jnp.broadcast_to(xlu_src[pl.ds(t, 1), :], (hk, hk)).T)
        if also is not None:
            also(t)
        return carry

    lax.fori_loop(0, tt, body, 0, unroll=8)


def _splat(rep_ref, row):
    return jnp.broadcast_to(rep_ref[pl.ds(row, 1), :], (SUBLANES, LANES))


def _peer_u_kernel(*refs, tt, hk):
    n_off = tt * hk // SUBLANES
    off_refs = [refs[0].at[pl.ds(j * n_off, n_off)] for j in range(SUBLANES)]
    sh_ref, shn_ref, h_ref, gate_ref, tab_ref, coef_ref, part_ref, srep_ref, pre_ref = refs[1:]
    ngrp = hk // SUBLANES

    def store_srep(t, block):
        srep_ref[pl.ds(pl.multiple_of(t * hk, hk), hk), :] = block

    @pl.when(pl.program_id(0) == 0)
    def _():
        _splat_rows(sh_ref, store_srep, tt=tt, hk=hk)

    chunk = U_EDGES_PER_DOT
    sel = (lax.broadcasted_iota(I32, (chunk, chunk * SUBLANES), 1) // SUBLANES
           == lax.broadcasted_iota(I32, (chunk, chunk * SUBLANES), 0)).astype(BF16)

    def tok(t, carry):
        h = h_ref[t]
        base = pl.multiple_of(t * hk, hk)
        for c in range(hk // chunk):
            e0 = base + c * chunk
            ps = []
            for k0 in range(0, chunk, SUBLANES):
                sblk = srep_ref[pl.ds(e0 + k0, SUBLANES), :].astype(U32)
                for j in range(SUBLANES):
                    k = k0 + j
                    sv = jnp.broadcast_to(sblk[j:j + 1, :], (SUBLANES, LANES))
                    ps.append(_expert_row(
                        tab_ref, off_refs[j][t * ngrp + (c * chunk + k) // SUBLANES], sv) * h)
            stacked = jnp.concatenate(ps, axis=0).astype(BF16)
            part_ref[pl.ds(e0, chunk), :] = jnp.dot(sel, stacked, preferred_element_type=F32)
        return carry

    lax.fori_loop(0, tt, tok, 0, unroll=32)

    def lane_sums(t):
        p = part_ref[pl.ds(pl.multiple_of(t * hk, hk), hk), :]
        pre_ref[pl.ds(t, 1), :] = jnp.sum(p.T, axis=0, keepdims=True)

    _splat_rows(shn_ref, store_srep, tt=tt, hk=hk, also=lane_sums)
    coef_ref[...] = gate_ref[...] * _gelu_tanh(pre_ref[...])


def _peer_v_kernel(*refs, tt, hk, n_acc):
    n_off = tt * hk // SUBLANES
    off_refs = [refs[0].at[pl.ds(j * n_off, n_off)] for j in range(SUBLANES)]
    sh_ref, coef_ref, x_ref, tab_ref, o_ref, rep_ref = refs[1:]
    ngrp = hk // SUBLANES

    def store_coef(t, block):
        rep_ref[pl.ds(2 * t * hk, hk, stride=2), :] = block

    def store_shift(t, block):
        rep_ref[pl.ds(2 * t * hk + 1, hk, stride=2), :] = lax.bitcast_convert_type(block, F32)

    _splat_rows(sh_ref, store_shift, coef_ref, store_coef, tt=tt, hk=hk)

    chunk = U_EDGES_PER_DOT
    add_rows = (lax.broadcasted_iota(I32, (SUBLANES, chunk * SUBLANES), 1) % SUBLANES
                == lax.broadcasted_iota(I32, (SUBLANES, chunk * SUBLANES), 0)).astype(BF16)

    def tok(t, carry):
        accs = [None] * n_acc
        e0 = 2 * t * hk
        per_load = SUBLANES // 2
        for c0 in range(0, hk, chunk):
            terms = []
            for k0 in range(c0, c0 + chunk, per_load):
                blk = rep_ref[pl.ds(pl.multiple_of(e0 + 2 * k0, SUBLANES), SUBLANES), :]
                for j in range(per_load):
                    k = k0 + j
                    off = off_refs[k % SUBLANES][t * ngrp + k // SUBLANES]
                    sv = lax.bitcast_convert_type(
                        jnp.broadcast_to(blk[2 * j + 1:2 * j + 2, :], (SUBLANES, LANES)), U32)
                    terms.append(_splat(rep_ref, e0 + 2 * k) * _expert_row(tab_ref, off, sv))
            part = jnp.dot(add_rows, jnp.concatenate(terms, axis=0).astype(BF16),
                           preferred_element_type=F32)
            a = (c0 // chunk) % n_acc
            accs[a] = part if accs[a] is None else accs[a] + part
        accs = [a for a in accs if a is not None]
        while len(accs) > 1:
            accs = [accs[j] + accs[j + len(accs) // 2] for j in range(len(accs) // 2)]
        o_ref[t] = x_ref[t] + accs[0]
        return carry

    lax.fori_loop(0, tt, tok, 0, unroll=32)


def _resident(a):
    return pl.BlockSpec(a.shape, lambda i: (0,) * a.ndim, pipeline_mode=pl.Buffered(1))


def _split_offsets(off, tt):
    t, hk = off.shape
    grouped = off.reshape(t // tt, tt, hk // SUBLANES, SUBLANES)
    return jnp.transpose(grouped, (0, 3, 1, 2)).reshape(-1)


def _peer_u(off, sh, h, gates, tab, *, tt):
    t, hk = off.shape
    assert hk == LANES and h.shape[1:] == (SUBLANES, LANES)
    smem = pl.BlockSpec((tt * hk,), lambda i: (i,), memory_space=pltpu.SMEM)
    compact = pl.BlockSpec((tt, hk), lambda i: (i, 0))
    last = t // tt - 1
    nxt = pl.BlockSpec((tt, hk), lambda i: (jnp.minimum(i + 1, last), 0))
    return pl.pallas_call(
        functools.partial(_peer_u_kernel, tt=tt, hk=hk),
        grid=(t // tt,),
        in_specs=[
            smem, compact, nxt, pl.BlockSpec((tt, SUBLANES, LANES), lambda i: (i, 0, 0)), compact,
            _resident(tab)],
        out_specs=compact,
        out_shape=jax.ShapeDtypeStruct((t, hk), F32),
        scratch_shapes=[pltpu.VMEM((tt * hk, LANES), F32), pltpu.VMEM((tt * hk, LANES), I32),
                        pltpu.VMEM((tt, hk), F32)],
        compiler_params=_params(("arbitrary",), VMEM_LIMIT_BYTES),
        name="peer_u",
    )(_split_offsets(off, tt), sh, sh, h, gates, tab)


def _peer_v(off, sh, coef, x, tab, *, tt):
    t, hk = off.shape
    assert hk == LANES and x.shape[1:] == (SUBLANES, LANES)
    smem = pl.BlockSpec((tt * hk,), lambda i: (i,), memory_space=pltpu.SMEM)
    tile = pl.BlockSpec((tt, SUBLANES, LANES), lambda i: (i, 0, 0))
    compact = pl.BlockSpec((tt, hk), lambda i: (i, 0))
    return pl.pallas_call(
        functools.partial(_peer_v_kernel, tt=tt, hk=hk, n_acc=4),
        grid=(t // tt,),
        in_specs=[smem, compact, compact, tile, _resident(tab)],
        out_specs=tile,
        out_shape=jax.ShapeDtypeStruct(x.shape, F32),
        scratch_shapes=[pltpu.VMEM((2 * tt * hk, LANES), F32)],
        compiler_params=_params(("arbitrary",), VMEM_LIMIT_BYTES),
        name="peer_v",
    )(_split_offsets(off, tt), sh, coef, x, tab)


def _pack_kernel(hi_ref, lo_ref, o_ref):
    def bf16_bits(x):
        bits = lax.bitcast_convert_type(x, U32)
        r = bits + (jnp.uint32(0x7FFF) + ((bits >> 16) & jnp.uint32(1)))
        return jnp.where(x != x, jnp.uint32(0x7FC00000), r)

    w = (bf16_bits(hi_ref[...]) & jnp.uint32(0xFFFF0000)) | (bf16_bits(lo_ref[...]) >> 16)
    _store_tiled(o_ref, w, w.shape[0])


def _pack_table(tab, rows=256):
    n, d = tab.shape
    assert d == SUBLANES * LANES and (n // 2) % rows == 0
    nblk = n // 2 // rows
    return pl.pallas_call(
        _pack_kernel,
        grid=(nblk,),
        in_specs=[pl.BlockSpec((rows, d), lambda i: (i, 0)),
                  pl.BlockSpec((rows, d), lambda i: (i + nblk, 0))],
        out_specs=pl.BlockSpec((rows * SUBLANES, LANES), lambda i: (i, 0)),
        out_shape=jax.ShapeDtypeStruct((n // 2 * SUBLANES, LANES), U32),
        compiler_params=_params(("arbitrary",)),
        name="pack_table",
    )(tab, tab)


def _block_diag(blocks):
    g, r, c = blocks.shape
    eye = jnp.eye(g, dtype=blocks.dtype)
    return (eye[:, None, :, None] * blocks[:, :, None, :]).reshape(g * r, g * c)


def _s5_params(a_re, a_im, log_dt, b_re, b_im, c_re, c_im):
    a = lax.complex(a_re, a_im)
    dt = jnp.exp(log_dt)[:, None]
    a_bar = jnp.exp(a * dt)
    b_bar = ((a_bar - 1.0) / a)[..., None] * lax.complex(b_re, b_im)
    bt = jnp.swapaxes(b_bar, 1, 2)
    bblk = jnp.concatenate([_block_diag(jnp.real(bt)), _block_diag(jnp.imag(bt))], axis=1)
    ct_re = jnp.swapaxes(c_re, 1, 2)
    ct_im = jnp.swapaxes(c_im, 1, 2)
    cblk = jnp.concatenate([_block_diag(ct_re), -_block_diag(ct_im)], axis=0)
    n = a_bar.size
    return (jnp.real(a_bar).reshape(1, n), jnp.imag(a_bar).reshape(1, n),
            bblk.astype(BF16), cblk.astype(BF16))


def _tiles(s, t):
    ts = min(512, s)
    blk = min(512, s)
    tc = min(64, s)
    tm = min(512, t)
    tt = min(128, t)
    return ts, blk, tc, tm, tt


def _layer(x, norm1_g, w_in, f_bias, q_gain, k_gain, pool_w, pool_scale,
           a_re, a_im, log_dt, b_re, b_im, c_re, c_im, ssm_d, glu_w, glu_b,
           out_norm_g, w_out, norm2_g, peer_wq, peer_keys, peer_u, peer_v):
    b, s, d = x.shape[0], x.shape[1] // SUBLANES, SUBLANES * LANES
    t = b * s
    n_heads = f_bias.shape[0]
    att_w = n_heads * HEAD_DIM
    pool_width = pool_scale.shape[0]
    ssm_w = ssm_d.shape[0]
    ts, blk, tc, tm, tt = _tiles(s, t)
    row = lambda v: v.reshape(1, -1)

    c0, c1 = 3 * att_w, 3 * att_w + n_heads
    wf = jnp.zeros((d, LANES), F32).at[:, :n_heads].set(w_in[:, c0:c1])
    fb = jnp.zeros((1, LANES), F32).at[0, :n_heads].set(f_bias)
    seg = _block_diag(jnp.full((n_heads, HEAD_DIM, HEAD_DIM), 1.0 / HEAD_DIM, F32)).astype(BF16)
    q, k, v, c_tok, pool, u_ssm = _inproj(
        x, row(norm1_g), w_in[:, :c0].astype(BF16), wf,
        w_in[:, c1:c1 + pool_width].astype(BF16), w_in[:, c1 + pool_width:].astype(BF16), fb,
        row(jnp.tile(q_gain, n_heads) * HEAD_DIM ** -0.5), row(jnp.tile(k_gain, n_heads)),
        seg, _block_diag(pool_w).astype(BF16), row(pool_scale), ts=ts)
    att = _attention(q, k, v, c_tok, jnp.swapaxes(c_tok, 1, 2), blk=blk)
    are, aim, bblk, cblk = _s5_params(a_re, a_im, log_dt, b_re, b_im, c_re, c_im)
    ssm = _s5(u_ssm, are, aim, bblk, cblk, row(ssm_d), glu_w.astype(BF16), row(glu_b), tc=tc)
    x1 = _merge(x, att, pool, ssm, row(out_norm_g), w_out.astype(BF16), ts=ts)

    n_experts = peer_u.shape[0]
    kh = peer_keys.astype(BF16)
    h2, off, sh, gates = _route(x1.reshape(t * SUBLANES, LANES), row(norm2_g),
                                peer_wq.astype(BF16), kh, tm=tm, n_experts=n_experts)
    tiles = lambda a: a.reshape(t, SUBLANES, LANES)
    coef = _peer_u(off, sh, tiles(h2), gates, _pack_table(peer_u), tt=tt)
    x2 = _peer_v(off, sh, coef, tiles(x1), _pack_table(peer_v), tt=tt)
    return x2.reshape(b, s * SUBLANES, LANES)


def kernel(x, norm1_g, w_in, f_bias, q_gain, k_gain, pool_w, pool_scale, ssm_a_re, ssm_a_im,
           ssm_log_dt, ssm_b_re, ssm_b_im, ssm_c_re, ssm_c_im, ssm_d, glu_w, glu_b, out_norm_g,
           w_out, norm2_g, peer_wq, peer_keys, peer_u, peer_v):
    per_layer = (norm1_g, w_in, f_bias, q_gain, k_gain, pool_w, pool_scale, ssm_a_re, ssm_a_im,
                 ssm_log_dt, ssm_b_re, ssm_b_im, ssm_c_re, ssm_c_im, ssm_d, glu_w, glu_b,
                 out_norm_g, w_out, norm2_g, peer_wq, peer_keys, peer_u, peer_v)
    b, s, d = x.shape
    assert d == SUBLANES * LANES
    x = x.reshape(b, s * SUBLANES, LANES)
    for l in range(norm1_g.shape[0]):
        x = _layer(x, *[p[l] for p in per_layer])
    return x.reshape(b, s, d)
```
